```python
import jax
import jax.numpy as jnp
from jax import lax
import numpy as np

D_MODEL = 2048
BATCH = 2
SEQ = 4096
DEPTH = 2
DEC_BATCH = 8
DEC_SEQ = 4
PAST_LEN = 16384
PAGE_SIZE = 128

N_MIXERS = 2
N_NSA_LAYERS = (DEPTH + 1) // 2
N_RET_LAYERS = DEPTH // 2
N_HEADS = 16
HEAD_DIM = D_MODEL // N_HEADS
N_KV_HEADS = 4
GQA_GROUP = N_HEADS // N_KV_HEADS
CMP_BLOCK = 32
CMP_STRIDE = 16
CMP_RATIO = CMP_BLOCK // CMP_STRIDE
CMP_HIDDEN = HEAD_DIM
SEL_BLOCK = 64
N_SEL = 16
N_LOCAL = 2
WINDOW = 512
Q_BLOCK = 128
ROPE_THETA = 10000.0
RET_HEADS = 8
RET_KDIM = D_MODEL // RET_HEADS
RET_VDIM = 2 * D_MODEL // RET_HEADS
RET_CHUNK = 128
D_FF = 4 * D_MODEL
NSA_PROJ = N_HEADS * HEAD_DIM + 6 * N_KV_HEADS * HEAD_DIM + 3 * N_HEADS
RET_PROJ = 2 * RET_HEADS * RET_KDIM + 2 * RET_HEADS * RET_VDIM
RMS_EPS = 1e-6
GN_EPS = 1e-5
NEG_INF = -1e30
FORCE_SCORE = 1e9
F32 = jnp.float32

kernel_name = 'nsa_retention_hybrid_step'


def rms_norm(x, g):
    xf = x.astype(F32)
    y = xf * lax.rsqrt(jnp.mean(xf * xf, axis=-1, keepdims=True) + RMS_EPS)
    return (y * g.astype(F32)).astype(x.dtype)


def rope(x, pos):
    half = x.shape[-1] // 2
    inv = ROPE_THETA ** (-jnp.arange(half, dtype=F32) / half)
    ang = pos.astype(F32)[:, None] * inv[None, :]
    cos = jnp.cos(ang)[None, :, None, :]
    sin = jnp.sin(ang)[None, :, None, :]
    xf = x.astype(F32)
    x1, x2 = xf[..., :half], xf[..., half:]
    return jnp.concatenate([x1 * cos - x2 * sin, x2 * cos + x1 * sin], axis=-1).astype(x.dtype)


def masked_softmax(s, mask):
    p = jax.nn.softmax(jnp.where(mask, s, NEG_INF), axis=-1)
    return jnp.where(mask, p, 0.0)


def gqa_attend(q, k, v, mask):
    b, tq, h, dh = q.shape
    g = k.shape[2]
    qg = q.reshape(b, tq, g, h // g, dh)
    s = jnp.einsum('bqgrd,bsgd->bgrqs', qg, k).astype(F32) * dh ** -0.5
    p = masked_softmax(s, mask)
    o = jnp.einsum('bgrqs,bsgd->bqgrd', p.astype(v.dtype), v)
    return o.reshape(b, tq, h, dh)


def window_mask(q_pos, k_pos):
    d = q_pos[:, None] - k_pos[None, :]
    return (d >= 0) & (d < WINDOW) & (k_pos[None, :] >= 0)


def nsa_project(x, w_in, pos):
    b, t, _ = x.shape
    nq = N_HEADS * HEAD_DIM
    nkv = 6 * N_KV_HEADS * HEAD_DIM
    z = x @ w_in
    q = rope(z[..., :nq].reshape(b, t, N_HEADS, HEAD_DIM), pos)
    kv = z[..., nq:nq + nkv].reshape(b, t, 3, 2, N_KV_HEADS, HEAD_DIM)
    k = rope(kv[:, :, :, 0].reshape(b, t, 3 * N_KV_HEADS, HEAD_DIM), pos).reshape(b, t, 3, N_KV_HEADS, HEAD_DIM)
    kv = jnp.stack([k, kv[:, :, :, 1]], axis=3)
    gates = jax.nn.sigmoid(z[..., nq + nkv:].astype(F32)).reshape(b, t, 3, N_HEADS)
    return q, kv[:, :, 0], kv[:, :, 1], kv[:, :, 2], gates


def compress_blocks(kv_raw, cmp_pos, w1, b1, w2):
    b, length = kv_raw.shape[:2]
    n_chunks = length // CMP_STRIDE
    n_cmp = n_chunks - CMP_RATIO + 1
    xc = kv_raw[:, :n_chunks * CMP_STRIDE].reshape(b, n_chunks, CMP_STRIDE, 2, N_KV_HEADS, HEAD_DIM)
    w1r = w1.reshape(2, CMP_RATIO, CMP_STRIDE, HEAD_DIM, CMP_HIDDEN)
    per = jnp.einsum('bcsegd,ersdh->rbcegh', xc, w1r)
    pos_term = jnp.einsum('ersd,ersdh->eh', cmp_pos.reshape(2, CMP_RATIO, CMP_STRIDE, HEAD_DIM), w1r)
    h = per[0, :, 0:n_cmp]
    for r in range(1, CMP_RATIO):
        h = h + per[r, :, r:r + n_cmp]
    h = jax.nn.gelu(h.astype(F32) + (pos_term + b1).astype(F32)[:, None, :])
    out = jnp.einsum('bcegh,ehd->bcegd', h.astype(w2.dtype), w2)
    return out[:, :, 0], out[:, :, 1]


def cmp_branch(q, k_c, v_c, q_pos):
    b, tq, h, dh = q.shape
    n_cmp = k_c.shape[1]
    blk_last = jnp.arange(n_cmp) * CMP_STRIDE + CMP_BLOCK - 1
    mask = blk_last[None, :] <= q_pos[:, None]
    qg = q.reshape(b, tq, N_KV_HEADS, GQA_GROUP, dh)
    s = jnp.einsum('bqgrd,bcgd->bgrqc', qg, k_c).astype(F32) * dh ** -0.5
    p = masked_softmax(s, mask)
    o = jnp.einsum('bgrqc,bcgd->bqgrd', p.astype(v_c.dtype), v_c).reshape(b, tq, h, dh)
    return o, p.sum(axis=2)


def select_blocks(imp_cmp, q_pos, n_blk):
    n_cmp = imp_cmp.shape[-1]
    c0 = jnp.arange(n_cmp) * CMP_STRIDE
    j0 = jnp.arange(n_blk) * SEL_BLOCK
    cover = ((c0[:, None] < j0[None, :] + SEL_BLOCK) & (c0[:, None] + CMP_BLOCK > j0[None, :])).astype(F32)
    imp = jnp.einsum('bgqc,cj->bgqj', imp_cmp, cover)
    jb = jnp.arange(n_blk)[None, :]
    cur = (q_pos // SEL_BLOCK)[:, None]
    visible = jb <= cur
    forced = (jb == 0) | (visible & (jb > cur - N_LOCAL))
    imp = jnp.where(forced, FORCE_SCORE, jnp.where(visible, imp, -FORCE_SCORE))
    _, idx = lax.top_k(imp, min(N_SEL, n_blk))
    return idx


def sel_branch(q, k_blk, v_blk, idx, q_pos):
    b, tq, h, dh = q.shape
    n_top = idx.shape[-1]
    take = jax.vmap(jax.vmap(lambda blocks, i: blocks[i]))
    kg = take(k_blk.transpose(0, 3, 1, 2, 4), idx).reshape(b, N_KV_HEADS, tq, n_top * SEL_BLOCK, dh)
    vg = take(v_blk.transpose(0, 3, 1, 2, 4), idx).reshape(b, N_KV_HEADS, tq, n_top * SEL_BLOCK, dh)
    k_pos = (idx[..., None] * SEL_BLOCK + jnp.arange(SEL_BLOCK)).reshape(b, N_KV_HEADS, tq, n_top * SEL_BLOCK)
    mask = (k_pos <= q_pos[None, None, :, None])[:, :, None]
    qg = q.reshape(b, tq, N_KV_HEADS, GQA_GROUP, dh)
    s = jnp.einsum('bqgrd,bgqsd->bgrqs', qg, kg).astype(F32) * dh ** -0.5
    p = masked_softmax(s, mask)
    o = jnp.einsum('bgrqs,bgqsd->bqgrd', p.astype(vg.dtype), vg)
    return o.reshape(b, tq, h, dh)


def merge_branches(gates, o_cmp, o_sel, o_win, w_out):
    g = gates.astype(o_cmp.dtype)[..., None]
    o = g[:, :, 0] * o_cmp + g[:, :, 1] * o_sel + g[:, :, 2] * o_win
    return o.reshape(o.shape[0], o.shape[1], -1) @ w_out


def nsa_prompt(x, w_in, w_out, cmp_pos, cmp_w1, cmp_b1, cmp_w2):
    b, t, _ = x.shape
    pos = jnp.arange(t)
    q, kv_cmp, kv_sel, kv_win, gates = nsa_project(x, w_in, pos)
    k_c, v_c = compress_blocks(kv_cmp, cmp_pos, cmp_w1, cmp_b1, cmp_w2)
    o_cmp, imp = cmp_branch(q, k_c, v_c, pos)
    n_blk = t // SEL_BLOCK
    idx = select_blocks(imp, pos, n_blk)
    k_blk = kv_sel[:, :, 0].reshape(b, n_blk, SEL_BLOCK, N_KV_HEADS, HEAD_DIM)
    v_blk = kv_sel[:, :, 1].reshape(b, n_blk, SEL_BLOCK, N_KV_HEADS, HEAD_DIM)
    win_pad = jnp.pad(kv_win, ((0, 0), (WINDOW, 0), (0, 0), (0, 0), (0, 0)))

    def query_block(i):
        start = i * Q_BLOCK
        qb = lax.dynamic_slice_in_dim(q, start, Q_BLOCK, axis=1)
        pb = start + jnp.arange(Q_BLOCK)
        ib = lax.dynamic_slice_in_dim(idx, start, Q_BLOCK, axis=2)
        o_s = sel_branch(qb, k_blk, v_blk, ib, pb)
        wb = lax.dynamic_slice_in_dim(win_pad, start, Q_BLOCK + WINDOW, axis=1)
        kp = start - WINDOW + jnp.arange(Q_BLOCK + WINDOW)
        o_w = gqa_attend(qb, wb[:, :, 0], wb[:, :, 1], window_mask(pb, kp))
        return o_s, o_w

    o_sel, o_win = lax.map(query_block, jnp.arange(t // Q_BLOCK))
    o_sel = o_sel.transpose(1, 0, 2, 3, 4).reshape(b, t, N_HEADS, HEAD_DIM)
    o_win = o_win.transpose(1, 0, 2, 3, 4).reshape(b, t, N_HEADS, HEAD_DIM)
    y = merge_branches(gates, o_cmp, o_sel, o_win, w_out)
    w_keep = min(WINDOW, t)
    return y, kv_cmp, kv_sel, kv_win[:, t - w_keep:]


def nsa_sample(x, cache_cmp, cache_sel, win_buf, page_table, past_len, w_in, w_out, cmp_pos, cmp_w1, cmp_b1, cmp_w2):
    b, tn, _ = x.shape
    pos = past_len + jnp.arange(tn)
    q, kv_cmp, kv_sel, kv_win, gates = nsa_project(x, w_in, pos)
    total = past_len + tn
    full_cmp = jnp.concatenate([cache_cmp[page_table].reshape(b, past_len, 2, N_KV_HEADS, HEAD_DIM), kv_cmp], axis=1)
    k_c, v_c = compress_blocks(full_cmp, cmp_pos, cmp_w1, cmp_b1, cmp_w2)
    o_cmp, imp = cmp_branch(q, k_c, v_c, pos)
    n_blk = -(-total // SEL_BLOCK)
    full_sel = jnp.concatenate([cache_sel[page_table].reshape(b, past_len, 2, N_KV_HEADS, HEAD_DIM), kv_sel], axis=1)
    full_sel = jnp.pad(full_sel, ((0, 0), (0, n_blk * SEL_BLOCK - total), (0, 0), (0, 0), (0, 0)))
    idx = select_blocks(imp, pos, n_blk)
    k_blk = full_sel[:, :, 0].reshape(b, n_blk, SEL_BLOCK, N_KV_HEADS, HEAD_DIM)
    v_blk = full_sel[:, :, 1].reshape(b, n_blk, SEL_BLOCK, N_KV_HEADS, HEAD_DIM)
    o_sel = sel_branch(q, k_blk, v_blk, idx, pos)
    w_buf = win_buf.shape[1]
    win_all = jnp.concatenate([win_buf, kv_win], axis=1)
    kp = past_len - w_buf + jnp.arange(w_buf + tn)
    o_win = gqa_attend(q, win_all[:, :, 0], win_all[:, :, 1], window_mask(pos, kp))
    y = merge_branches(gates, o_cmp, o_sel, o_win, w_out)
    return y, kv_cmp, kv_sel, win_all[:, tn:]


def ret_log_decay():
    return jnp.log(1.0 - 2.0 ** (-5.0 - jnp.arange(RET_HEADS, dtype=F32)))


def ret_project(x, w_in, pos):
    b, t, _ = x.shape
    nk = RET_HEADS * RET_KDIM
    nv = RET_HEADS * RET_VDIM
    z = x @ w_in
    q = rope(z[..., :nk].reshape(b, t, RET_HEADS, RET_KDIM), pos)
    k = rope(z[..., nk:2 * nk].reshape(b, t, RET_HEADS, RET_KDIM), pos)
    v = z[..., 2 * nk:2 * nk + nv].reshape(b, t, RET_HEADS, RET_VDIM)
    g = z[..., 2 * nk + nv:]
    return q, k, v, g


def ret_chunk(q, k, v, state, log_g):
    c = q.shape[1]
    i = jnp.arange(c, dtype=F32)
    rel = i[:, None] - i[None, :]
    decay = jnp.where(rel >= 0, jnp.exp(jnp.maximum(rel, 0.0)[None] * log_g[:, None, None]), 0.0)
    qf = q.astype(F32)
    kf = k.astype(F32) * RET_KDIM ** -0.5
    vf = v.astype(F32)
    s = jnp.einsum('bihd,bjhd->bhij', qf, kf) * decay[None]
    inner = jnp.einsum('bhij,bjhe->bihe', s, vf)
    q_dec = jnp.exp((i[:, None] + 1.0) * log_g[None, :])
    cross = jnp.einsum('bihd,bhde->bihe', qf, state) * q_dec[None, :, :, None]
    k_dec = jnp.exp((c - 1.0 - i)[:, None] * log_g[None, :])
    new_state = jnp.exp(c * log_g)[None, :, None, None] * state + jnp.einsum('bjhd,bjhe->bhde', kf * k_dec[None, :, :, None], vf)
    return inner + cross, new_state


def ret_output(o, g, gn, w_out):
    b, t = o.shape[:2]
    mu = jnp.mean(o, axis=-1, keepdims=True)
    var = jnp.mean(jnp.square(o - mu), axis=-1, keepdims=True)
    on = ((o - mu) * lax.rsqrt(var + GN_EPS)).reshape(b, t, -1) * gn.astype(F32)
    gated = (jax.nn.silu(g.astype(F32)) * on).astype(g.dtype)
    return gated @ w_out


def ret_prompt(x, w_in, gn, w_out):
    b, t, _ = x.shape
    q, k, v, g = ret_project(x, w_in, jnp.arange(t))
    log_g = ret_log_decay()
    nc = t // RET_CHUNK

    def to_chunks(a):
        return a.reshape(b, nc, RET_CHUNK, a.shape[2], a.shape[3]).swapaxes(0, 1)

    def step(state, qkv):
        o, state = ret_chunk(qkv[0], qkv[1], qkv[2], state, log_g)
        return state, o

    state0 = jnp.zeros((b, RET_HEADS, RET_KDIM, RET_VDIM), F32)
    state, o = lax.scan(step, state0, (to_chunks(q), to_chunks(k), to_chunks(v)))
    o = o.swapaxes(0, 1).reshape(b, t, RET_HEADS, RET_VDIM)
    return ret_output(o, g, gn, w_out), state


def ret_sample(x, state, past_len, w_in, gn, w_out):
    tn = x.shape[1]
    q, k, v, g = ret_project(x, w_in, past_len + jnp.arange(tn))
    o, new_state = ret_chunk(q, k, v, state.astype(F32), ret_log_decay())
    return ret_output(o, g, gn, w_out), new_state


def sq_relu_mlp(x, w1, w2):
    h = jax.nn.relu(x @ w1)
    return (h * h) @ w2


def setup_inputs(seed: int = 0) -> dict:
    key = jax.random.key(seed)
    ks = jax.random.split(key, 24)
    n_pages = PAST_LEN // PAGE_SIZE
    n_used = DEC_BATCH * n_pages
    pool = n_used + n_used // 4

    def nrm(k, shape, scale):
        return jax.random.normal(k, shape, F32) * scale

    kv_row = (2, N_KV_HEADS, HEAD_DIM)
    return {
        'x_prompt': nrm(ks[0], (BATCH, SEQ, D_MODEL), 1.0),
        'x_sample': nrm(ks[1], (DEC_BATCH, DEC_SEQ, D_MODEL), 1.0),
        'cache_cmp_kv': nrm(ks[2], (N_NSA_LAYERS, pool, PAGE_SIZE) + kv_row, 1.0),
        'cache_sel_kv': nrm(ks[3], (N_NSA_LAYERS, pool, PAGE_SIZE) + kv_row, 1.0),
        'state_win_kv': nrm(ks[4], (N_NSA_LAYERS, DEC_BATCH, min(WINDOW, PAST_LEN)) + kv_row, 1.0),
        'state_ret': nrm(ks[5], (N_RET_LAYERS, DEC_BATCH, RET_HEADS, RET_KDIM, RET_VDIM), 1.0),
        'page_table': jax.random.permutation(ks[6], pool)[:n_used].reshape(DEC_BATCH, n_pages).astype(jnp.int32),
        'norm_mix': 1.0 + nrm(ks[7], (DEPTH, D_MODEL), 0.02),
        'norm_ffn': 1.0 + nrm(ks[8], (DEPTH, D_MODEL), 0.02),
        'norm_final': 1.0 + nrm(ks[9], (D_MODEL,), 0.02),
        'nsa_w_in': nrm(ks[10], (N_NSA_LAYERS, D_MODEL, NSA_PROJ), D_MODEL ** -0.5),
        'nsa_w_out': nrm(ks[11], (N_NSA_LAYERS, N_HEADS * HEAD_DIM, D_MODEL), (N_HEADS * HEAD_DIM) ** -0.5),
        'nsa_cmp_pos': nrm(ks[12], (N_NSA_LAYERS, 2, CMP_BLOCK, HEAD_DIM), 0.1),
        'nsa_cmp_w1': nrm(ks[13], (N_NSA_LAYERS, 2, CMP_BLOCK, HEAD_DIM, CMP_HIDDEN), (CMP_BLOCK * HEAD_DIM) ** -0.5),
        'nsa_cmp_b1': nrm(ks[14], (N_NSA_LAYERS, 2, CMP_HIDDEN), 0.01),
        'nsa_cmp_w2': nrm(ks[15], (N_NSA_LAYERS, 2, CMP_HIDDEN, HEAD_DIM), CMP_HIDDEN ** -0.5),
        'ret_w_in': nrm(ks[16], (N_RET_LAYERS, D_MODEL, RET_PROJ), D_MODEL ** -0.5),
        'ret_gn': 1.0 + nrm(ks[17], (N_RET_LAYERS, RET_HEADS * RET_VDIM), 0.02),
        'ret_w_out': nrm(ks[18], (N_RET_LAYERS, RET_HEADS * RET_VDIM, D_MODEL), (RET_HEADS * RET_VDIM) ** -0.5),
        'ffn_w1': nrm(ks[19], (DEPTH, D_MODEL, D_FF), D_MODEL ** -0.5),
        'ffn_w2': nrm(ks[20], (DEPTH, D_FF, D_MODEL), D_FF ** -0.5),
    }


def reference(x_prompt, x_sample, cache_cmp_kv, cache_sel_kv, state_win_kv, state_ret, page_table,
              norm_mix, norm_ffn, norm_final, nsa_w_in, nsa_w_out, nsa_cmp_pos, nsa_cmp_w1, nsa_cmp_b1,
              nsa_cmp_w2, ret_w_in, ret_gn, ret_w_out, ffn_w1, ffn_w2):
    past_len = page_table.shape[1] * cache_sel_kv.shape[2]
    hp, hs = x_prompt, x_sample
    cmp_p, cmp_s, sel_p, sel_s, win_p, win_s, ret_p, ret_s = [], [], [], [], [], [], [], []
    for layer in range(DEPTH):
        xp = rms_norm(hp, norm_mix[layer])
        xs = rms_norm(hs, norm_mix[layer])
        if layer % N_MIXERS == 0:
            a = layer // N_MIXERS
            nsa_w = (nsa_w_in[a], nsa_w_out[a], nsa_cmp_pos[a], nsa_cmp_w1[a], nsa_cmp_b1[a], nsa_cmp_w2[a])
            yp, c_p, s_p, w_p = nsa_prompt(xp, *nsa_w)
            ys, c_s, s_s, w_s = nsa_sample(xs, cache_cmp_kv[a], cache_sel_kv[a], state_win_kv[a], page_table, past_len, *nsa_w)
            cmp_p.append(c_p)
            cmp_s.append(c_s)
            sel_p.append(s_p)
            sel_s.append(s_s)
            win_p.append(w_p)
            win_s.append(w_s)
        else:
            r = layer // N_MIXERS
            yp, st_p = ret_prompt(xp, ret_w_in[r], ret_gn[r], ret_w_out[r])
            ys, st_s = ret_sample(xs, state_ret[r], past_len, ret_w_in[r], ret_gn[r], ret_w_out[r])
            ret_p.append(st_p)
            ret_s.append(st_s)
        hp = hp + yp
        hs = hs + ys
        hp = hp + sq_relu_mlp(rms_norm(hp, norm_ffn[layer]), ffn_w1[layer], ffn_w2[layer])
        hs = hs + sq_relu_mlp(rms_norm(hs, norm_ffn[layer]), ffn_w1[layer], ffn_w2[layer])
    y_prompt = rms_norm(hp, norm_final)
    y_sample = rms_norm(hs, norm_final)
    return (y_prompt, y_sample, jnp.stack(cmp_p), jnp.stack(cmp_s), jnp.stack(sel_p), jnp.stack(sel_s),
            jnp.stack(win_p), jnp.stack(win_s), jnp.stack(ret_p), jnp.stack(ret_s))
```

```python
import functools

import numpy as np
import jax
import jax.numpy as jnp
from jax import lax
from jax.experimental import pallas as pl
from jax.experimental.pallas import tpu as pltpu

F32 = jnp.float32
BF16 = jnp.bfloat16

N_HEADS = 16
HEAD_DIM = 128
N_KV_HEADS = 4
GQA_GROUP = N_HEADS // N_KV_HEADS
KV_ROW = 2 * N_KV_HEADS * HEAD_DIM
CMP_BLOCK = 32
CMP_STRIDE = 16
SEL_BLOCK = 64
N_SEL = 16
N_LOCAL = 2
WINDOW = 512
ROPE_THETA = 10000.0
RET_HEADS = 8
RET_KDIM = 256
RET_VDIM = 512
RET_CHUNK = 128
RMS_EPS = 1e-6
GN_EPS = 1e-5
NEG_INF = -1e30
FORCE_SCORE = 1e9

SUBLANES = 8
LANES = 128
VMEM_LIMIT_BYTES = 56 * 1024 * 1024
SAMPLE_PAD = 16

_NT = (((1,), (1,)), ((), ()))
_TN = (((0,), (0,)), ((), ()))


def _params(*semantics):
    return pltpu.CompilerParams(dimension_semantics=semantics, vmem_limit_bytes=VMEM_LIMIT_BYTES)


def _rmsnorm_kernel(x_ref, g_ref, o_ref):
    x = x_ref[...]
    ms = jnp.mean(x * x, axis=-1, keepdims=True)
    o_ref[...] = (x * lax.rsqrt(ms + RMS_EPS) * g_ref[...]).astype(o_ref.dtype)


def rmsnorm(x, g, out_dtype, tm):
    m, d = x.shape
    return pl.pallas_call(
        _rmsnorm_kernel,
        grid=(m // tm,),
        in_specs=[pl.BlockSpec((tm, d), lambda i: (i, 0)), pl.BlockSpec((1, d), lambda i: (0, 0))],
        out_specs=pl.BlockSpec((tm, d), lambda i: (i, 0)),
        out_shape=jax.ShapeDtypeStruct((m, d), out_dtype),
        compiler_params=_params("parallel"),
        name="rmsnorm",
    )(x, g.reshape(1, d))


def _mm_kernel(*refs, n_x, n_extra, n_out, nk, epilogue):
    x_refs = refs[:n_x]
    w_ref = refs[n_x]
    extra = refs[n_x + 1:n_x + 1 + n_extra]
    outs = refs[n_x + 1 + n_extra:n_x + 1 + n_extra + n_out]
    x = x_refs[0][...]
    for r in x_refs[1:]:
        x = x + r[...]
    part = jnp.dot(x.astype(BF16), w_ref[...].astype(BF16), preferred_element_type=F32)
    if nk == 1:
        epilogue(part, extra, outs)
        return
    acc_ref = refs[-1]
    k = pl.program_id(2)

    @pl.when(k == 0)
    def _():
        acc_ref[...] = part

    @pl.when(k > 0)
    def _():
        acc_ref[...] += part

    @pl.when(k == nk - 1)
    def _():
        epilogue(acc_ref[...], extra, outs)


def matmul(xs, w, layer, *, col_block0, n_cols, tm, tn, tk, extras, extra_specs, out_shapes, out_specs,
           epilogue, name):
    m, kdim = xs[0].shape
    nk = kdim // tk
    grid = (m // tm, pl.cdiv(n_cols, tn), nk)
    in_specs = [pl.BlockSpec((tm, tk), lambda i, j, k: (i, k)) for _ in xs]
    in_specs.append(pl.BlockSpec((None, tk, tn), lambda i, j, k: (layer, k, j + col_block0)))
    in_specs.extend(extra_specs)
    scratch = [pltpu.VMEM((tm, tn), F32)] if nk > 1 else []
    kern = functools.partial(_mm_kernel, n_x=len(xs), n_extra=len(extras), n_out=len(out_shapes), nk=nk,
                             epilogue=epilogue)
    return pl.pallas_call(
        kern,
        grid=grid,
        in_specs=in_specs,
        out_specs=out_specs,
        out_shape=out_shapes,
        scratch_shapes=scratch,
        compiler_params=_params("parallel", "parallel", "arbitrary"),
        name=name,
    )(*xs, w, *extras)


def _rope_half_lane(x, cos, sin_signed):
    return x * cos + pltpu.roll(x, HEAD_DIM // 2, 1) * sin_signed


def _ep_rope_q(part, extra, outs, *, scale):
    cos = extra[0][...]
    sin = extra[1][...]
    for h in range(part.shape[1] // HEAD_DIM):
        sl = slice(h * HEAD_DIM, (h + 1) * HEAD_DIM)
        outs[0][:, sl] = (_rope_half_lane(part[:, sl], cos, sin) * scale).astype(outs[0].dtype)


def _ep_nsa_kv(part, extra, outs):
    cos = extra[0][...]
    sin = extra[1][...]
    half = N_KV_HEADS * HEAD_DIM
    for h in range(N_KV_HEADS):
        sl = slice(h * HEAD_DIM, (h + 1) * HEAD_DIM)
        r = _rope_half_lane(part[:, sl], cos, sin)
        outs[0][:, sl] = r
        outs[1][:, sl] = r.astype(BF16)
    outs[0][:, half:] = part[:, half:]
    outs[1][:, half:] = part[:, half:].astype(BF16)


def _ep_sigmoid(part, extra, outs):
    outs[0][...] = jax.nn.sigmoid(part)


def _ep_residual(part, extra, outs):
    outs[0][...] = extra[0][...] + part


def _ep_relu2(part, extra, outs):
    h = jnp.maximum(part, 0.0)
    outs[0][...] = (h * h).astype(outs[0].dtype)


def _ep_cast(part, extra, outs):
    outs[0][...] = part.astype(outs[0].dtype)


def _ep_rope_ret(part, extra, outs, *, scale):
    cos = extra[0][...]
    sin = extra[1][...]
    for h in range(part.shape[1] // RET_KDIM):
        a = slice(h * RET_KDIM, h * RET_KDIM + LANES)
        b = slice(h * RET_KDIM + LANES, (h + 1) * RET_KDIM)
        x1 = part[:, a]
        x2 = part[:, b]
        outs[0][:, a] = ((x1 * cos - x2 * sin) * scale).astype(outs[0].dtype)
        outs[0][:, b] = ((x2 * cos + x1 * sin) * scale).astype(outs[0].dtype)


def _rope_tables(pos, half):
    inv = ROPE_THETA ** (-jnp.arange(half, dtype=F32) / half)
    ang = pos.astype(F32)[:, None] * inv[None, :]
    return jnp.cos(ang), jnp.sin(ang)


def _compress_kernel(*refs, n_pages, page_rows, paged):
    if paged:
        refs = refs[1:]
    page_refs = refs[:n_pages]
    next_ref, w1_ref, bias_ref, w2_ref, out_ref, per1_ref, stage_ref = refs[n_pages:]
    nc = page_rows // CMP_STRIDE
    n_main = n_pages * nc
    mg = n_main + SUBLANES
    k_half = CMP_STRIDE * HEAD_DIM
    for eg in range(2 * N_KV_HEADS):
        csl = slice(eg * HEAD_DIM, (eg + 1) * HEAD_DIM)
        for p in range(n_pages):
            stage_ref[eg, p * page_rows:(p + 1) * page_rows, :] = page_refs[p][:, csl]
        stage_ref[eg, n_pages * page_rows:, :] = next_ref[:, csl]
    for e in range(2):
        w1e = w1_ref[e]
        wcat = jnp.concatenate([w1e[:k_half], w1e[k_half:]], axis=1).astype(BF16)
        rows = []
        for g in range(N_KV_HEADS):
            eg = e * N_KV_HEADS + g
            rows.append(jnp.concatenate(
                [stage_ref[eg, pl.ds(s, mg, stride=CMP_STRIDE), :] for s in range(CMP_STRIDE)], axis=1))
        x = jnp.concatenate(rows, axis=0).astype(BF16)
        per = jnp.dot(x, wcat, preferred_element_type=F32)
        hid = per.shape[1] // 2
        per1_ref[...] = per[:, hid:]
        w2e = w2_ref[e].astype(BF16)
        for g in range(N_KV_HEADS):
            col = (e * N_KV_HEADS + g) * HEAD_DIM
            h = per[g * mg:g * mg + n_main, :hid] + per1_ref[pl.ds(g * mg + 1, n_main), :]
            h = jax.nn.gelu(h + bias_ref[e:e + 1, :])
            y = jnp.dot(h.astype(BF16), w2e, preferred_element_type=F32)
            out_ref[:, col:col + HEAD_DIM] = y.astype(out_ref.dtype)


def compress(kv, layer, page_table, w1, bias, w2, *, n_seq, n_chunks, page_rows, n_pages):
    paged = page_table is not None
    nc = page_rows // CMP_STRIDE
    n_main = n_pages * nc
    steps = n_chunks // n_main
    sub = page_rows // LANES
    pages_per_seq = steps * n_pages

    if paged:
        def page_map(p):
            return lambda b, s, pt: (layer, pt[b, s * n_pages + p], 0, 0)

        def next_map(b, s, pt):
            return (layer, pt[b, jnp.minimum((s + 1) * n_pages, pages_per_seq - 1)], 0, 0)

        def fixed(*idx):
            return lambda b, s, pt: idx

        out_map = lambda b, s, pt: (b, s, 0)
    else:
        def page_map(p):
            return lambda b, s: (layer, b * pages_per_seq + s * n_pages + p, 0, 0)

        def next_map(b, s):
            return (layer, b * pages_per_seq * sub + jnp.minimum((s + 1) * n_pages, pages_per_seq - 1) * sub, 0, 0)

        def fixed(*idx):
            return lambda b, s: idx

        out_map = lambda b, s: (b, s, 0)

    if paged:
        kv_next = kv
        next_rows = page_rows
    else:
        kv_next = kv.reshape(kv.shape[0], kv.shape[1] * sub, LANES, KV_ROW)
        next_rows = LANES
    in_specs = [pl.BlockSpec((None, None, page_rows, KV_ROW), page_map(p)) for p in range(n_pages)]
    in_specs += [
        pl.BlockSpec((None, None, next_rows, KV_ROW), next_map),
        pl.BlockSpec((None, 2, CMP_BLOCK * HEAD_DIM, HEAD_DIM), fixed(layer, 0, 0, 0)),
        pl.BlockSpec((2, HEAD_DIM), fixed(0, 0)),
        pl.BlockSpec((None, 2, HEAD_DIM, HEAD_DIM), fixed(layer, 0, 0, 0)),
    ]
    kern = functools.partial(_compress_kernel, n_pages=n_pages, page_rows=page_rows, paged=paged)
    grid_spec = pltpu.PrefetchScalarGridSpec(
        num_scalar_prefetch=1 if paged else 0,
        grid=(n_seq, steps),
        in_specs=in_specs,
        out_specs=pl.BlockSpec((None, n_main, KV_ROW), out_map),
        scratch_shapes=[
            pltpu.VMEM((N_KV_HEADS * (n_main + SUBLANES), HEAD_DIM), F32),
            pltpu.VMEM((2 * N_KV_HEADS, n_pages * page_rows + next_rows, HEAD_DIM), F32),
        ],
    )
    args = ([page_table] if paged else []) + [kv] * n_pages + [kv_next, w1, bias, w2]
    return pl.pallas_call(
        kern,
        grid_spec=grid_spec,
        out_shape=jax.ShapeDtypeStruct((n_seq, n_chunks, KV_ROW), BF16),
        compiler_params=_params("parallel", "arbitrary"),
        name="nsa_compress",
    )(*args)


def _group_rows(q):
    return jnp.concatenate([q[:, h * HEAD_DIM:(h + 1) * HEAD_DIM] for h in range(GQA_GROUP)], axis=0)


def _cmp_kernel(q_ref, k_ref, v_ref, cov_ref, gate_ref, o_ref, val_ref, *, tq, pos0):
    qi = pl.program_id(2)
    q4 = _group_rows(q_ref[...])
    s = lax.dot_general(q4, k_ref[...], _NT, preferred_element_type=F32)
    row = lax.broadcasted_iota(jnp.int32, s.shape, 0)
    col = lax.broadcasted_iota(jnp.int32, s.shape, 1)
    qpos = pos0 + qi * tq + jnp.bitwise_and(row, tq - 1)
    valid = col * CMP_STRIDE + (CMP_BLOCK - 1) <= qpos
    s = jnp.where(valid, s, NEG_INF)
    m = jnp.max(s, axis=1, keepdims=True)
    e = jnp.where(valid, jnp.exp(s - m), 0.0)
    l = jnp.sum(e, axis=1, keepdims=True)
    p = e / jnp.where(l > 0.0, l, 1.0)
    o4 = jnp.dot(p.astype(BF16), v_ref[...], preferred_element_type=F32)
    for h in range(GQA_GROUP):
        o_ref[:, h * HEAD_DIM:(h + 1) * HEAD_DIM] = o4[h * tq:(h + 1) * tq] * gate_ref[:, h:h + 1]
    imp = p[0:tq] + p[tq:2 * tq] + p[2 * tq:3 * tq] + p[3 * tq:4 * tq]
    hi = imp.astype(BF16)
    r1 = imp - hi.astype(F32)
    mid = r1.astype(BF16)
    lo = (r1 - mid.astype(F32)).astype(BF16)
    cov = cov_ref[...]
    score = (lax.dot_general(cov, hi, _NT, preferred_element_type=F32)
             + lax.dot_general(cov, mid, _NT, preferred_element_type=F32)
             + lax.dot_general(cov, lo, _NT, preferred_element_type=F32))
    jb = lax.broadcasted_iota(jnp.int32, score.shape, 0)
    qp = pos0 + qi * tq + lax.broadcasted_iota(jnp.int32, score.shape, 1)
    cur = jnp.right_shift(qp, SEL_BLOCK.bit_length() - 1)
    visible = jb <= cur
    forced = (jb == 0) | (visible & (jb > cur - N_LOCAL))
    val_ref[...] = jnp.where(forced, FORCE_SCORE, jnp.where(visible, score, -FORCE_SCORE))


def cmp_attention(q, kc, cover_t, gates_t, *, n_seq, seq, tq, pos0):
    nq = seq // tq
    ncp = kc.shape[1]
    nbv = cover_t.shape[0]
    gw = GQA_GROUP * HEAD_DIM
    kern = functools.partial(_cmp_kernel, tq=tq, pos0=pos0)
    return pl.pallas_call(
        kern,
        grid=(n_seq, N_KV_HEADS, nq),
        in_specs=[
            pl.BlockSpec((tq, gw), lambda b, g, i: (b * nq + i, g)),
            pl.BlockSpec((None, ncp, HEAD_DIM), lambda b, g, i: (b, 0, g)),
            pl.BlockSpec((None, ncp, HEAD_DIM), lambda b, g, i: (b, 0, N_KV_HEADS + g)),
            pl.BlockSpec((nbv, ncp), lambda b, g, i: (0, 0)),
            pl.BlockSpec((None, None, tq, GQA_GROUP), lambda b, g, i: (0, g, b * nq + i, 0)),
        ],
        out_specs=[
            pl.BlockSpec((tq, gw), lambda b, g, i: (b * nq + i, g)),
            pl.BlockSpec((None, None, nbv, tq), lambda b, g, i: (b, g, 0, i)),
        ],
        out_shape=[
            jax.ShapeDtypeStruct((n_seq * seq, N_HEADS * HEAD_DIM), F32),
            jax.ShapeDtypeStruct((n_seq, N_KV_HEADS, nbv, seq), F32),
        ],
        compiler_params=_params("parallel", "parallel", "parallel"),
        name="nsa_cmp_attention",
    )(q, kc, kc, cover_t, gates_t)


def _cover_t(n_chunks, n_blk, nbv):
    n_cmp = n_chunks - CMP_BLOCK // CMP_STRIDE + 1
    c0 = np.arange(n_chunks) * CMP_STRIDE
    j0 = np.arange(nbv) * SEL_BLOCK
    cov = (c0[None, :] < j0[:, None] + SEL_BLOCK) & (c0[None, :] + CMP_BLOCK > j0[:, None])
    cov &= (np.arange(n_chunks)[None, :] < n_cmp) & (np.arange(nbv)[:, None] < n_blk)
    return jnp.asarray(cov.astype(np.float32), dtype=BF16)


def _rank_kernel(val_ref, out_ref, *, n_blk, n_sel):
    val = val_ref[...]
    jb = lax.broadcasted_iota(jnp.int32, val.shape, 0)

    def body(i, rank):
        row = val_ref[pl.ds(i, 1), :]
        beats = (row > val) | ((row == val) & (i < jb))
        return rank + jnp.where(beats, 1, 0)

    rank = lax.fori_loop(0, n_blk, body, jnp.zeros(val.shape, jnp.int32))
    chosen = (rank < n_sel) & (val > -0.5 * FORCE_SCORE)
    out_ref[...] = jnp.where(chosen, 0.0, NEG_INF).astype(out_ref.dtype)


def select_bias(val, *, n_blk, tc):
    nbv, ncols = val.shape
    kern = functools.partial(_rank_kernel, n_blk=n_blk, n_sel=min(N_SEL, n_blk))
    return pl.pallas_call(
        kern,
        grid=(ncols // tc,),
        in_specs=[pl.BlockSpec((nbv, tc), lambda i: (0, i))],
        out_specs=pl.BlockSpec((nbv, tc), lambda i: (0, i)),
        out_shape=jax.ShapeDtypeStruct((nbv, ncols), BF16),
        compiler_params=_params("parallel"),
        name="nsa_select_rank",
    )(val)


def _flash_kernel(*refs, tq, tk, nsteps, use_bias, window, out_dtype):
    if use_bias:
        q_ref, bias_ref, k_ref, v_ref, e_ref, gate_ref, prev_ref, o_ref, qa_ref, m_ref, l_ref, acc_ref = refs
    else:
        q_ref, k_ref, v_ref, gate_ref, prev_ref, o_ref, qa_ref, m_ref, l_ref, acc_ref = refs
    qi = pl.program_id(2)
    ki = pl.program_id(3)
    if use_bias:
        kt = ki
        active = ki * tk <= qi * tq + tq - 1
    else:
        kt = qi * (tq // tk) - window // tk + ki
        active = kt >= 0

    @pl.when(ki == 0)
    def _():
        q = q_ref[...]
        for h in range(GQA_GROUP):
            qa_ref[h * tq:(h + 1) * tq, 0:HEAD_DIM] = q[:, h * HEAD_DIM:(h + 1) * HEAD_DIM]
            if use_bias:
                qa_ref[h * tq:(h + 1) * tq, HEAD_DIM:] = bias_ref[...]
        m_ref[...] = jnp.full(m_ref.shape, NEG_INF, F32)
        l_ref[...] = jnp.zeros(l_ref.shape, F32)
        acc_ref[...] = jnp.zeros(acc_ref.shape, F32)

    @pl.when(active)
    def _():
        if use_bias:
            ka = jnp.concatenate([k_ref[...], e_ref[...]], axis=1)
        else:
            ka = k_ref[...]
        s = lax.dot_general(qa_ref[...], ka, _NT, preferred_element_type=F32)
        row = lax.broadcasted_iota(jnp.int32, s.shape, 0)
        col = lax.broadcasted_iota(jnp.int32, s.shape, 1)
        d = (qi * tq + jnp.bitwise_and(row, tq - 1)) - (kt * tk + col)
        ok = d >= 0
        if window is not None:
            ok = ok & (d < window)
        s = jnp.where(ok, s, NEG_INF)
        m_old = m_ref[...]
        m_new = jnp.maximum(m_old, jnp.max(s, axis=1, keepdims=True))
        alpha = jnp.exp(m_old - m_new)
        p = jnp.exp(s - m_new)
        l_ref[...] = alpha * l_ref[...] + jnp.sum(p, axis=1, keepdims=True)
        acc_ref[...] = alpha * acc_ref[...] + jnp.dot(p.astype(BF16), v_ref[...], preferred_element_type=F32)
        m_ref[...] = m_new

    @pl.when(ki == nsteps - 1)
    def _():
        o4 = acc_ref[...] / l_ref[...]
        for h in range(GQA_GROUP):
            sl = slice(h * HEAD_DIM, (h + 1) * HEAD_DIM)
            o_ref[:, sl] = (prev_ref[:, sl] + o4[h * tq:(h + 1) * tq] * gate_ref[:, h:h + 1]).astype(out_dtype)


def flash_attention(q, kv, branch, gates_t, prev, *, n_seq, seq, tq, tk, bias=None, onehot=None, window=None,
                    out_dtype=F32):
    use_bias = bias is not None
    nq = seq // tq
    nkt = seq // tk
    gw = GQA_GROUP * HEAD_DIM
    if use_bias:
        nsteps = nkt

        def kt_of(i, k):
            return jnp.minimum(k, (i * tq + tq - 1) // tk)
    else:
        nsteps = window // tk + tq // tk

        def kt_of(i, k):
            return jnp.maximum(i * (tq // tk) - window // tk + k, 0)

    q_spec = pl.BlockSpec((tq, gw), lambda b, g, i, k: (b * nq + i, g))
    k_spec = pl.BlockSpec((None, tk, HEAD_DIM), lambda b, g, i, k: (branch, b * nkt + kt_of(i, k), g))
    v_spec = pl.BlockSpec((None, tk, HEAD_DIM), lambda b, g, i, k: (branch, b * nkt + kt_of(i, k), N_KV_HEADS + g))
    gate_spec = pl.BlockSpec((None, None, tq, GQA_GROUP), lambda b, g, i, k: (branch, g, b * nq + i, 0))
    in_specs = [q_spec]
    args = [q]
    ka_width = HEAD_DIM
    if use_bias:
        nbp = bias.shape[-1]
        ka_width += nbp
        in_specs.append(pl.BlockSpec((None, None, tq, nbp), lambda b, g, i, k: (b, g, i, 0)))
        args.append(bias)
    in_specs += [k_spec, v_spec]
    args += [kv, kv]
    if use_bias:
        in_specs.append(pl.BlockSpec((tk, nbp), lambda b, g, i, k: (kt_of(i, k), 0)))
        args.append(onehot)
    in_specs += [gate_spec, q_spec]
    args += [gates_t, prev]
    kern = functools.partial(_flash_kernel, tq=tq, tk=tk, nsteps=nsteps, use_bias=use_bias, window=window,
                             out_dtype=out_dtype)
    rows = GQA_GROUP * tq
    return pl.pallas_call(
        kern,
        grid=(n_seq, N_KV_HEADS, nq, nsteps),
        in_specs=in_specs,
        out_specs=q_spec,
        out_shape=jax.ShapeDtypeStruct((n_seq * seq, N_HEADS * HEAD_DIM), out_dtype),
        scratch_shapes=[
            pltpu.VMEM((rows, ka_width), BF16),
            pltpu.VMEM((rows, 1), F32),
            pltpu.VMEM((rows, 1), F32),
            pltpu.VMEM((rows, HEAD_DIM), F32),
        ],
        compiler_params=_params("parallel", "parallel", "parallel", "arbitrary"),
        name="nsa_sel_attention" if use_bias else "nsa_win_attention",
    )(*args)


def _block_onehot(n_keys, nbp):
    e = (np.arange(n_keys)[:, None] // SEL_BLOCK) == np.arange(nbp)[None, :]
    return jnp.asarray(e.astype(np.float32), dtype=BF16)


def _sel_sample_kernel(*refs, n_pages, nbp, past_len, nsteps, tq):
    refs = refs[1:]
    q_ref, bias_ref, new_ref = refs[:3]
    page_refs = refs[3:3 + n_pages]
    gate_ref, prev_ref, o_ref, qa_ref, m_ref, l_ref, acc_ref = refs[3 + n_pages:]
    step = pl.program_id(1)
    page_rows = page_refs[0].shape[0]
    rows = GQA_GROUP * tq
    blk_shift = SEL_BLOCK.bit_length() - 1

    def scores(g, k, key0):
        n = k.shape[0]
        r = lax.broadcasted_iota(jnp.int32, (n, nbp), 0)
        j = lax.broadcasted_iota(jnp.int32, (n, nbp), 1)
        onehot = jnp.where(jnp.right_shift(key0 + r, blk_shift) == j, 1.0, 0.0).astype(BF16)
        ka = jnp.concatenate([k, onehot], axis=1)
        return lax.dot_general(qa_ref[g], ka, _NT, preferred_element_type=F32)

    @pl.when(step == 0)
    def _():
        q = q_ref[...]
        for g in range(N_KV_HEADS):
            for h in range(GQA_GROUP):
                hh = g * GQA_GROUP + h
                qa_ref[g, h * tq:(h + 1) * tq, 0:HEAD_DIM] = q[:, hh * HEAD_DIM:(hh + 1) * HEAD_DIM]
                qa_ref[g, h * tq:(h + 1) * tq, HEAD_DIM:] = bias_ref[g]
            kn = new_ref[:, g * HEAD_DIM:(g + 1) * HEAD_DIM]
            vn = new_ref[:, (N_KV_HEADS + g) * HEAD_DIM:(N_KV_HEADS + g + 1) * HEAD_DIM]
            s = scores(g, kn, past_len)
            row = lax.broadcasted_iota(jnp.int32, s.shape, 0)
            col = lax.broadcasted_iota(jnp.int32, s.shape, 1)
            s = jnp.where(col <= jnp.bitwise_and(row, tq - 1), s, NEG_INF)
            m = jnp.max(s, axis=1, keepdims=True)
            p = jnp.exp(s - m)
            m_ref[g] = m
            l_ref[g] = jnp.sum(p, axis=1, keepdims=True)
            acc_ref[g] = jnp.dot(p.astype(BF16), vn, preferred_element_type=F32)

    for g in range(N_KV_HEADS):
        ksl = slice(g * HEAD_DIM, (g + 1) * HEAD_DIM)
        vsl = slice((N_KV_HEADS + g) * HEAD_DIM, (N_KV_HEADS + g + 1) * HEAD_DIM)
        k = jnp.concatenate([page_refs[p][:, ksl] for p in range(n_pages)], axis=0).astype(BF16)
        v = jnp.concatenate([page_refs[p][:, vsl] for p in range(n_pages)], axis=0).astype(BF16)
        s = scores(g, k, step * (n_pages * page_rows))
        m_old = m_ref[g]
        m_new = jnp.maximum(m_old, jnp.max(s, axis=1, keepdims=True))
        alpha = jnp.exp(m_old - m_new)
        p = jnp.exp(s - m_new)
        l_ref[g] = alpha * l_ref[g] + jnp.sum(p, axis=1, keepdims=True)
        acc_ref[g] = alpha * acc_ref[g] + jnp.dot(p.astype(BF16), v, preferred_element_type=F32)
        m_ref[g] = m_new

    @pl.when(step == nsteps - 1)
    def _():
        for g in range(N_KV_HEADS):
            o4 = acc_ref[g] / l_ref[g]
            for h in range(GQA_GROUP):
                hh = g * GQA_GROUP + h
                sl = slice(hh * HEAD_DIM, (hh + 1) * HEAD_DIM)
                o_ref[:, sl] = prev_ref[:, sl] + o4[h * tq:(h + 1) * tq] * gate_ref[g, :, h:h + 1]


def sel_attention_sample(q, bias, kv_new, cache, layer, page_table, gates_t, prev, *, n_pages_step, past_len):
    n_seq, pages = page_table.shape
    tq = SAMPLE_PAD
    nbp = bias.shape[-1]
    page_rows = cache.shape[2]
    nsteps = pages // n_pages_step
    dm = N_HEADS * HEAD_DIM
    row_spec = pl.BlockSpec((tq, dm), lambda b, s, pt: (b, 0))

    def page_map(p):
        return lambda b, s, pt: (layer, pt[b, s * n_pages_step + p], 0, 0)

    in_specs = [
        row_spec,
        pl.BlockSpec((None, N_KV_HEADS, tq, nbp), lambda b, s, pt: (b, 0, 0, 0)),
        pl.BlockSpec((tq, KV_ROW), lambda b, s, pt: (b, 0)),
    ]
    in_specs += [pl.BlockSpec((None, None, page_rows, KV_ROW), page_map(p)) for p in range(n_pages_step)]
    in_specs += [
        pl.BlockSpec((None, N_KV_HEADS, tq, GQA_GROUP), lambda b, s, pt: (1, 0, b, 0)),
        row_spec,
    ]
    rows = GQA_GROUP * tq
    kern = functools.partial(_sel_sample_kernel, n_pages=n_pages_step, nbp=nbp, past_len=past_len, nsteps=nsteps,
                             tq=tq)
    grid_spec = pltpu.PrefetchScalarGridSpec(
        num_scalar_prefetch=1,
        grid=(n_seq, nsteps),
        in_specs=in_specs,
        out_specs=row_spec,
        scratch_shapes=[
            pltpu.VMEM((N_KV_HEADS, rows, HEAD_DIM + nbp), BF16),
            pltpu.VMEM((N_KV_HEADS, rows, 1), F32),
            pltpu.VMEM((N_KV_HEADS, rows, 1), F32),
            pltpu.VMEM((N_KV_HEADS, rows, HEAD_DIM), F32),
        ],
    )
    return pl.pallas_call(
        kern,
        grid_spec=grid_spec,
        out_shape=jax.ShapeDtypeStruct((n_seq * tq, dm), F32),
        compiler_params=_params("parallel", "arbitrary"),
        name="nsa_sel_attention_sample",
    )(page_table, q, bias, kv_new, *([cache] * n_pages_step), gates_t, prev)


def _win_sample_kernel(q_ref, kb_ref, vb_ref, kn_ref, vn_ref, gate_ref, prev_ref, o_ref, *, tq, out_dtype):
    q4 = _group_rows(q_ref[...])
    w_buf = kb_ref.shape[0]
    sb = lax.dot_general(q4, kb_ref[...].astype(BF16), _NT, preferred_element_type=F32)
    sn = lax.dot_general(q4, kn_ref[...], _NT, preferred_element_type=F32)
    tb = jnp.bitwise_and(lax.broadcasted_iota(jnp.int32, sb.shape, 0), tq - 1)
    rb = lax.broadcasted_iota(jnp.int32, sb.shape, 1)
    sb = jnp.where(tb + w_buf - rb < WINDOW, sb, NEG_INF)
    tn = jnp.bitwise_and(lax.broadcasted_iota(jnp.int32, sn.shape, 0), tq - 1)
    rn = lax.broadcasted_iota(jnp.int32, sn.shape, 1)
    sn = jnp.where(rn <= tn, sn, NEG_INF)
    m = jnp.maximum(jnp.max(sb, axis=1, keepdims=True), jnp.max(sn, axis=1, keepdims=True))
    pb = jnp.exp(sb - m)
    pn = jnp.exp(sn - m)
    l = jnp.sum(pb, axis=1, keepdims=True) + jnp.sum(pn, axis=1, keepdims=True)
    o4 = (jnp.dot(pb.astype(BF16), vb_ref[...].astype(BF16), preferred_element_type=F32)
          + jnp.dot(pn.astype(BF16), vn_ref[...], preferred_element_type=F32)) / l
    for h in range(GQA_GROUP):
        sl = slice(h * HEAD_DIM, (h + 1) * HEAD_DIM)
        o_ref[:, sl] = (prev_ref[:, sl] + o4[h * tq:(h + 1) * tq] * gate_ref[:, h:h + 1]).astype(out_dtype)


def win_attention_sample(q, win_buf, layer, kv_new, gates_t, prev, *, out_dtype):
    n_seq, w_buf = win_buf.shape[1], win_buf.shape[2]
    tq = SAMPLE_PAD
    gw = GQA_GROUP * HEAD_DIM
    q_spec = pl.BlockSpec((tq, gw), lambda b, g: (b, g))
    kern = functools.partial(_win_sample_kernel, tq=tq, out_dtype=out_dtype)
    return pl.pallas_call(
        kern,
        grid=(n_seq, N_KV_HEADS),
        in_specs=[
            q_spec,
            pl.BlockSpec((None, None, w_buf, HEAD_DIM), lambda b, g: (layer, b, 0, g)),
            pl.BlockSpec((None, None, w_buf, HEAD_DIM), lambda b, g: (layer, b, 0, N_KV_HEADS + g)),
            pl.BlockSpec((tq, HEAD_DIM), lambda b, g: (b, g)),
            pl.BlockSpec((tq, HEAD_DIM), lambda b, g: (b, N_KV_HEADS + g)),
            pl.BlockSpec((None, None, tq, GQA_GROUP), lambda b, g: (2, g, b, 0)),
            q_spec,
        ],
        out_specs=q_spec,
        out_shape=jax.ShapeDtypeStruct((n_seq * tq, N_HEADS * HEAD_DIM), out_dtype),
        compiler_params=_params("parallel", "parallel"),
        name="nsa_win_attention_sample",
    )(q, win_buf, win_buf, kv_new, kv_new, gates_t, prev)


def _ret_kernel(q_ref, k_ref, v_ref, g_ref, gn_ref, dec_ref, qd_ref, kd_ref, gc_ref, s0_ref, o_ref, sout_ref,
                state_ref, *, n_chunks):
    c = pl.program_id(2)

    @pl.when(c == 0)
    def _():
        state_ref[...] = s0_ref[...]

    q = q_ref[...]
    kf = k_ref[...]
    v = v_ref[...]
    state = state_ref[...]
    s = lax.dot_general(q, kf.astype(BF16), _NT, preferred_element_type=F32) * dec_ref[...]
    inner = jnp.dot(s.astype(BF16), v, preferred_element_type=F32)
    cross = jnp.dot(q, state.astype(BF16), preferred_element_type=F32) * qd_ref[...]
    o = inner + cross
    kd = (kf * kd_ref[...]).astype(BF16)
    state_ref[...] = gc_ref[...] * state + lax.dot_general(kd, v, _TN, preferred_element_type=F32)
    mu = jnp.mean(o, axis=-1, keepdims=True)
    d = o - mu
    var = jnp.mean(d * d, axis=-1, keepdims=True)
    on = d * lax.rsqrt(var + GN_EPS) * gn_ref[...]
    g = g_ref[...]
    o_ref[...] = (g * jax.nn.sigmoid(g) * on).astype(o_ref.dtype)

    @pl.when(c == n_chunks - 1)
    def _():
        sout_ref[...] = state_ref[...]


def _ret_tables(chunk, n_tok):
    log_g = jnp.log(1.0 - 2.0 ** (-5.0 - jnp.arange(RET_HEADS, dtype=F32)))
    i = jnp.arange(chunk, dtype=F32)
    rel = i[:, None] - i[None, :]
    decay = jnp.where(rel >= 0, jnp.exp(jnp.maximum(rel, 0.0)[None] * log_g[:, None, None]), 0.0)
    q_dec = jnp.exp((i[None, :] + 1.0) * log_g[:, None])[..., None]
    k_dec = jnp.where(i[None, :] < n_tok, jnp.exp((n_tok - 1.0 - i)[None, :] * log_g[:, None]), 0.0)[..., None]
    g_c = jnp.exp(n_tok * log_g)[:, None, None]
    return decay, q_dec, k_dec, g_c


def retention(q, k, v, g, gn, layer, state0, *, n_seq, seq, chunk, n_tok):
    nc = seq // chunk
    decay, q_dec, k_dec, g_c = _ret_tables(chunk, n_tok)
    kern = functools.partial(_ret_kernel, n_chunks=nc)
    tok_k = pl.BlockSpec((chunk, RET_KDIM), lambda b, h, c: (b * nc + c, h))
    tok_v = pl.BlockSpec((chunk, RET_VDIM), lambda b, h, c: (b * nc + c, h))
    st_spec = pl.BlockSpec((None, None, RET_KDIM, RET_VDIM), lambda b, h, c: (b, h, 0, 0))
    return pl.pallas_call(
        kern,
        grid=(n_seq, RET_HEADS, nc),
        in_specs=[
            tok_k, tok_k, tok_v, tok_v,
            pl.BlockSpec((None, 1, RET_VDIM), lambda b, h, c: (layer, 0, h)),
            pl.BlockSpec((None, chunk, chunk), lambda b, h, c: (h, 0, 0)),
            pl.BlockSpec((None, chunk, 1), lambda b, h, c: (h, 0, 0)),
            pl.BlockSpec((None, chunk, 1), lambda b, h, c: (h, 0, 0)),
            pl.BlockSpec((None, 1, 1), lambda b, h, c: (h, 0, 0)),
            st_spec,
        ],
        out_specs=[tok_v, st_spec],
        out_shape=[
            jax.ShapeDtypeStruct((n_seq * seq, RET_HEADS * RET_VDIM), BF16),
            jax.ShapeDtypeStruct((n_seq, RET_HEADS, RET_KDIM, RET_VDIM), F32),
        ],
        scratch_shapes=[pltpu.VMEM((RET_KDIM, RET_VDIM), F32)],
        compiler_params=_params("parallel", "parallel", "arbitrary"),
        name="retention_chunk",
    )(q, k, v, g, gn, decay, q_dec, k_dec, g_c, state0)


def _tile_rows(m, pref):
    return pref if m % pref == 0 else m


def _project_nsa(xn, w_in, layer, cos, sin_signed, *, tm):
    m, d = xn.shape
    nq = N_HEADS * HEAD_DIM
    tabs_per_seq = cos.shape[0] // tm
    tab_spec = pl.BlockSpec((tm, HEAD_DIM), lambda i, j, k: (i % tabs_per_seq, 0))
    tn_q = 512
    q = matmul(
        [xn], w_in, layer, col_block0=0, n_cols=nq, tm=tm, tn=tn_q, tk=d,
        extras=[cos, sin_signed], extra_specs=[tab_spec, tab_spec],
        out_shapes=[jax.ShapeDtypeStruct((m, nq), BF16)],
        out_specs=[pl.BlockSpec((tm, tn_q), lambda i, j, k: (i, j))],
        epilogue=functools.partial(_ep_rope_q, scale=HEAD_DIM ** -0.5), name="nsa_proj_q")[0]
    kv_f32, kv_bf16 = matmul(
        [xn], w_in, layer, col_block0=nq // KV_ROW, n_cols=3 * KV_ROW, tm=tm, tn=KV_ROW, tk=d,
        extras=[cos, sin_signed], extra_specs=[tab_spec, tab_spec],
        out_shapes=[jax.ShapeDtypeStruct((3, m, KV_ROW), F32), jax.ShapeDtypeStruct((3, m, KV_ROW), BF16)],
        out_specs=[pl.BlockSpec((None, tm, KV_ROW), lambda i, j, k: (j, i, 0))] * 2,
        epilogue=_ep_nsa_kv, name="nsa_proj_kv")
    n_gate = 3 * N_HEADS
    gates = matmul(
        [xn], w_in, layer, col_block0=(nq + 3 * KV_ROW) // LANES, n_cols=n_gate, tm=tm, tn=LANES, tk=d,
        extras=[], extra_specs=[],
        out_shapes=[jax.ShapeDtypeStruct((m, LANES), F32)],
        out_specs=[pl.BlockSpec((tm, LANES), lambda i, j, k: (i, 0))],
        epilogue=_ep_sigmoid, name="nsa_proj_gate")[0]
    gates_t = gates[:, :n_gate].reshape(m, 3, N_KV_HEADS, GQA_GROUP).transpose(1, 2, 0, 3)
    return q, kv_f32, kv_bf16, gates_t


def _cmp_bias(w1, pos, b1, layer):
    k_full = CMP_BLOCK * HEAD_DIM
    pos_rows = jnp.zeros((2, SUBLANES, k_full), F32).at[:, 0].set(pos[layer].reshape(2, k_full))
    terms = []
    for e in range(2):
        t = matmul(
            [pos_rows[e]], w1.reshape(-1, k_full, HEAD_DIM), layer * 2 + e, col_block0=0, n_cols=HEAD_DIM,
            tm=SUBLANES, tn=HEAD_DIM, tk=k_full, extras=[], extra_specs=[],
            out_shapes=[jax.ShapeDtypeStruct((SUBLANES, HEAD_DIM), F32)],
            out_specs=[pl.BlockSpec((SUBLANES, HEAD_DIM), lambda i, j, k: (0, 0))],
            epilogue=_ep_cast, name="nsa_cmp_pos_term")[0]
        terms.append(t[0])
    return jnp.stack(terms) + b1[layer]


def _mlp(h, norm_g, w1, w2, layer, *, tm):
    m, d = h.shape
    d_ff = w1.shape[-1]
    xn = rmsnorm(h, norm_g, BF16, min(tm, 256))
    tn1 = 1024
    a = matmul(
        [xn], w1, layer, col_block0=0, n_cols=d_ff, tm=tm, tn=tn1, tk=d, extras=[], extra_specs=[],
        out_shapes=[jax.ShapeDtypeStruct((m, d_ff), BF16)],
        out_specs=[pl.BlockSpec((tm, tn1), lambda i, j, k: (i, j))],
        epilogue=_ep_relu2, name="mlp_up")[0]
    tn2 = 1024
    return matmul(
        [a], w2, layer, col_block0=0, n_cols=d, tm=tm, tn=tn2, tk=1024, extras=[h],
        extra_specs=[pl.BlockSpec((tm, tn2), lambda i, j, k: (i, j))],
        out_shapes=[jax.ShapeDtypeStruct((m, d), F32)],
        out_specs=[pl.BlockSpec((tm, tn2), lambda i, j, k: (i, j))],
        epilogue=_ep_residual, name="mlp_down")[0]


def _out_proj(o, w_out, layer, h, *, tm, name):
    m, d = h.shape
    tn = 512
    return matmul(
        [o], w_out, layer, col_block0=0, n_cols=d, tm=tm, tn=tn, tk=1024, extras=[h],
        extra_specs=[pl.BlockSpec((tm, tn), lambda i, j, k: (i, j))],
        out_shapes=[jax.ShapeDtypeStruct((m, d), F32)],
        out_specs=[pl.BlockSpec((tm, tn), lambda i, j, k: (i, j))],
        epilogue=_ep_residual, name=name)[0]


def _nsa_tables(pos):
    cos, sin = _rope_tables(pos, HEAD_DIM // 2)
    return jnp.concatenate([cos, cos], axis=1), jnp.concatenate([-sin, sin], axis=1)


def _nsa_prompt(h, xn, layer, w_in, w_out, cmp_w1, cmp_bias, cmp_w2, *, n_seq, seq):
    tm = 1024
    cos, sin_signed = _nsa_tables(jnp.arange(seq))
    q, kv_f32, kv_bf16, gates_t = _project_nsa(xn, w_in, layer, cos, sin_signed, tm=tm)
    n_chunks = seq // CMP_STRIDE
    page_rows = 2048
    kc = compress(kv_f32.reshape(3, (n_seq * seq) // page_rows, page_rows, KV_ROW), 0, None, cmp_w1, cmp_bias,
                  cmp_w2, n_seq=n_seq, n_chunks=n_chunks, page_rows=page_rows, n_pages=1)
    n_blk = seq // SEL_BLOCK
    cover_t = _cover_t(n_chunks, n_blk, n_blk)
    tq_cmp = 256
    o1, val = cmp_attention(q, kc, cover_t, gates_t, n_seq=n_seq, seq=seq, tq=tq_cmp, pos0=0)
    bias_t = select_bias(val.transpose(2, 0, 1, 3).reshape(n_blk, n_seq * N_KV_HEADS * seq), n_blk=n_blk, tc=512)
    nbp = pl.cdiv(n_blk, LANES) * LANES
    bias = bias_t.reshape(n_blk, n_seq, N_KV_HEADS, seq).transpose(1, 2, 3, 0)
    bias = jnp.pad(bias, ((0, 0), (0, 0), (0, 0), (0, nbp - n_blk)))
    o2 = flash_attention(q, kv_bf16, 1, gates_t, o1, n_seq=n_seq, seq=seq, tq=128, tk=512, bias=bias,
                         onehot=_block_onehot(seq, nbp))
    o3 = flash_attention(q, kv_bf16, 2, gates_t, o2, n_seq=n_seq, seq=seq, tq=128, tk=128, window=WINDOW,
                         out_dtype=BF16)
    h = _out_proj(o3, w_out, layer, h, tm=tm, name="nsa_out_proj")
    return h, kv_f32


def _nsa_sample(h, xn, layer, w_in, w_out, cmp_w1, cmp_bias, cmp_w2, cache_cmp, cache_sel, win_buf, page_table,
                *, n_seq, past_len):
    tq = SAMPLE_PAD
    m = n_seq * tq
    cos, sin_signed = _nsa_tables(jnp.tile(past_len + jnp.arange(tq), n_seq))
    q, kv_f32, kv_bf16, gates_t = _project_nsa(xn, w_in, layer, cos, sin_signed, tm=m)
    page_rows = cache_cmp.shape[2]
    n_chunks = past_len // CMP_STRIDE
    kc = compress(cache_cmp, layer, page_table, cmp_w1, cmp_bias, cmp_w2, n_seq=n_seq, n_chunks=n_chunks,
                  page_rows=page_rows, n_pages=16)
    n_blk = pl.cdiv(past_len + tq, SEL_BLOCK)
    nbv = pl.cdiv(n_blk, SUBLANES) * SUBLANES
    cover_t = _cover_t(n_chunks, n_blk, nbv)
    o1, val = cmp_attention(q, kc, cover_t, gates_t, n_seq=n_seq, seq=tq, tq=tq, pos0=past_len)
    ncols = n_seq * N_KV_HEADS * tq
    bias_t = select_bias(val.transpose(2, 0, 1, 3).reshape(nbv, ncols), n_blk=n_blk, tc=ncols)
    nbp = pl.cdiv(nbv, LANES) * LANES
    bias = bias_t.reshape(nbv, n_seq, N_KV_HEADS, tq).transpose(1, 2, 3, 0)
    bias = jnp.pad(bias, ((0, 0), (0, 0), (0, 0), (0, nbp - nbv)))
    o2 = sel_attention_sample(q, bias, kv_bf16[1], cache_sel, layer, page_table, gates_t, o1, n_pages_step=8,
                              past_len=past_len)
    o3 = win_attention_sample(q, win_buf, layer, kv_bf16[2], gates_t, o2, out_dtype=BF16)
    h = _out_proj(o3, w_out, layer, h, tm=m, name="nsa_out_proj_sample")
    return h, kv_f32


def _ret_layer(h, xn, layer, w_in, gn, w_out, state0, pos, *, n_seq, seq, chunk, n_tok, tm):
    m, d = xn.shape
    nk = RET_HEADS * RET_KDIM
    nv = RET_HEADS * RET_VDIM
    cos, sin = _rope_tables(pos, RET_KDIM // 2)
    tabs_per_seq = cos.shape[0] // tm
    tab_spec = pl.BlockSpec((tm, LANES), lambda i, j, k: (i % tabs_per_seq, 0))
    tn = 512

    def proj(col0, n_cols, dtype, epilogue, extras, extra_specs, name):
        return matmul(
            [xn], w_in, layer, col_block0=col0 // tn, n_cols=n_cols, tm=tm, tn=tn, tk=d,
            extras=extras, extra_specs=extra_specs,
            out_shapes=[jax.ShapeDtypeStruct((m, n_cols), dtype)],
            out_specs=[pl.BlockSpec((tm, tn), lambda i, j, k: (i, j))],
            epilogue=epilogue, name=name)[0]

    q = proj(0, nk, BF16, functools.partial(_ep_rope_ret, scale=1.0), [cos, sin], [tab_spec, tab_spec], "ret_proj_q")
    k = proj(nk, nk, F32, functools.partial(_ep_rope_ret, scale=RET_KDIM ** -0.5), [cos, sin],
             [tab_spec, tab_spec], "ret_proj_k")
    v = proj(2 * nk, nv, BF16, _ep_cast, [], [], "ret_proj_v")
    g = proj(2 * nk + nv, nv, F32, _ep_cast, [], [], "ret_proj_g")
    gated, state = retention(q, k, v, g, gn.reshape(gn.shape[0], 1, nv), layer, state0, n_seq=n_seq, seq=seq,
                             chunk=chunk, n_tok=n_tok)
    h = _out_proj(gated, w_out, layer, h, tm=tm, name="ret_out_proj")
    return h, state


def kernel(x_prompt, x_sample, cache_cmp_kv, cache_sel_kv, state_win_kv, state_ret, page_table, norm_mix, norm_ffn,
           norm_final, nsa_w_in, nsa_w_out, nsa_cmp_pos, nsa_cmp_w1, nsa_cmp_b1, nsa_cmp_w2, ret_w_in, ret_gn,
           ret_w_out, ffn_w1, ffn_w2):
    n_seq_p, seq_p, d = x_prompt.shape
    n_seq_s, n_tok_s, _ = x_sample.shape
    depth = norm_mix.shape[0]
    n_nsa = cache_sel_kv.shape[0]
    pool, page_rows = cache_sel_kv.shape[1], cache_sel_kv.shape[2]
    past_len = page_table.shape[1] * page_rows
    assert past_len % CMP_STRIDE == 0 and n_tok_s < CMP_STRIDE and n_tok_s <= SAMPLE_PAD
    assert state_win_kv.shape[2] == WINDOW and seq_p >= WINDOW

    hp = x_prompt.reshape(n_seq_p * seq_p, d)
    hs = jnp.pad(x_sample, ((0, 0), (0, SAMPLE_PAD - n_tok_s), (0, 0))).reshape(n_seq_s * SAMPLE_PAD, d)
    m_s = n_seq_s * SAMPLE_PAD
    cache_cmp = cache_cmp_kv.reshape(n_nsa, pool, page_rows, KV_ROW)
    cache_sel = cache_sel_kv.reshape(n_nsa, pool, page_rows, KV_ROW)
    win_buf = state_win_kv.reshape(n_nsa, n_seq_s, WINDOW, KV_ROW)
    cmp_w1 = nsa_cmp_w1.reshape(n_nsa, 2, CMP_BLOCK * HEAD_DIM, HEAD_DIM)
    kv_shape = (2, N_KV_HEADS, HEAD_DIM)

    cmp_p, cmp_s, sel_p, sel_s, win_p, win_s, ret_p, ret_s = [], [], [], [], [], [], [], []
    for layer in range(depth):
        xp = rmsnorm(hp, norm_mix[layer], BF16, 256)
        xs = rmsnorm(hs, norm_mix[layer], BF16, m_s)
        if layer % 2 == 0:
            a = layer // 2
            cmp_bias = _cmp_bias(cmp_w1, nsa_cmp_pos, nsa_cmp_b1, a)
            hp, kv_p = _nsa_prompt(hp, xp, a, nsa_w_in, nsa_w_out, cmp_w1, cmp_bias, nsa_cmp_w2, n_seq=n_seq_p,
                                   seq=seq_p)
            hs, kv_s = _nsa_sample(hs, xs, a, nsa_w_in, nsa_w_out, cmp_w1, cmp_bias, nsa_cmp_w2, cache_cmp,
                                   cache_sel, win_buf, page_table, n_seq=n_seq_s, past_len=past_len)
            kv_p = kv_p.reshape((3, n_seq_p, seq_p) + kv_shape)
            kv_s = kv_s.reshape((3, n_seq_s, SAMPLE_PAD) + kv_shape)[:, :, :n_tok_s]
            cmp_p.append(kv_p[0])
            sel_p.append(kv_p[1])
            win_p.append(kv_p[2][:, seq_p - WINDOW:])
            cmp_s.append(kv_s[0])
            sel_s.append(kv_s[1])
            win_s.append(jnp.concatenate([state_win_kv[a][:, n_tok_s:], kv_s[2]], axis=1))
        else:
            r = layer // 2
            zeros = jnp.zeros((n_seq_p, RET_HEADS, RET_KDIM, RET_VDIM), F32)
            hp, st_p = _ret_layer(hp, xp, r, ret_w_in, ret_gn, ret_w_out, zeros, jnp.arange(seq_p), n_seq=n_seq_p,
                                  seq=seq_p, chunk=RET_CHUNK, n_tok=RET_CHUNK, tm=1024)
            hs, st_s = _ret_layer(hs, xs, r, ret_w_in, ret_gn, ret_w_out, state_ret[r],
                                  jnp.tile(past_len + jnp.arange(SAMPLE_PAD), n_seq_s), n_seq=n_seq_s,
                                  seq=SAMPLE_PAD, chunk=SAMPLE_PAD,
                                  n_tok=n_tok_s, tm=m_s)
            ret_p.append(st_p)
            ret_s.append(st_s)
        hp = _mlp(hp, norm_ffn[layer], ffn_w1, ffn_w2, layer, tm=1024)
        hs = _mlp(hs, norm_ffn[layer], ffn_w1, ffn_w2, layer, tm=m_s)
    y_prompt = rmsnorm(hp, norm_final, F32, 256).reshape(n_seq_p, seq_p, d)
    y_sample = rmsnorm(hs, norm_final, F32, m_s).reshape(n_seq_s, SAMPLE_PAD, d)[:, :n_tok_s]
    return (y_prompt, y_sample, jnp.stack(cmp_p), jnp.stack(cmp_s), jnp.stack(sel_p), jnp.stack(sel_s),
            jnp.stack(win_p), jnp.stack(win_s), jnp.stack(ret_p), jnp.stack(ret_s))
```

```python
import functools

import numpy as np
import jax
import jax.numpy as jnp
from jax import lax
from jax.experimental import pallas as pl
from jax.experimental.pallas import tpu as pltpu

F32 = jnp.float32
BF16 = jnp.bfloat16

N_HEADS = 16
HEAD_DIM = 128
N_KV_HEADS = 4
GQA_GROUP = N_HEADS // N_KV_HEADS
KV_ROW = 2 * N_KV_HEADS * HEAD_DIM
CMP_BLOCK = 32
CMP_STRIDE = 16
SEL_BLOCK = 64
N_SEL = 16
N_LOCAL = 2
WINDOW = 512
ROPE_THETA = 10000.0
RET_HEADS = 8
RET_KDIM = 256
RET_VDIM = 512
RET_CHUNK = 128
RMS_EPS = 1e-6
GN_EPS = 1e-5
NEG_INF = -1e30
FORCE_SCORE = 1e9

SUBLANES = 8
LANES = 128
VMEM_LIMIT_BYTES = 56 * 1024 * 1024
SAMPLE_PAD = 16

_NT = (((1,), (1,)), ((), ()))
_TN = (((0,), (0,)), ((), ()))


def _params(*semantics):
    return pltpu.CompilerParams(dimension_semantics=semantics, vmem_limit_bytes=VMEM_LIMIT_BYTES)


def _rmsnorm_kernel(x_ref, g_ref, o_ref):
    x = x_ref[...]
    ms = jnp.mean(x * x, axis=-1, keepdims=True)
    o_ref[...] = (x * lax.rsqrt(ms + RMS_EPS) * g_ref[...]).astype(o_ref.dtype)


def rmsnorm(x, g, out_dtype, tm):
    m, d = x.shape
    return pl.pallas_call(
        _rmsnorm_kernel,
        grid=(m // tm,),
        in_specs=[pl.BlockSpec((tm, d), lambda i: (i, 0)), pl.BlockSpec((1, d), lambda i: (0, 0))],
        out_specs=pl.BlockSpec((tm, d), lambda i: (i, 0)),
        out_shape=jax.ShapeDtypeStruct((m, d), out_dtype),
        compiler_params=_params("parallel"),
        name="rmsnorm",
    )(x, g.reshape(1, d))


def _mm_kernel(*refs, n_x, n_extra, n_out, nk, epilogue):
    x_refs = refs[:n_x]
    w_ref = refs[n_x]
    extra = refs[n_x + 1:n_x + 1 + n_extra]
    outs = refs[n_x + 1 + n_extra:n_x + 1 + n_extra + n_out]
    x = x_refs[0][...]
    for r in x_refs[1:]:
        x = x + r[...]
    part = jnp.dot(x.astype(BF16), w_ref[...].astype(BF16), preferred_element_type=F32)
    if nk == 1:
        epilogue(part, extra, outs)
        return
    acc_ref = refs[-1]
    k = pl.program_id(2)

    @pl.when(k == 0)
    def _():
        acc_ref[...] = part

    @pl.when(k > 0)
    def _():
        acc_ref[...] += part

    @pl.when(k == nk - 1)
    def _():
        epilogue(acc_ref[...], extra, outs)


def matmul(xs, w, layer, *, col_block0, n_cols, tm, tn, tk, extras, extra_specs, out_shapes, out_specs,
           epilogue, name):
    m, kdim = xs[0].shape
    nk = kdim // tk
    grid = (m // tm, pl.cdiv(n_cols, tn), nk)
    in_specs = [pl.BlockSpec((tm, tk), lambda i, j, k: (i, k)) for _ in xs]
    in_specs.append(pl.BlockSpec((None, tk, tn), lambda i, j, k: (layer, k, j + col_block0)))
    in_specs.extend(extra_specs)
    scratch = [pltpu.VMEM((tm, tn), F32)] if nk > 1 else []
    kern = functools.partial(_mm_kernel, n_x=len(xs), n_extra=len(extras), n_out=len(out_shapes), nk=nk,
                             epilogue=epilogue)
    return pl.pallas_call(
        kern,
        grid=grid,
        in_specs=in_specs,
        out_specs=out_specs,
        out_shape=out_shapes,
        scratch_shapes=scratch,
        compiler_params=_params("parallel", "parallel", "arbitrary"),
        name=name,
    )(*xs, w, *extras)


def _rope_half_lane(x, cos, sin_signed):
    return x * cos + pltpu.roll(x, HEAD_DIM // 2, 1) * sin_signed


def _ep_rope_q(part, extra, outs, *, scale):
    cos = extra[0][...]
    sin = extra[1][...]
    for h in range(part.shape[1] // HEAD_DIM):
        sl = slice(h * HEAD_DIM, (h + 1) * HEAD_DIM)
        r = _rope_half_lane(part[:, sl], cos, sin) * scale
        outs[0][:, sl] = r.astype(outs[0].dtype)
        if len(outs) > 1:
            outs[1][sl, :] = r.T.astype(outs[1].dtype)


def _ep_nsa_kv(part, extra, outs, *, transposed_v):
    cos = extra[0][...]
    sin = extra[1][...]
    tm = part.shape[0]
    half = N_KV_HEADS * HEAD_DIM
    rows_per_tok = 2 * N_KV_HEADS
    for h in range(N_KV_HEADS):
        sl = slice(h * HEAD_DIM, (h + 1) * HEAD_DIM)
        vsl = slice(half + h * HEAD_DIM, half + (h + 1) * HEAD_DIM)
        k = _rope_half_lane(part[:, sl], cos, sin)
        v = part[:, vsl]
        outs[0][pl.ds(h, tm, stride=rows_per_tok), :] = k
        outs[0][pl.ds(N_KV_HEADS + h, tm, stride=rows_per_tok), :] = v
        outs[1][:, sl] = k.astype(BF16)
        if transposed_v:
            outs[2][sl, :] = v.T.astype(BF16)
        else:
            outs[1][:, vsl] = v.astype(BF16)


def _ep_sigmoid(part, extra, outs):
    outs[0][...] = jax.nn.sigmoid(part)


def _ep_residual(part, extra, outs):
    outs[0][...] = extra[0][...] + part


def _ep_relu2(part, extra, outs):
    h = jnp.maximum(part, 0.0)
    outs[0][...] = (h * h).astype(outs[0].dtype)


def _ep_cast(part, extra, outs):
    outs[0][...] = part.astype(outs[0].dtype)


def _ep_rope_ret(part, extra, outs, *, scale):
    cos = extra[0][...]
    sin = extra[1][...]
    for h in range(part.shape[1] // RET_KDIM):
        a = slice(h * RET_KDIM, h * RET_KDIM + LANES)
        b = slice(h * RET_KDIM + LANES, (h + 1) * RET_KDIM)
        x1 = part[:, a]
        x2 = part[:, b]
        outs[0][:, a] = ((x1 * cos - x2 * sin) * scale).astype(outs[0].dtype)
        outs[0][:, b] = ((x2 * cos + x1 * sin) * scale).astype(outs[0].dtype)


def _rope_tables(pos, half):
    inv = ROPE_THETA ** (-jnp.arange(half, dtype=F32) / half)
    ang = pos.astype(F32)[:, None] * inv[None, :]
    return jnp.cos(ang), jnp.sin(ang)


def _compress_kernel(*refs, n_pages, paged):
    if paged:
        refs = refs[1:]
    page_refs = refs[:n_pages]
    next_ref, w1_ref, bias_ref, w2_ref, out_ref, y_ref = refs[n_pages:]
    nc = page_refs[0].shape[0]
    rows_per_tok = 2 * N_KV_HEADS
    n_main = n_pages * nc * rows_per_tok
    blocks = [
        jnp.concatenate([page_refs[p][:, s].reshape(nc * rows_per_tok, HEAD_DIM) for s in range(CMP_STRIDE)], axis=1)
        for p in range(n_pages)]
    blocks.append(jnp.concatenate([next_ref[0, s] for s in range(CMP_STRIDE)], axis=1))
    x = jnp.concatenate(blocks, axis=0).astype(BF16)
    k_half = CMP_STRIDE * HEAD_DIM
    wcat = jnp.concatenate([w1_ref[0, :k_half], w1_ref[0, k_half:], w1_ref[1, :k_half], w1_ref[1, k_half:]],
                           axis=1).astype(BF16)
    per = jnp.dot(x, wcat, preferred_element_type=F32)
    hid = per.shape[1] // 4
    row = lax.broadcasted_iota(jnp.int32, (per.shape[0], 2 * hid), 0)
    is_v = jnp.bitwise_and(row, N_KV_HEADS) != 0
    per = jnp.where(is_v, per[:, 2 * hid:], per[:, :2 * hid])
    h = per[:n_main, :hid] + per[rows_per_tok:, hid:]
    is_v = is_v[:n_main, :hid]
    h = jax.nn.gelu(h + jnp.where(is_v, bias_ref[1:2, :], bias_ref[0:1, :]))
    w2cat = jnp.concatenate([w2_ref[0], w2_ref[1]], axis=1).astype(BF16)
    y2 = jnp.dot(h.astype(BF16), w2cat, preferred_element_type=F32)
    y_ref[...] = jnp.where(is_v, y2[:, hid:], y2[:, :hid])
    for eg in range(rows_per_tok):
        out_ref[eg] = y_ref[pl.ds(eg, n_main // rows_per_tok, stride=rows_per_tok), :].astype(out_ref.dtype)


def compress(kv, layer, page_table, w1, bias, w2, *, n_seq, n_chunks, n_pages):
    paged = page_table is not None
    nc = kv.shape[2]
    n_main = n_pages * nc
    steps = n_chunks // n_main
    pages_per_seq = steps * n_pages
    rows_per_tok = 2 * N_KV_HEADS

    def nxt(s):
        return jnp.minimum((s + 1) * n_pages, pages_per_seq - 1)

    if paged:
        def page_map(p):
            return lambda b, s, pt: (layer, pt[b, s * n_pages + p], 0, 0, 0, 0)

        def next_map(b, s, pt):
            return (layer, pt[b, nxt(s)], 0, 0, 0, 0)

        def fixed(*idx):
            return lambda b, s, pt: idx

        out_map = lambda b, s, pt: (b, 0, s, 0)
    else:
        def page_map(p):
            return lambda b, s: (layer, b * pages_per_seq + s * n_pages + p, 0, 0, 0, 0)

        def next_map(b, s):
            return (layer, b * pages_per_seq + nxt(s), 0, 0, 0, 0)

        def fixed(*idx):
            return lambda b, s: idx

        out_map = lambda b, s: (b, 0, s, 0)

    tile = (CMP_STRIDE, rows_per_tok, HEAD_DIM)
    in_specs = [pl.BlockSpec((None, None, nc) + tile, page_map(p)) for p in range(n_pages)]
    in_specs += [
        pl.BlockSpec((None, None, 1) + tile, next_map),
        pl.BlockSpec((None, 2, CMP_BLOCK * HEAD_DIM, HEAD_DIM), fixed(layer, 0, 0, 0)),
        pl.BlockSpec((2, HEAD_DIM), fixed(0, 0)),
        pl.BlockSpec((None, 2, HEAD_DIM, HEAD_DIM), fixed(layer, 0, 0, 0)),
    ]
    kern = functools.partial(_compress_kernel, n_pages=n_pages, paged=paged)
    grid_spec = pltpu.PrefetchScalarGridSpec(
        num_scalar_prefetch=1 if paged else 0,
        grid=(n_seq, steps),
        in_specs=in_specs,
        out_specs=pl.BlockSpec((None, rows_per_tok, n_main, HEAD_DIM), out_map),
        scratch_shapes=[pltpu.VMEM((n_main * rows_per_tok, HEAD_DIM), F32)],
    )
    args = ([page_table] if paged else []) + [kv] * (n_pages + 1) + [w1, bias, w2]
    return pl.pallas_call(
        kern,
        grid_spec=grid_spec,
        out_shape=jax.ShapeDtypeStruct((n_seq, rows_per_tok, n_chunks, HEAD_DIM), BF16),
        compiler_params=_params("parallel", "arbitrary"),
        name="nsa_compress",
    )(*args)


def _group_rows(q):
    return jnp.concatenate([q[:, h * HEAD_DIM:(h + 1) * HEAD_DIM] for h in range(GQA_GROUP)], axis=0)


def _cmp_kernel(q_ref, k_ref, v_ref, cov_ref, gate_ref, o_ref, val_ref, *, tq, pos0):
    qi = pl.program_id(2)
    q4 = _group_rows(q_ref[...])
    s = lax.dot_general(q4, k_ref[...], _NT, preferred_element_type=F32)
    row = lax.broadcasted_iota(jnp.int32, s.shape, 0)
    col = lax.broadcasted_iota(jnp.int32, s.shape, 1)
    qpos = pos0 + qi * tq + jnp.bitwise_and(row, tq - 1)
    valid = col * CMP_STRIDE + (CMP_BLOCK - 1) <= qpos
    s = jnp.where(valid, s, NEG_INF)
    m = jnp.max(s, axis=1, keepdims=True)
    e = jnp.where(valid, jnp.exp(s - m), 0.0)
    l = jnp.sum(e, axis=1, keepdims=True)
    p = e / jnp.where(l > 0.0, l, 1.0)
    o4 = jnp.dot(p.astype(BF16), v_ref[...], preferred_element_type=F32)
    for h in range(GQA_GROUP):
        o_ref[:, h * HEAD_DIM:(h + 1) * HEAD_DIM] = o4[h * tq:(h + 1) * tq] * gate_ref[:, h:h + 1]
    imp = p[0:tq] + p[tq:2 * tq] + p[2 * tq:3 * tq] + p[3 * tq:4 * tq]
    hi = imp.astype(BF16)
    r1 = imp - hi.astype(F32)
    mid = r1.astype(BF16)
    lo = (r1 - mid.astype(F32)).astype(BF16)
    cov = cov_ref[...]
    score = (lax.dot_general(cov, hi, _NT, preferred_element_type=F32)
             + lax.dot_general(cov, mid, _NT, preferred_element_type=F32)
             + lax.dot_general(cov, lo, _NT, preferred_element_type=F32))
    jb = lax.broadcasted_iota(jnp.int32, score.shape, 0)
    qp = pos0 + qi * tq + lax.broadcasted_iota(jnp.int32, score.shape, 1)
    cur = jnp.right_shift(qp, SEL_BLOCK.bit_length() - 1)
    visible = jb <= cur
    forced = (jb == 0) | (visible & (jb > cur - N_LOCAL))
    val_ref[...] = jnp.where(forced, FORCE_SCORE, jnp.where(visible, score, -FORCE_SCORE))


def cmp_attention(q, kc, cover_t, gates_t, *, n_seq, seq, tq, pos0):
    nq = seq // tq
    ncp = kc.shape[2]
    nbv = cover_t.shape[0]
    gw = GQA_GROUP * HEAD_DIM
    kern = functools.partial(_cmp_kernel, tq=tq, pos0=pos0)
    if tq % LANES == 0:
        val_spec = pl.BlockSpec((nbv, tq), lambda b, g, i: (0, (b * N_KV_HEADS + g) * nq + i))
        val_shape = jax.ShapeDtypeStruct((nbv, n_seq * N_KV_HEADS * seq), F32)
    else:
        val_spec = pl.BlockSpec((None, None, nbv, tq), lambda b, g, i: (b, g, 0, i))
        val_shape = jax.ShapeDtypeStruct((n_seq, N_KV_HEADS, nbv, seq), F32)
    return pl.pallas_call(
        kern,
        grid=(n_seq, N_KV_HEADS, nq),
        in_specs=[
            pl.BlockSpec((tq, gw), lambda b, g, i: (b * nq + i, g)),
            pl.BlockSpec((None, None, ncp, HEAD_DIM), lambda b, g, i: (b, g, 0, 0)),
            pl.BlockSpec((None, None, ncp, HEAD_DIM), lambda b, g, i: (b, N_KV_HEADS + g, 0, 0)),
            pl.BlockSpec((nbv, ncp), lambda b, g, i: (0, 0)),
            pl.BlockSpec((None, None, tq, GQA_GROUP), lambda b, g, i: (0, g, b * nq + i, 0)),
        ],
        out_specs=[pl.BlockSpec((tq, gw), lambda b, g, i: (b * nq + i, g)), val_spec],
        out_shape=[jax.ShapeDtypeStruct((n_seq * seq, N_HEADS * HEAD_DIM), F32), val_shape],
        compiler_params=_params("parallel", "parallel", "parallel"),
        name="nsa_cmp_attention",
    )(q, kc, kc, cover_t, gates_t)


def _cover_t(n_chunks, n_blk, nbv):
    n_cmp = n_chunks - CMP_BLOCK // CMP_STRIDE + 1
    c0 = np.arange(n_chunks) * CMP_STRIDE
    j0 = np.arange(nbv) * SEL_BLOCK
    cov = (c0[None, :] < j0[:, None] + SEL_BLOCK) & (c0[None, :] + CMP_BLOCK > j0[:, None])
    cov &= (np.arange(n_chunks)[None, :] < n_cmp) & (np.arange(nbv)[:, None] < n_blk)
    return jnp.asarray(cov.astype(np.float32), dtype=BF16)


def _rank_kernel(val_ref, out_ref, *, n_blk, n_sel, causal_seq):
    val = val_ref[...]
    tc = val.shape[1]
    jb = lax.broadcasted_iota(jnp.int32, val.shape, 0)
    if causal_seq is None:
        n_iter = n_blk
    else:
        pos0 = (pl.program_id(0) % (causal_seq // tc)) * tc
        n_iter = jnp.minimum(n_blk, (pos0 + tc - 1) // SEL_BLOCK + 1)

    def body(i, rank):
        row = val_ref[pl.ds(i, 1), :]
        beats = (row > val) | ((row == val) & (i < jb))
        return rank + jnp.where(beats, 1, 0)

    rank = lax.fori_loop(0, n_iter, body, jnp.zeros(val.shape, jnp.int32))
    chosen = (rank < n_sel) & (val > -0.5 * FORCE_SCORE)
    out_ref[...] = jnp.where(chosen, 0.0, NEG_INF).astype(out_ref.dtype)


def select_bias(val, *, n_blk, tc, causal_seq=None):
    nbv, ncols = val.shape
    kern = functools.partial(_rank_kernel, n_blk=n_blk, n_sel=min(N_SEL, n_blk), causal_seq=causal_seq)
    return pl.pallas_call(
        kern,
        grid=(ncols // tc,),
        in_specs=[pl.BlockSpec((nbv, tc), lambda i: (0, i))],
        out_specs=pl.BlockSpec((nbv, tc), lambda i: (0, i)),
        out_shape=jax.ShapeDtypeStruct((nbv, ncols), BF16),
        compiler_params=_params("parallel"),
        name="nsa_select_rank",
    )(val)


def _flash_kernel(*refs, tq, tk, nsteps, use_bias, window, out_dtype):
    if use_bias:
        qt_ref, bias_ref, k_ref, vt_ref, e_ref, gate_ref, prev_ref, o_ref, qa_ref, m_ref, acc_ref = refs
    else:
        qt_ref, k_ref, vt_ref, gate_ref, prev_ref, o_ref, qa_ref, m_ref, acc_ref = refs
    qi = pl.program_id(2)
    ki = pl.program_id(3)
    if use_bias:
        kt = ki
        last = (qi * tq + tq - 1) // tk
        active = ki <= last
        edge = ki == last
    else:
        kt = qi * (tq // tk) - window // tk + ki
        active = kt >= 0
        edge = (ki < tq // tk) | (ki >= window // tk)

    @pl.when(ki == 0)
    def _():
        for h in range(GQA_GROUP):
            qa_ref[0:HEAD_DIM, h * tq:(h + 1) * tq] = qt_ref[h * HEAD_DIM:(h + 1) * HEAD_DIM, :]
            if use_bias:
                nbv = bias_ref.shape[0]
                qa_ref[HEAD_DIM:HEAD_DIM + nbv, h * tq:(h + 1) * tq] = bias_ref[...]
                if HEAD_DIM + nbv < qa_ref.shape[0]:
                    qa_ref[HEAD_DIM + nbv:, h * tq:(h + 1) * tq] = jnp.zeros(
                        (qa_ref.shape[0] - HEAD_DIM - nbv, tq), BF16)
        m_ref[...] = jnp.full(m_ref.shape, NEG_INF, F32)
        acc_ref[...] = jnp.zeros(acc_ref.shape, F32)

    def step(masked):
        if use_bias:
            ka = jnp.concatenate([k_ref[...], e_ref[...]], axis=1)
        else:
            ka = k_ref[...]
        st = jnp.dot(ka, qa_ref[...], preferred_element_type=F32)
        if masked:
            kpos = kt * tk + lax.broadcasted_iota(jnp.int32, st.shape, 0)
            qpos = qi * tq + jnp.bitwise_and(lax.broadcasted_iota(jnp.int32, st.shape, 1), tq - 1)
            d = qpos - kpos
            ok = d >= 0
            if window is not None:
                ok = ok & (d < window)
            st = jnp.where(ok, st, NEG_INF)
        m_old = m_ref[...]
        m_new = jnp.maximum(m_old, jnp.max(st, axis=0, keepdims=True))
        alpha = jnp.exp(m_old - m_new)
        p = jnp.exp(st - m_new).astype(BF16)
        va = jnp.concatenate([vt_ref[...], jnp.ones((acc_ref.shape[0] - HEAD_DIM, tk), BF16)], axis=0)
        acc_ref[...] = alpha * acc_ref[...] + jnp.dot(va, p, preferred_element_type=F32)
        m_ref[...] = m_new

    @pl.when(active & edge)
    def _():
        step(True)

    @pl.when(active & jnp.logical_not(edge))
    def _():
        step(False)

    @pl.when(ki == nsteps - 1)
    def _():
        acc = acc_ref[...]
        ot = acc[:HEAD_DIM] / acc[HEAD_DIM:HEAD_DIM + 1]
        for h in range(GQA_GROUP):
            sl = slice(h * HEAD_DIM, (h + 1) * HEAD_DIM)
            o_h = ot[:, h * tq:(h + 1) * tq].T
            o_ref[:, sl] = (prev_ref[:, sl] + o_h * gate_ref[:, h:h + 1]).astype(out_dtype)


def flash_attention(qt, k, vt, branch, gates_t, prev, *, n_seq, seq, tq, tk, bias=None, onehot=None, window=None,
                    out_dtype=F32):
    use_bias = bias is not None
    nq = seq // tq
    nkt = seq // tk
    gw = GQA_GROUP * HEAD_DIM
    if use_bias:
        nsteps = nkt

        def kt_of(i, k):
            return jnp.minimum(k, (i * tq + tq - 1) // tk)
    else:
        nsteps = window // tk + tq // tk

        def kt_of(i, k):
            return jnp.maximum(i * (tq // tk) - window // tk + k, 0)

    o_spec = pl.BlockSpec((tq, gw), lambda b, g, i, k: (b * nq + i, g))
    qt_spec = pl.BlockSpec((gw, tq), lambda b, g, i, k: (g, b * nq + i))
    k_spec = pl.BlockSpec((None, tk, HEAD_DIM), lambda b, g, i, k: (branch, b * nkt + kt_of(i, k), g))
    vt_spec = pl.BlockSpec((None, HEAD_DIM, tk), lambda b, g, i, k: (branch, g, b * nkt + kt_of(i, k)))
    gate_spec = pl.BlockSpec((None, None, tq, GQA_GROUP), lambda b, g, i, k: (branch, g, b * nq + i, 0))
    in_specs = [qt_spec]
    args = [qt]
    qa_rows = HEAD_DIM
    if use_bias:
        nbv = bias.shape[0]
        nbp = onehot.shape[1]
        qa_rows += nbp
        in_specs.append(pl.BlockSpec((nbv, tq), lambda b, g, i, k: (0, (b * N_KV_HEADS + g) * nq + i)))
        args.append(bias)
    in_specs += [k_spec, vt_spec]
    args += [k, vt]
    if use_bias:
        in_specs.append(pl.BlockSpec((tk, nbp), lambda b, g, i, k: (kt_of(i, k), 0)))
        args.append(onehot)
    in_specs += [gate_spec, o_spec]
    args += [gates_t, prev]
    kern = functools.partial(_flash_kernel, tq=tq, tk=tk, nsteps=nsteps, use_bias=use_bias, window=window,
                             out_dtype=out_dtype)
    cols = GQA_GROUP * tq
    return pl.pallas_call(
        kern,
        grid=(n_seq, N_KV_HEADS, nq, nsteps),
        in_specs=in_specs,
        out_specs=o_spec,
        out_shape=jax.ShapeDtypeStruct((n_seq * seq, N_HEADS * HEAD_DIM), out_dtype),
        scratch_shapes=[
            pltpu.VMEM((qa_rows, cols), BF16),
            pltpu.VMEM((1, cols), F32),
            pltpu.VMEM((HEAD_DIM + 2 * SUBLANES, cols), F32),
        ],
        compiler_params=_params("parallel", "parallel", "parallel", "arbitrary"),
        name="nsa_sel_attention" if use_bias else "nsa_win_attention",
    )(*args)


def _block_onehot(n_keys, nbp):
    e = (np.arange(n_keys)[:, None] // SEL_BLOCK) == np.arange(nbp)[None, :]
    return jnp.asarray(e.astype(np.float32), dtype=BF16)


def _sel_sample_kernel(*refs, n_pages, nbp, past_len, nsteps, tq):
    refs = refs[1:]
    q_ref, bias_ref, new_ref = refs[:3]
    page_refs = refs[3:3 + n_pages]
    gate_ref, prev_ref, o_ref, qa_ref, m_ref, l_ref, acc_ref = refs[3 + n_pages:]
    step = pl.program_id(1)
    page_rows = page_refs[0].shape[0]
    rows = GQA_GROUP * tq
    blk_shift = SEL_BLOCK.bit_length() - 1

    def scores(g, k, key0):
        n = k.shape[0]
        r = lax.broadcasted_iota(jnp.int32, (n, nbp), 0)
        j = lax.broadcasted_iota(jnp.int32, (n, nbp), 1)
        onehot = jnp.where(jnp.right_shift(key0 + r, blk_shift) == j, 1.0, 0.0).astype(BF16)
        ka = jnp.concatenate([k, onehot], axis=1)
        return lax.dot_general(qa_ref[g], ka, _NT, preferred_element_type=F32)

    @pl.when(step == 0)
    def _():
        q = q_ref[...]
        for g in range(N_KV_HEADS):
            for h in range(GQA_GROUP):
                hh = g * GQA_GROUP + h
                qa_ref[g, h * tq:(h + 1) * tq, 0:HEAD_DIM] = q[:, hh * HEAD_DIM:(hh + 1) * HEAD_DIM]
                qa_ref[g, h * tq:(h + 1) * tq, HEAD_DIM:] = bias_ref[g]
            kn = new_ref[:, g * HEAD_DIM:(g + 1) * HEAD_DIM]
            vn = new_ref[:, (N_KV_HEADS + g) * HEAD_DIM:(N_KV_HEADS + g + 1) * HEAD_DIM]
            s = scores(g, kn, past_len)
            row = lax.broadcasted_iota(jnp.int32, s.shape, 0)
            col = lax.broadcasted_iota(jnp.int32, s.shape, 1)
            s = jnp.where(col <= jnp.bitwise_and(row, tq - 1), s, NEG_INF)
            m = jnp.max(s, axis=1, keepdims=True)
            p = jnp.exp(s - m)
            m_ref[g] = m
            l_ref[g] = jnp.sum(p, axis=1, keepdims=True)
            acc_ref[g] = jnp.dot(p.astype(BF16), vn, preferred_element_type=F32)

    rows_per_tok = 2 * N_KV_HEADS
    page_tokens = page_rows // rows_per_tok
    for g in range(N_KV_HEADS):
        k = jnp.concatenate([page_refs[p][pl.ds(g, page_tokens, stride=rows_per_tok), :] for p in range(n_pages)],
                            axis=0).astype(BF16)
        v = jnp.concatenate([page_refs[p][pl.ds(N_KV_HEADS + g, page_tokens, stride=rows_per_tok), :]
                             for p in range(n_pages)], axis=0).astype(BF16)
        s = scores(g, k, step * (n_pages * page_tokens))
        m_old = m_ref[g]
        m_new = jnp.maximum(m_old, jnp.max(s, axis=1, keepdims=True))
        alpha = jnp.exp(m_old - m_new)
        p = jnp.exp(s - m_new)
        l_ref[g] = alpha * l_ref[g] + jnp.sum(p, axis=1, keepdims=True)
        acc_ref[g] = alpha * acc_ref[g] + jnp.dot(p.astype(BF16), v, preferred_element_type=F32)
        m_ref[g] = m_new

    @pl.when(step == nsteps - 1)
    def _():
        for g in range(N_KV_HEADS):
            o4 = acc_ref[g] / l_ref[g]
            for h in range(GQA_GROUP):
                hh = g * GQA_GROUP + h
                sl = slice(hh * HEAD_DIM, (hh + 1) * HEAD_DIM)
                o_ref[:, sl] = prev_ref[:, sl] + o4[h * tq:(h + 1) * tq] * gate_ref[g, :, h:h + 1]


def sel_attention_sample(q, bias, kv_new, cache, layer, page_table, gates_t, prev, *, n_pages_step, past_len):
    n_seq, pages = page_table.shape
    tq = SAMPLE_PAD
    nbp = bias.shape[-1]
    page_rows = cache.shape[2]
    nsteps = pages // n_pages_step
    dm = N_HEADS * HEAD_DIM
    row_spec = pl.BlockSpec((tq, dm), lambda b, s, pt: (b, 0))

    def page_map(p):
        return lambda b, s, pt: (layer, pt[b, s * n_pages_step + p], 0, 0)

    in_specs = [
        row_spec,
        pl.BlockSpec((None, N_KV_HEADS, tq, nbp), lambda b, s, pt: (b, 0, 0, 0)),
        pl.BlockSpec((tq, KV_ROW), lambda b, s, pt: (b, 0)),
    ]
    in_specs += [pl.BlockSpec((None, None, page_rows, HEAD_DIM), page_map(p)) for p in range(n_pages_step)]
    in_specs += [
        pl.BlockSpec((None, N_KV_HEADS, tq, GQA_GROUP), lambda b, s, pt: (1, 0, b, 0)),
        row_spec,
    ]
    rows = GQA_GROUP * tq
    kern = functools.partial(_sel_sample_kernel, n_pages=n_pages_step, nbp=nbp, past_len=past_len, nsteps=nsteps,
                             tq=tq)
    grid_spec = pltpu.PrefetchScalarGridSpec(
        num_scalar_prefetch=1,
        grid=(n_seq, nsteps),
        in_specs=in_specs,
        out_specs=row_spec,
        scratch_shapes=[
            pltpu.VMEM((N_KV_HEADS, rows, HEAD_DIM + nbp), BF16),
            pltpu.VMEM((N_KV_HEADS, rows, 1), F32),
            pltpu.VMEM((N_KV_HEADS, rows, 1), F32),
            pltpu.VMEM((N_KV_HEADS, rows, HEAD_DIM), F32),
        ],
    )
    return pl.pallas_call(
        kern,
        grid_spec=grid_spec,
        out_shape=jax.ShapeDtypeStruct((n_seq * tq, dm), F32),
        compiler_params=_params("parallel", "arbitrary"),
        name="nsa_sel_attention_sample",
    )(page_table, q, bias, kv_new, *([cache] * n_pages_step), gates_t, prev)


def _win_sample_kernel(q_ref, buf_ref, new_ref, gate_ref, prev_ref, o_ref, *, tq, out_dtype):
    rows_per_tok = 2 * N_KV_HEADS
    w_buf = buf_ref.shape[0] // rows_per_tok
    gw = GQA_GROUP * HEAD_DIM
    for g in range(N_KV_HEADS):
        q4 = _group_rows(q_ref[:, g * gw:(g + 1) * gw])
        kb = buf_ref[pl.ds(g, w_buf, stride=rows_per_tok), :].astype(BF16)
        vb = buf_ref[pl.ds(N_KV_HEADS + g, w_buf, stride=rows_per_tok), :].astype(BF16)
        kn = new_ref[:, g * HEAD_DIM:(g + 1) * HEAD_DIM]
        vn = new_ref[:, (N_KV_HEADS + g) * HEAD_DIM:(N_KV_HEADS + g + 1) * HEAD_DIM]
        sb = lax.dot_general(q4, kb, _NT, preferred_element_type=F32)
        sn = lax.dot_general(q4, kn, _NT, preferred_element_type=F32)
        tb = jnp.bitwise_and(lax.broadcasted_iota(jnp.int32, sb.shape, 0), tq - 1)
        rb = lax.broadcasted_iota(jnp.int32, sb.shape, 1)
        sb = jnp.where(tb + w_buf - rb < WINDOW, sb, NEG_INF)
        tn = jnp.bitwise_and(lax.broadcasted_iota(jnp.int32, sn.shape, 0), tq - 1)
        rn = lax.broadcasted_iota(jnp.int32, sn.shape, 1)
        sn = jnp.where(rn <= tn, sn, NEG_INF)
        m = jnp.maximum(jnp.max(sb, axis=1, keepdims=True), jnp.max(sn, axis=1, keepdims=True))
        pb = jnp.exp(sb - m)
        pn = jnp.exp(sn - m)
        l = jnp.sum(pb, axis=1, keepdims=True) + jnp.sum(pn, axis=1, keepdims=True)
        o4 = (jnp.dot(pb.astype(BF16), vb, preferred_element_type=F32)
              + jnp.dot(pn.astype(BF16), vn, preferred_element_type=F32)) / l
        for h in range(GQA_GROUP):
            hh = g * GQA_GROUP + h
            sl = slice(hh * HEAD_DIM, (hh + 1) * HEAD_DIM)
            o_ref[:, sl] = (prev_ref[:, sl] + o4[h * tq:(h + 1) * tq] * gate_ref[g, :, h:h + 1]).astype(out_dtype)


def win_attention_sample(q, win_buf, layer, kv_new, gates_t, prev, *, out_dtype):
    n_seq, buf_rows = win_buf.shape[1], win_buf.shape[2]
    tq = SAMPLE_PAD
    q_spec = pl.BlockSpec((tq, N_HEADS * HEAD_DIM), lambda b: (b, 0))
    kern = functools.partial(_win_sample_kernel, tq=tq, out_dtype=out_dtype)
    return pl.pallas_call(
        kern,
        grid=(n_seq,),
        in_specs=[
            q_spec,
            pl.BlockSpec((None, None, buf_rows, HEAD_DIM), lambda b: (layer, b, 0, 0)),
            pl.BlockSpec((tq, KV_ROW), lambda b: (b, 0)),
            pl.BlockSpec((None, N_KV_HEADS, tq, GQA_GROUP), lambda b: (2, 0, b, 0)),
            q_spec,
        ],
        out_specs=q_spec,
        out_shape=jax.ShapeDtypeStruct((n_seq * tq, N_HEADS * HEAD_DIM), out_dtype),
        compiler_params=_params("parallel"),
        name="nsa_win_attention_sample",
    )(q, win_buf, kv_new, gates_t, prev)


def _ret_kernel(q_ref, k_ref, v_ref, g_ref, gn_ref, dec_ref, qd_ref, kd_ref, gc_ref, s0_ref, o_ref, sout_ref,
                state_ref, *, n_chunks):
    c = pl.program_id(2)

    @pl.when(c == 0)
    def _():
        state_ref[...] = s0_ref[...]

    q = q_ref[...]
    kf = k_ref[...]
    v = v_ref[...]
    state = state_ref[...]
    s = lax.dot_general(q, kf.astype(BF16), _NT, preferred_element_type=F32) * dec_ref[...]
    inner = jnp.dot(s.astype(BF16), v, preferred_element_type=F32)
    cross = jnp.dot(q, state.astype(BF16), preferred_element_type=F32) * qd_ref[...]
    o = inner + cross
    kd = (kf * kd_ref[...]).astype(BF16)
    state_ref[...] = gc_ref[...] * state + lax.dot_general(kd, v, _TN, preferred_element_type=F32)
    mu = jnp.mean(o, axis=-1, keepdims=True)
    d = o - mu
    var = jnp.mean(d * d, axis=-1, keepdims=True)
    on = d * lax.rsqrt(var + GN_EPS) * gn_ref[...]
    g = g_ref[...]
    o_ref[...] = (g * jax.nn.sigmoid(g) * on).astype(o_ref.dtype)

    @pl.when(c == n_chunks - 1)
    def _():
        sout_ref[...] = state_ref[...]


def _ret_tables(chunk, n_tok):
    log_g = jnp.log(1.0 - 2.0 ** (-5.0 - jnp.arange(RET_HEADS, dtype=F32)))
    i = jnp.arange(chunk, dtype=F32)
    rel = i[:, None] - i[None, :]
    decay = jnp.where(rel >= 0, jnp.exp(jnp.maximum(rel, 0.0)[None] * log_g[:, None, None]), 0.0)
    q_dec = jnp.exp((i[None, :] + 1.0) * log_g[:, None])[..., None]
    k_dec = jnp.where(i[None, :] < n_tok, jnp.exp((n_tok - 1.0 - i)[None, :] * log_g[:, None]), 0.0)[..., None]
    g_c = jnp.exp(n_tok * log_g)[:, None, None]
    return decay, q_dec, k_dec, g_c


def retention(q, k, v, g, gn, layer, state0, *, n_seq, seq, chunk, n_tok):
    nc = seq // chunk
    decay, q_dec, k_dec, g_c = _ret_tables(chunk, n_tok)
    kern = functools.partial(_ret_kernel, n_chunks=nc)
    tok_k = pl.BlockSpec((chunk, RET_KDIM), lambda b, h, c: (b * nc + c, h))
    tok_v = pl.BlockSpec((chunk, RET_VDIM), lambda b, h, c: (b * nc + c, h))
    st_spec = pl.BlockSpec((None, None, RET_KDIM, RET_VDIM), lambda b, h, c: (b, h, 0, 0))
    return pl.pallas_call(
        kern,
        grid=(n_seq, RET_HEADS, nc),
        in_specs=[
            tok_k, tok_k, tok_v, tok_v,
            pl.BlockSpec((None, 1, RET_VDIM), lambda b, h, c: (layer, 0, h)),
            pl.BlockSpec((None, chunk, chunk), lambda b, h, c: (h, 0, 0)),
            pl.BlockSpec((None, chunk, 1), lambda b, h, c: (h, 0, 0)),
            pl.BlockSpec((None, chunk, 1), lambda b, h, c: (h, 0, 0)),
            pl.BlockSpec((None, 1, 1), lambda b, h, c: (h, 0, 0)),
            st_spec,
        ],
        out_specs=[tok_v, st_spec],
        out_shape=[
            jax.ShapeDtypeStruct((n_seq * seq, RET_HEADS * RET_VDIM), BF16),
            jax.ShapeDtypeStruct((n_seq, RET_HEADS, RET_KDIM, RET_VDIM), F32),
        ],
        scratch_shapes=[pltpu.VMEM((RET_KDIM, RET_VDIM), F32)],
        compiler_params=_params("parallel", "parallel", "arbitrary"),
        name="retention_chunk",
    )(q, k, v, g, gn, decay, q_dec, k_dec, g_c, state0)


def _tile_rows(m, pref):
    return pref if m % pref == 0 else m


def _project_nsa(xn, w_in, layer, cos, sin_signed, *, tm, transposed):
    m, d = xn.shape
    nq = N_HEADS * HEAD_DIM
    half = N_KV_HEADS * HEAD_DIM
    rows_per_tok = 2 * N_KV_HEADS
    tabs_per_seq = cos.shape[0] // tm
    tab_spec = pl.BlockSpec((tm, HEAD_DIM), lambda i, j, k: (i % tabs_per_seq, 0))
    tn_q = 512
    q_shapes = [jax.ShapeDtypeStruct((m, nq), BF16)]
    q_specs = [pl.BlockSpec((tm, tn_q), lambda i, j, k: (i, j))]
    kv_shapes = [jax.ShapeDtypeStruct((3, m * rows_per_tok, HEAD_DIM), F32)]
    kv_specs = [pl.BlockSpec((None, tm * rows_per_tok, HEAD_DIM), lambda i, j, k: (j, i, 0))]
    if transposed:
        q_shapes.append(jax.ShapeDtypeStruct((nq, m), BF16))
        q_specs.append(pl.BlockSpec((tn_q, tm), lambda i, j, k: (j, i)))
        kv_shapes += [jax.ShapeDtypeStruct((3, m, half), BF16), jax.ShapeDtypeStruct((3, half, m), BF16)]
        kv_specs += [pl.BlockSpec((None, tm, half), lambda i, j, k: (j, i, 0)),
                     pl.BlockSpec((None, half, tm), lambda i, j, k: (j, 0, i))]
    else:
        kv_shapes.append(jax.ShapeDtypeStruct((3, m, KV_ROW), BF16))
        kv_specs.append(pl.BlockSpec((None, tm, KV_ROW), lambda i, j, k: (j, i, 0)))
    q = matmul(
        [xn], w_in, layer, col_block0=0, n_cols=nq, tm=tm, tn=tn_q, tk=d,
        extras=[cos, sin_signed], extra_specs=[tab_spec, tab_spec], out_shapes=q_shapes, out_specs=q_specs,
        epilogue=functools.partial(_ep_rope_q, scale=HEAD_DIM ** -0.5), name="nsa_proj_q")
    kv = matmul(
        [xn], w_in, layer, col_block0=nq // KV_ROW, n_cols=3 * KV_ROW, tm=tm, tn=KV_ROW, tk=d,
        extras=[cos, sin_signed], extra_specs=[tab_spec, tab_spec], out_shapes=kv_shapes, out_specs=kv_specs,
        epilogue=functools.partial(_ep_nsa_kv, transposed_v=transposed), name="nsa_proj_kv")
    n_gate = 3 * N_HEADS
    gates = matmul(
        [xn], w_in, layer, col_block0=(nq + 3 * KV_ROW) // LANES, n_cols=n_gate, tm=tm, tn=LANES, tk=d,
        extras=[], extra_specs=[],
        out_shapes=[jax.ShapeDtypeStruct((m, LANES), F32)],
        out_specs=[pl.BlockSpec((tm, LANES), lambda i, j, k: (i, 0))],
        epilogue=_ep_sigmoid, name="nsa_proj_gate")[0]
    gates_t = gates[:, :n_gate].reshape(m, 3, N_KV_HEADS, GQA_GROUP).transpose(1, 2, 0, 3)
    return q, kv, gates_t


def _cmp_bias(w1, pos, b1, layer):
    k_full = CMP_BLOCK * HEAD_DIM
    pos_rows = jnp.zeros((2, SUBLANES, k_full), F32).at[:, 0].set(pos[layer].reshape(2, k_full))
    terms = []
    for e in range(2):
        t = matmul(
            [pos_rows[e]], w1.reshape(-1, k_full, HEAD_DIM), layer * 2 + e, col_block0=0, n_cols=HEAD_DIM,
            tm=SUBLANES, tn=HEAD_DIM, tk=k_full, extras=[], extra_specs=[],
            out_shapes=[jax.ShapeDtypeStruct((SUBLANES, HEAD_DIM), F32)],
            out_specs=[pl.BlockSpec((SUBLANES, HEAD_DIM), lambda i, j, k: (0, 0))],
            epilogue=_ep_cast, name="nsa_cmp_pos_term")[0]
        terms.append(t[0])
    return jnp.stack(terms) + b1[layer]


def _mlp(h, norm_g, w1, w2, layer, *, tm):
    m, d = h.shape
    d_ff = w1.shape[-1]
    xn = rmsnorm(h, norm_g, BF16, min(tm, 256))
    tn1 = 1024
    a = matmul(
        [xn], w1, layer, col_block0=0, n_cols=d_ff, tm=tm, tn=tn1, tk=d, extras=[], extra_specs=[],
        out_shapes=[jax.ShapeDtypeStruct((m, d_ff), BF16)],
        out_specs=[pl.BlockSpec((tm, tn1), lambda i, j, k: (i, j))],
        epilogue=_ep_relu2, name="mlp_up")[0]
    tn2 = 1024
    return matmul(
        [a], w2, layer, col_block0=0, n_cols=d, tm=tm, tn=tn2, tk=1024, extras=[h],
        extra_specs=[pl.BlockSpec((tm, tn2), lambda i, j, k: (i, j))],
        out_shapes=[jax.ShapeDtypeStruct((m, d), F32)],
        out_specs=[pl.BlockSpec((tm, tn2), lambda i, j, k: (i, j))],
        epilogue=_ep_residual, name="mlp_down")[0]


def _out_proj(o, w_out, layer, h, *, tm, name):
    m, d = h.shape
    tn = 512
    return matmul(
        [o], w_out, layer, col_block0=0, n_cols=d, tm=tm, tn=tn, tk=1024, extras=[h],
        extra_specs=[pl.BlockSpec((tm, tn), lambda i, j, k: (i, j))],
        out_shapes=[jax.ShapeDtypeStruct((m, d), F32)],
        out_specs=[pl.BlockSpec((tm, tn), lambda i, j, k: (i, j))],
        epilogue=_ep_residual, name=name)[0]


def _nsa_tables(pos):
    cos, sin = _rope_tables(pos, HEAD_DIM // 2)
    return jnp.concatenate([cos, cos], axis=1), jnp.concatenate([-sin, sin], axis=1)


def _nsa_prompt(h, xn, layer, w_in, w_out, cmp_w1, cmp_bias, cmp_w2, *, n_seq, seq):
    tm = 1024
    cos, sin_signed = _nsa_tables(jnp.arange(seq))
    (q, qt), (kv_rows, k_bf, vt_bf), gates_t = _project_nsa(xn, w_in, layer, cos, sin_signed, tm=tm, transposed=True)
    n_chunks = seq // CMP_STRIDE
    chunks_per_step = 128
    kv_chunks = kv_rows.reshape(3, (n_seq * n_chunks) // chunks_per_step, chunks_per_step, CMP_STRIDE,
                                2 * N_KV_HEADS, HEAD_DIM)
    kc = compress(kv_chunks, 0, None, cmp_w1, cmp_bias, cmp_w2, n_seq=n_seq, n_chunks=n_chunks, n_pages=1)
    n_blk = seq // SEL_BLOCK
    cover_t = _cover_t(n_chunks, n_blk, n_blk)
    o1, val = cmp_attention(q, kc, cover_t, gates_t, n_seq=n_seq, seq=seq, tq=256, pos0=0)
    bias_t = select_bias(val, n_blk=n_blk, tc=512, causal_seq=seq)
    nbp = pl.cdiv(n_blk, LANES) * LANES
    o2 = flash_attention(qt, k_bf, vt_bf, 1, gates_t, o1, n_seq=n_seq, seq=seq, tq=256, tk=512, bias=bias_t,
                         onehot=_block_onehot(seq, nbp))
    o3 = flash_attention(qt, k_bf, vt_bf, 2, gates_t, o2, n_seq=n_seq, seq=seq, tq=256, tk=256, window=WINDOW,
                         out_dtype=BF16)
    h = _out_proj(o3, w_out, layer, h, tm=tm, name="nsa_out_proj")
    return h, kv_rows


def _nsa_sample(h, xn, layer, w_in, w_out, cmp_w1, cmp_bias, cmp_w2, cache_cmp, cache_sel, win_buf, page_table,
                *, n_seq, past_len):
    tq = SAMPLE_PAD
    m = n_seq * tq
    cos, sin_signed = _nsa_tables(jnp.tile(past_len + jnp.arange(tq), n_seq))
    (q,), (kv_rows, kv_bf16), gates_t = _project_nsa(xn, w_in, layer, cos, sin_signed, tm=m, transposed=False)
    n_chunks = past_len // CMP_STRIDE
    kc = compress(cache_cmp, layer, page_table, cmp_w1, cmp_bias, cmp_w2, n_seq=n_seq, n_chunks=n_chunks,
                  n_pages=16)
    n_blk = pl.cdiv(past_len + tq, SEL_BLOCK)
    nbv = pl.cdiv(n_blk, SUBLANES) * SUBLANES
    cover_t = _cover_t(n_chunks, n_blk, nbv)
    o1, val = cmp_attention(q, kc, cover_t, gates_t, n_seq=n_seq, seq=tq, tq=tq, pos0=past_len)
    ncols = n_seq * N_KV_HEADS * tq
    bias_t = select_bias(val.transpose(2, 0, 1, 3).reshape(nbv, ncols), n_blk=n_blk, tc=ncols)
    nbp = pl.cdiv(nbv, LANES) * LANES
    bias = bias_t.reshape(nbv, n_seq, N_KV_HEADS, tq).transpose(1, 2, 3, 0)
    bias = jnp.pad(bias, ((0, 0), (0, 0), (0, 0), (0, nbp - nbv)))
    o2 = sel_attention_sample(q, bias, kv_bf16[1], cache_sel, layer, page_table, gates_t, o1, n_pages_step=8,
                              past_len=past_len)
    o3 = win_attention_sample(q, win_buf, layer, kv_bf16[2], gates_t, o2, out_dtype=BF16)
    h = _out_proj(o3, w_out, layer, h, tm=m, name="nsa_out_proj_sample")
    return h, kv_rows


def _ret_layer(h, xn, layer, w_in, gn, w_out, state0, pos, *, n_seq, seq, chunk, n_tok, tm):
    m, d = xn.shape
    nk = RET_HEADS * RET_KDIM
    nv = RET_HEADS * RET_VDIM
    cos, sin = _rope_tables(pos, RET_KDIM // 2)
    tabs_per_seq = cos.shape[0] // tm
    tab_spec = pl.BlockSpec((tm, LANES), lambda i, j, k: (i % tabs_per_seq, 0))
    tn = 512

    def proj(col0, n_cols, dtype, epilogue, extras, extra_specs, name):
        return matmul(
            [xn], w_in, layer, col_block0=col0 // tn, n_cols=n_cols, tm=tm, tn=tn, tk=d,
            extras=extras, extra_specs=extra_specs,
            out_shapes=[jax.ShapeDtypeStruct((m, n_cols), dtype)],
            out_specs=[pl.BlockSpec((tm, tn), lambda i, j, k: (i, j))],
            epilogue=epilogue, name=name)[0]

    q = proj(0, nk, BF16, functools.partial(_ep_rope_ret, scale=1.0), [cos, sin], [tab_spec, tab_spec], "ret_proj_q")
    k = proj(nk, nk, F32, functools.partial(_ep_rope_ret, scale=RET_KDIM ** -0.5), [cos, sin],
             [tab_spec, tab_spec], "ret_proj_k")
    v = proj(2 * nk, nv, BF16, _ep_cast, [], [], "ret_proj_v")
    g = proj(2 * nk + nv, nv, F32, _ep_cast, [], [], "ret_proj_g")
    gated, state = retention(q, k, v, g, gn.reshape(gn.shape[0], 1, nv), layer, state0, n_seq=n_seq, seq=seq,
                             chunk=chunk, n_tok=n_tok)
    h = _out_proj(gated, w_out, layer, h, tm=tm, name="ret_out_proj")
    return h, state


def kernel(x_prompt, x_sample, cache_cmp_kv, cache_sel_kv, state_win_kv, state_ret, page_table, norm_mix, norm_ffn,
           norm_final, nsa_w_in, nsa_w_out, nsa_cmp_pos, nsa_cmp_w1, nsa_cmp_b1, nsa_cmp_w2, ret_w_in, ret_gn,
           ret_w_out, ffn_w1, ffn_w2):
    n_seq_p, seq_p, d = x_prompt.shape
    n_seq_s, n_tok_s, _ = x_sample.shape
    depth = norm_mix.shape[0]
    n_nsa = cache_sel_kv.shape[0]
    pool, page_rows = cache_sel_kv.shape[1], cache_sel_kv.shape[2]
    past_len = page_table.shape[1] * page_rows
    assert past_len % CMP_STRIDE == 0 and n_tok_s < CMP_STRIDE and n_tok_s <= SAMPLE_PAD
    assert state_win_kv.shape[2] == WINDOW and seq_p >= WINDOW

    hp = x_prompt.reshape(n_seq_p * seq_p, d)
    hs = jnp.pad(x_sample, ((0, 0), (0, SAMPLE_PAD - n_tok_s), (0, 0))).reshape(n_seq_s * SAMPLE_PAD, d)
    m_s = n_seq_s * SAMPLE_PAD
    rows_per_tok = 2 * N_KV_HEADS
    cache_cmp = cache_cmp_kv.reshape(n_nsa, pool, page_rows // CMP_STRIDE, CMP_STRIDE, rows_per_tok, HEAD_DIM)
    cache_sel = cache_sel_kv.reshape(n_nsa, pool, page_rows * rows_per_tok, HEAD_DIM)
    win_buf = state_win_kv.reshape(n_nsa, n_seq_s, WINDOW * rows_per_tok, HEAD_DIM)
    cmp_w1 = nsa_cmp_w1.reshape(n_nsa, 2, CMP_BLOCK * HEAD_DIM, HEAD_DIM)
    kv_shape = (2, N_KV_HEADS, HEAD_DIM)

    cmp_p, cmp_s, sel_p, sel_s, win_p, win_s, ret_p, ret_s = [], [], [], [], [], [], [], []
    for layer in range(depth):
        xp = rmsnorm(hp, norm_mix[layer], BF16, 256)
        xs = rmsnorm(hs, norm_mix[layer], BF16, m_s)
        if layer % 2 == 0:
            a = layer // 2
            cmp_bias = _cmp_bias(cmp_w1, nsa_cmp_pos, nsa_cmp_b1, a)
            hp, kv_p = _nsa_prompt(hp, xp, a, nsa_w_in, nsa_w_out, cmp_w1, cmp_bias, nsa_cmp_w2, n_seq=n_seq_p,
                                   seq=seq_p)
            hs, kv_s = _nsa_sample(hs, xs, a, nsa_w_in, nsa_w_out, cmp_w1, cmp_bias, nsa_cmp_w2, cache_cmp,
                                   cache_sel, win_buf, page_table, n_seq=n_seq_s, past_len=past_len)
            kv_p = kv_p.reshape((3, n_seq_p, seq_p) + kv_shape)
            kv_s = kv_s.reshape((3, n_seq_s, SAMPLE_PAD) + kv_shape)[:, :, :n_tok_s]
            cmp_p.append(kv_p[0])
            sel_p.append(kv_p[1])
            win_p.append(kv_p[2][:, seq_p - WINDOW:])
            cmp_s.append(kv_s[0])
            sel_s.append(kv_s[1])
            win_s.append(jnp.concatenate([state_win_kv[a][:, n_tok_s:], kv_s[2]], axis=1))
        else:
            r = layer // 2
            zeros = jnp.zeros((n_seq_p, RET_HEADS, RET_KDIM, RET_VDIM), F32)
            hp, st_p = _ret_layer(hp, xp, r, ret_w_in, ret_gn, ret_w_out, zeros, jnp.arange(seq_p), n_seq=n_seq_p,
                                  seq=seq_p, chunk=RET_CHUNK, n_tok=RET_CHUNK, tm=1024)
            hs, st_s = _ret_layer(hs, xs, r, ret_w_in, ret_gn, ret_w_out, state_ret[r],
                                  jnp.tile(past_len + jnp.arange(SAMPLE_PAD), n_seq_s), n_seq=n_seq_s,
                                  seq=SAMPLE_PAD, chunk=SAMPLE_PAD,
                                  n_tok=n_tok_s, tm=m_s)
            ret_p.append(st_p)
            ret_s.append(st_s)
        hp = _mlp(hp, norm_ffn[layer], ffn_w1, ffn_w2, layer, tm=1024)
        hs = _mlp(hs, norm_ffn[layer], ffn_w1, ffn_w2, layer, tm=m_s)
    y_prompt = rmsnorm(hp, norm_final, F32, 256).reshape(n_seq_p, seq_p, d)
    y_sample = rmsnorm(hs, norm_final, F32, m_s).reshape(n_seq_s, SAMPLE_PAD, d)[:, :n_tok_s]
    return (y_prompt, y_sample, jnp.stack(cmp_p), jnp.stack(cmp_s), jnp.stack(sel_p), jnp.stack(sel_s),
            jnp.stack(win_p), jnp.stack(win_s), jnp.stack(ret_p), jnp.stack(ret_s))
```

```python
import functools

import numpy as np
import jax
import jax.numpy as jnp
from jax import lax
from jax.experimental import pallas as pl
from jax.experimental.pallas import tpu as pltpu

F32 = jnp.float32
BF16 = jnp.bfloat16

N_HEADS = 16
HEAD_DIM = 128
N_KV_HEADS = 4
GQA_GROUP = N_HEADS // N_KV_HEADS
KV_ROW = 2 * N_KV_HEADS * HEAD_DIM
CMP_BLOCK = 32
CMP_STRIDE = 16
SEL_BLOCK = 64
N_SEL = 16
N_LOCAL = 2
WINDOW = 512
ROPE_THETA = 10000.0
RET_HEADS = 8
RET_KDIM = 256
RET_VDIM = 512
RET_CHUNK = 128
RET_HEADS_PER_STEP = 4
RMS_EPS = 1e-6
GN_EPS = 1e-5
NEG_INF = -1e30
FORCE_SCORE = 1e9
LOG2_E = 1.4426950408889634

SUBLANES = 8
LANES = 128
VMEM_LIMIT_BYTES = 56 * 1024 * 1024
SAMPLE_PAD = 16
PROMPT_ROW_TILE = 2048
X_TILE_BYTES = 8 * 1024 * 1024

_NT = (((1,), (1,)), ((), ()))
_TN = (((0,), (0,)), ((), ()))


def _params(*semantics):
    return pltpu.CompilerParams(dimension_semantics=semantics, vmem_limit_bytes=VMEM_LIMIT_BYTES)


def _rmsnorm_kernel(x_ref, g_ref, o_ref):
    x = x_ref[...]
    ms = jnp.mean(x * x, axis=-1, keepdims=True)
    o_ref[...] = (x * lax.rsqrt(ms + RMS_EPS) * g_ref[...]).astype(o_ref.dtype)


def rmsnorm(x, g, out_dtype, tm, row_block0=0, n_blocks=None):
    m, d = x.shape
    if n_blocks is None:
        n_blocks = m // tm
    return pl.pallas_call(
        _rmsnorm_kernel,
        grid=(n_blocks,),
        in_specs=[pl.BlockSpec((tm, d), lambda i: (i + row_block0, 0)), pl.BlockSpec((1, d), lambda i: (0, 0))],
        out_specs=pl.BlockSpec((tm, d), lambda i: (i, 0)),
        out_shape=jax.ShapeDtypeStruct((n_blocks * tm, d), out_dtype),
        compiler_params=_params("parallel"),
        name="rmsnorm",
    )(x, g.reshape(1, d))


def _mm_kernel(*refs, n_x, n_extra, n_out, nk, epilogue):
    x_refs = refs[:n_x]
    w_ref = refs[n_x]
    extra = refs[n_x + 1:n_x + 1 + n_extra]
    outs = refs[n_x + 1 + n_extra:n_x + 1 + n_extra + n_out]
    x = x_refs[0][...]
    for r in x_refs[1:]:
        x = x + r[...]
    part = jnp.dot(x.astype(BF16), w_ref[...].astype(BF16), preferred_element_type=F32)
    if nk == 1:
        epilogue(part, extra, outs)
        return
    k = pl.program_id(2)
    if epilogue is _ep_residual:
        @pl.when(k == 0)
        def _():
            outs[0][...] = extra[0][...] + part

        @pl.when(k > 0)
        def _():
            outs[0][...] += part
        return
    acc_ref = refs[-1]

    @pl.when(k == 0)
    def _():
        acc_ref[...] = part

    @pl.when(k > 0)
    def _():
        acc_ref[...] += part

    @pl.when(k == nk - 1)
    def _():
        epilogue(acc_ref[...], extra, outs)


def matmul(xs, w, layer, *, col_block0, n_cols, tm, tn, tk, extras, extra_specs, out_shapes, out_specs,
           epilogue, name):
    m, kdim = xs[0].shape
    nk = kdim // tk
    grid = (m // tm, pl.cdiv(n_cols, tn), nk)
    in_specs = [pl.BlockSpec((tm, tk), lambda i, j, k: (i, k)) for _ in xs]
    in_specs.append(pl.BlockSpec((None, tk, tn), lambda i, j, k: (layer, k, j + col_block0)))
    in_specs.extend(extra_specs)
    scratch = [pltpu.VMEM((tm, tn), F32)] if nk > 1 and epilogue is not _ep_residual else []
    kern = functools.partial(_mm_kernel, n_x=len(xs), n_extra=len(extras), n_out=len(out_shapes), nk=nk,
                             epilogue=epilogue)
    return pl.pallas_call(
        kern,
        grid=grid,
        in_specs=in_specs,
        out_specs=out_specs,
        out_shape=out_shapes,
        scratch_shapes=scratch,
        compiler_params=_params("parallel", "parallel", "arbitrary"),
        name=name,
    )(*xs, w, *extras)


def _rope_half_lane(x, cos, sin_signed):
    return x * cos + pltpu.roll(x, HEAD_DIM // 2, 1) * sin_signed


def _ep_rope_q(part, extra, outs, *, scale):
    cos = extra[0][...]
    sin = extra[1][...]
    for h in range(part.shape[1] // HEAD_DIM):
        sl = slice(h * HEAD_DIM, (h + 1) * HEAD_DIM)
        r = _rope_half_lane(part[:, sl], cos, sin) * scale
        outs[0][:, sl] = r.astype(outs[0].dtype)
        outs[1][sl, :] = r.T.astype(outs[1].dtype)


def _ep_nsa_kv(part, extra, outs):
    cos = extra[0][...]
    sin = extra[1][...]
    tm = part.shape[0]
    half = N_KV_HEADS * HEAD_DIM
    rows_per_tok = 2 * N_KV_HEADS
    for h in range(N_KV_HEADS):
        sl = slice(h * HEAD_DIM, (h + 1) * HEAD_DIM)
        vsl = slice(half + h * HEAD_DIM, half + (h + 1) * HEAD_DIM)
        k = _rope_half_lane(part[:, sl], cos, sin)
        v = part[:, vsl]
        outs[0][pl.ds(h, tm, stride=rows_per_tok), :] = k
        outs[0][pl.ds(N_KV_HEADS + h, tm, stride=rows_per_tok), :] = v
        outs[1][:, sl] = k.astype(BF16)
        outs[2][sl, :] = v.T.astype(BF16)


def _ep_sigmoid(part, extra, outs):
    outs[0][...] = jax.nn.sigmoid(part)


def _ep_residual(part, extra, outs):
    outs[0][...] = extra[0][...] + part


def _ep_relu2(part, extra, outs):
    h = jnp.maximum(part, 0.0)
    outs[0][...] = (h * h).astype(outs[0].dtype)


def _ep_cast(part, extra, outs):
    outs[0][...] = part.astype(outs[0].dtype)


def _ep_rope_ret(part, extra, outs, *, scale):
    cos = extra[0][...]
    sin = extra[1][...]
    for h in range(part.shape[1] // RET_KDIM):
        a = slice(h * RET_KDIM, h * RET_KDIM + LANES)
        b = slice(h * RET_KDIM + LANES, (h + 1) * RET_KDIM)
        x1 = part[:, a]
        x2 = part[:, b]
        outs[0][:, a] = ((x1 * cos - x2 * sin) * scale).astype(outs[0].dtype)
        outs[0][:, b] = ((x2 * cos + x1 * sin) * scale).astype(outs[0].dtype)


def _rope_tables(pos, half):
    inv = ROPE_THETA ** (-jnp.arange(half, dtype=F32) / half)
    ang = pos.astype(F32)[:, None] * inv[None, :]
    return jnp.cos(ang), jnp.sin(ang)


def _compress_kernel(*refs, n_pages, paged):
    if paged:
        refs = refs[1:]
    page_refs = refs[:n_pages]
    next_ref, w1_ref, bias_ref, w2_ref, out_ref, y_ref = refs[n_pages:]
    nc = page_refs[0].shape[0]
    rows_per_tok = 2 * N_KV_HEADS
    n_main = n_pages * nc * rows_per_tok
    blocks = [
        jnp.concatenate([page_refs[p][:, s].reshape(nc * rows_per_tok, HEAD_DIM) for s in range(CMP_STRIDE)], axis=1)
        for p in range(n_pages)]
    blocks.append(jnp.concatenate([next_ref[0, s] for s in range(CMP_STRIDE)], axis=1))
    x = jnp.concatenate(blocks, axis=0).astype(BF16)
    k_half = CMP_STRIDE * HEAD_DIM
    wcat = jnp.concatenate([w1_ref[0, :k_half], w1_ref[0, k_half:], w1_ref[1, :k_half], w1_ref[1, k_half:]],
                           axis=1).astype(BF16)
    per = jnp.dot(x, wcat, preferred_element_type=F32)
    hid = per.shape[1] // 4
    row = lax.broadcasted_iota(jnp.int32, (per.shape[0], 2 * hid), 0)
    is_v = jnp.bitwise_and(row, N_KV_HEADS) != 0
    per = jnp.where(is_v, per[:, 2 * hid:], per[:, :2 * hid])
    h = per[:n_main, :hid] + per[rows_per_tok:, hid:]
    is_v = is_v[:n_main, :hid]
    h = jax.nn.gelu(h + jnp.where(is_v, bias_ref[1:2, :], bias_ref[0:1, :]))
    w2cat = jnp.concatenate([w2_ref[0], w2_ref[1]], axis=1).astype(BF16)
    y2 = jnp.dot(h.astype(BF16), w2cat, preferred_element_type=F32)
    y_ref[...] = jnp.where(is_v, y2[:, hid:], y2[:, :hid])
    for eg in range(rows_per_tok):
        out_ref[eg] = y_ref[pl.ds(eg, n_main // rows_per_tok, stride=rows_per_tok), :].astype(out_ref.dtype)


def compress(kv, layer, page_table, w1, bias, w2, *, n_seq, n_chunks, n_pages):
    paged = page_table is not None
    nc = kv.shape[2]
    n_main = n_pages * nc
    steps = n_chunks // n_main
    pages_per_seq = steps * n_pages
    rows_per_tok = 2 * N_KV_HEADS

    def nxt(s):
        return jnp.minimum((s + 1) * n_pages, pages_per_seq - 1)

    if paged:
        def page_map(p):
            return lambda b, s, pt: (layer, pt[b, s * n_pages + p], 0, 0, 0, 0)

        def next_map(b, s, pt):
            return (layer, pt[b, nxt(s)], 0, 0, 0, 0)

        def fixed(*idx):
            return lambda b, s, pt: idx

        out_map = lambda b, s, pt: (b, 0, s, 0)
    else:
        def page_map(p):
            return lambda b, s: (layer, b * pages_per_seq + s * n_pages + p, 0, 0, 0, 0)

        def next_map(b, s):
            return (layer, b * pages_per_seq + nxt(s), 0, 0, 0, 0)

        def fixed(*idx):
            return lambda b, s: idx

        out_map = lambda b, s: (b, 0, s, 0)

    tile = (CMP_STRIDE, rows_per_tok, HEAD_DIM)
    in_specs = [pl.BlockSpec((None, None, nc) + tile, page_map(p)) for p in range(n_pages)]
    in_specs += [
        pl.BlockSpec((None, None, 1) + tile, next_map),
        pl.BlockSpec((None, 2, CMP_BLOCK * HEAD_DIM, HEAD_DIM), fixed(layer, 0, 0, 0)),
        pl.BlockSpec((2, HEAD_DIM), fixed(0, 0)),
        pl.BlockSpec((None, 2, HEAD_DIM, HEAD_DIM), fixed(layer, 0, 0, 0)),
    ]
    kern = functools.partial(_compress_kernel, n_pages=n_pages, paged=paged)
    grid_spec = pltpu.PrefetchScalarGridSpec(
        num_scalar_prefetch=1 if paged else 0,
        grid=(n_seq, steps),
        in_specs=in_specs,
        out_specs=pl.BlockSpec((None, rows_per_tok, n_main, HEAD_DIM), out_map),
        scratch_shapes=[pltpu.VMEM((n_main * rows_per_tok, HEAD_DIM), F32)],
    )
    args = ([page_table] if paged else []) + [kv] * (n_pages + 1) + [w1, bias, w2]
    return pl.pallas_call(
        kern,
        grid_spec=grid_spec,
        out_shape=jax.ShapeDtypeStruct((n_seq, rows_per_tok, n_chunks, HEAD_DIM), BF16),
        compiler_params=_params("parallel", "arbitrary"),
        name="nsa_compress",
    )(*args)


def _group_rows(q):
    return jnp.concatenate([q[:, h * HEAD_DIM:(h + 1) * HEAD_DIM] for h in range(GQA_GROUP)], axis=0)


def _cmp_kernel(q_ref, k_ref, v_ref, cov_ref, gate_ref, o_ref, val_ref, *, tq, pos0):
    qi = pl.program_id(2)
    q4 = _group_rows(q_ref[...])
    s = lax.dot_general(q4, k_ref[...], _NT, preferred_element_type=F32)
    row = lax.broadcasted_iota(jnp.int32, s.shape, 0)
    col = lax.broadcasted_iota(jnp.int32, s.shape, 1)
    qpos = pos0 + qi * tq + jnp.bitwise_and(row, tq - 1)
    valid = col * CMP_STRIDE + (CMP_BLOCK - 1) <= qpos
    s = jnp.where(valid, s, NEG_INF)
    m = jnp.max(s, axis=1, keepdims=True)
    e = jnp.where(valid, jnp.exp2(s - m), 0.0)
    l = jnp.sum(e, axis=1, keepdims=True)
    p = e / jnp.where(l > 0.0, l, 1.0)
    o4 = jnp.dot(p.astype(BF16), v_ref[...], preferred_element_type=F32)
    for h in range(GQA_GROUP):
        o_ref[:, h * HEAD_DIM:(h + 1) * HEAD_DIM] = o4[h * tq:(h + 1) * tq] * gate_ref[:, h:h + 1]
    imp = p[0:tq] + p[tq:2 * tq] + p[2 * tq:3 * tq] + p[3 * tq:4 * tq]
    hi = imp.astype(BF16)
    r1 = imp - hi.astype(F32)
    mid = r1.astype(BF16)
    lo = (r1 - mid.astype(F32)).astype(BF16)
    cov = cov_ref[...]
    score = (lax.dot_general(cov, hi, _NT, preferred_element_type=F32)
             + lax.dot_general(cov, mid, _NT, preferred_element_type=F32)
             + lax.dot_general(cov, lo, _NT, preferred_element_type=F32))
    jb = lax.broadcasted_iota(jnp.int32, score.shape, 0)
    qp = pos0 + qi * tq + lax.broadcasted_iota(jnp.int32, score.shape, 1)
    cur = jnp.right_shift(qp, SEL_BLOCK.bit_length() - 1)
    visible = jb <= cur
    forced = (jb == 0) | (visible & (jb > cur - N_LOCAL))
    val_ref[...] = jnp.where(forced, FORCE_SCORE, jnp.where(visible, score, -FORCE_SCORE))


def cmp_attention(q, kc, cover_t, gates_t, *, n_seq, seq, tq, pos0):
    nq = seq // tq
    ncp = kc.shape[2]
    nbv = cover_t.shape[0]
    gw = GQA_GROUP * HEAD_DIM
    kern = functools.partial(_cmp_kernel, tq=tq, pos0=pos0)
    if tq % LANES == 0:
        val_spec = pl.BlockSpec((nbv, tq), lambda b, g, i: (0, (b * N_KV_HEADS + g) * nq + i))
        val_shape = jax.ShapeDtypeStruct((nbv, n_seq * N_KV_HEADS * seq), F32)
    else:
        val_spec = pl.BlockSpec((None, None, nbv, tq), lambda b, g, i: (b, g, 0, i))
        val_shape = jax.ShapeDtypeStruct((n_seq, N_KV_HEADS, nbv, seq), F32)
    return pl.pallas_call(
        kern,
        grid=(n_seq, N_KV_HEADS, nq),
        in_specs=[
            pl.BlockSpec((tq, gw), lambda b, g, i: (b * nq + i, g)),
            pl.BlockSpec((None, None, ncp, HEAD_DIM), lambda b, g, i: (b, g, 0, 0)),
            pl.BlockSpec((None, None, ncp, HEAD_DIM), lambda b, g, i: (b, N_KV_HEADS + g, 0, 0)),
            pl.BlockSpec((nbv, ncp), lambda b, g, i: (0, 0)),
            pl.BlockSpec((None, None, tq, GQA_GROUP), lambda b, g, i: (0, g, b * nq + i, 0)),
        ],
        out_specs=[pl.BlockSpec((tq, gw), lambda b, g, i: (b * nq + i, g)), val_spec],
        out_shape=[jax.ShapeDtypeStruct((n_seq * seq, N_HEADS * HEAD_DIM), F32), val_shape],
        compiler_params=_params("parallel", "parallel", "parallel"),
        name="nsa_cmp_attention",
    )(q, kc, kc, cover_t, gates_t)


def _cover_t(n_chunks, n_blk, nbv):
    n_cmp = n_chunks - CMP_BLOCK // CMP_STRIDE + 1
    c0 = np.arange(n_chunks) * CMP_STRIDE
    j0 = np.arange(nbv) * SEL_BLOCK
    cov = (c0[None, :] < j0[:, None] + SEL_BLOCK) & (c0[None, :] + CMP_BLOCK > j0[:, None])
    cov &= (np.arange(n_chunks)[None, :] < n_cmp) & (np.arange(nbv)[:, None] < n_blk)
    return jnp.asarray(cov.astype(np.float32), dtype=BF16)


def _rank_kernel(val_ref, out_ref, *, n_blk, n_sel, causal_seq):
    val = val_ref[...]
    tc = val.shape[1]
    jb = lax.broadcasted_iota(jnp.int32, val.shape, 0)
    if causal_seq is None:
        n_iter = n_blk
    else:
        pos0 = (pl.program_id(0) % (causal_seq // tc)) * tc
        n_iter = jnp.minimum(n_blk, (pos0 + tc - 1) // SEL_BLOCK + 1)

    def body(i, rank):
        row = val_ref[pl.ds(i, 1), :]
        beats = (row > val) | ((row == val) & (i < jb))
        return rank + jnp.where(beats, 1, 0)

    rank = lax.fori_loop(0, n_iter, body, jnp.zeros(val.shape, jnp.int32))
    chosen = (rank < n_sel) & (val > -0.5 * FORCE_SCORE)
    out_ref[...] = jnp.where(chosen, 0.0, NEG_INF).astype(out_ref.dtype)


def select_bias(val, *, n_blk, tc, causal_seq=None):
    nbv, ncols = val.shape
    kern = functools.partial(_rank_kernel, n_blk=n_blk, n_sel=min(N_SEL, n_blk), causal_seq=causal_seq)
    return pl.pallas_call(
        kern,
        grid=(ncols // tc,),
        in_specs=[pl.BlockSpec((nbv, tc), lambda i: (0, i))],
        out_specs=pl.BlockSpec((nbv, tc), lambda i: (0, i)),
        out_shape=jax.ShapeDtypeStruct((nbv, ncols), BF16),
        compiler_params=_params("parallel"),
        name="nsa_select_rank",
    )(val)


def _flash_kernel(*refs, tq, tk, nsteps, use_bias, window, out_dtype):
    if use_bias:
        qt_ref, bias_ref, k_ref, vt_ref, e_ref, gate_ref, prev_ref, o_ref, qa_ref, m_ref, acc_ref = refs
    else:
        qt_ref, k_ref, vt_ref, gate_ref, prev_ref, o_ref, qa_ref, m_ref, acc_ref = refs
    qi = pl.program_id(2)
    ki = pl.program_id(3)
    if use_bias:
        kt = ki
        last = (qi * tq + tq - 1) // tk
        active = ki <= last
        edge = ki == last
    else:
        kt = qi * (tq // tk) - window // tk + ki
        active = kt >= 0
        edge = (ki < tq // tk) | (ki >= window // tk)

    @pl.when(ki == 0)
    def _():
        for h in range(GQA_GROUP):
            qa_ref[0:HEAD_DIM, h * tq:(h + 1) * tq] = qt_ref[h * HEAD_DIM:(h + 1) * HEAD_DIM, :]
            if use_bias:
                nbv = bias_ref.shape[0]
                qa_ref[HEAD_DIM:HEAD_DIM + nbv, h * tq:(h + 1) * tq] = bias_ref[...]
                if HEAD_DIM + nbv < qa_ref.shape[0]:
                    qa_ref[HEAD_DIM + nbv:, h * tq:(h + 1) * tq] = jnp.zeros(
                        (qa_ref.shape[0] - HEAD_DIM - nbv, tq), BF16)
        m_ref[...] = jnp.full(m_ref.shape, NEG_INF, F32)
        acc_ref[...] = jnp.zeros(acc_ref.shape, F32)

    def step(masked):
        if use_bias:
            ka = jnp.concatenate([k_ref[...], e_ref[...]], axis=1)
        else:
            ka = k_ref[...]
        st = jnp.dot(ka, qa_ref[...], preferred_element_type=F32)
        if masked:
            kpos = kt * tk + lax.broadcasted_iota(jnp.int32, st.shape, 0)
            qpos = qi * tq + jnp.bitwise_and(lax.broadcasted_iota(jnp.int32, st.shape, 1), tq - 1)
            d = qpos - kpos
            ok = d >= 0
            if window is not None:
                ok = ok & (d < window)
            st = jnp.where(ok, st, NEG_INF)
        m_old = m_ref[...]
        m_new = jnp.maximum(m_old, jnp.max(st, axis=0, keepdims=True))
        alpha = jnp.exp2(m_old - m_new)
        p = jnp.exp2(st - m_new).astype(BF16)
        va = jnp.concatenate([vt_ref[...], jnp.ones((acc_ref.shape[0] - HEAD_DIM, tk), BF16)], axis=0)
        acc_ref[...] = alpha * acc_ref[...] + jnp.dot(va, p, preferred_element_type=F32)
        m_ref[...] = m_new

    @pl.when(active & edge)
    def _():
        step(True)

    @pl.when(active & jnp.logical_not(edge))
    def _():
        step(False)

    @pl.when(ki == nsteps - 1)
    def _():
        acc = acc_ref[...]
        ot = acc[:HEAD_DIM] / acc[HEAD_DIM:HEAD_DIM + 1]
        for h in range(GQA_GROUP):
            sl = slice(h * HEAD_DIM, (h + 1) * HEAD_DIM)
            o_h = ot[:, h * tq:(h + 1) * tq].T
            o_ref[:, sl] = (prev_ref[:, sl] + o_h * gate_ref[:, h:h + 1]).astype(out_dtype)


def flash_attention(qt, k, vt, branch, gates_t, prev, *, n_seq, seq, tq, tk, bias=None, onehot=None, window=None,
                    out_dtype=F32):
    use_bias = bias is not None
    nq = seq // tq
    nkt = seq // tk
    gw = GQA_GROUP * HEAD_DIM
    if use_bias:
        nsteps = nkt

        def kt_of(i, k):
            return jnp.minimum(k, (i * tq + tq - 1) // tk)
    else:
        nsteps = window // tk + tq // tk

        def kt_of(i, k):
            return jnp.maximum(i * (tq // tk) - window // tk + k, 0)

    o_spec = pl.BlockSpec((tq, gw), lambda b, g, i, k: (b * nq + i, g))
    qt_spec = pl.BlockSpec((gw, tq), lambda b, g, i, k: (g, b * nq + i))
    k_spec = pl.BlockSpec((None, tk, HEAD_DIM), lambda b, g, i, k: (branch, b * nkt + kt_of(i, k), g))
    vt_spec = pl.BlockSpec((None, HEAD_DIM, tk), lambda b, g, i, k: (branch, g, b * nkt + kt_of(i, k)))
    gate_spec = pl.BlockSpec((None, None, tq, GQA_GROUP), lambda b, g, i, k: (branch, g, b * nq + i, 0))
    in_specs = [qt_spec]
    args = [qt]
    qa_rows = HEAD_DIM
    if use_bias:
        nbv = bias.shape[0]
        nbp = onehot.shape[1]
        qa_rows += nbp
        in_specs.append(pl.BlockSpec((nbv, tq), lambda b, g, i, k: (0, (b * N_KV_HEADS + g) * nq + i)))
        args.append(bias)
    in_specs += [k_spec, vt_spec]
    args += [k, vt]
    if use_bias:
        in_specs.append(pl.BlockSpec((tk, nbp), lambda b, g, i, k: (kt_of(i, k), 0)))
        args.append(onehot)
    in_specs += [gate_spec, o_spec]
    args += [gates_t, prev]
    kern = functools.partial(_flash_kernel, tq=tq, tk=tk, nsteps=nsteps, use_bias=use_bias, window=window,
                             out_dtype=out_dtype)
    cols = GQA_GROUP * tq
    return pl.pallas_call(
        kern,
        grid=(n_seq, N_KV_HEADS, nq, nsteps),
        in_specs=in_specs,
        out_specs=o_spec,
        out_shape=jax.ShapeDtypeStruct((n_seq * seq, N_HEADS * HEAD_DIM), out_dtype),
        scratch_shapes=[
            pltpu.VMEM((qa_rows, cols), BF16),
            pltpu.VMEM((1, cols), F32),
            pltpu.VMEM((HEAD_DIM + 2 * SUBLANES, cols), F32),
        ],
        compiler_params=_params("parallel", "parallel", "parallel", "arbitrary"),
        name="nsa_sel_attention" if use_bias else "nsa_win_attention",
    )(*args)


def _block_onehot(n_keys, nbp):
    e = (np.arange(n_keys)[:, None] // SEL_BLOCK) == np.arange(nbp)[None, :]
    return jnp.asarray(e.astype(np.float32), dtype=BF16)


def _sel_sample_kernel(*refs, n_pages, nbp, past_len, nsteps, tq):
    refs = refs[1:]
    q_ref, bias_ref, new_ref = refs[:3]
    page_refs = refs[3:3 + n_pages]
    gate_ref, prev_ref, o_ref, qa_ref, m_ref, l_ref, acc_ref = refs[3 + n_pages:]
    step = pl.program_id(1)
    page_rows = page_refs[0].shape[0]
    rows = GQA_GROUP * tq
    blk_shift = SEL_BLOCK.bit_length() - 1

    def scores(g, k, key0):
        n = k.shape[0]
        r = lax.broadcasted_iota(jnp.int32, (n, nbp), 0)
        j = lax.broadcasted_iota(jnp.int32, (n, nbp), 1)
        onehot = jnp.where(jnp.right_shift(key0 + r, blk_shift) == j, 1.0, 0.0).astype(BF16)
        ka = jnp.concatenate([k, onehot], axis=1)
        return lax.dot_general(qa_ref[g], ka, _NT, preferred_element_type=F32)

    @pl.when(step == 0)
    def _():
        q = q_ref[...]
        for g in range(N_KV_HEADS):
            for h in range(GQA_GROUP):
                hh = g * GQA_GROUP + h
                qa_ref[g, h * tq:(h + 1) * tq, 0:HEAD_DIM] = q[:, hh * HEAD_DIM:(hh + 1) * HEAD_DIM]
                qa_ref[g, h * tq:(h + 1) * tq, HEAD_DIM:] = bias_ref[g]
            kn = new_ref[pl.ds(g, tq, stride=2 * N_KV_HEADS), :].astype(BF16)
            vn = new_ref[pl.ds(N_KV_HEADS + g, tq, stride=2 * N_KV_HEADS), :].astype(BF16)
            s = scores(g, kn, past_len)
            row = lax.broadcasted_iota(jnp.int32, s.shape, 0)
            col = lax.broadcasted_iota(jnp.int32, s.shape, 1)
            s = jnp.where(col <= jnp.bitwise_and(row, tq - 1), s, NEG_INF)
            m = jnp.max(s, axis=1, keepdims=True)
            p = jnp.exp2(s - m)
            m_ref[g] = m
            l_ref[g] = jnp.sum(p, axis=1, keepdims=True)
            acc_ref[g] = jnp.dot(p.astype(BF16), vn, preferred_element_type=F32)

    rows_per_tok = 2 * N_KV_HEADS
    page_tokens = page_rows // rows_per_tok
    for g in range(N_KV_HEADS):
        k = jnp.concatenate([page_refs[p][pl.ds(g, page_tokens, stride=rows_per_tok), :] for p in range(n_pages)],
                            axis=0).astype(BF16)
        v = jnp.concatenate([page_refs[p][pl.ds(N_KV_HEADS + g, page_tokens, stride=rows_per_tok), :]
                             for p in range(n_pages)], axis=0).astype(BF16)
        s = scores(g, k, step * (n_pages * page_tokens))
        m_old = m_ref[g]
        m_new = jnp.maximum(m_old, jnp.max(s, axis=1, keepdims=True))
        alpha = jnp.exp2(m_old - m_new)
        p = jnp.exp2(s - m_new)
        l_ref[g] = alpha * l_ref[g] + jnp.sum(p, axis=1, keepdims=True)
        acc_ref[g] = alpha * acc_ref[g] + jnp.dot(p.astype(BF16), v, preferred_element_type=F32)
        m_ref[g] = m_new

    @pl.when(step == nsteps - 1)
    def _():
        for g in range(N_KV_HEADS):
            o4 = acc_ref[g] / l_ref[g]
            for h in range(GQA_GROUP):
                hh = g * GQA_GROUP + h
                sl = slice(hh * HEAD_DIM, (hh + 1) * HEAD_DIM)
                o_ref[:, sl] = prev_ref[:, sl] + o4[h * tq:(h + 1) * tq] * gate_ref[g, :, h:h + 1]


def sel_attention_sample(q, bias, kv_new, cache, layer, page_table, gates_t, prev, *, n_pages_step, past_len):
    n_seq, pages = page_table.shape
    tq = SAMPLE_PAD
    nbp = bias.shape[-1]
    page_rows = cache.shape[2]
    nsteps = pages // n_pages_step
    dm = N_HEADS * HEAD_DIM
    row_spec = pl.BlockSpec((tq, dm), lambda b, s, pt: (b, 0))

    def page_map(p):
        return lambda b, s, pt: (layer, pt[b, s * n_pages_step + p], 0, 0)

    in_specs = [
        row_spec,
        pl.BlockSpec((None, N_KV_HEADS, tq, nbp), lambda b, s, pt: (b, 0, 0, 0)),
        pl.BlockSpec((None, tq * 2 * N_KV_HEADS, HEAD_DIM), lambda b, s, pt: (1, b, 0)),
    ]
    in_specs += [pl.BlockSpec((None, None, page_rows, HEAD_DIM), page_map(p)) for p in range(n_pages_step)]
    in_specs += [
        pl.BlockSpec((None, N_KV_HEADS, tq, GQA_GROUP), lambda b, s, pt: (1, 0, b, 0)),
        row_spec,
    ]
    rows = GQA_GROUP * tq
    kern = functools.partial(_sel_sample_kernel, n_pages=n_pages_step, nbp=nbp, past_len=past_len, nsteps=nsteps,
                             tq=tq)
    grid_spec = pltpu.PrefetchScalarGridSpec(
        num_scalar_prefetch=1,
        grid=(n_seq, nsteps),
        in_specs=in_specs,
        out_specs=row_spec,
        scratch_shapes=[
            pltpu.VMEM((N_KV_HEADS, rows, HEAD_DIM + nbp), BF16),
            pltpu.VMEM((N_KV_HEADS, rows, 1), F32),
            pltpu.VMEM((N_KV_HEADS, rows, 1), F32),
            pltpu.VMEM((N_KV_HEADS, rows, HEAD_DIM), F32),
        ],
    )
    return pl.pallas_call(
        kern,
        grid_spec=grid_spec,
        out_shape=jax.ShapeDtypeStruct((n_seq * tq, dm), F32),
        compiler_params=_params("parallel", "arbitrary"),
        name="nsa_sel_attention_sample",
    )(page_table, q, bias, kv_new, *([cache] * n_pages_step), gates_t, prev)


def _win_sample_kernel(q_ref, buf_ref, new_ref, gate_ref, prev_ref, o_ref, *, tq, out_dtype):
    rows_per_tok = 2 * N_KV_HEADS
    w_buf = buf_ref.shape[0] // rows_per_tok
    gw = GQA_GROUP * HEAD_DIM
    for g in range(N_KV_HEADS):
        q4 = _group_rows(q_ref[:, g * gw:(g + 1) * gw])
        kb = buf_ref[pl.ds(g, w_buf, stride=rows_per_tok), :].astype(BF16)
        vb = buf_ref[pl.ds(N_KV_HEADS + g, w_buf, stride=rows_per_tok), :].astype(BF16)
        kn = new_ref[pl.ds(g, tq, stride=rows_per_tok), :].astype(BF16)
        vn = new_ref[pl.ds(N_KV_HEADS + g, tq, stride=rows_per_tok), :].astype(BF16)
        sb = lax.dot_general(q4, kb, _NT, preferred_element_type=F32)
        sn = lax.dot_general(q4, kn, _NT, preferred_element_type=F32)
        tb = jnp.bitwise_and(lax.broadcasted_iota(jnp.int32, sb.shape, 0), tq - 1)
        rb = lax.broadcasted_iota(jnp.int32, sb.shape, 1)
        sb = jnp.where(tb + w_buf - rb < WINDOW, sb, NEG_INF)
        tn = jnp.bitwise_and(lax.broadcasted_iota(jnp.int32, sn.shape, 0), tq - 1)
        rn = lax.broadcasted_iota(jnp.int32, sn.shape, 1)
        sn = jnp.where(rn <= tn, sn, NEG_INF)
        m = jnp.maximum(jnp.max(sb, axis=1, keepdims=True), jnp.max(sn, axis=1, keepdims=True))
        pb = jnp.exp2(sb - m)
        pn = jnp.exp2(sn - m)
        l = jnp.sum(pb, axis=1, keepdims=True) + jnp.sum(pn, axis=1, keepdims=True)
        o4 = (jnp.dot(pb.astype(BF16), vb, preferred_element_type=F32)
              + jnp.dot(pn.astype(BF16), vn, preferred_element_type=F32)) / l
        for h in range(GQA_GROUP):
            hh = g * GQA_GROUP + h
            sl = slice(hh * HEAD_DIM, (hh + 1) * HEAD_DIM)
            o_ref[:, sl] = (prev_ref[:, sl] + o4[h * tq:(h + 1) * tq] * gate_ref[g, :, h:h + 1]).astype(out_dtype)


def win_attention_sample(q, win_buf, layer, kv_new, gates_t, prev, *, out_dtype):
    n_seq, buf_rows = win_buf.shape[1], win_buf.shape[2]
    tq = SAMPLE_PAD
    q_spec = pl.BlockSpec((tq, N_HEADS * HEAD_DIM), lambda b: (b, 0))
    kern = functools.partial(_win_sample_kernel, tq=tq, out_dtype=out_dtype)
    return pl.pallas_call(
        kern,
        grid=(n_seq,),
        in_specs=[
            q_spec,
            pl.BlockSpec((None, None, buf_rows, HEAD_DIM), lambda b: (layer, b, 0, 0)),
            pl.BlockSpec((None, tq * 2 * N_KV_HEADS, HEAD_DIM), lambda b: (2, b, 0)),
            pl.BlockSpec((None, N_KV_HEADS, tq, GQA_GROUP), lambda b: (2, 0, b, 0)),
            q_spec,
        ],
        out_specs=q_spec,
        out_shape=jax.ShapeDtypeStruct((n_seq * tq, N_HEADS * HEAD_DIM), out_dtype),
        compiler_params=_params("parallel"),
        name="nsa_win_attention_sample",
    )(q, win_buf, kv_new, gates_t, prev)


def _ret_kernel(q_ref, k_ref, v_ref, g_ref, gn_ref, dec_ref, qd_ref, kd_ref, gc_ref, s0_ref, o_ref, sout_ref,
                state_ref, *, n_chunks):
    c = pl.program_id(2)

    @pl.when(c == 0)
    def _():
        state_ref[...] = s0_ref[...]

    for h in range(state_ref.shape[0]):
        ksl = slice(h * RET_KDIM, (h + 1) * RET_KDIM)
        vsl = slice(h * RET_VDIM, (h + 1) * RET_VDIM)
        q = q_ref[:, ksl]
        kf = k_ref[:, ksl]
        v = v_ref[:, vsl]
        state = state_ref[h]
        s = lax.dot_general(q, kf.astype(BF16), _NT, preferred_element_type=F32) * dec_ref[h]
        inner = jnp.dot(s.astype(BF16), v, preferred_element_type=F32)
        cross = jnp.dot(q, state.astype(BF16), preferred_element_type=F32) * qd_ref[h]
        o = inner + cross
        kd = (kf * kd_ref[h]).astype(BF16)
        state_ref[h] = gc_ref[h] * state + lax.dot_general(kd, v, _TN, preferred_element_type=F32)
        mu = jnp.mean(o, axis=-1, keepdims=True)
        d = o - mu
        var = jnp.mean(d * d, axis=-1, keepdims=True)
        on = d * lax.rsqrt(var + GN_EPS) * gn_ref[:, vsl]
        g = g_ref[:, vsl]
        o_ref[:, vsl] = (g * jax.nn.sigmoid(g) * on).astype(o_ref.dtype)

    @pl.when(c == n_chunks - 1)
    def _():
        sout_ref[...] = state_ref[...]


def _ret_tables(chunk, n_tok):
    log_g = jnp.log(1.0 - 2.0 ** (-5.0 - jnp.arange(RET_HEADS, dtype=F32)))
    i = jnp.arange(chunk, dtype=F32)
    rel = i[:, None] - i[None, :]
    decay = jnp.where(rel >= 0, jnp.exp(jnp.maximum(rel, 0.0)[None] * log_g[:, None, None]), 0.0)
    q_dec = jnp.exp((i[None, :] + 1.0) * log_g[:, None])[..., None]
    k_dec = jnp.where(i[None, :] < n_tok, jnp.exp((n_tok - 1.0 - i)[None, :] * log_g[:, None]), 0.0)[..., None]
    g_c = jnp.exp(n_tok * log_g)[:, None, None]
    return decay, q_dec, k_dec, g_c


def retention(q, k, v, g, gn, layer, state0, *, n_seq, seq, chunk, n_tok):
    nc = seq // chunk
    hps = RET_HEADS_PER_STEP
    decay, q_dec, k_dec, g_c = _ret_tables(chunk, n_tok)
    kern = functools.partial(_ret_kernel, n_chunks=nc)
    tok_k = pl.BlockSpec((chunk, hps * RET_KDIM), lambda b, h, c: (b * nc + c, h))
    tok_v = pl.BlockSpec((chunk, hps * RET_VDIM), lambda b, h, c: (b * nc + c, h))
    st_spec = pl.BlockSpec((None, hps, RET_KDIM, RET_VDIM), lambda b, h, c: (b, h, 0, 0))
    return pl.pallas_call(
        kern,
        grid=(n_seq, RET_HEADS // hps, nc),
        in_specs=[
            tok_k, tok_k, tok_v, tok_v,
            pl.BlockSpec((None, 1, hps * RET_VDIM), lambda b, h, c: (layer, 0, h)),
            pl.BlockSpec((hps, chunk, chunk), lambda b, h, c: (h, 0, 0)),
            pl.BlockSpec((hps, chunk, 1), lambda b, h, c: (h, 0, 0)),
            pl.BlockSpec((hps, chunk, 1), lambda b, h, c: (h, 0, 0)),
            pl.BlockSpec((hps, 1, 1), lambda b, h, c: (h, 0, 0)),
            st_spec,
        ],
        out_specs=[tok_v, st_spec],
        out_shape=[
            jax.ShapeDtypeStruct((n_seq * seq, RET_HEADS * RET_VDIM), BF16),
            jax.ShapeDtypeStruct((n_seq, RET_HEADS, RET_KDIM, RET_VDIM), F32),
        ],
        scratch_shapes=[pltpu.VMEM((hps, RET_KDIM, RET_VDIM), F32)],
        compiler_params=_params("parallel", "parallel", "arbitrary"),
        name="retention_chunk",
    )(q, k, v, g, gn, decay, q_dec, k_dec, g_c, state0)


def _project_nsa(xn, w_in, layer, cos, sin_signed, *, tm):
    m, d = xn.shape
    nq = N_HEADS * HEAD_DIM
    half = N_KV_HEADS * HEAD_DIM
    rows_per_tok = 2 * N_KV_HEADS
    tabs_per_seq = cos.shape[0] // tm
    tab_spec = pl.BlockSpec((tm, HEAD_DIM), lambda i, j, k: (i % tabs_per_seq, 0))
    tn_q = 512
    q = matmul(
        [xn], w_in, layer, col_block0=0, n_cols=nq, tm=tm, tn=tn_q, tk=d,
        extras=[cos, sin_signed], extra_specs=[tab_spec, tab_spec],
        out_shapes=[jax.ShapeDtypeStruct((m, nq), BF16), jax.ShapeDtypeStruct((nq, m), BF16)],
        out_specs=[pl.BlockSpec((tm, tn_q), lambda i, j, k: (i, j)), pl.BlockSpec((tn_q, tm), lambda i, j, k: (j, i))],
        epilogue=functools.partial(_ep_rope_q, scale=HEAD_DIM ** -0.5 * LOG2_E), name="nsa_proj_q")
    kv = matmul(
        [xn], w_in, layer, col_block0=nq // KV_ROW, n_cols=3 * KV_ROW, tm=tm, tn=KV_ROW, tk=d,
        extras=[cos, sin_signed], extra_specs=[tab_spec, tab_spec],
        out_shapes=[jax.ShapeDtypeStruct((3, m * rows_per_tok, HEAD_DIM), F32),
                    jax.ShapeDtypeStruct((3, m, half), BF16), jax.ShapeDtypeStruct((3, half, m), BF16)],
        out_specs=[pl.BlockSpec((None, tm * rows_per_tok, HEAD_DIM), lambda i, j, k: (j, i, 0)),
                   pl.BlockSpec((None, tm, half), lambda i, j, k: (j, i, 0)),
                   pl.BlockSpec((None, half, tm), lambda i, j, k: (j, 0, i))],
        epilogue=_ep_nsa_kv, name="nsa_proj_kv")
    n_gate = 3 * N_HEADS
    w_gate = jnp.pad(w_in[layer, :, nq + 3 * KV_ROW:], ((0, 0), (0, LANES - n_gate)))[None]
    gates = matmul(
        [xn], w_gate, 0, col_block0=0, n_cols=LANES, tm=tm, tn=LANES, tk=d, extras=[], extra_specs=[],
        out_shapes=[jax.ShapeDtypeStruct((m, LANES), F32)],
        out_specs=[pl.BlockSpec((tm, LANES), lambda i, j, k: (i, 0))],
        epilogue=_ep_sigmoid, name="nsa_proj_gate")[0]
    gates_t = gates[:, :n_gate].reshape(m, 3, N_KV_HEADS, GQA_GROUP).transpose(1, 2, 0, 3)
    return q, kv, gates_t


def _cmp_bias(w1, pos, b1, layer):
    k_full = CMP_BLOCK * HEAD_DIM
    pos_rows = jnp.zeros((2, SUBLANES, k_full), F32).at[:, 0].set(pos[layer].reshape(2, k_full))
    terms = []
    for e in range(2):
        t = matmul(
            [pos_rows[e]], w1.reshape(-1, k_full, HEAD_DIM), layer * 2 + e, col_block0=0, n_cols=HEAD_DIM,
            tm=SUBLANES, tn=HEAD_DIM, tk=k_full, extras=[], extra_specs=[],
            out_shapes=[jax.ShapeDtypeStruct((SUBLANES, HEAD_DIM), F32)],
            out_specs=[pl.BlockSpec((SUBLANES, HEAD_DIM), lambda i, j, k: (0, 0))],
            epilogue=_ep_cast, name="nsa_cmp_pos_term")[0]
        terms.append(t[0])
    return jnp.stack(terms) + b1[layer]


def _mlp(h, norm_g, w1, w2, layer, *, tm):
    m, d = h.shape
    d_ff = w1.shape[-1]
    xn = rmsnorm(h, norm_g, BF16, min(tm, 256))
    tn1 = 1024
    a = matmul(
        [xn], w1, layer, col_block0=0, n_cols=d_ff, tm=tm, tn=tn1, tk=d, extras=[], extra_specs=[],
        out_shapes=[jax.ShapeDtypeStruct((m, d_ff), BF16)],
        out_specs=[pl.BlockSpec((tm, tn1), lambda i, j, k: (i, j))],
        epilogue=_ep_relu2, name="mlp_up")[0]
    tn2 = 1024
    return matmul(
        [a], w2, layer, col_block0=0, n_cols=d, tm=tm, tn=tn2, tk=512, extras=[h],
        extra_specs=[pl.BlockSpec((tm, tn2), lambda i, j, k: (i, j))],
        out_shapes=[jax.ShapeDtypeStruct((m, d), F32)],
        out_specs=[pl.BlockSpec((tm, tn2), lambda i, j, k: (i, j))],
        epilogue=_ep_residual, name="mlp_down")[0]


def _out_proj(o, w_out, layer, h, *, tm, name):
    m, d = h.shape
    tn = 512
    tm = min(tm, X_TILE_BYTES // (o.shape[1] * o.dtype.itemsize))
    return matmul(
        [o], w_out, layer, col_block0=0, n_cols=d, tm=tm, tn=tn, tk=o.shape[1], extras=[h],
        extra_specs=[pl.BlockSpec((tm, tn), lambda i, j, k: (i, j))],
        out_shapes=[jax.ShapeDtypeStruct((m, d), F32)],
        out_specs=[pl.BlockSpec((tm, tn), lambda i, j, k: (i, j))],
        epilogue=_ep_residual, name=name)[0]


def _nsa_tables(pos):
    cos, sin = _rope_tables(pos, HEAD_DIM // 2)
    return jnp.concatenate([cos, cos], axis=1), jnp.concatenate([-sin, sin], axis=1)


def _nsa_prompt(h, xn, layer, w_in, w_out, cmp_w1, cmp_bias, cmp_w2, *, n_seq, seq):
    tm = PROMPT_ROW_TILE
    cos, sin_signed = _nsa_tables(jnp.arange(seq))
    (q, qt), (kv_rows, k_bf, vt_bf), gates_t = _project_nsa(xn, w_in, layer, cos, sin_signed, tm=tm // 2)
    n_chunks = seq // CMP_STRIDE
    chunks_per_step = 128
    kv_chunks = kv_rows.reshape(3, (n_seq * n_chunks) // chunks_per_step, chunks_per_step, CMP_STRIDE,
                                2 * N_KV_HEADS, HEAD_DIM)
    kc = compress(kv_chunks, 0, None, cmp_w1, cmp_bias, cmp_w2, n_seq=n_seq, n_chunks=n_chunks, n_pages=1)
    n_blk = seq // SEL_BLOCK
    cover_t = _cover_t(n_chunks, n_blk, n_blk)
    o1, val = cmp_attention(q, kc, cover_t, gates_t, n_seq=n_seq, seq=seq, tq=256, pos0=0)
    bias_t = select_bias(val, n_blk=n_blk, tc=512, causal_seq=seq)
    nbp = pl.cdiv(n_blk, LANES) * LANES
    o2 = flash_attention(qt, k_bf, vt_bf, 1, gates_t, o1, n_seq=n_seq, seq=seq, tq=256, tk=512, bias=bias_t,
                         onehot=_block_onehot(seq, nbp))
    o3 = flash_attention(qt, k_bf, vt_bf, 2, gates_t, o2, n_seq=n_seq, seq=seq, tq=256, tk=256, window=WINDOW,
                         out_dtype=BF16)
    h = _out_proj(o3, w_out, layer, h, tm=tm, name="nsa_out_proj")
    return h, kv_rows


def _nsa_sample(h, xn, layer, w_in, w_out, cmp_w1, cmp_bias, cmp_w2, cache_cmp, cache_sel, win_buf, page_table,
                *, n_seq, past_len):
    tq = SAMPLE_PAD
    m = n_seq * tq
    cos, sin_signed = _nsa_tables(jnp.tile(past_len + jnp.arange(tq), n_seq))
    (q, _), (kv_rows, _, _), gates_t = _project_nsa(xn, w_in, layer, cos, sin_signed, tm=m)
    n_chunks = past_len // CMP_STRIDE
    kc = compress(cache_cmp, layer, page_table, cmp_w1, cmp_bias, cmp_w2, n_seq=n_seq, n_chunks=n_chunks,
                  n_pages=16)
    n_blk = pl.cdiv(past_len + tq, SEL_BLOCK)
    nbv = pl.cdiv(n_blk, SUBLANES) * SUBLANES
    cover_t = _cover_t(n_chunks, n_blk, nbv)
    o1, val = cmp_attention(q, kc, cover_t, gates_t, n_seq=n_seq, seq=tq, tq=tq, pos0=past_len)
    ncols = n_seq * N_KV_HEADS * tq
    bias_t = select_bias(val.transpose(2, 0, 1, 3).reshape(nbv, ncols), n_blk=n_blk, tc=ncols)
    nbp = pl.cdiv(nbv, LANES) * LANES
    bias = bias_t.reshape(nbv, n_seq, N_KV_HEADS, tq).transpose(1, 2, 3, 0)
    bias = jnp.pad(bias, ((0, 0), (0, 0), (0, 0), (0, nbp - nbv)))
    o2 = sel_attention_sample(q, bias, kv_rows, cache_sel, layer, page_table, gates_t, o1, n_pages_step=8,
                              past_len=past_len)
    o3 = win_attention_sample(q, win_buf, layer, kv_rows, gates_t, o2, out_dtype=BF16)
    h = _out_proj(o3, w_out, layer, h, tm=m, name="nsa_out_proj_sample")
    return h, kv_rows


def _ret_layer(h, xn, layer, w_in, gn, w_out, state0, pos, *, n_seq, seq, chunk, n_tok, tm):
    m, d = xn.shape
    nk = RET_HEADS * RET_KDIM
    nv = RET_HEADS * RET_VDIM
    cos, sin = _rope_tables(pos, RET_KDIM // 2)
    tabs_per_seq = cos.shape[0] // tm
    tab_spec = pl.BlockSpec((tm, LANES), lambda i, j, k: (i % tabs_per_seq, 0))
    tn = 512

    def proj(col0, n_cols, dtype, epilogue, extras, extra_specs, name):
        return matmul(
            [xn], w_in, layer, col_block0=col0 // tn, n_cols=n_cols, tm=tm, tn=tn, tk=d,
            extras=extras, extra_specs=extra_specs,
            out_shapes=[jax.ShapeDtypeStruct((m, n_cols), dtype)],
            out_specs=[pl.BlockSpec((tm, tn), lambda i, j, k: (i, j))],
            epilogue=epilogue, name=name)[0]

    q = proj(0, nk, BF16, functools.partial(_ep_rope_ret, scale=1.0), [cos, sin], [tab_spec, tab_spec], "ret_proj_q")
    k = proj(nk, nk, F32, functools.partial(_ep_rope_ret, scale=RET_KDIM ** -0.5), [cos, sin],
             [tab_spec, tab_spec], "ret_proj_k")
    v = proj(2 * nk, nv, BF16, _ep_cast, [], [], "ret_proj_v")
    g = proj(2 * nk + nv, nv, F32, _ep_cast, [], [], "ret_proj_g")
    gated, state = retention(q, k, v, g, gn.reshape(gn.shape[0], 1, nv), layer, state0, n_seq=n_seq, seq=seq,
                             chunk=chunk, n_tok=n_tok)
    h = _out_proj(gated, w_out, layer, h, tm=tm, name="ret_out_proj")
    return h, state


def kernel(x_prompt, x_sample, cache_cmp_kv, cache_sel_kv, state_win_kv, state_ret, page_table, norm_mix, norm_ffn,
           norm_final, nsa_w_in, nsa_w_out, nsa_cmp_pos, nsa_cmp_w1, nsa_cmp_b1, nsa_cmp_w2, ret_w_in, ret_gn,
           ret_w_out, ffn_w1, ffn_w2):
    n_seq_p, seq_p, d = x_prompt.shape
    n_seq_s, n_tok_s, _ = x_sample.shape
    depth = norm_mix.shape[0]
    n_nsa = cache_sel_kv.shape[0]
    pool, page_rows = cache_sel_kv.shape[1], cache_sel_kv.shape[2]
    past_len = page_table.shape[1] * page_rows
    assert past_len % CMP_STRIDE == 0 and n_tok_s < CMP_STRIDE and n_tok_s <= SAMPLE_PAD
    assert state_win_kv.shape[2] == WINDOW and seq_p >= WINDOW

    hp = x_prompt.reshape(n_seq_p * seq_p, d)
    hs = jnp.pad(x_sample, ((0, 0), (0, SAMPLE_PAD - n_tok_s), (0, 0))).reshape(n_seq_s * SAMPLE_PAD, d)
    m_s = n_seq_s * SAMPLE_PAD
    rows_per_tok = 2 * N_KV_HEADS
    cache_cmp = cache_cmp_kv.reshape(n_nsa, pool, page_rows // CMP_STRIDE, CMP_STRIDE, rows_per_tok, HEAD_DIM)
    cache_sel = cache_sel_kv.reshape(n_nsa, pool, page_rows * rows_per_tok, HEAD_DIM)
    win_buf = state_win_kv.reshape(n_nsa, n_seq_s, WINDOW * rows_per_tok, HEAD_DIM)
    cmp_w1 = nsa_cmp_w1.reshape(n_nsa, 2, CMP_BLOCK * HEAD_DIM, HEAD_DIM)
    kv_shape = (2, N_KV_HEADS, HEAD_DIM)

    cmp_p, cmp_s, sel_p, sel_s, win_p, win_s, ret_p, ret_s = [], [], [], [], [], [], [], []
    for layer in range(depth):
        xp = rmsnorm(hp, norm_mix[layer], BF16, 256)
        xs = rmsnorm(hs, norm_mix[layer], BF16, m_s)
        if layer % 2 == 0:
            a = layer // 2
            cmp_bias = _cmp_bias(cmp_w1, nsa_cmp_pos, nsa_cmp_b1, a)
            hp, kv_p = _nsa_prompt(hp, xp, a, nsa_w_in, nsa_w_out, cmp_w1, cmp_bias, nsa_cmp_w2, n_seq=n_seq_p,
                                   seq=seq_p)
            hs, kv_s = _nsa_sample(hs, xs, a, nsa_w_in, nsa_w_out, cmp_w1, cmp_bias, nsa_cmp_w2, cache_cmp,
                                   cache_sel, win_buf, page_table, n_seq=n_seq_s, past_len=past_len)
            kv_p = kv_p.reshape((3, n_seq_p, seq_p) + kv_shape)
            kv_s = kv_s.reshape((3, n_seq_s, SAMPLE_PAD) + kv_shape)[:, :, :n_tok_s]
            cmp_p.append(kv_p[0])
            sel_p.append(kv_p[1])
            win_p.append(kv_p[2][:, seq_p - WINDOW:])
            cmp_s.append(kv_s[0])
            sel_s.append(kv_s[1])
            win_s.append(jnp.concatenate([state_win_kv[a][:, n_tok_s:], kv_s[2]], axis=1))
        else:
            r = layer // 2
            zeros = jnp.zeros((n_seq_p, RET_HEADS, RET_KDIM, RET_VDIM), F32)
            hp, st_p = _ret_layer(hp, xp, r, ret_w_in, ret_gn, ret_w_out, zeros, jnp.arange(seq_p), n_seq=n_seq_p,
                                  seq=seq_p, chunk=RET_CHUNK, n_tok=RET_CHUNK, tm=PROMPT_ROW_TILE)
            hs, st_s = _ret_layer(hs, xs, r, ret_w_in, ret_gn, ret_w_out, state_ret[r],
                                  jnp.tile(past_len + jnp.arange(SAMPLE_PAD), n_seq_s), n_seq=n_seq_s,
                                  seq=SAMPLE_PAD, chunk=SAMPLE_PAD,
                                  n_tok=n_tok_s, tm=m_s)
            ret_p.append(st_p)
            ret_s.append(st_s)
        hp = _mlp(hp, norm_ffn[layer], ffn_w1, ffn_w2, layer, tm=PROMPT_ROW_TILE)
        hs = _mlp(hs, norm_ffn[layer], ffn_w1, ffn_w2, layer, tm=m_s)
    y_prompt = rmsnorm(hp, norm_final, F32, 256).reshape(n_seq_p, seq_p, d)
    y_sample = rmsnorm(hs, norm_final, F32, m_s).reshape(n_seq_s, SAMPLE_PAD, d)[:, :n_tok_s]
    return (y_prompt, y_sample, jnp.stack(cmp_p), jnp.stack(cmp_s), jnp.stack(sel_p), jnp.stack(sel_s),
            jnp.stack(win_p), jnp.stack(win_s), jnp.stack(ret_p), jnp.stack(ret_s))
```

```python
import functools

import numpy as np
import jax
import jax.numpy as jnp
from jax import lax
from jax.experimental import pallas as pl
from jax.experimental.pallas import tpu as pltpu

F32 = jnp.float32
BF16 = jnp.bfloat16

N_HEADS = 16
HEAD_DIM = 128
N_KV_HEADS = 4
GQA_GROUP = N_HEADS // N_KV_HEADS
KV_ROW = 2 * N_KV_HEADS * HEAD_DIM
CMP_BLOCK = 32
CMP_STRIDE = 16
SEL_BLOCK = 64
N_SEL = 16
N_LOCAL = 2
WINDOW = 512
ROPE_THETA = 10000.0
RET_HEADS = 8
RET_KDIM = 256
RET_VDIM = 512
RET_CHUNK = 128
RET_HEADS_PER_STEP = 4
RMS_EPS = 1e-6
GN_EPS = 1e-5
NEG_INF = -1e30
FORCE_SCORE = 1e9
LOG2_E = 1.4426950408889634

SUBLANES = 8
LANES = 128
VMEM_LIMIT_BYTES = 56 * 1024 * 1024
SAMPLE_PAD = 16
PROMPT_ROW_TILE = 2048
X_TILE_BYTES = 8 * 1024 * 1024

_NT = (((1,), (1,)), ((), ()))
_TN = (((0,), (0,)), ((), ()))


def _params(*semantics):
    return pltpu.CompilerParams(dimension_semantics=semantics, vmem_limit_bytes=VMEM_LIMIT_BYTES)


def _rmsnorm_kernel(x_ref, g_ref, o_ref):
    x = x_ref[...]
    ms = jnp.mean(x * x, axis=-1, keepdims=True)
    o_ref[...] = (x * lax.rsqrt(ms + RMS_EPS) * g_ref[...]).astype(o_ref.dtype)


def rmsnorm(x, g, out_dtype, tm, row_block0=0, n_blocks=None):
    m, d = x.shape
    if n_blocks is None:
        n_blocks = m // tm
    return pl.pallas_call(
        _rmsnorm_kernel,
        grid=(n_blocks,),
        in_specs=[pl.BlockSpec((tm, d), lambda i: (i + row_block0, 0)), pl.BlockSpec((1, d), lambda i: (0, 0))],
        out_specs=pl.BlockSpec((tm, d), lambda i: (i, 0)),
        out_shape=jax.ShapeDtypeStruct((n_blocks * tm, d), out_dtype),
        compiler_params=_params("parallel"),
        name="rmsnorm",
    )(x, g.reshape(1, d))


def _mm_kernel(*refs, n_x, n_extra, n_out, nk, epilogue):
    x_refs = refs[:n_x]
    w_ref = refs[n_x]
    extra = refs[n_x + 1:n_x + 1 + n_extra]
    outs = refs[n_x + 1 + n_extra:n_x + 1 + n_extra + n_out]
    x = x_refs[0][...]
    for r in x_refs[1:]:
        x = x + r[...]
    part = jnp.dot(x.astype(BF16), w_ref[...].astype(BF16), preferred_element_type=F32)
    if nk == 1:
        epilogue(part, extra, outs)
        return
    k = pl.program_id(2)
    if epilogue is _ep_residual:
        @pl.when(k == 0)
        def _():
            outs[0][...] = extra[0][...] + part

        @pl.when(k > 0)
        def _():
            outs[0][...] += part
        return
    acc_ref = refs[-1]

    @pl.when(k == 0)
    def _():
        acc_ref[...] = part

    @pl.when(k > 0)
    def _():
        acc_ref[...] += part

    @pl.when(k == nk - 1)
    def _():
        epilogue(acc_ref[...], extra, outs)


def matmul(xs, w, layer, *, col_block0, n_cols, tm, tn, tk, extras, extra_specs, out_shapes, out_specs,
           epilogue, name, column_steps_ordered=False):
    m, kdim = xs[0].shape
    nk = kdim // tk
    grid = (m // tm, pl.cdiv(n_cols, tn), nk)
    in_specs = [pl.BlockSpec((tm, tk), lambda i, j, k: (i, k)) for _ in xs]
    in_specs.append(pl.BlockSpec((None, tk, tn), lambda i, j, k: (layer, k, j + col_block0)))
    in_specs.extend(extra_specs)
    scratch = [pltpu.VMEM((tm, tn), F32)] if nk > 1 and epilogue is not _ep_residual else []
    kern = functools.partial(_mm_kernel, n_x=len(xs), n_extra=len(extras), n_out=len(out_shapes), nk=nk,
                             epilogue=epilogue)
    return pl.pallas_call(
        kern,
        grid=grid,
        in_specs=in_specs,
        out_specs=out_specs,
        out_shape=out_shapes,
        scratch_shapes=scratch,
        compiler_params=_params("parallel", "arbitrary" if column_steps_ordered else "parallel", "arbitrary"),
        name=name,
    )(*xs, w, *extras)


def _rope_half_lane(x, cos, sin_signed):
    return x * cos + pltpu.roll(x, HEAD_DIM // 2, 1) * sin_signed


def _ep_rope_q(part, extra, outs, *, scale):
    cos = extra[0][...]
    sin = extra[1][...]
    for h in range(part.shape[1] // HEAD_DIM):
        sl = slice(h * HEAD_DIM, (h + 1) * HEAD_DIM)
        r = _rope_half_lane(part[:, sl], cos, sin) * scale
        outs[0][:, sl] = r.astype(outs[0].dtype)
        outs[1][sl, :] = r.T.astype(outs[1].dtype)


def _ep_nsa_kv(part, extra, outs):
    tm = part.shape[0]
    rows_per_tok = 2 * N_KV_HEADS
    j = pl.program_id(1)

    def heads():
        return [part[:, h * HEAD_DIM:(h + 1) * HEAD_DIM] for h in range(N_KV_HEADS)]

    for br in range(3):
        @pl.when(j == 2 * br)
        def _(br=br):
            cos = extra[0][...]
            sin = extra[1][...]
            for h, x in enumerate(heads()):
                k = _rope_half_lane(x, cos, sin)
                outs[br][pl.ds(h, tm, stride=rows_per_tok), :] = k
                outs[3][h] = k.astype(BF16)

        @pl.when(j == 2 * br + 1)
        def _(br=br):
            for h, v in enumerate(heads()):
                outs[br][pl.ds(N_KV_HEADS + h, tm, stride=rows_per_tok), :] = v
                outs[4][h] = v.T.astype(BF16)


def _ep_sigmoid(part, extra, outs):
    outs[0][...] = jax.nn.sigmoid(part)


def _ep_residual(part, extra, outs):
    outs[0][...] = extra[0][...] + part


def _ep_relu2(part, extra, outs):
    h = jnp.maximum(part, 0.0)
    outs[0][...] = (h * h).astype(outs[0].dtype)


def _ep_cast(part, extra, outs):
    outs[0][...] = part.astype(outs[0].dtype)


def _ep_rope_ret(part, extra, outs, *, scale):
    cos = extra[0][...]
    sin = extra[1][...]
    for h in range(part.shape[1] // RET_KDIM):
        a = slice(h * RET_KDIM, h * RET_KDIM + LANES)
        b = slice(h * RET_KDIM + LANES, (h + 1) * RET_KDIM)
        x1 = part[:, a]
        x2 = part[:, b]
        outs[0][:, a] = ((x1 * cos - x2 * sin) * scale).astype(outs[0].dtype)
        outs[0][:, b] = ((x2 * cos + x1 * sin) * scale).astype(outs[0].dtype)


def _rope_tables(pos, half):
    inv = ROPE_THETA ** (-jnp.arange(half, dtype=F32) / half)
    ang = pos.astype(F32)[:, None] * inv[None, :]
    return jnp.cos(ang), jnp.sin(ang)


def _compress_kernel(*refs, n_pages, paged):
    if paged:
        refs = refs[1:]
    page_refs = refs[:n_pages]
    next_ref, w1_ref, bias_ref, w2_ref, out_ref, y_ref = refs[n_pages:]
    nc = page_refs[0].shape[0]
    rows_per_tok = 2 * N_KV_HEADS
    n_main = n_pages * nc * rows_per_tok
    blocks = [
        jnp.concatenate([page_refs[p][:, s].reshape(nc * rows_per_tok, HEAD_DIM) for s in range(CMP_STRIDE)], axis=1)
        for p in range(n_pages)]
    blocks.append(jnp.concatenate([next_ref[0, s] for s in range(CMP_STRIDE)], axis=1))
    x = jnp.concatenate(blocks, axis=0).astype(BF16)
    k_half = CMP_STRIDE * HEAD_DIM
    wcat = jnp.concatenate([w1_ref[0, :k_half], w1_ref[0, k_half:], w1_ref[1, :k_half], w1_ref[1, k_half:]],
                           axis=1).astype(BF16)
    per = jnp.dot(x, wcat, preferred_element_type=F32)
    hid = per.shape[1] // 4
    row = lax.broadcasted_iota(jnp.int32, (per.shape[0], 2 * hid), 0)
    is_v = jnp.bitwise_and(row, N_KV_HEADS) != 0
    per = jnp.where(is_v, per[:, 2 * hid:], per[:, :2 * hid])
    h = per[:n_main, :hid] + per[rows_per_tok:, hid:]
    is_v = is_v[:n_main, :hid]
    h = jax.nn.gelu(h + jnp.where(is_v, bias_ref[1:2, :], bias_ref[0:1, :]))
    w2cat = jnp.concatenate([w2_ref[0], w2_ref[1]], axis=1).astype(BF16)
    y2 = jnp.dot(h.astype(BF16), w2cat, preferred_element_type=F32)
    y_ref[...] = jnp.where(is_v, y2[:, hid:], y2[:, :hid])
    for eg in range(rows_per_tok):
        out_ref[eg] = y_ref[pl.ds(eg, n_main // rows_per_tok, stride=rows_per_tok), :].astype(out_ref.dtype)


def compress(kv, layer, page_table, w1, bias, w2, *, n_seq, n_chunks, n_pages):
    paged = page_table is not None
    nc = kv.shape[2]
    n_main = n_pages * nc
    steps = n_chunks // n_main
    pages_per_seq = steps * n_pages
    rows_per_tok = 2 * N_KV_HEADS

    def nxt(s):
        return jnp.minimum((s + 1) * n_pages, pages_per_seq - 1)

    if paged:
        def page_map(p):
            return lambda b, s, pt: (layer, pt[b, s * n_pages + p], 0, 0, 0, 0)

        def next_map(b, s, pt):
            return (layer, pt[b, nxt(s)], 0, 0, 0, 0)

        def fixed(*idx):
            return lambda b, s, pt: idx

        out_map = lambda b, s, pt: (b, 0, s, 0)
    else:
        def page_map(p):
            return lambda b, s: (layer, b * pages_per_seq + s * n_pages + p, 0, 0, 0, 0)

        def next_map(b, s):
            return (layer, b * pages_per_seq + nxt(s), 0, 0, 0, 0)

        def fixed(*idx):
            return lambda b, s: idx

        out_map = lambda b, s: (b, 0, s, 0)

    tile = (CMP_STRIDE, rows_per_tok, HEAD_DIM)
    in_specs = [pl.BlockSpec((None, None, nc) + tile, page_map(p)) for p in range(n_pages)]
    in_specs += [
        pl.BlockSpec((None, None, 1) + tile, next_map),
        pl.BlockSpec((None, 2, CMP_BLOCK * HEAD_DIM, HEAD_DIM), fixed(layer, 0, 0, 0)),
        pl.BlockSpec((2, HEAD_DIM), fixed(0, 0)),
        pl.BlockSpec((None, 2, HEAD_DIM, HEAD_DIM), fixed(layer, 0, 0, 0)),
    ]
    kern = functools.partial(_compress_kernel, n_pages=n_pages, paged=paged)
    grid_spec = pltpu.PrefetchScalarGridSpec(
        num_scalar_prefetch=1 if paged else 0,
        grid=(n_seq, steps),
        in_specs=in_specs,
        out_specs=pl.BlockSpec((None, rows_per_tok, n_main, HEAD_DIM), out_map),
        scratch_shapes=[pltpu.VMEM((n_main * rows_per_tok, HEAD_DIM), F32)],
    )
    args = ([page_table] if paged else []) + [kv] * (n_pages + 1) + [w1, bias, w2]
    return pl.pallas_call(
        kern,
        grid_spec=grid_spec,
        out_shape=jax.ShapeDtypeStruct((n_seq, rows_per_tok, n_chunks, HEAD_DIM), BF16),
        compiler_params=_params("parallel", "arbitrary"),
        name="nsa_compress",
    )(*args)


def _group_rows(q):
    return jnp.concatenate([q[:, h * HEAD_DIM:(h + 1) * HEAD_DIM] for h in range(GQA_GROUP)], axis=0)


def _cmp_kernel(q_ref, k_ref, v_ref, cov_ref, gate_ref, o_ref, val_ref, *, tq, pos0):
    qi = pl.program_id(2)
    q4 = _group_rows(q_ref[...])
    s = lax.dot_general(q4, k_ref[...], _NT, preferred_element_type=F32)
    row = lax.broadcasted_iota(jnp.int32, s.shape, 0)
    col = lax.broadcasted_iota(jnp.int32, s.shape, 1)
    qpos = pos0 + qi * tq + jnp.bitwise_and(row, tq - 1)
    valid = col * CMP_STRIDE + (CMP_BLOCK - 1) <= qpos
    s = jnp.where(valid, s, NEG_INF)
    m = jnp.max(s, axis=1, keepdims=True)
    e = jnp.where(valid, jnp.exp2(s - m), 0.0)
    l = jnp.sum(e, axis=1, keepdims=True)
    p = e / jnp.where(l > 0.0, l, 1.0)
    o4 = jnp.dot(p.astype(BF16), v_ref[...], preferred_element_type=F32)
    for h in range(GQA_GROUP):
        o_ref[:, h * HEAD_DIM:(h + 1) * HEAD_DIM] = o4[h * tq:(h + 1) * tq] * gate_ref[:, h:h + 1]
    imp = p[0:tq] + p[tq:2 * tq] + p[2 * tq:3 * tq] + p[3 * tq:4 * tq]
    hi = imp.astype(BF16)
    r1 = imp - hi.astype(F32)
    mid = r1.astype(BF16)
    lo = (r1 - mid.astype(F32)).astype(BF16)
    cov = cov_ref[...]
    score = (lax.dot_general(cov, hi, _NT, preferred_element_type=F32)
             + lax.dot_general(cov, mid, _NT, preferred_element_type=F32)
             + lax.dot_general(cov, lo, _NT, preferred_element_type=F32))
    jb = lax.broadcasted_iota(jnp.int32, score.shape, 0)
    qp = pos0 + qi * tq + lax.broadcasted_iota(jnp.int32, score.shape, 1)
    cur = jnp.right_shift(qp, SEL_BLOCK.bit_length() - 1)
    visible = jb <= cur
    forced = (jb == 0) | (visible & (jb > cur - N_LOCAL))
    val_ref[...] = jnp.where(forced, FORCE_SCORE, jnp.where(visible, score, -FORCE_SCORE))


def cmp_attention(q, kc, cover_t, gates_t, *, n_seq, seq, tq, pos0):
    nq = seq // tq
    ncp = kc.shape[2]
    nbv = cover_t.shape[0]
    gw = GQA_GROUP * HEAD_DIM
    kern = functools.partial(_cmp_kernel, tq=tq, pos0=pos0)
    if tq % LANES == 0:
        val_spec = pl.BlockSpec((nbv, tq), lambda b, g, i: (0, (b * N_KV_HEADS + g) * nq + i))
        val_shape = jax.ShapeDtypeStruct((nbv, n_seq * N_KV_HEADS * seq), F32)
    else:
        val_spec = pl.BlockSpec((None, None, nbv, tq), lambda b, g, i: (b, g, 0, i))
        val_shape = jax.ShapeDtypeStruct((n_seq, N_KV_HEADS, nbv, seq), F32)
    return pl.pallas_call(
        kern,
        grid=(n_seq, N_KV_HEADS, nq),
        in_specs=[
            pl.BlockSpec((tq, gw), lambda b, g, i: (b * nq + i, g)),
            pl.BlockSpec((None, None, ncp, HEAD_DIM), lambda b, g, i: (b, g, 0, 0)),
            pl.BlockSpec((None, None, ncp, HEAD_DIM), lambda b, g, i: (b, N_KV_HEADS + g, 0, 0)),
            pl.BlockSpec((nbv, ncp), lambda b, g, i: (0, 0)),
            pl.BlockSpec((None, None, tq, GQA_GROUP), lambda b, g, i: (0, g, b * nq + i, 0)),
        ],
        out_specs=[pl.BlockSpec((tq, gw), lambda b, g, i: (b * nq + i, g)), val_spec],
        out_shape=[jax.ShapeDtypeStruct((n_seq * seq, N_HEADS * HEAD_DIM), F32), val_shape],
        compiler_params=_params("parallel", "parallel", "parallel"),
        name="nsa_cmp_attention",
    )(q, kc, kc, cover_t, gates_t)


def _cover_t(n_chunks, n_blk, nbv):
    n_cmp = n_chunks - CMP_BLOCK // CMP_STRIDE + 1
    c0 = np.arange(n_chunks) * CMP_STRIDE
    j0 = np.arange(nbv) * SEL_BLOCK
    cov = (c0[None, :] < j0[:, None] + SEL_BLOCK) & (c0[None, :] + CMP_BLOCK > j0[:, None])
    cov &= (np.arange(n_chunks)[None, :] < n_cmp) & (np.arange(nbv)[:, None] < n_blk)
    return jnp.asarray(cov.astype(np.float32), dtype=BF16)


def _rank_kernel(val_ref, out_ref, *, n_blk, n_sel, causal_seq):
    val = val_ref[...]
    tc = val.shape[1]
    jb = lax.broadcasted_iota(jnp.int32, val.shape, 0)
    if causal_seq is None:
        n_iter = n_blk
    else:
        pos0 = (pl.program_id(0) % (causal_seq // tc)) * tc
        n_iter = jnp.minimum(n_blk, (pos0 + tc - 1) // SEL_BLOCK + 1)

    def body(i, rank):
        row = val_ref[pl.ds(i, 1), :]
        beats = (row > val) | ((row == val) & (i < jb))
        return rank + jnp.where(beats, 1, 0)

    rank = lax.fori_loop(0, n_iter, body, jnp.zeros(val.shape, jnp.int32))
    chosen = (rank < n_sel) & (val > -0.5 * FORCE_SCORE)
    out_ref[...] = jnp.where(chosen, 0.0, NEG_INF).astype(out_ref.dtype)


def select_bias(val, *, n_blk, tc, causal_seq=None):
    nbv, ncols = val.shape
    kern = functools.partial(_rank_kernel, n_blk=n_blk, n_sel=min(N_SEL, n_blk), causal_seq=causal_seq)
    return pl.pallas_call(
        kern,
        grid=(ncols // tc,),
        in_specs=[pl.BlockSpec((nbv, tc), lambda i: (0, i))],
        out_specs=pl.BlockSpec((nbv, tc), lambda i: (0, i)),
        out_shape=jax.ShapeDtypeStruct((nbv, ncols), BF16),
        compiler_params=_params("parallel"),
        name="nsa_select_rank",
    )(val)


def _flash_kernel(*refs, tq, tk, nsteps, use_bias, window, out_dtype):
    if use_bias:
        qt_ref, bias_ref, k_ref, vt_ref, e_ref, gate_ref, prev_ref, o_ref, qa_ref, m_ref, acc_ref = refs
    else:
        qt_ref, k_ref, vt_ref, gate_ref, prev_ref, o_ref, qa_ref, m_ref, acc_ref = refs
    qi = pl.program_id(2)
    ki = pl.program_id(3)
    if use_bias:
        kt = ki
        last = (qi * tq + tq - 1) // tk
        active = ki <= last
        edge = ki == last
    else:
        kt = qi * (tq // tk) - window // tk + ki
        active = kt >= 0
        edge = (ki < tq // tk) | (ki >= window // tk)

    @pl.when(ki == 0)
    def _():
        for h in range(GQA_GROUP):
            qa_ref[0:HEAD_DIM, h * tq:(h + 1) * tq] = qt_ref[h * HEAD_DIM:(h + 1) * HEAD_DIM, :]
            if use_bias:
                nbv = bias_ref.shape[0]
                qa_ref[HEAD_DIM:HEAD_DIM + nbv, h * tq:(h + 1) * tq] = bias_ref[...]
                if HEAD_DIM + nbv < qa_ref.shape[0]:
                    qa_ref[HEAD_DIM + nbv:, h * tq:(h + 1) * tq] = jnp.zeros(
                        (qa_ref.shape[0] - HEAD_DIM - nbv, tq), BF16)
        m_ref[...] = jnp.full(m_ref.shape, NEG_INF, F32)
        acc_ref[...] = jnp.zeros(acc_ref.shape, F32)

    def step(masked):
        if use_bias:
            ka = jnp.concatenate([k_ref[...], e_ref[...]], axis=1)
        else:
            ka = k_ref[...]
        st = jnp.dot(ka, qa_ref[...], preferred_element_type=F32)
        if masked:
            kpos = kt * tk + lax.broadcasted_iota(jnp.int32, st.shape, 0)
            qpos = qi * tq + jnp.bitwise_and(lax.broadcasted_iota(jnp.int32, st.shape, 1), tq - 1)
            d = qpos - kpos
            ok = d >= 0
            if window is not None:
                ok = ok & (d < window)
            st = jnp.where(ok, st, NEG_INF)
        m_old = m_ref[...]
        m_new = jnp.maximum(m_old, jnp.max(st, axis=0, keepdims=True))
        alpha = jnp.exp2(m_old - m_new)
        p = jnp.exp2(st - m_new).astype(BF16)
        va = jnp.concatenate([vt_ref[...], jnp.ones((acc_ref.shape[0] - HEAD_DIM, tk), BF16)], axis=0)
        acc_ref[...] = alpha * acc_ref[...] + jnp.dot(va, p, preferred_element_type=F32)
        m_ref[...] = m_new

    @pl.when(active & edge)
    def _():
        step(True)

    @pl.when(active & jnp.logical_not(edge))
    def _():
        step(False)

    @pl.when(ki == nsteps - 1)
    def _():
        acc = acc_ref[...]
        ot = acc[:HEAD_DIM] / acc[HEAD_DIM:HEAD_DIM + 1]
        for h in range(GQA_GROUP):
            sl = slice(h * HEAD_DIM, (h + 1) * HEAD_DIM)
            o_h = ot[:, h * tq:(h + 1) * tq].T
            o_ref[:, sl] = (prev_ref[:, sl] + o_h * gate_ref[:, h:h + 1]).astype(out_dtype)


def flash_attention(qt, k, vt, branch, gates_t, prev, *, n_seq, seq, tq, tk, bias=None, onehot=None, window=None,
                    out_dtype=F32):
    use_bias = bias is not None
    nq = seq // tq
    nkt = seq // tk
    gw = GQA_GROUP * HEAD_DIM
    if use_bias:
        nsteps = nkt

        def kt_of(i, k):
            return jnp.minimum(k, (i * tq + tq - 1) // tk)
    else:
        nsteps = window // tk + tq // tk

        def kt_of(i, k):
            return jnp.maximum(i * (tq // tk) - window // tk + k, 0)

    o_spec = pl.BlockSpec((tq, gw), lambda b, g, i, k: (b * nq + i, g))
    qt_spec = pl.BlockSpec((gw, tq), lambda b, g, i, k: (g, b * nq + i))
    k_spec = pl.BlockSpec((None, None, tk, HEAD_DIM), lambda b, g, i, k: (branch, g, b * nkt + kt_of(i, k), 0))
    vt_spec = pl.BlockSpec((None, None, HEAD_DIM, tk), lambda b, g, i, k: (branch, g, 0, b * nkt + kt_of(i, k)))
    gate_spec = pl.BlockSpec((None, None, tq, GQA_GROUP), lambda b, g, i, k: (branch, g, b * nq + i, 0))
    in_specs = [qt_spec]
    args = [qt]
    qa_rows = HEAD_DIM
    if use_bias:
        nbv = bias.shape[0]
        nbp = onehot.shape[1]
        qa_rows += nbp
        in_specs.append(pl.BlockSpec((nbv, tq), lambda b, g, i, k: (0, (b * N_KV_HEADS + g) * nq + i)))
        args.append(bias)
    in_specs += [k_spec, vt_spec]
    args += [k, vt]
    if use_bias:
        in_specs.append(pl.BlockSpec((tk, nbp), lambda b, g, i, k: (kt_of(i, k), 0)))
        args.append(onehot)
    in_specs += [gate_spec, o_spec]
    args += [gates_t, prev]
    kern = functools.partial(_flash_kernel, tq=tq, tk=tk, nsteps=nsteps, use_bias=use_bias, window=window,
                             out_dtype=out_dtype)
    cols = GQA_GROUP * tq
    return pl.pallas_call(
        kern,
        grid=(n_seq, N_KV_HEADS, nq, nsteps),
        in_specs=in_specs,
        out_specs=o_spec,
        out_shape=jax.ShapeDtypeStruct((n_seq * seq, N_HEADS * HEAD_DIM), out_dtype),
        scratch_shapes=[
            pltpu.VMEM((qa_rows, cols), BF16),
            pltpu.VMEM((1, cols), F32),
            pltpu.VMEM((HEAD_DIM + 2 * SUBLANES, cols), F32),
        ],
        compiler_params=_params("parallel", "parallel", "parallel", "arbitrary"),
        name="nsa_sel_attention" if use_bias else "nsa_win_attention",
    )(*args)


def _block_onehot(n_keys, nbp):
    e = (np.arange(n_keys)[:, None] // SEL_BLOCK) == np.arange(nbp)[None, :]
    return jnp.asarray(e.astype(np.float32), dtype=BF16)


def _sel_sample_kernel(*refs, n_pages, nbp, past_len, nsteps, tq):
    logical_ref, count_ref = refs[1:3]
    q_ref, bias_ref, new_ref = refs[3:6]
    page_refs = refs[6:6 + n_pages]
    gate_ref, prev_ref, o_ref, qa_ref, m_ref, l_ref, acc_ref = refs[6 + n_pages:]
    b = pl.program_id(0)
    step = pl.program_id(1)
    page_rows = page_refs[0].shape[0]
    rows_per_tok = 2 * N_KV_HEADS
    page_tokens = page_rows // rows_per_tok
    blk_shift = SEL_BLOCK.bit_length() - 1
    n_needed = count_ref[b]

    def block_onehot(block_id):
        j = lax.broadcasted_iota(jnp.int32, (block_id.shape[0], nbp), 1)
        return jnp.where(block_id == j, 1.0, 0.0).astype(BF16)

    def scores(g, k, onehot):
        return lax.dot_general(qa_ref[g], jnp.concatenate([k, onehot], axis=1), _NT, preferred_element_type=F32)

    @pl.when(step == 0)
    def _():
        q = q_ref[...]
        new_onehot = block_onehot(
            jnp.right_shift(past_len + lax.broadcasted_iota(jnp.int32, (tq, 1), 0), blk_shift))
        for g in range(N_KV_HEADS):
            for h in range(GQA_GROUP):
                hh = g * GQA_GROUP + h
                qa_ref[g, h * tq:(h + 1) * tq, 0:HEAD_DIM] = q[:, hh * HEAD_DIM:(hh + 1) * HEAD_DIM]
                qa_ref[g, h * tq:(h + 1) * tq, HEAD_DIM:] = bias_ref[g]
            kn = new_ref[pl.ds(g, tq, stride=2 * N_KV_HEADS), :].astype(BF16)
            vn = new_ref[pl.ds(N_KV_HEADS + g, tq, stride=2 * N_KV_HEADS), :].astype(BF16)
            s = scores(g, kn, new_onehot)
            row = lax.broadcasted_iota(jnp.int32, s.shape, 0)
            col = lax.broadcasted_iota(jnp.int32, s.shape, 1)
            s = jnp.where(col <= jnp.bitwise_and(row, tq - 1), s, NEG_INF)
            m = jnp.max(s, axis=1, keepdims=True)
            p = jnp.exp2(s - m)
            m_ref[g] = m
            l_ref[g] = jnp.sum(p, axis=1, keepdims=True)
            acc_ref[g] = jnp.dot(p.astype(BF16), vn, preferred_element_type=F32)

    @pl.when(step * n_pages < n_needed)
    def _():
        r = lax.broadcasted_iota(jnp.int32, (page_tokens, 1), 0)
        ids = []
        for p in range(n_pages):
            slot = step * n_pages + p
            first_block = logical_ref[b, slot] * (page_tokens // SEL_BLOCK)
            ids.append(jnp.where(slot < n_needed, first_block + jnp.right_shift(r, blk_shift), nbp - 1))
        onehot = block_onehot(jnp.concatenate(ids, axis=0))
        for g in range(N_KV_HEADS):
            k = jnp.concatenate(
                [page_refs[p][pl.ds(g, page_tokens, stride=rows_per_tok), :] for p in range(n_pages)],
                axis=0).astype(BF16)
            v = jnp.concatenate(
                [page_refs[p][pl.ds(N_KV_HEADS + g, page_tokens, stride=rows_per_tok), :] for p in range(n_pages)],
                axis=0).astype(BF16)
            s = scores(g, k, onehot)
            m_old = m_ref[g]
            m_new = jnp.maximum(m_old, jnp.max(s, axis=1, keepdims=True))
            alpha = jnp.exp2(m_old - m_new)
            p = jnp.exp2(s - m_new)
            l_ref[g] = alpha * l_ref[g] + jnp.sum(p, axis=1, keepdims=True)
            acc_ref[g] = alpha * acc_ref[g] + jnp.dot(p.astype(BF16), v, preferred_element_type=F32)
            m_ref[g] = m_new

    @pl.when(step == nsteps - 1)
    def _():
        for g in range(N_KV_HEADS):
            o4 = acc_ref[g] / l_ref[g]
            for h in range(GQA_GROUP):
                hh = g * GQA_GROUP + h
                sl = slice(hh * HEAD_DIM, (hh + 1) * HEAD_DIM)
                o_ref[:, sl] = prev_ref[:, sl] + o4[h * tq:(h + 1) * tq] * gate_ref[g, :, h:h + 1]


def sel_attention_sample(q, bias, needed, kv_new, cache, layer, page_table, gates_t, prev, *, n_pages_step,
                         past_len):
    n_seq, pages = page_table.shape
    tq = SAMPLE_PAD
    nbp = bias.shape[-1]
    page_rows = cache.shape[2]
    nsteps = pages // n_pages_step
    dm = N_HEADS * HEAD_DIM
    order = jnp.argsort(jnp.logical_not(needed), axis=1, stable=True).astype(jnp.int32)
    count = jnp.sum(needed, axis=1).astype(jnp.int32)
    slot = jnp.minimum(jnp.arange(pages, dtype=jnp.int32)[None, :], count[:, None] - 1)
    logical = jnp.take_along_axis(order, slot, axis=1)
    physical = jnp.take_along_axis(page_table, logical, axis=1)
    row_spec = pl.BlockSpec((tq, dm), lambda b, s, *_: (b, 0))

    def page_map(p):
        return lambda b, s, phys, *_: (layer, phys[b, s * n_pages_step + p], 0, 0)

    in_specs = [
        row_spec,
        pl.BlockSpec((None, N_KV_HEADS, tq, nbp), lambda b, s, *_: (b, 0, 0, 0)),
        pl.BlockSpec((tq * 2 * N_KV_HEADS, HEAD_DIM), lambda b, s, *_: (b, 0)),
    ]
    in_specs += [pl.BlockSpec((None, None, page_rows, HEAD_DIM), page_map(p)) for p in range(n_pages_step)]
    in_specs += [
        pl.BlockSpec((None, N_KV_HEADS, tq, GQA_GROUP), lambda b, s, *_: (1, 0, b, 0)),
        row_spec,
    ]
    rows = GQA_GROUP * tq
    kern = functools.partial(_sel_sample_kernel, n_pages=n_pages_step, nbp=nbp, past_len=past_len, nsteps=nsteps,
                             tq=tq)
    grid_spec = pltpu.PrefetchScalarGridSpec(
        num_scalar_prefetch=3,
        grid=(n_seq, nsteps),
        in_specs=in_specs,
        out_specs=row_spec,
        scratch_shapes=[
            pltpu.VMEM((N_KV_HEADS, rows, HEAD_DIM + nbp), BF16),
            pltpu.VMEM((N_KV_HEADS, rows, 1), F32),
            pltpu.VMEM((N_KV_HEADS, rows, 1), F32),
            pltpu.VMEM((N_KV_HEADS, rows, HEAD_DIM), F32),
        ],
    )
    return pl.pallas_call(
        kern,
        grid_spec=grid_spec,
        out_shape=jax.ShapeDtypeStruct((n_seq * tq, dm), F32),
        compiler_params=_params("parallel", "arbitrary"),
        name="nsa_sel_attention_sample",
    )(physical, logical, count, q, bias, kv_new, *([cache] * n_pages_step), gates_t, prev)


def _win_sample_kernel(q_ref, buf_ref, new_ref, gate_ref, prev_ref, o_ref, *, tq, out_dtype):
    rows_per_tok = 2 * N_KV_HEADS
    w_buf = buf_ref.shape[0] // rows_per_tok
    gw = GQA_GROUP * HEAD_DIM
    for g in range(N_KV_HEADS):
        q4 = _group_rows(q_ref[:, g * gw:(g + 1) * gw])
        kb = buf_ref[pl.ds(g, w_buf, stride=rows_per_tok), :].astype(BF16)
        vb = buf_ref[pl.ds(N_KV_HEADS + g, w_buf, stride=rows_per_tok), :].astype(BF16)
        kn = new_ref[pl.ds(g, tq, stride=rows_per_tok), :].astype(BF16)
        vn = new_ref[pl.ds(N_KV_HEADS + g, tq, stride=rows_per_tok), :].astype(BF16)
        sb = lax.dot_general(q4, kb, _NT, preferred_element_type=F32)
        sn = lax.dot_general(q4, kn, _NT, preferred_element_type=F32)
        tb = jnp.bitwise_and(lax.broadcasted_iota(jnp.int32, sb.shape, 0), tq - 1)
        rb = lax.broadcasted_iota(jnp.int32, sb.shape, 1)
        sb = jnp.where(tb + w_buf - rb < WINDOW, sb, NEG_INF)
        tn = jnp.bitwise_and(lax.broadcasted_iota(jnp.int32, sn.shape, 0), tq - 1)
        rn = lax.broadcasted_iota(jnp.int32, sn.shape, 1)
        sn = jnp.where(rn <= tn, sn, NEG_INF)
        m = jnp.maximum(jnp.max(sb, axis=1, keepdims=True), jnp.max(sn, axis=1, keepdims=True))
        pb = jnp.exp2(sb - m)
        pn = jnp.exp2(sn - m)
        l = jnp.sum(pb, axis=1, keepdims=True) + jnp.sum(pn, axis=1, keepdims=True)
        o4 = (jnp.dot(pb.astype(BF16), vb, preferred_element_type=F32)
              + jnp.dot(pn.astype(BF16), vn, preferred_element_type=F32)) / l
        for h in range(GQA_GROUP):
            hh = g * GQA_GROUP + h
            sl = slice(hh * HEAD_DIM, (hh + 1) * HEAD_DIM)
            o_ref[:, sl] = (prev_ref[:, sl] + o4[h * tq:(h + 1) * tq] * gate_ref[g, :, h:h + 1]).astype(out_dtype)


def win_attention_sample(q, win_buf, layer, kv_new, gates_t, prev, *, out_dtype):
    n_seq, buf_rows = win_buf.shape[1], win_buf.shape[2]
    tq = SAMPLE_PAD
    q_spec = pl.BlockSpec((tq, N_HEADS * HEAD_DIM), lambda b: (b, 0))
    kern = functools.partial(_win_sample_kernel, tq=tq, out_dtype=out_dtype)
    return pl.pallas_call(
        kern,
        grid=(n_seq,),
        in_specs=[
            q_spec,
            pl.BlockSpec((None, None, buf_rows, HEAD_DIM), lambda b: (layer, b, 0, 0)),
            pl.BlockSpec((tq * 2 * N_KV_HEADS, HEAD_DIM), lambda b: (b, 0)),
            pl.BlockSpec((None, N_KV_HEADS, tq, GQA_GROUP), lambda b: (2, 0, b, 0)),
            q_spec,
        ],
        out_specs=q_spec,
        out_shape=jax.ShapeDtypeStruct((n_seq * tq, N_HEADS * HEAD_DIM), out_dtype),
        compiler_params=_params("parallel"),
        name="nsa_win_attention_sample",
    )(q, win_buf, kv_new, gates_t, prev)


def _ret_kernel(q_ref, k_ref, v_ref, g_ref, gn_ref, dec_ref, qd_ref, kd_ref, gc_ref, s0_ref, o_ref, sout_ref,
                state_ref, *, n_chunks):
    c = pl.program_id(2)

    @pl.when(c == 0)
    def _():
        state_ref[...] = s0_ref[...]

    for h in range(state_ref.shape[0]):
        ksl = slice(h * RET_KDIM, (h + 1) * RET_KDIM)
        vsl = slice(h * RET_VDIM, (h + 1) * RET_VDIM)
        q = q_ref[:, ksl]
        kf = k_ref[:, ksl]
        v = v_ref[:, vsl]
        state = state_ref[h]
        s = lax.dot_general(q, kf.astype(BF16), _NT, preferred_element_type=F32) * dec_ref[h]
        inner = jnp.dot(s.astype(BF16), v, preferred_element_type=F32)
        cross = jnp.dot(q, state.astype(BF16), preferred_element_type=F32) * qd_ref[h]
        o = inner + cross
        kd = (kf * kd_ref[h]).astype(BF16)
        state_ref[h] = gc_ref[h] * state + lax.dot_general(kd, v, _TN, preferred_element_type=F32)
        mu = jnp.mean(o, axis=-1, keepdims=True)
        d = o - mu
        var = jnp.mean(d * d, axis=-1, keepdims=True)
        on = d * lax.rsqrt(var + GN_EPS) * gn_ref[:, vsl]
        g = g_ref[:, vsl]
        o_ref[:, vsl] = (g * jax.nn.sigmoid(g) * on).astype(o_ref.dtype)

    @pl.when(c == n_chunks - 1)
    def _():
        sout_ref[...] = state_ref[...]


def _ret_tables(chunk, n_tok):
    log_g = jnp.log(1.0 - 2.0 ** (-5.0 - jnp.arange(RET_HEADS, dtype=F32)))
    i = jnp.arange(chunk, dtype=F32)
    rel = i[:, None] - i[None, :]
    decay = jnp.where(rel >= 0, jnp.exp(jnp.maximum(rel, 0.0)[None] * log_g[:, None, None]), 0.0)
    q_dec = jnp.exp((i[None, :] + 1.0) * log_g[:, None])[..., None]
    k_dec = jnp.where(i[None, :] < n_tok, jnp.exp((n_tok - 1.0 - i)[None, :] * log_g[:, None]), 0.0)[..., None]
    g_c = jnp.exp(n_tok * log_g)[:, None, None]
    return decay, q_dec, k_dec, g_c


def retention(q, k, v, g, gn, layer, state0, *, n_seq, seq, chunk, n_tok):
    nc = seq // chunk
    hps = RET_HEADS_PER_STEP
    decay, q_dec, k_dec, g_c = _ret_tables(chunk, n_tok)
    kern = functools.partial(_ret_kernel, n_chunks=nc)
    tok_k = pl.BlockSpec((chunk, hps * RET_KDIM), lambda b, h, c: (b * nc + c, h))
    tok_v = pl.BlockSpec((chunk, hps * RET_VDIM), lambda b, h, c: (b * nc + c, h))
    st_spec = pl.BlockSpec((None, hps, RET_KDIM, RET_VDIM), lambda b, h, c: (b, h, 0, 0))
    return pl.pallas_call(
        kern,
        grid=(n_seq, RET_HEADS // hps, nc),
        in_specs=[
            tok_k, tok_k, tok_v, tok_v,
            pl.BlockSpec((None, 1, hps * RET_VDIM), lambda b, h, c: (layer, 0, h)),
            pl.BlockSpec((hps, chunk, chunk), lambda b, h, c: (h, 0, 0)),
            pl.BlockSpec((hps, chunk, 1), lambda b, h, c: (h, 0, 0)),
            pl.BlockSpec((hps, chunk, 1), lambda b, h, c: (h, 0, 0)),
            pl.BlockSpec((hps, 1, 1), lambda b, h, c: (h, 0, 0)),
            st_spec,
        ],
        out_specs=[tok_v, st_spec],
        out_shape=[
            jax.ShapeDtypeStruct((n_seq * seq, RET_HEADS * RET_VDIM), BF16),
            jax.ShapeDtypeStruct((n_seq, RET_HEADS, RET_KDIM, RET_VDIM), F32),
        ],
        scratch_shapes=[pltpu.VMEM((hps, RET_KDIM, RET_VDIM), F32)],
        compiler_params=_params("parallel", "parallel", "arbitrary"),
        name="retention_chunk",
    )(q, k, v, g, gn, decay, q_dec, k_dec, g_c, state0)


def _project_nsa(xn, w_in, layer, cos, sin_signed, *, tm):
    m, d = xn.shape
    nq = N_HEADS * HEAD_DIM
    half = N_KV_HEADS * HEAD_DIM
    rows_per_tok = 2 * N_KV_HEADS
    tabs_per_seq = cos.shape[0] // tm
    tab_spec = pl.BlockSpec((tm, HEAD_DIM), lambda i, j, k: (i % tabs_per_seq, 0))
    tn_q = 512
    q = matmul(
        [xn], w_in, layer, col_block0=0, n_cols=nq, tm=tm, tn=tn_q, tk=d,
        extras=[cos, sin_signed], extra_specs=[tab_spec, tab_spec],
        out_shapes=[jax.ShapeDtypeStruct((m, nq), BF16), jax.ShapeDtypeStruct((nq, m), BF16)],
        out_specs=[pl.BlockSpec((tm, tn_q), lambda i, j, k: (i, j)), pl.BlockSpec((tn_q, tm), lambda i, j, k: (j, i))],
        epilogue=functools.partial(_ep_rope_q, scale=HEAD_DIM ** -0.5 * LOG2_E), name="nsa_proj_q")
    kv = matmul(
        [xn], w_in, layer, col_block0=nq // half, n_cols=3 * KV_ROW, tm=tm, tn=half, tk=d,
        extras=[cos, sin_signed], extra_specs=[tab_spec, tab_spec],
        out_shapes=[jax.ShapeDtypeStruct((m * rows_per_tok, HEAD_DIM), F32)] * 3
        + [jax.ShapeDtypeStruct((3, N_KV_HEADS, m, HEAD_DIM), BF16),
           jax.ShapeDtypeStruct((3, N_KV_HEADS, HEAD_DIM, m), BF16)],
        out_specs=[pl.BlockSpec((tm * rows_per_tok, HEAD_DIM), lambda i, j, k: (i, 0))] * 3
        + [pl.BlockSpec((None, N_KV_HEADS, tm, HEAD_DIM), lambda i, j, k: (j // 2, 0, i, 0)),
           pl.BlockSpec((None, N_KV_HEADS, HEAD_DIM, tm), lambda i, j, k: (j // 2, 0, 0, i))],
        epilogue=_ep_nsa_kv, name="nsa_proj_kv", column_steps_ordered=True)
    kv = (kv[:3], kv[3], kv[4])
    n_gate = 3 * N_HEADS
    w_gate = jnp.pad(w_in[layer, :, nq + 3 * KV_ROW:], ((0, 0), (0, LANES - n_gate)))[None]
    gates = matmul(
        [xn], w_gate, 0, col_block0=0, n_cols=LANES, tm=tm, tn=LANES, tk=d, extras=[], extra_specs=[],
        out_shapes=[jax.ShapeDtypeStruct((m, LANES), F32)],
        out_specs=[pl.BlockSpec((tm, LANES), lambda i, j, k: (i, 0))],
        epilogue=_ep_sigmoid, name="nsa_proj_gate")[0]
    gates_t = gates[:, :n_gate].reshape(m, 3, N_KV_HEADS, GQA_GROUP).transpose(1, 2, 0, 3)
    return q, kv, gates_t


def _cmp_bias(w1, pos, b1, layer):
    k_full = CMP_BLOCK * HEAD_DIM
    pos_rows = jnp.zeros((2, SUBLANES, k_full), F32).at[:, 0].set(pos[layer].reshape(2, k_full))
    terms = []
    for e in range(2):
        t = matmul(
            [pos_rows[e]], w1.reshape(-1, k_full, HEAD_DIM), layer * 2 + e, col_block0=0, n_cols=HEAD_DIM,
            tm=SUBLANES, tn=HEAD_DIM, tk=k_full, extras=[], extra_specs=[],
            out_shapes=[jax.ShapeDtypeStruct((SUBLANES, HEAD_DIM), F32)],
            out_specs=[pl.BlockSpec((SUBLANES, HEAD_DIM), lambda i, j, k: (0, 0))],
            epilogue=_ep_cast, name="nsa_cmp_pos_term")[0]
        terms.append(t[0])
    return jnp.stack(terms) + b1[layer]


def _mlp(h, norm_g, w1, w2, layer, *, tm):
    m, d = h.shape
    d_ff = w1.shape[-1]
    xn = rmsnorm(h, norm_g, BF16, min(tm, 256))
    tn1 = 1024
    a = matmul(
        [xn], w1, layer, col_block0=0, n_cols=d_ff, tm=tm, tn=tn1, tk=d, extras=[], extra_specs=[],
        out_shapes=[jax.ShapeDtypeStruct((m, d_ff), BF16)],
        out_specs=[pl.BlockSpec((tm, tn1), lambda i, j, k: (i, j))],
        epilogue=_ep_relu2, name="mlp_up")[0]
    tn2 = 1024
    tm2 = min(tm, 1024)
    return matmul(
        [a], w2, layer, col_block0=0, n_cols=d, tm=tm2, tn=tn2, tk=1024, extras=[h],
        extra_specs=[pl.BlockSpec((tm2, tn2), lambda i, j, k: (i, j))],
        out_shapes=[jax.ShapeDtypeStruct((m, d), F32)],
        out_specs=[pl.BlockSpec((tm2, tn2), lambda i, j, k: (i, j))],
        epilogue=_ep_residual, name="mlp_down")[0]


def _out_proj(o, w_out, layer, h, *, tm, name):
    m, d = h.shape
    tn = 512
    tm = min(tm, X_TILE_BYTES // (o.shape[1] * o.dtype.itemsize))
    return matmul(
        [o], w_out, layer, col_block0=0, n_cols=d, tm=tm, tn=tn, tk=o.shape[1], extras=[h],
        extra_specs=[pl.BlockSpec((tm, tn), lambda i, j, k: (i, j))],
        out_shapes=[jax.ShapeDtypeStruct((m, d), F32)],
        out_specs=[pl.BlockSpec((tm, tn), lambda i, j, k: (i, j))],
        epilogue=_ep_residual, name=name)[0]


def _nsa_tables(pos):
    cos, sin = _rope_tables(pos, HEAD_DIM // 2)
    return jnp.concatenate([cos, cos], axis=1), jnp.concatenate([-sin, sin], axis=1)


def _nsa_prompt(h, xn, layer, w_in, w_out, cmp_w1, cmp_bias, cmp_w2, *, n_seq, seq):
    tm = PROMPT_ROW_TILE
    cos, sin_signed = _nsa_tables(jnp.arange(seq))
    (q, qt), (kv_rows, k_bf, vt_bf), gates_t = _project_nsa(xn, w_in, layer, cos, sin_signed, tm=tm // 2)
    n_chunks = seq // CMP_STRIDE
    chunks_per_step = 128
    kv_chunks = kv_rows[0].reshape(1, (n_seq * n_chunks) // chunks_per_step, chunks_per_step, CMP_STRIDE,
                                   2 * N_KV_HEADS, HEAD_DIM)
    kc = compress(kv_chunks, 0, None, cmp_w1, cmp_bias, cmp_w2, n_seq=n_seq, n_chunks=n_chunks, n_pages=1)
    n_blk = seq // SEL_BLOCK
    cover_t = _cover_t(n_chunks, n_blk, n_blk)
    o1, val = cmp_attention(q, kc, cover_t, gates_t, n_seq=n_seq, seq=seq, tq=256, pos0=0)
    bias_t = select_bias(val, n_blk=n_blk, tc=512, causal_seq=seq)
    nbp = pl.cdiv(n_blk, LANES) * LANES
    o2 = flash_attention(qt, k_bf, vt_bf, 1, gates_t, o1, n_seq=n_seq, seq=seq, tq=256, tk=512, bias=bias_t,
                         onehot=_block_onehot(seq, nbp))
    o3 = flash_attention(qt, k_bf, vt_bf, 2, gates_t, o2, n_seq=n_seq, seq=seq, tq=256, tk=256, window=WINDOW,
                         out_dtype=BF16)
    h = _out_proj(o3, w_out, layer, h, tm=tm, name="nsa_out_proj")
    return h, kv_rows


def _nsa_sample(h, xn, layer, w_in, w_out, cmp_w1, cmp_bias, cmp_w2, cache_cmp, cache_sel, win_buf, page_table,
                *, n_seq, n_tok, past_len):
    tq = SAMPLE_PAD
    m = n_seq * tq
    cos, sin_signed = _nsa_tables(jnp.tile(past_len + jnp.arange(tq), n_seq))
    (q, _), (kv_rows, _, _), gates_t = _project_nsa(xn, w_in, layer, cos, sin_signed, tm=m)
    n_chunks = past_len // CMP_STRIDE
    kc = compress(cache_cmp, layer, page_table, cmp_w1, cmp_bias, cmp_w2, n_seq=n_seq, n_chunks=n_chunks,
                  n_pages=16)
    n_blk = pl.cdiv(past_len + n_tok, SEL_BLOCK)
    nbv = pl.cdiv(n_blk, SUBLANES) * SUBLANES
    cover_t = _cover_t(n_chunks, n_blk, nbv)
    o1, val = cmp_attention(q, kc, cover_t, gates_t, n_seq=n_seq, seq=tq, tq=tq, pos0=past_len)
    ncols = n_seq * N_KV_HEADS * n_tok
    bias_t = select_bias(val[..., :n_tok].transpose(2, 0, 1, 3).reshape(nbv, ncols), n_blk=n_blk, tc=ncols)
    blocks_per_page = cache_sel.shape[2] // (2 * N_KV_HEADS * SEL_BLOCK)
    n_pages = page_table.shape[1]
    chosen = (bias_t == 0).reshape(nbv, n_seq, N_KV_HEADS * n_tok).any(axis=2)
    needed = chosen[:n_pages * blocks_per_page].reshape(n_pages, blocks_per_page, n_seq).any(axis=1).T
    nbp = pl.cdiv(n_blk + 1, LANES) * LANES
    bias = bias_t.reshape(nbv, n_seq, N_KV_HEADS, n_tok).transpose(1, 2, 3, 0)
    bias = jnp.pad(bias, ((0, 0), (0, 0), (0, 0), (0, nbp - nbv)), constant_values=NEG_INF)
    bias = jnp.pad(bias, ((0, 0), (0, 0), (0, tq - n_tok), (0, 0)))
    o2 = sel_attention_sample(q, bias, needed, kv_rows[1], cache_sel, layer, page_table, gates_t, o1,
                              n_pages_step=8, past_len=past_len)
    o3 = win_attention_sample(q, win_buf, layer, kv_rows[2], gates_t, o2, out_dtype=BF16)
    h = _out_proj(o3, w_out, layer, h, tm=m, name="nsa_out_proj_sample")
    return h, kv_rows


def _ret_layer(h, xn, layer, w_in, gn, w_out, state0, pos, *, n_seq, seq, chunk, n_tok, tm):
    m, d = xn.shape
    nk = RET_HEADS * RET_KDIM
    nv = RET_HEADS * RET_VDIM
    cos, sin = _rope_tables(pos, RET_KDIM // 2)
    tabs_per_seq = cos.shape[0] // tm
    tab_spec = pl.BlockSpec((tm, LANES), lambda i, j, k: (i % tabs_per_seq, 0))
    tn = 512

    def proj(col0, n_cols, dtype, epilogue, extras, extra_specs, name):
        return matmul(
            [xn], w_in, layer, col_block0=col0 // tn, n_cols=n_cols, tm=tm, tn=tn, tk=d,
            extras=extras, extra_specs=extra_specs,
            out_shapes=[jax.ShapeDtypeStruct((m, n_cols), dtype)],
            out_specs=[pl.BlockSpec((tm, tn), lambda i, j, k: (i, j))],
            epilogue=epilogue, name=name)[0]

    q = proj(0, nk, BF16, functools.partial(_ep_rope_ret, scale=1.0), [cos, sin], [tab_spec, tab_spec], "ret_proj_q")
    k = proj(nk, nk, F32, functools.partial(_ep_rope_ret, scale=RET_KDIM ** -0.5), [cos, sin],
             [tab_spec, tab_spec], "ret_proj_k")
    v = proj(2 * nk, nv, BF16, _ep_cast, [], [], "ret_proj_v")
    g = proj(2 * nk + nv, nv, F32, _ep_cast, [], [], "ret_proj_g")
    gated, state = retention(q, k, v, g, gn.reshape(gn.shape[0], 1, nv), layer, state0, n_seq=n_seq, seq=seq,
                             chunk=chunk, n_tok=n_tok)
    h = _out_proj(gated, w_out, layer, h, tm=tm, name="ret_out_proj")
    return h, state


def kernel(x_prompt, x_sample, cache_cmp_kv, cache_sel_kv, state_win_kv, state_ret, page_table, norm_mix, norm_ffn,
           norm_final, nsa_w_in, nsa_w_out, nsa_cmp_pos, nsa_cmp_w1, nsa_cmp_b1, nsa_cmp_w2, ret_w_in, ret_gn,
           ret_w_out, ffn_w1, ffn_w2):
    n_seq_p, seq_p, d = x_prompt.shape
    n_seq_s, n_tok_s, _ = x_sample.shape
    depth = norm_mix.shape[0]
    n_nsa = cache_sel_kv.shape[0]
    pool, page_rows = cache_sel_kv.shape[1], cache_sel_kv.shape[2]
    past_len = page_table.shape[1] * page_rows
    assert past_len % CMP_STRIDE == 0 and n_tok_s < CMP_STRIDE and n_tok_s <= SAMPLE_PAD
    assert state_win_kv.shape[2] == WINDOW and seq_p >= WINDOW

    hp = x_prompt.reshape(n_seq_p * seq_p, d)
    hs = jnp.pad(x_sample, ((0, 0), (0, SAMPLE_PAD - n_tok_s), (0, 0))).reshape(n_seq_s * SAMPLE_PAD, d)
    m_s = n_seq_s * SAMPLE_PAD
    rows_per_tok = 2 * N_KV_HEADS
    cache_cmp = cache_cmp_kv.reshape(n_nsa, pool, page_rows // CMP_STRIDE, CMP_STRIDE, rows_per_tok, HEAD_DIM)
    cache_sel = cache_sel_kv.reshape(n_nsa, pool, page_rows * rows_per_tok, HEAD_DIM)
    win_buf = state_win_kv.reshape(n_nsa, n_seq_s, WINDOW * rows_per_tok, HEAD_DIM)
    cmp_w1 = nsa_cmp_w1.reshape(n_nsa, 2, CMP_BLOCK * HEAD_DIM, HEAD_DIM)
    kv_shape = (2, N_KV_HEADS, HEAD_DIM)

    cmp_p, cmp_s, sel_p, sel_s, win_p, win_s, ret_p, ret_s = [], [], [], [], [], [], [], []
    for layer in range(depth):
        xp = rmsnorm(hp, norm_mix[layer], BF16, 256)
        xs = rmsnorm(hs, norm_mix[layer], BF16, m_s)
        if layer % 2 == 0:
            a = layer // 2
            cmp_bias = _cmp_bias(cmp_w1, nsa_cmp_pos, nsa_cmp_b1, a)
            hp, kv_p = _nsa_prompt(hp, xp, a, nsa_w_in, nsa_w_out, cmp_w1, cmp_bias, nsa_cmp_w2, n_seq=n_seq_p,
                                   seq=seq_p)
            hs, kv_s = _nsa_sample(hs, xs, a, nsa_w_in, nsa_w_out, cmp_w1, cmp_bias, nsa_cmp_w2, cache_cmp,
                                   cache_sel, win_buf, page_table, n_seq=n_seq_s, n_tok=n_tok_s, past_len=past_len)
            kv_p = [r.reshape((n_seq_p, seq_p) + kv_shape) for r in kv_p]
            kv_s = [r.reshape((n_seq_s, SAMPLE_PAD) + kv_shape)[:, :n_tok_s] for r in kv_s]
            cmp_p.append(kv_p[0])
            sel_p.append(kv_p[1])
            win_p.append(kv_p[2][:, seq_p - WINDOW:])
            cmp_s.append(kv_s[0])
            sel_s.append(kv_s[1])
            win_s.append(jnp.concatenate([state_win_kv[a][:, n_tok_s:], kv_s[2]], axis=1))
        else:
            r = layer // 2
            zeros = jnp.zeros((n_seq_p, RET_HEADS, RET_KDIM, RET_VDIM), F32)
            hp, st_p = _ret_layer(hp, xp, r, ret_w_in, ret_gn, ret_w_out, zeros, jnp.arange(seq_p), n_seq=n_seq_p,
                                  seq=seq_p, chunk=RET_CHUNK, n_tok=RET_CHUNK, tm=PROMPT_ROW_TILE)
            hs, st_s = _ret_layer(hs, xs, r, ret_w_in, ret_gn, ret_w_out, state_ret[r],
                                  jnp.tile(past_len + jnp.arange(SAMPLE_PAD), n_seq_s), n_seq=n_seq_s,
                                  seq=SAMPLE_PAD, chunk=SAMPLE_PAD,
                                  n_tok=n_tok_s, tm=m_s)
            ret_p.append(st_p)
            ret_s.append(st_s)
        hp = _mlp(hp, norm_ffn[layer], ffn_w1, ffn_w2, layer, tm=PROMPT_ROW_TILE)
        hs = _mlp(hs, norm_ffn[layer], ffn_w1, ffn_w2, layer, tm=m_s)
    y_prompt = rmsnorm(hp, norm_final, F32, 256).reshape(n_seq_p, seq_p, d)
    y_sample = rmsnorm(hs, norm_final, F32, m_s).reshape(n_seq_s, SAMPLE_PAD, d)[:, :n_tok_s]
    return (y_prompt, y_sample, jnp.stack(cmp_p), jnp.stack(cmp_s), jnp.stack(sel_p), jnp.stack(sel_s),
            jnp.stack(win_p), jnp.stack(win_s), jnp.stack(ret_p), jnp.stack(ret_s))
```

```python
import functools

import numpy as np
import jax
import jax.numpy as jnp
from jax import lax
from jax.experimental import pallas as pl
from jax.experimental.pallas import tpu as pltpu

F32 = jnp.float32
BF16 = jnp.bfloat16

N_HEADS = 16
HEAD_DIM = 128
N_KV_HEADS = 4
GQA_GROUP = N_HEADS // N_KV_HEADS
KV_ROW = 2 * N_KV_HEADS * HEAD_DIM
CMP_BLOCK = 32
CMP_STRIDE = 16
SEL_BLOCK = 64
N_SEL = 16
N_LOCAL = 2
WINDOW = 512
ROPE_THETA = 10000.0
RET_HEADS = 8
RET_KDIM = 256
RET_VDIM = 512
RET_CHUNK = 128
RET_HEADS_PER_STEP = 4
RMS_EPS = 1e-6
GN_EPS = 1e-5
NEG_INF = -1e30
FORCE_SCORE = 1e9
LOG2_E = 1.4426950408889634

SUBLANES = 8
LANES = 128
VMEM_LIMIT_BYTES = 56 * 1024 * 1024
SAMPLE_PAD = 16
PROMPT_ROW_TILE = 2048
X_TILE_BYTES = 8 * 1024 * 1024

_NT = (((1,), (1,)), ((), ()))
_TN = (((0,), (0,)), ((), ()))


def _params(*semantics):
    return pltpu.CompilerParams(dimension_semantics=semantics, vmem_limit_bytes=VMEM_LIMIT_BYTES)


def _rmsnorm_kernel(x_ref, g_ref, o_ref):
    x = x_ref[...]
    ms = jnp.mean(x * x, axis=-1, keepdims=True)
    o_ref[...] = (x * lax.rsqrt(ms + RMS_EPS) * g_ref[...]).astype(o_ref.dtype)


def rmsnorm(x, g, out_dtype, tm, row_block0=0, n_blocks=None):
    m, d = x.shape
    if n_blocks is None:
        n_blocks = m // tm
    return pl.pallas_call(
        _rmsnorm_kernel,
        grid=(n_blocks,),
        in_specs=[pl.BlockSpec((tm, d), lambda i: (i + row_block0, 0)), pl.BlockSpec((1, d), lambda i: (0, 0))],
        out_specs=pl.BlockSpec((tm, d), lambda i: (i, 0)),
        out_shape=jax.ShapeDtypeStruct((n_blocks * tm, d), out_dtype),
        compiler_params=_params("parallel"),
        name="rmsnorm",
    )(x, g.reshape(1, d))


def _mm_kernel(*refs, n_x, n_extra, n_out, nk, epilogue):
    x_refs = refs[:n_x]
    w_ref = refs[n_x]
    extra = refs[n_x + 1:n_x + 1 + n_extra]
    outs = refs[n_x + 1 + n_extra:n_x + 1 + n_extra + n_out]
    x = x_refs[0][...]
    for r in x_refs[1:]:
        x = x + r[...]
    part = jnp.dot(x.astype(BF16), w_ref[...].astype(BF16), preferred_element_type=F32)
    if nk == 1:
        epilogue(part, extra, outs)
        return
    k = pl.program_id(2)
    if epilogue is _ep_residual:
        @pl.when(k == 0)
        def _():
            outs[0][...] = extra[0][...] + part

        @pl.when(k > 0)
        def _():
            outs[0][...] += part
        return
    acc_ref = refs[-1]

    @pl.when(k == 0)
    def _():
        acc_ref[...] = part

    @pl.when(k > 0)
    def _():
        acc_ref[...] += part

    @pl.when(k == nk - 1)
    def _():
        epilogue(acc_ref[...], extra, outs)


def matmul(xs, w, layer, *, col_block0, n_cols, tm, tn, tk, extras, extra_specs, out_shapes, out_specs,
           epilogue, name, column_steps_ordered=False):
    m, kdim = xs[0].shape
    nk = kdim // tk
    grid = (m // tm, pl.cdiv(n_cols, tn), nk)
    in_specs = [pl.BlockSpec((tm, tk), lambda i, j, k: (i, k)) for _ in xs]
    in_specs.append(pl.BlockSpec((None, tk, tn), lambda i, j, k: (layer, k, j + col_block0)))
    in_specs.extend(extra_specs)
    scratch = [pltpu.VMEM((tm, tn), F32)] if nk > 1 and epilogue is not _ep_residual else []
    kern = functools.partial(_mm_kernel, n_x=len(xs), n_extra=len(extras), n_out=len(out_shapes), nk=nk,
                             epilogue=epilogue)
    return pl.pallas_call(
        kern,
        grid=grid,
        in_specs=in_specs,
        out_specs=out_specs,
        out_shape=out_shapes,
        scratch_shapes=scratch,
        compiler_params=_params("parallel", "arbitrary" if column_steps_ordered else "parallel", "arbitrary"),
        name=name,
    )(*xs, w, *extras)


def _rope_half_lane(x, cos, sin_signed):
    return x * cos + pltpu.roll(x, HEAD_DIM // 2, 1) * sin_signed


def _ep_rope_q(part, extra, outs, *, scale):
    cos = extra[0][...]
    sin = extra[1][...]
    for h in range(part.shape[1] // HEAD_DIM):
        sl = slice(h * HEAD_DIM, (h + 1) * HEAD_DIM)
        r = _rope_half_lane(part[:, sl], cos, sin) * scale
        outs[0][:, sl] = r.astype(outs[0].dtype)
        outs[1][sl, :] = r.T.astype(outs[1].dtype)


def _ep_nsa_kv(part, extra, outs):
    tm = part.shape[0]
    rows_per_tok = 2 * N_KV_HEADS
    j = pl.program_id(1)

    def heads():
        return [part[:, h * HEAD_DIM:(h + 1) * HEAD_DIM] for h in range(N_KV_HEADS)]

    for br in range(3):
        @pl.when(j == 2 * br)
        def _(br=br):
            cos = extra[0][...]
            sin = extra[1][...]
            for h, x in enumerate(heads()):
                k = _rope_half_lane(x, cos, sin)
                outs[br][pl.ds(h, tm, stride=rows_per_tok), :] = k
                outs[3][h] = k.astype(BF16)

        @pl.when(j == 2 * br + 1)
        def _(br=br):
            for h, v in enumerate(heads()):
                outs[br][pl.ds(N_KV_HEADS + h, tm, stride=rows_per_tok), :] = v
                outs[4][h] = v.T.astype(BF16)


def _ep_sigmoid(part, extra, outs):
    outs[0][...] = jax.nn.sigmoid(part)


def _ep_residual(part, extra, outs):
    outs[0][...] = extra[0][...] + part


def _ep_relu2(part, extra, outs):
    h = jnp.maximum(part, 0.0)
    outs[0][...] = (h * h).astype(outs[0].dtype)


def _ep_cast(part, extra, outs):
    outs[0][...] = part.astype(outs[0].dtype)


def _ep_rope_ret(part, extra, outs, *, scale):
    cos = extra[0][...]
    sin = extra[1][...]
    for h in range(part.shape[1] // RET_KDIM):
        a = slice(h * RET_KDIM, h * RET_KDIM + LANES)
        b = slice(h * RET_KDIM + LANES, (h + 1) * RET_KDIM)
        x1 = part[:, a]
        x2 = part[:, b]
        outs[0][:, a] = ((x1 * cos - x2 * sin) * scale).astype(outs[0].dtype)
        outs[0][:, b] = ((x2 * cos + x1 * sin) * scale).astype(outs[0].dtype)


def _rope_tables(pos, half):
    inv = ROPE_THETA ** (-jnp.arange(half, dtype=F32) / half)
    ang = pos.astype(F32)[:, None] * inv[None, :]
    return jnp.cos(ang), jnp.sin(ang)


def _compress_kernel(*refs, n_pages, paged):
    if paged:
        refs = refs[1:]
    page_refs = refs[:n_pages]
    next_ref, w1_ref, bias_ref, w2_ref, out_ref, y_ref = refs[n_pages:]
    nc = page_refs[0].shape[0]
    rows_per_tok = 2 * N_KV_HEADS
    n_main = n_pages * nc * rows_per_tok
    blocks = [
        jnp.concatenate([page_refs[p][:, s].reshape(nc * rows_per_tok, HEAD_DIM) for s in range(CMP_STRIDE)], axis=1)
        for p in range(n_pages)]
    blocks.append(jnp.concatenate([next_ref[0, s] for s in range(CMP_STRIDE)], axis=1))
    x = jnp.concatenate(blocks, axis=0).astype(BF16)
    k_half = CMP_STRIDE * HEAD_DIM
    wcat = jnp.concatenate([w1_ref[0, :k_half], w1_ref[0, k_half:], w1_ref[1, :k_half], w1_ref[1, k_half:]],
                           axis=1).astype(BF16)
    per = jnp.dot(x, wcat, preferred_element_type=F32)
    hid = per.shape[1] // 4
    row = lax.broadcasted_iota(jnp.int32, (per.shape[0], 2 * hid), 0)
    is_v = jnp.bitwise_and(row, N_KV_HEADS) != 0
    per = jnp.where(is_v, per[:, 2 * hid:], per[:, :2 * hid])
    h = per[:n_main, :hid] + per[rows_per_tok:, hid:]
    is_v = is_v[:n_main, :hid]
    h = jax.nn.gelu(h + jnp.where(is_v, bias_ref[1:2, :], bias_ref[0:1, :]))
    w2cat = jnp.concatenate([w2_ref[0], w2_ref[1]], axis=1).astype(BF16)
    y2 = jnp.dot(h.astype(BF16), w2cat, preferred_element_type=F32)
    y_ref[...] = jnp.where(is_v, y2[:, hid:], y2[:, :hid])
    for eg in range(rows_per_tok):
        out_ref[eg] = y_ref[pl.ds(eg, n_main // rows_per_tok, stride=rows_per_tok), :].astype(out_ref.dtype)


def compress(kv, layer, page_table, w1, bias, w2, *, n_seq, n_chunks, n_pages):
    paged = page_table is not None
    nc = kv.shape[2]
    n_main = n_pages * nc
    steps = n_chunks // n_main
    pages_per_seq = steps * n_pages
    rows_per_tok = 2 * N_KV_HEADS

    def nxt(s):
        return jnp.minimum((s + 1) * n_pages, pages_per_seq - 1)

    if paged:
        def page_map(p):
            return lambda b, s, pt: (layer, pt[b, s * n_pages + p], 0, 0, 0, 0)

        def next_map(b, s, pt):
            return (layer, pt[b, nxt(s)], 0, 0, 0, 0)

        def fixed(*idx):
            return lambda b, s, pt: idx

        out_map = lambda b, s, pt: (b, 0, s, 0)
    else:
        def page_map(p):
            return lambda b, s: (layer, b * pages_per_seq + s * n_pages + p, 0, 0, 0, 0)

        def next_map(b, s):
            return (layer, b * pages_per_seq + nxt(s), 0, 0, 0, 0)

        def fixed(*idx):
            return lambda b, s: idx

        out_map = lambda b, s: (b, 0, s, 0)

    tile = (CMP_STRIDE, rows_per_tok, HEAD_DIM)
    in_specs = [pl.BlockSpec((None, None, nc) + tile, page_map(p)) for p in range(n_pages)]
    in_specs += [
        pl.BlockSpec((None, None, 1) + tile, next_map),
        pl.BlockSpec((None, 2, CMP_BLOCK * HEAD_DIM, HEAD_DIM), fixed(layer, 0, 0, 0)),
        pl.BlockSpec((2, HEAD_DIM), fixed(0, 0)),
        pl.BlockSpec((None, 2, HEAD_DIM, HEAD_DIM), fixed(layer, 0, 0, 0)),
    ]
    kern = functools.partial(_compress_kernel, n_pages=n_pages, paged=paged)
    grid_spec = pltpu.PrefetchScalarGridSpec(
        num_scalar_prefetch=1 if paged else 0,
        grid=(n_seq, steps),
        in_specs=in_specs,
        out_specs=pl.BlockSpec((None, rows_per_tok, n_main, HEAD_DIM), out_map),
        scratch_shapes=[pltpu.VMEM((n_main * rows_per_tok, HEAD_DIM), F32)],
    )
    args = ([page_table] if paged else []) + [kv] * (n_pages + 1) + [w1, bias, w2]
    return pl.pallas_call(
        kern,
        grid_spec=grid_spec,
        out_shape=jax.ShapeDtypeStruct((n_seq, rows_per_tok, n_chunks, HEAD_DIM), BF16),
        compiler_params=_params("parallel", "arbitrary"),
        name="nsa_compress",
    )(*args)


def _group_rows(q):
    return jnp.concatenate([q[:, h * HEAD_DIM:(h + 1) * HEAD_DIM] for h in range(GQA_GROUP)], axis=0)


def _cmp_kernel(q_ref, k_ref, v_ref, cov_ref, gate_ref, o_ref, val_ref, *, tq, pos0):
    qi = pl.program_id(2)
    q4 = _group_rows(q_ref[...])
    s = lax.dot_general(q4, k_ref[...], _NT, preferred_element_type=F32)
    row = lax.broadcasted_iota(jnp.int32, s.shape, 0)
    col = lax.broadcasted_iota(jnp.int32, s.shape, 1)
    qpos = pos0 + qi * tq + jnp.bitwise_and(row, tq - 1)
    valid = col * CMP_STRIDE + (CMP_BLOCK - 1) <= qpos
    s = jnp.where(valid, s, NEG_INF)
    m = jnp.max(s, axis=1, keepdims=True)
    e = jnp.where(valid, jnp.exp2(s - m), 0.0)
    l = jnp.sum(e, axis=1, keepdims=True)
    p = e / jnp.where(l > 0.0, l, 1.0)
    o4 = jnp.dot(p.astype(BF16), v_ref[...], preferred_element_type=F32)
    for h in range(GQA_GROUP):
        o_ref[:, h * HEAD_DIM:(h + 1) * HEAD_DIM] = o4[h * tq:(h + 1) * tq] * gate_ref[:, h:h + 1]
    imp = p[0:tq] + p[tq:2 * tq] + p[2 * tq:3 * tq] + p[3 * tq:4 * tq]
    hi = imp.astype(BF16)
    r1 = imp - hi.astype(F32)
    mid = r1.astype(BF16)
    lo = (r1 - mid.astype(F32)).astype(BF16)
    cov = cov_ref[...]
    score = (lax.dot_general(cov, hi, _NT, preferred_element_type=F32)
             + lax.dot_general(cov, mid, _NT, preferred_element_type=F32)
             + lax.dot_general(cov, lo, _NT, preferred_element_type=F32))
    jb = lax.broadcasted_iota(jnp.int32, score.shape, 0)
    qp = pos0 + qi * tq + lax.broadcasted_iota(jnp.int32, score.shape, 1)
    cur = jnp.right_shift(qp, SEL_BLOCK.bit_length() - 1)
    visible = jb <= cur
    forced = (jb == 0) | (visible & (jb > cur - N_LOCAL))
    val_ref[...] = jnp.where(forced, FORCE_SCORE, jnp.where(visible, score, -FORCE_SCORE))


def cmp_attention(q, kc, cover_t, gates_t, *, n_seq, seq, tq, pos0):
    nq = seq // tq
    ncp = kc.shape[2]
    nbv = cover_t.shape[0]
    gw = GQA_GROUP * HEAD_DIM
    kern = functools.partial(_cmp_kernel, tq=tq, pos0=pos0)
    if tq % LANES == 0:
        val_spec = pl.BlockSpec((nbv, tq), lambda b, g, i: (0, (b * N_KV_HEADS + g) * nq + i))
        val_shape = jax.ShapeDtypeStruct((nbv, n_seq * N_KV_HEADS * seq), F32)
    else:
        val_spec = pl.BlockSpec((None, None, nbv, tq), lambda b, g, i: (b, g, 0, i))
        val_shape = jax.ShapeDtypeStruct((n_seq, N_KV_HEADS, nbv, seq), F32)
    return pl.pallas_call(
        kern,
        grid=(n_seq, N_KV_HEADS, nq),
        in_specs=[
            pl.BlockSpec((tq, gw), lambda b, g, i: (b * nq + i, g)),
            pl.BlockSpec((None, None, ncp, HEAD_DIM), lambda b, g, i: (b, g, 0, 0)),
            pl.BlockSpec((None, None, ncp, HEAD_DIM), lambda b, g, i: (b, N_KV_HEADS + g, 0, 0)),
            pl.BlockSpec((nbv, ncp), lambda b, g, i: (0, 0)),
            pl.BlockSpec((None, None, tq, GQA_GROUP), lambda b, g, i: (0, g, b * nq + i, 0)),
        ],
        out_specs=[pl.BlockSpec((tq, gw), lambda b, g, i: (b * nq + i, g)), val_spec],
        out_shape=[jax.ShapeDtypeStruct((n_seq * seq, N_HEADS * HEAD_DIM), F32), val_shape],
        compiler_params=_params("parallel", "parallel", "parallel"),
        name="nsa_cmp_attention",
    )(q, kc, kc, cover_t, gates_t)


def _cover_t(n_chunks, n_blk, nbv):
    n_cmp = n_chunks - CMP_BLOCK // CMP_STRIDE + 1
    c0 = np.arange(n_chunks) * CMP_STRIDE
    j0 = np.arange(nbv) * SEL_BLOCK
    cov = (c0[None, :] < j0[:, None] + SEL_BLOCK) & (c0[None, :] + CMP_BLOCK > j0[:, None])
    cov &= (np.arange(n_chunks)[None, :] < n_cmp) & (np.arange(nbv)[:, None] < n_blk)
    return jnp.asarray(cov.astype(np.float32), dtype=BF16)


def _rank_kernel(val_ref, out_ref, *, n_blk, n_sel, causal_seq):
    val = val_ref[...]
    tc = val.shape[1]
    jb = lax.broadcasted_iota(jnp.int32, val.shape, 0)
    if causal_seq is None:
        n_iter = n_blk
    else:
        pos0 = (pl.program_id(0) % (causal_seq // tc)) * tc
        n_iter = jnp.minimum(n_blk, (pos0 + tc - 1) // SEL_BLOCK + 1)

    def body(i, rank):
        row = val_ref[pl.ds(i, 1), :]
        beats = (row > val) | ((row == val) & (i < jb))
        return rank + jnp.where(beats, 1, 0)

    rank = lax.fori_loop(0, n_iter, body, jnp.zeros(val.shape, jnp.int32))
    chosen = (rank < n_sel) & (val > -0.5 * FORCE_SCORE)
    out_ref[...] = jnp.where(chosen, 0.0, NEG_INF).astype(out_ref.dtype)


def select_bias(val, *, n_blk, tc, causal_seq=None):
    nbv, ncols = val.shape
    kern = functools.partial(_rank_kernel, n_blk=n_blk, n_sel=min(N_SEL, n_blk), causal_seq=causal_seq)
    return pl.pallas_call(
        kern,
        grid=(ncols // tc,),
        in_specs=[pl.BlockSpec((nbv, tc), lambda i: (0, i))],
        out_specs=pl.BlockSpec((nbv, tc), lambda i: (0, i)),
        out_shape=jax.ShapeDtypeStruct((nbv, ncols), BF16),
        compiler_params=_params("parallel"),
        name="nsa_select_rank",
    )(val)


_STEP_FIRST, _STEP_LAST, _STEP_EDGE = 1, 2, 4


def _flash_steps(seq, tq, tk, window):
    qi_l, kt_l, fl_l = [], [], []
    for qi in range(seq // tq):
        q0, q1 = qi * tq, qi * tq + tq - 1
        kts = [kt for kt in range(seq // tk)
               if kt * tk <= q1 and (window is None or q0 - (kt * tk + tk - 1) < window)]
        for n, kt in enumerate(kts):
            k0, k1 = kt * tk, kt * tk + tk - 1
            all_visible = q0 - k1 >= 0 and (window is None or q1 - k0 < window)
            flags = (_STEP_FIRST if n == 0 else 0) | (_STEP_LAST if n == len(kts) - 1 else 0)
            flags |= 0 if all_visible else _STEP_EDGE
            qi_l.append(qi)
            kt_l.append(kt)
            fl_l.append(flags)
    return (jnp.asarray(np.array(qi_l, np.int32)), jnp.asarray(np.array(kt_l, np.int32)),
            jnp.asarray(np.array(fl_l, np.int32)))


def _flash_kernel(*refs, tq, tk, use_bias, window, out_dtype):
    qi_ref, kt_ref, flag_ref = refs[:3]
    if use_bias:
        qt_ref, bias_ref, k_ref, vt_ref, e_ref, gate_ref, prev_ref, o_ref, qa_ref, m_ref, acc_ref = refs[3:]
    else:
        qt_ref, k_ref, vt_ref, gate_ref, prev_ref, o_ref, qa_ref, m_ref, acc_ref = refs[3:]
    s = pl.program_id(2)
    qi = qi_ref[s]
    kt = kt_ref[s]
    flags = flag_ref[s]
    first = jnp.bitwise_and(flags, _STEP_FIRST) != 0
    last = jnp.bitwise_and(flags, _STEP_LAST) != 0
    edge = jnp.bitwise_and(flags, _STEP_EDGE) != 0

    @pl.when(first)
    def _():
        for h in range(GQA_GROUP):
            qa_ref[0:HEAD_DIM, h * tq:(h + 1) * tq] = qt_ref[h * HEAD_DIM:(h + 1) * HEAD_DIM, :]
            if use_bias:
                nbv = bias_ref.shape[0]
                qa_ref[HEAD_DIM:HEAD_DIM + nbv, h * tq:(h + 1) * tq] = bias_ref[...]
                if HEAD_DIM + nbv < qa_ref.shape[0]:
                    qa_ref[HEAD_DIM + nbv:, h * tq:(h + 1) * tq] = jnp.zeros(
                        (qa_ref.shape[0] - HEAD_DIM - nbv, tq), BF16)
        m_ref[...] = jnp.full(m_ref.shape, NEG_INF, F32)
        acc_ref[...] = jnp.zeros(acc_ref.shape, F32)

    def step(masked):
        if use_bias:
            ka = jnp.concatenate([k_ref[...], e_ref[...]], axis=1)
        else:
            ka = k_ref[...]
        st = jnp.dot(ka, qa_ref[...], preferred_element_type=F32)
        if masked:
            kpos = kt * tk + lax.broadcasted_iota(jnp.int32, st.shape, 0)
            qpos = qi * tq + jnp.bitwise_and(lax.broadcasted_iota(jnp.int32, st.shape, 1), tq - 1)
            d = qpos - kpos
            ok = d >= 0
            if window is not None:
                ok = ok & (d < window)
            st = jnp.where(ok, st, NEG_INF)
        m_old = m_ref[...]
        m_new = jnp.maximum(m_old, jnp.max(st, axis=0, keepdims=True))
        alpha = jnp.exp2(m_old - m_new)
        p = jnp.exp2(st - m_new).astype(BF16)
        va = jnp.concatenate([vt_ref[...], jnp.ones((acc_ref.shape[0] - HEAD_DIM, tk), BF16)], axis=0)
        acc_ref[...] = alpha * acc_ref[...] + jnp.dot(va, p, preferred_element_type=F32)
        m_ref[...] = m_new

    @pl.when(edge)
    def _():
        step(True)

    @pl.when(jnp.logical_not(edge))
    def _():
        step(False)

    @pl.when(last)
    def _():
        acc = acc_ref[...]
        ot = acc[:HEAD_DIM] / acc[HEAD_DIM:HEAD_DIM + 1]
        for h in range(GQA_GROUP):
            sl = slice(h * HEAD_DIM, (h + 1) * HEAD_DIM)
            o_h = ot[:, h * tq:(h + 1) * tq].T
            o_ref[:, sl] = (prev_ref[:, sl] + o_h * gate_ref[:, h:h + 1]).astype(out_dtype)


def flash_attention(qt, k, vt, branch, gates_t, prev, *, n_seq, seq, tq, tk, bias=None, onehot=None, window=None,
                    out_dtype=F32):
    use_bias = bias is not None
    nq = seq // tq
    nkt = seq // tk
    gw = GQA_GROUP * HEAD_DIM
    qi_tab, kt_tab, flag_tab = _flash_steps(seq, tq, tk, window)

    o_spec = pl.BlockSpec((tq, gw), lambda b, g, s, qi, kt, fl: (b * nq + qi[s], g))
    qt_spec = pl.BlockSpec((gw, tq), lambda b, g, s, qi, kt, fl: (g, b * nq + qi[s]))
    k_spec = pl.BlockSpec((None, None, tk, HEAD_DIM), lambda b, g, s, qi, kt, fl: (branch, g, b * nkt + kt[s], 0))
    vt_spec = pl.BlockSpec((None, None, HEAD_DIM, tk), lambda b, g, s, qi, kt, fl: (branch, g, 0, b * nkt + kt[s]))
    gate_spec = pl.BlockSpec((None, None, tq, GQA_GROUP), lambda b, g, s, qi, kt, fl: (branch, g, b * nq + qi[s], 0))
    in_specs = [qt_spec]
    args = [qt]
    qa_rows = HEAD_DIM
    if use_bias:
        nbv = bias.shape[0]
        nbp = onehot.shape[1]
        qa_rows += nbp
        in_specs.append(
            pl.BlockSpec((nbv, tq), lambda b, g, s, qi, kt, fl: (0, (b * N_KV_HEADS + g) * nq + qi[s])))
        args.append(bias)
    in_specs += [k_spec, vt_spec]
    args += [k, vt]
    if use_bias:
        in_specs.append(pl.BlockSpec((tk, nbp), lambda b, g, s, qi, kt, fl: (kt[s], 0)))
        args.append(onehot)
    in_specs += [gate_spec, o_spec]
    args += [gates_t, prev]
    kern = functools.partial(_flash_kernel, tq=tq, tk=tk, use_bias=use_bias, window=window, out_dtype=out_dtype)
    cols = GQA_GROUP * tq
    grid_spec = pltpu.PrefetchScalarGridSpec(
        num_scalar_prefetch=3,
        grid=(n_seq, N_KV_HEADS, qi_tab.shape[0]),
        in_specs=in_specs,
        out_specs=o_spec,
        scratch_shapes=[
            pltpu.VMEM((qa_rows, cols), BF16),
            pltpu.VMEM((1, cols), F32),
            pltpu.VMEM((HEAD_DIM + 2 * SUBLANES, cols), F32),
        ],
    )
    return pl.pallas_call(
        kern,
        grid_spec=grid_spec,
        out_shape=jax.ShapeDtypeStruct((n_seq * seq, N_HEADS * HEAD_DIM), out_dtype),
        compiler_params=_params("parallel", "parallel", "arbitrary"),
        name="nsa_sel_attention" if use_bias else "nsa_win_attention",
    )(qi_tab, kt_tab, flag_tab, *args)


def _block_onehot(n_keys, nbp):
    e = (np.arange(n_keys)[:, None] // SEL_BLOCK) == np.arange(nbp)[None, :]
    return jnp.asarray(e.astype(np.float32), dtype=BF16)


def _sel_sample_kernel(*refs, n_pages, nbp, past_len, nsteps, tq):
    logical_ref, count_ref = refs[1:3]
    q_ref, bias_ref, new_ref = refs[3:6]
    page_refs = refs[6:6 + n_pages]
    gate_ref, prev_ref, o_ref, qa_ref, m_ref, l_ref, acc_ref = refs[6 + n_pages:]
    b = pl.program_id(0)
    step = pl.program_id(1)
    page_rows = page_refs[0].shape[0]
    rows_per_tok = 2 * N_KV_HEADS
    page_tokens = page_rows // rows_per_tok
    blk_shift = SEL_BLOCK.bit_length() - 1
    n_needed = count_ref[b]

    def block_onehot(block_id):
        j = lax.broadcasted_iota(jnp.int32, (block_id.shape[0], nbp), 1)
        return jnp.where(block_id == j, 1.0, 0.0).astype(BF16)

    def scores(g, k, onehot):
        return lax.dot_general(qa_ref[g], jnp.concatenate([k, onehot], axis=1), _NT, preferred_element_type=F32)

    @pl.when(step == 0)
    def _():
        q = q_ref[...]
        new_onehot = block_onehot(
            jnp.right_shift(past_len + lax.broadcasted_iota(jnp.int32, (tq, 1), 0), blk_shift))
        for g in range(N_KV_HEADS):
            for h in range(GQA_GROUP):
                hh = g * GQA_GROUP + h
                qa_ref[g, h * tq:(h + 1) * tq, 0:HEAD_DIM] = q[:, hh * HEAD_DIM:(hh + 1) * HEAD_DIM]
                qa_ref[g, h * tq:(h + 1) * tq, HEAD_DIM:] = bias_ref[g]
            kn = new_ref[pl.ds(g, tq, stride=2 * N_KV_HEADS), :].astype(BF16)
            vn = new_ref[pl.ds(N_KV_HEADS + g, tq, stride=2 * N_KV_HEADS), :].astype(BF16)
            s = scores(g, kn, new_onehot)
            row = lax.broadcasted_iota(jnp.int32, s.shape, 0)
            col = lax.broadcasted_iota(jnp.int32, s.shape, 1)
            s = jnp.where(col <= jnp.bitwise_and(row, tq - 1), s, NEG_INF)
            m = jnp.max(s, axis=1, keepdims=True)
            p = jnp.exp2(s - m)
            m_ref[g] = m
            l_ref[g] = jnp.sum(p, axis=1, keepdims=True)
            acc_ref[g] = jnp.dot(p.astype(BF16), vn, preferred_element_type=F32)

    @pl.when(step * n_pages < n_needed)
    def _():
        r = lax.broadcasted_iota(jnp.int32, (page_tokens, 1), 0)
        ids = []
        for p in range(n_pages):
            slot = step * n_pages + p
            first_block = logical_ref[b, slot] * (page_tokens // SEL_BLOCK)
            ids.append(jnp.where(slot < n_needed, first_block + jnp.right_shift(r, blk_shift), nbp - 1))
        onehot = block_onehot(jnp.concatenate(ids, axis=0))
        for g in range(N_KV_HEADS):
            k = jnp.concatenate(
                [page_refs[p][pl.ds(g, page_tokens, stride=rows_per_tok), :] for p in range(n_pages)],
                axis=0).astype(BF16)
            v = jnp.concatenate(
                [page_refs[p][pl.ds(N_KV_HEADS + g, page_tokens, stride=rows_per_tok), :] for p in range(n_pages)],
                axis=0).astype(BF16)
            s = scores(g, k, onehot)
            m_old = m_ref[g]
            m_new = jnp.maximum(m_old, jnp.max(s, axis=1, keepdims=True))
            alpha = jnp.exp2(m_old - m_new)
            p = jnp.exp2(s - m_new)
            l_ref[g] = alpha * l_ref[g] + jnp.sum(p, axis=1, keepdims=True)
            acc_ref[g] = alpha * acc_ref[g] + jnp.dot(p.astype(BF16), v, preferred_element_type=F32)
            m_ref[g] = m_new

    @pl.when(step == nsteps - 1)
    def _():
        for g in range(N_KV_HEADS):
            o4 = acc_ref[g] / l_ref[g]
            for h in range(GQA_GROUP):
                hh = g * GQA_GROUP + h
                sl = slice(hh * HEAD_DIM, (hh + 1) * HEAD_DIM)
                o_ref[:, sl] = prev_ref[:, sl] + o4[h * tq:(h + 1) * tq] * gate_ref[g, :, h:h + 1]


def sel_attention_sample(q, bias, needed, kv_new, cache, layer, page_table, gates_t, prev, *, n_pages_step,
                         past_len):
    n_seq, pages = page_table.shape
    tq = SAMPLE_PAD
    nbp = bias.shape[-1]
    page_rows = cache.shape[2]
    nsteps = pages // n_pages_step
    dm = N_HEADS * HEAD_DIM
    order = jnp.argsort(jnp.logical_not(needed), axis=1, stable=True).astype(jnp.int32)
    count = jnp.sum(needed, axis=1).astype(jnp.int32)
    slot = jnp.minimum(jnp.arange(pages, dtype=jnp.int32)[None, :], count[:, None] - 1)
    logical = jnp.take_along_axis(order, slot, axis=1)
    physical = jnp.take_along_axis(page_table, logical, axis=1)
    row_spec = pl.BlockSpec((tq, dm), lambda b, s, *_: (b, 0))

    def page_map(p):
        return lambda b, s, phys, *_: (layer, phys[b, s * n_pages_step + p], 0, 0)

    in_specs = [
        row_spec,
        pl.BlockSpec((None, N_KV_HEADS, tq, nbp), lambda b, s, *_: (b, 0, 0, 0)),
        pl.BlockSpec((tq * 2 * N_KV_HEADS, HEAD_DIM), lambda b, s, *_: (b, 0)),
    ]
    in_specs += [pl.BlockSpec((None, None, page_rows, HEAD_DIM), page_map(p)) for p in range(n_pages_step)]
    in_specs += [
        pl.BlockSpec((None, N_KV_HEADS, tq, GQA_GROUP), lambda b, s, *_: (1, 0, b, 0)),
        row_spec,
    ]
    rows = GQA_GROUP * tq
    kern = functools.partial(_sel_sample_kernel, n_pages=n_pages_step, nbp=nbp, past_len=past_len, nsteps=nsteps,
                             tq=tq)
    grid_spec = pltpu.PrefetchScalarGridSpec(
        num_scalar_prefetch=3,
        grid=(n_seq, nsteps),
        in_specs=in_specs,
        out_specs=row_spec,
        scratch_shapes=[
            pltpu.VMEM((N_KV_HEADS, rows, HEAD_DIM + nbp), BF16),
            pltpu.VMEM((N_KV_HEADS, rows, 1), F32),
            pltpu.VMEM((N_KV_HEADS, rows, 1), F32),
            pltpu.VMEM((N_KV_HEADS, rows, HEAD_DIM), F32),
        ],
    )
    return pl.pallas_call(
        kern,
        grid_spec=grid_spec,
        out_shape=jax.ShapeDtypeStruct((n_seq * tq, dm), F32),
        compiler_params=_params("parallel", "arbitrary"),
        name="nsa_sel_attention_sample",
    )(physical, logical, count, q, bias, kv_new, *([cache] * n_pages_step), gates_t, prev)


def _win_sample_kernel(q_ref, buf_ref, new_ref, gate_ref, prev_ref, o_ref, *, tq, out_dtype):
    rows_per_tok = 2 * N_KV_HEADS
    w_buf = buf_ref.shape[0] // rows_per_tok
    gw = GQA_GROUP * HEAD_DIM
    for g in range(N_KV_HEADS):
        q4 = _group_rows(q_ref[:, g * gw:(g + 1) * gw])
        kb = buf_ref[pl.ds(g, w_buf, stride=rows_per_tok), :].astype(BF16)
        vb = buf_ref[pl.ds(N_KV_HEADS + g, w_buf, stride=rows_per_tok), :].astype(BF16)
        kn = new_ref[pl.ds(g, tq, stride=rows_per_tok), :].astype(BF16)
        vn = new_ref[pl.ds(N_KV_HEADS + g, tq, stride=rows_per_tok), :].astype(BF16)
        sb = lax.dot_general(q4, kb, _NT, preferred_element_type=F32)
        sn = lax.dot_general(q4, kn, _NT, preferred_element_type=F32)
        tb = jnp.bitwise_and(lax.broadcasted_iota(jnp.int32, sb.shape, 0), tq - 1)
        rb = lax.broadcasted_iota(jnp.int32, sb.shape, 1)
        sb = jnp.where(tb + w_buf - rb < WINDOW, sb, NEG_INF)
        tn = jnp.bitwise_and(lax.broadcasted_iota(jnp.int32, sn.shape, 0), tq - 1)
        rn = lax.broadcasted_iota(jnp.int32, sn.shape, 1)
        sn = jnp.where(rn <= tn, sn, NEG_INF)
        m = jnp.maximum(jnp.max(sb, axis=1, keepdims=True), jnp.max(sn, axis=1, keepdims=True))
        pb = jnp.exp2(sb - m)
        pn = jnp.exp2(sn - m)
        l = jnp.sum(pb, axis=1, keepdims=True) + jnp.sum(pn, axis=1, keepdims=True)
        o4 = (jnp.dot(pb.astype(BF16), vb, preferred_element_type=F32)
              + jnp.dot(pn.astype(BF16), vn, preferred_element_type=F32)) / l
        for h in range(GQA_GROUP):
            hh = g * GQA_GROUP + h
            sl = slice(hh * HEAD_DIM, (hh + 1) * HEAD_DIM)
            o_ref[:, sl] = (prev_ref[:, sl] + o4[h * tq:(h + 1) * tq] * gate_ref[g, :, h:h + 1]).astype(out_dtype)


def win_attention_sample(q, win_buf, layer, kv_new, gates_t, prev, *, out_dtype):
    n_seq, buf_rows = win_buf.shape[1], win_buf.shape[2]
    tq = SAMPLE_PAD
    q_spec = pl.BlockSpec((tq, N_HEADS * HEAD_DIM), lambda b: (b, 0))
    kern = functools.partial(_win_sample_kernel, tq=tq, out_dtype=out_dtype)
    return pl.pallas_call(
        kern,
        grid=(n_seq,),
        in_specs=[
            q_spec,
            pl.BlockSpec((None, None, buf_rows, HEAD_DIM), lambda b: (layer, b, 0, 0)),
            pl.BlockSpec((tq * 2 * N_KV_HEADS, HEAD_DIM), lambda b: (b, 0)),
            pl.BlockSpec((None, N_KV_HEADS, tq, GQA_GROUP), lambda b: (2, 0, b, 0)),
            q_spec,
        ],
        out_specs=q_spec,
        out_shape=jax.ShapeDtypeStruct((n_seq * tq, N_HEADS * HEAD_DIM), out_dtype),
        compiler_params=_params("parallel"),
        name="nsa_win_attention_sample",
    )(q, win_buf, kv_new, gates_t, prev)


def _ret_kernel(q_ref, k_ref, v_ref, g_ref, gn_ref, dec_ref, qd_ref, kd_ref, gc_ref, s0_ref, o_ref, sout_ref,
                state_ref, *, n_chunks):
    c = pl.program_id(2)

    @pl.when(c == 0)
    def _():
        state_ref[...] = s0_ref[...]

    for h in range(state_ref.shape[0]):
        ksl = slice(h * RET_KDIM, (h + 1) * RET_KDIM)
        vsl = slice(h * RET_VDIM, (h + 1) * RET_VDIM)
        q = q_ref[:, ksl]
        kf = k_ref[:, ksl]
        v = v_ref[:, vsl]
        state = state_ref[h]
        s = lax.dot_general(q, kf.astype(BF16), _NT, preferred_element_type=F32) * dec_ref[h]
        inner = jnp.dot(s.astype(BF16), v, preferred_element_type=F32)
        cross = jnp.dot(q, state.astype(BF16), preferred_element_type=F32) * qd_ref[h]
        o = inner + cross
        kd = (kf * kd_ref[h]).astype(BF16)
        state_ref[h] = gc_ref[h] * state + lax.dot_general(kd, v, _TN, preferred_element_type=F32)
        mu = jnp.mean(o, axis=-1, keepdims=True)
        d = o - mu
        var = jnp.mean(d * d, axis=-1, keepdims=True)
        on = d * lax.rsqrt(var + GN_EPS) * gn_ref[:, vsl]
        g = g_ref[:, vsl]
        o_ref[:, vsl] = (g * jax.nn.sigmoid(g) * on).astype(o_ref.dtype)

    @pl.when(c == n_chunks - 1)
    def _():
        sout_ref[...] = state_ref[...]


def _ret_tables(chunk, n_tok):
    log_g = jnp.log(1.0 - 2.0 ** (-5.0 - jnp.arange(RET_HEADS, dtype=F32)))
    i = jnp.arange(chunk, dtype=F32)
    rel = i[:, None] - i[None, :]
    decay = jnp.where(rel >= 0, jnp.exp(jnp.maximum(rel, 0.0)[None] * log_g[:, None, None]), 0.0)
    q_dec = jnp.exp((i[None, :] + 1.0) * log_g[:, None])[..., None]
    k_dec = jnp.where(i[None, :] < n_tok, jnp.exp((n_tok - 1.0 - i)[None, :] * log_g[:, None]), 0.0)[..., None]
    g_c = jnp.exp(n_tok * log_g)[:, None, None]
    return decay, q_dec, k_dec, g_c


def retention(q, k, v, g, gn, layer, state0, *, n_seq, seq, chunk, n_tok):
    nc = seq // chunk
    hps = RET_HEADS_PER_STEP
    decay, q_dec, k_dec, g_c = _ret_tables(chunk, n_tok)
    kern = functools.partial(_ret_kernel, n_chunks=nc)
    tok_k = pl.BlockSpec((chunk, hps * RET_KDIM), lambda b, h, c: (b * nc + c, h))
    tok_v = pl.BlockSpec((chunk, hps * RET_VDIM), lambda b, h, c: (b * nc + c, h))
    st_spec = pl.BlockSpec((None, hps, RET_KDIM, RET_VDIM), lambda b, h, c: (b, h, 0, 0))
    return pl.pallas_call(
        kern,
        grid=(n_seq, RET_HEADS // hps, nc),
        in_specs=[
            tok_k, tok_k, tok_v, tok_v,
            pl.BlockSpec((None, 1, hps * RET_VDIM), lambda b, h, c: (layer, 0, h)),
            pl.BlockSpec((hps, chunk, chunk), lambda b, h, c: (h, 0, 0)),
            pl.BlockSpec((hps, chunk, 1), lambda b, h, c: (h, 0, 0)),
            pl.BlockSpec((hps, chunk, 1), lambda b, h, c: (h, 0, 0)),
            pl.BlockSpec((hps, 1, 1), lambda b, h, c: (h, 0, 0)),
            st_spec,
        ],
        out_specs=[tok_v, st_spec],
        out_shape=[
            jax.ShapeDtypeStruct((n_seq * seq, RET_HEADS * RET_VDIM), BF16),
            jax.ShapeDtypeStruct((n_seq, RET_HEADS, RET_KDIM, RET_VDIM), F32),
        ],
        scratch_shapes=[pltpu.VMEM((hps, RET_KDIM, RET_VDIM), F32)],
        compiler_params=_params("parallel", "parallel", "arbitrary"),
        name="retention_chunk",
    )(q, k, v, g, gn, decay, q_dec, k_dec, g_c, state0)


def _project_nsa(xn, w_in, layer, cos, sin_signed, *, tm):
    m, d = xn.shape
    nq = N_HEADS * HEAD_DIM
    half = N_KV_HEADS * HEAD_DIM
    rows_per_tok = 2 * N_KV_HEADS
    tabs_per_seq = cos.shape[0] // tm
    tab_spec = pl.BlockSpec((tm, HEAD_DIM), lambda i, j, k: (i % tabs_per_seq, 0))
    tn_q = 512
    q = matmul(
        [xn], w_in, layer, col_block0=0, n_cols=nq, tm=tm, tn=tn_q, tk=d,
        extras=[cos, sin_signed], extra_specs=[tab_spec, tab_spec],
        out_shapes=[jax.ShapeDtypeStruct((m, nq), BF16), jax.ShapeDtypeStruct((nq, m), BF16)],
        out_specs=[pl.BlockSpec((tm, tn_q), lambda i, j, k: (i, j)), pl.BlockSpec((tn_q, tm), lambda i, j, k: (j, i))],
        epilogue=functools.partial(_ep_rope_q, scale=HEAD_DIM ** -0.5 * LOG2_E), name="nsa_proj_q")
    kv = matmul(
        [xn], w_in, layer, col_block0=nq // half, n_cols=3 * KV_ROW, tm=tm, tn=half, tk=d,
        extras=[cos, sin_signed], extra_specs=[tab_spec, tab_spec],
        out_shapes=[jax.ShapeDtypeStruct((m * rows_per_tok, HEAD_DIM), F32)] * 3
        + [jax.ShapeDtypeStruct((3, N_KV_HEADS, m, HEAD_DIM), BF16),
           jax.ShapeDtypeStruct((3, N_KV_HEADS, HEAD_DIM, m), BF16)],
        out_specs=[pl.BlockSpec((tm * rows_per_tok, HEAD_DIM), lambda i, j, k: (i, 0))] * 3
        + [pl.BlockSpec((None, N_KV_HEADS, tm, HEAD_DIM), lambda i, j, k: (j // 2, 0, i, 0)),
           pl.BlockSpec((None, N_KV_HEADS, HEAD_DIM, tm), lambda i, j, k: (j // 2, 0, 0, i))],
        epilogue=_ep_nsa_kv, name="nsa_proj_kv", column_steps_ordered=True)
    kv = (kv[:3], kv[3], kv[4])
    n_gate = 3 * N_HEADS
    w_gate = jnp.pad(w_in[layer, :, nq + 3 * KV_ROW:], ((0, 0), (0, LANES - n_gate)))[None]
    gates = matmul(
        [xn], w_gate, 0, col_block0=0, n_cols=LANES, tm=tm, tn=LANES, tk=d, extras=[], extra_specs=[],
        out_shapes=[jax.ShapeDtypeStruct((m, LANES), F32)],
        out_specs=[pl.BlockSpec((tm, LANES), lambda i, j, k: (i, 0))],
        epilogue=_ep_sigmoid, name="nsa_proj_gate")[0]
    gates_t = gates[:, :n_gate].reshape(m, 3, N_KV_HEADS, GQA_GROUP).transpose(1, 2, 0, 3)
    return q, kv, gates_t


def _cmp_bias(w1, pos, b1, layer):
    k_full = CMP_BLOCK * HEAD_DIM
    pos_rows = jnp.zeros((2, SUBLANES, k_full), F32).at[:, 0].set(pos[layer].reshape(2, k_full))
    terms = []
    for e in range(2):
        t = matmul(
            [pos_rows[e]], w1.reshape(-1, k_full, HEAD_DIM), layer * 2 + e, col_block0=0, n_cols=HEAD_DIM,
            tm=SUBLANES, tn=HEAD_DIM, tk=k_full, extras=[], extra_specs=[],
            out_shapes=[jax.ShapeDtypeStruct((SUBLANES, HEAD_DIM), F32)],
            out_specs=[pl.BlockSpec((SUBLANES, HEAD_DIM), lambda i, j, k: (0, 0))],
            epilogue=_ep_cast, name="nsa_cmp_pos_term")[0]
        terms.append(t[0])
    return jnp.stack(terms) + b1[layer]


def _mlp(h, norm_g, w1, w2, layer, *, tm):
    m, d = h.shape
    d_ff = w1.shape[-1]
    xn = rmsnorm(h, norm_g, BF16, min(tm, 256))
    tn1 = 1024
    a = matmul(
        [xn], w1, layer, col_block0=0, n_cols=d_ff, tm=tm, tn=tn1, tk=d, extras=[], extra_specs=[],
        out_shapes=[jax.ShapeDtypeStruct((m, d_ff), BF16)],
        out_specs=[pl.BlockSpec((tm, tn1), lambda i, j, k: (i, j))],
        epilogue=_ep_relu2, name="mlp_up")[0]
    tn2 = 1024
    tm2 = tm
    return matmul(
        [a], w2, layer, col_block0=0, n_cols=d, tm=tm2, tn=tn2, tk=1024, extras=[h],
        extra_specs=[pl.BlockSpec((tm2, tn2), lambda i, j, k: (i, j))],
        out_shapes=[jax.ShapeDtypeStruct((m, d), F32)],
        out_specs=[pl.BlockSpec((tm2, tn2), lambda i, j, k: (i, j))],
        epilogue=_ep_residual, name="mlp_down")[0]


def _out_proj(o, w_out, layer, h, *, tm, name):
    m, d = h.shape
    tn = 512
    tm = min(tm, X_TILE_BYTES // (o.shape[1] * o.dtype.itemsize))
    return matmul(
        [o], w_out, layer, col_block0=0, n_cols=d, tm=tm, tn=tn, tk=o.shape[1], extras=[h],
        extra_specs=[pl.BlockSpec((tm, tn), lambda i, j, k: (i, j))],
        out_shapes=[jax.ShapeDtypeStruct((m, d), F32)],
        out_specs=[pl.BlockSpec((tm, tn), lambda i, j, k: (i, j))],
        epilogue=_ep_residual, name=name)[0]


def _nsa_tables(pos):
    cos, sin = _rope_tables(pos, HEAD_DIM // 2)
    return jnp.concatenate([cos, cos], axis=1), jnp.concatenate([-sin, sin], axis=1)


def _nsa_prompt(h, xn, layer, w_in, w_out, cmp_w1, cmp_bias, cmp_w2, *, n_seq, seq):
    tm = PROMPT_ROW_TILE
    cos, sin_signed = _nsa_tables(jnp.arange(seq))
    (q, qt), (kv_rows, k_bf, vt_bf), gates_t = _project_nsa(xn, w_in, layer, cos, sin_signed, tm=tm // 2)
    n_chunks = seq // CMP_STRIDE
    chunks_per_step = 128
    kv_chunks = kv_rows[0].reshape(1, (n_seq * n_chunks) // chunks_per_step, chunks_per_step, CMP_STRIDE,
                                   2 * N_KV_HEADS, HEAD_DIM)
    kc = compress(kv_chunks, 0, None, cmp_w1, cmp_bias, cmp_w2, n_seq=n_seq, n_chunks=n_chunks, n_pages=1)
    n_blk = seq // SEL_BLOCK
    cover_t = _cover_t(n_chunks, n_blk, n_blk)
    o1, val = cmp_attention(q, kc, cover_t, gates_t, n_seq=n_seq, seq=seq, tq=256, pos0=0)
    bias_t = select_bias(val, n_blk=n_blk, tc=512, causal_seq=seq)
    nbp = pl.cdiv(n_blk, LANES) * LANES
    o2 = flash_attention(qt, k_bf, vt_bf, 1, gates_t, o1, n_seq=n_seq, seq=seq, tq=512, tk=512, bias=bias_t,
                         onehot=_block_onehot(seq, nbp))
    o3 = flash_attention(qt, k_bf, vt_bf, 2, gates_t, o2, n_seq=n_seq, seq=seq, tq=256, tk=256, window=WINDOW,
                         out_dtype=BF16)
    h = _out_proj(o3, w_out, layer, h, tm=tm, name="nsa_out_proj")
    return h, kv_rows


def _nsa_sample(h, xn, layer, w_in, w_out, cmp_w1, cmp_bias, cmp_w2, cache_cmp, cache_sel, win_buf, page_table,
                *, n_seq, n_tok, past_len):
    tq = SAMPLE_PAD
    m = n_seq * tq
    cos, sin_signed = _nsa_tables(jnp.tile(past_len + jnp.arange(tq), n_seq))
    (q, _), (kv_rows, _, _), gates_t = _project_nsa(xn, w_in, layer, cos, sin_signed, tm=m)
    n_chunks = past_len // CMP_STRIDE
    kc = compress(cache_cmp, layer, page_table, cmp_w1, cmp_bias, cmp_w2, n_seq=n_seq, n_chunks=n_chunks,
                  n_pages=16)
    n_blk = pl.cdiv(past_len + n_tok, SEL_BLOCK)
    nbv = pl.cdiv(n_blk, SUBLANES) * SUBLANES
    cover_t = _cover_t(n_chunks, n_blk, nbv)
    o1, val = cmp_attention(q, kc, cover_t, gates_t, n_seq=n_seq, seq=tq, tq=tq, pos0=past_len)
    ncols = n_seq * N_KV_HEADS * n_tok
    bias_t = select_bias(val[..., :n_tok].transpose(2, 0, 1, 3).reshape(nbv, ncols), n_blk=n_blk, tc=ncols)
    blocks_per_page = cache_sel.shape[2] // (2 * N_KV_HEADS * SEL_BLOCK)
    n_pages = page_table.shape[1]
    chosen = (bias_t == 0).reshape(nbv, n_seq, N_KV_HEADS * n_tok).any(axis=2)
    needed = chosen[:n_pages * blocks_per_page].reshape(n_pages, blocks_per_page, n_seq).any(axis=1).T
    nbp = pl.cdiv(n_blk + 1, LANES) * LANES
    bias = bias_t.reshape(nbv, n_seq, N_KV_HEADS, n_tok).transpose(1, 2, 3, 0)
    bias = jnp.pad(bias, ((0, 0), (0, 0), (0, 0), (0, nbp - nbv)), constant_values=NEG_INF)
    bias = jnp.pad(bias, ((0, 0), (0, 0), (0, tq - n_tok), (0, 0)))
    o2 = sel_attention_sample(q, bias, needed, kv_rows[1], cache_sel, layer, page_table, gates_t, o1,
                              n_pages_step=16, past_len=past_len)
    o3 = win_attention_sample(q, win_buf, layer, kv_rows[2], gates_t, o2, out_dtype=BF16)
    h = _out_proj(o3, w_out, layer, h, tm=m, name="nsa_out_proj_sample")
    return h, kv_rows


def _ret_layer(h, xn, layer, w_in, gn, w_out, state0, pos, *, n_seq, seq, chunk, n_tok, tm):
    m, d = xn.shape
    nk = RET_HEADS * RET_KDIM
    nv = RET_HEADS * RET_VDIM
    cos, sin = _rope_tables(pos, RET_KDIM // 2)
    tabs_per_seq = cos.shape[0] // tm
    tab_spec = pl.BlockSpec((tm, LANES), lambda i, j, k: (i % tabs_per_seq, 0))
    tn = 512

    def proj(col0, n_cols, dtype, epilogue, extras, extra_specs, name):
        return matmul(
            [xn], w_in, layer, col_block0=col0 // tn, n_cols=n_cols, tm=tm, tn=tn, tk=d,
            extras=extras, extra_specs=extra_specs,
            out_shapes=[jax.ShapeDtypeStruct((m, n_cols), dtype)],
            out_specs=[pl.BlockSpec((tm, tn), lambda i, j, k: (i, j))],
            epilogue=epilogue, name=name)[0]

    q = proj(0, nk, BF16, functools.partial(_ep_rope_ret, scale=1.0), [cos, sin], [tab_spec, tab_spec], "ret_proj_q")
    k = proj(nk, nk, F32, functools.partial(_ep_rope_ret, scale=RET_KDIM ** -0.5), [cos, sin],
             [tab_spec, tab_spec], "ret_proj_k")
    v = proj(2 * nk, nv, BF16, _ep_cast, [], [], "ret_proj_v")
    g = proj(2 * nk + nv, nv, F32, _ep_cast, [], [], "ret_proj_g")
    gated, state = retention(q, k, v, g, gn.reshape(gn.shape[0], 1, nv), layer, state0, n_seq=n_seq, seq=seq,
                             chunk=chunk, n_tok=n_tok)
    h = _out_proj(gated, w_out, layer, h, tm=tm, name="ret_out_proj")
    return h, state


def kernel(x_prompt, x_sample, cache_cmp_kv, cache_sel_kv, state_win_kv, state_ret, page_table, norm_mix, norm_ffn,
           norm_final, nsa_w_in, nsa_w_out, nsa_cmp_pos, nsa_cmp_w1, nsa_cmp_b1, nsa_cmp_w2, ret_w_in, ret_gn,
           ret_w_out, ffn_w1, ffn_w2):
    n_seq_p, seq_p, d = x_prompt.shape
    n_seq_s, n_tok_s, _ = x_sample.shape
    depth = norm_mix.shape[0]
    n_nsa = cache_sel_kv.shape[0]
    pool, page_rows = cache_sel_kv.shape[1], cache_sel_kv.shape[2]
    past_len = page_table.shape[1] * page_rows
    assert past_len % CMP_STRIDE == 0 and n_tok_s < CMP_STRIDE and n_tok_s <= SAMPLE_PAD
    assert state_win_kv.shape[2] == WINDOW and seq_p >= WINDOW

    hp = x_prompt.reshape(n_seq_p * seq_p, d)
    hs = jnp.pad(x_sample, ((0, 0), (0, SAMPLE_PAD - n_tok_s), (0, 0))).reshape(n_seq_s * SAMPLE_PAD, d)
    m_s = n_seq_s * SAMPLE_PAD
    rows_per_tok = 2 * N_KV_HEADS
    cache_cmp = cache_cmp_kv.reshape(n_nsa, pool, page_rows // CMP_STRIDE, CMP_STRIDE, rows_per_tok, HEAD_DIM)
    cache_sel = cache_sel_kv.reshape(n_nsa, pool, page_rows * rows_per_tok, HEAD_DIM)
    win_buf = state_win_kv.reshape(n_nsa, n_seq_s, WINDOW * rows_per_tok, HEAD_DIM)
    cmp_w1 = nsa_cmp_w1.reshape(n_nsa, 2, CMP_BLOCK * HEAD_DIM, HEAD_DIM)
    kv_shape = (2, N_KV_HEADS, HEAD_DIM)
    ffn_w2 = ffn_w2.astype(BF16)
    ret_w_out = ret_w_out.astype(BF16)

    cmp_p, cmp_s, sel_p, sel_s, win_p, win_s, ret_p, ret_s = [], [], [], [], [], [], [], []
    for layer in range(depth):
        xp = rmsnorm(hp, norm_mix[layer], BF16, 256)
        xs = rmsnorm(hs, norm_mix[layer], BF16, m_s)
        if layer % 2 == 0:
            a = layer // 2
            cmp_bias = _cmp_bias(cmp_w1, nsa_cmp_pos, nsa_cmp_b1, a)
            hp, kv_p = _nsa_prompt(hp, xp, a, nsa_w_in, nsa_w_out, cmp_w1, cmp_bias, nsa_cmp_w2, n_seq=n_seq_p,
                                   seq=seq_p)
            hs, kv_s = _nsa_sample(hs, xs, a, nsa_w_in, nsa_w_out, cmp_w1, cmp_bias, nsa_cmp_w2, cache_cmp,
                                   cache_sel, win_buf, page_table, n_seq=n_seq_s, n_tok=n_tok_s, past_len=past_len)
            kv_p = [r.reshape((n_seq_p, seq_p) + kv_shape) for r in kv_p]
            kv_s = [r.reshape((n_seq_s, SAMPLE_PAD) + kv_shape)[:, :n_tok_s] for r in kv_s]
            cmp_p.append(kv_p[0])
            sel_p.append(kv_p[1])
            win_p.append(kv_p[2][:, seq_p - WINDOW:])
            cmp_s.append(kv_s[0])
            sel_s.append(kv_s[1])
            win_s.append(jnp.concatenate([state_win_kv[a][:, n_tok_s:], kv_s[2]], axis=1))
        else:
            r = layer // 2
            zeros = jnp.zeros((n_seq_p, RET_HEADS, RET_KDIM, RET_VDIM), F32)
            hp, st_p = _ret_layer(hp, xp, r, ret_w_in, ret_gn, ret_w_out, zeros, jnp.arange(seq_p), n_seq=n_seq_p,
                                  seq=seq_p, chunk=RET_CHUNK, n_tok=RET_CHUNK, tm=PROMPT_ROW_TILE)
            hs, st_s = _ret_layer(hs, xs, r, ret_w_in, ret_gn, ret_w_out, state_ret[r],
                                  jnp.tile(past_len + jnp.arange(SAMPLE_PAD), n_seq_s), n_seq=n_seq_s,
                                  seq=SAMPLE_PAD, chunk=SAMPLE_PAD,
                                  n_tok=n_tok_s, tm=m_s)
            ret_p.append(st_p)
            ret_s.append(st_s)
        hp = _mlp(hp, norm_ffn[layer], ffn_w1, ffn_w2, layer, tm=PROMPT_ROW_TILE)
        hs = _mlp(hs, norm_ffn[layer], ffn_w1, ffn_w2, layer, tm=m_s)
    y_prompt = rmsnorm(hp, norm_final, F32, 256).reshape(n_seq_p, seq_p, d)
    y_sample = rmsnorm(hs, norm_final, F32, m_s).reshape(n_seq_s, SAMPLE_PAD, d)[:, :n_tok_s]
    return (y_prompt, y_sample, jnp.stack(cmp_p), jnp.stack(cmp_s), jnp.stack(sel_p), jnp.stack(sel_s),
            jnp.stack(win_p), jnp.stack(win_s), jnp.stack(ret_p), jnp.stack(ret_s))
```

```python
import functools

import numpy as np
import jax
import jax.numpy as jnp
from jax import lax
from jax.experimental import pallas as pl
from jax.experimental.pallas import tpu as pltpu

F32 = jnp.float32
BF16 = jnp.bfloat16

N_HEADS = 16
HEAD_DIM = 128
N_KV_HEADS = 4
GQA_GROUP = N_HEADS // N_KV_HEADS
KV_ROW = 2 * N_KV_HEADS * HEAD_DIM
CMP_BLOCK = 32
CMP_STRIDE = 16
SEL_BLOCK = 64
N_SEL = 16
N_LOCAL = 2
WINDOW = 512
ROPE_THETA = 10000.0
RET_HEADS = 8
RET_KDIM = 256
RET_VDIM = 512
RET_CHUNK = 128
RET_HEADS_PER_STEP = 4
RMS_EPS = 1e-6
GN_EPS = 1e-5
NEG_INF = -1e30
FORCE_SCORE = 1e9
LOG2_E = 1.4426950408889634

SUBLANES = 8
LANES = 128
VMEM_LIMIT_BYTES = 56 * 1024 * 1024
SAMPLE_PAD = 16
PROMPT_ROW_TILE = 2048
X_TILE_BYTES = 8 * 1024 * 1024

_NT = (((1,), (1,)), ((), ()))
_TN = (((0,), (0,)), ((), ()))


def _params(*semantics):
    return pltpu.CompilerParams(dimension_semantics=semantics, vmem_limit_bytes=VMEM_LIMIT_BYTES)


def _rmsnorm_kernel(x_ref, g_ref, o_ref):
    x = x_ref[...]
    ms = jnp.mean(x * x, axis=-1, keepdims=True)
    o_ref[...] = (x * lax.rsqrt(ms + RMS_EPS) * g_ref[...]).astype(o_ref.dtype)


def rmsnorm(x, g, out_dtype, tm, row_block0=0, n_blocks=None):
    m, d = x.shape
    if n_blocks is None:
        n_blocks = m // tm
    return pl.pallas_call(
        _rmsnorm_kernel,
        grid=(n_blocks,),
        in_specs=[pl.BlockSpec((tm, d), lambda i: (i + row_block0, 0)), pl.BlockSpec((1, d), lambda i: (0, 0))],
        out_specs=pl.BlockSpec((tm, d), lambda i: (i, 0)),
        out_shape=jax.ShapeDtypeStruct((n_blocks * tm, d), out_dtype),
        compiler_params=_params("parallel"),
        name="rmsnorm",
    )(x, g.reshape(1, d))


def _mm_kernel(*refs, n_x, n_extra, n_out, nk, epilogue):
    x_refs = refs[:n_x]
    w_ref = refs[n_x]
    extra = refs[n_x + 1:n_x + 1 + n_extra]
    outs = refs[n_x + 1 + n_extra:n_x + 1 + n_extra + n_out]
    x = x_refs[0][...]
    for r in x_refs[1:]:
        x = x + r[...]
    part = jnp.dot(x.astype(BF16), w_ref[...].astype(BF16), preferred_element_type=F32)
    if nk == 1:
        epilogue(part, extra, outs)
        return
    k = pl.program_id(2)
    if epilogue is _ep_residual:
        @pl.when(k == 0)
        def _():
            outs[0][...] = extra[0][...] + part

        @pl.when(k > 0)
        def _():
            outs[0][...] += part
        return
    acc_ref = refs[-1]

    @pl.when(k == 0)
    def _():
        acc_ref[...] = part

    @pl.when(k > 0)
    def _():
        acc_ref[...] += part

    @pl.when(k == nk - 1)
    def _():
        epilogue(acc_ref[...], extra, outs)


def matmul(xs, w, layer, *, col_block0, n_cols, tm, tn, tk, extras, extra_specs, out_shapes, out_specs,
           epilogue, name, column_steps_ordered=False):
    m, kdim = xs[0].shape
    nk = kdim // tk
    grid = (m // tm, pl.cdiv(n_cols, tn), nk)
    in_specs = [pl.BlockSpec((tm, tk), lambda i, j, k: (i, k)) for _ in xs]
    in_specs.append(pl.BlockSpec((None, tk, tn), lambda i, j, k: (layer, k, j + col_block0)))
    in_specs.extend(extra_specs)
    scratch = [pltpu.VMEM((tm, tn), F32)] if nk > 1 and epilogue is not _ep_residual else []
    kern = functools.partial(_mm_kernel, n_x=len(xs), n_extra=len(extras), n_out=len(out_shapes), nk=nk,
                             epilogue=epilogue)
    return pl.pallas_call(
        kern,
        grid=grid,
        in_specs=in_specs,
        out_specs=out_specs,
        out_shape=out_shapes,
        scratch_shapes=scratch,
        compiler_params=_params("parallel", "arbitrary" if column_steps_ordered else "parallel", "arbitrary"),
        name=name,
    )(*xs, w, *extras)


def _rope_half_lane(x, cos, sin_signed):
    return x * cos + pltpu.roll(x, HEAD_DIM // 2, 1) * sin_signed


def _ep_rope_q(part, extra, outs, *, scale):
    cos = extra[0][...]
    sin = extra[1][...]
    for h in range(part.shape[1] // HEAD_DIM):
        sl = slice(h * HEAD_DIM, (h + 1) * HEAD_DIM)
        r = _rope_half_lane(part[:, sl], cos, sin) * scale
        outs[0][:, sl] = r.astype(outs[0].dtype)
        outs[1][sl, :] = r.T.astype(outs[1].dtype)


def _ep_nsa_kv(part, extra, outs):
    tm = part.shape[0]
    rows_per_tok = 2 * N_KV_HEADS
    j = pl.program_id(1)

    def heads():
        return [part[:, h * HEAD_DIM:(h + 1) * HEAD_DIM] for h in range(N_KV_HEADS)]

    for br in range(3):
        @pl.when(j == 2 * br)
        def _(br=br):
            cos = extra[0][...]
            sin = extra[1][...]
            for h, x in enumerate(heads()):
                k = _rope_half_lane(x, cos, sin)
                outs[br][pl.ds(h, tm, stride=rows_per_tok), :] = k
                outs[3][h] = k.astype(BF16)

        @pl.when(j == 2 * br + 1)
        def _(br=br):
            for h, v in enumerate(heads()):
                outs[br][pl.ds(N_KV_HEADS + h, tm, stride=rows_per_tok), :] = v
                outs[4][h] = v.T.astype(BF16)


def _ep_sigmoid(part, extra, outs):
    outs[0][...] = jax.nn.sigmoid(part)


def _ep_residual(part, extra, outs):
    outs[0][...] = extra[0][...] + part


def _ep_relu2(part, extra, outs):
    h = jnp.maximum(part, 0.0)
    outs[0][...] = (h * h).astype(outs[0].dtype)


def _ep_cast(part, extra, outs):
    outs[0][...] = part.astype(outs[0].dtype)


def _ep_rope_ret(part, extra, outs, *, scale):
    cos = extra[0][...]
    sin = extra[1][...]
    for h in range(part.shape[1] // RET_KDIM):
        a = slice(h * RET_KDIM, h * RET_KDIM + LANES)
        b = slice(h * RET_KDIM + LANES, (h + 1) * RET_KDIM)
        x1 = part[:, a]
        x2 = part[:, b]
        outs[0][:, a] = ((x1 * cos - x2 * sin) * scale).astype(outs[0].dtype)
        outs[0][:, b] = ((x2 * cos + x1 * sin) * scale).astype(outs[0].dtype)


def _rope_tables(pos, half):
    inv = ROPE_THETA ** (-jnp.arange(half, dtype=F32) / half)
    ang = pos.astype(F32)[:, None] * inv[None, :]
    return jnp.cos(ang), jnp.sin(ang)


def _compress_kernel(*refs, n_pages, paged):
    if paged:
        refs = refs[1:]
    page_refs = refs[:n_pages]
    next_ref, w1_ref, bias_ref, w2_ref, out_ref, shift_ref, y_ref = refs[n_pages:]
    nc = page_refs[0].shape[0]
    n_heads = N_KV_HEADS
    n_main = n_pages * nc * n_heads
    k_half = CMP_STRIDE * HEAD_DIM
    tiles = [[page_refs[p][:, s] for s in range(CMP_STRIDE)] for p in range(n_pages)]
    next_tiles = [next_ref[0, s] for s in range(CMP_STRIDE)]
    for e in range(2):
        hsl = slice(e * n_heads, (e + 1) * n_heads)
        blocks = [
            jnp.concatenate([tiles[p][s][:, hsl, :].reshape(nc * n_heads, HEAD_DIM) for s in range(CMP_STRIDE)],
                            axis=1)
            for p in range(n_pages)]
        blocks.append(jnp.concatenate([next_tiles[s][hsl, :] for s in range(CMP_STRIDE)], axis=1))
        blocks.append(jnp.zeros((SUBLANES - n_heads, CMP_STRIDE * HEAD_DIM), F32))
        x = jnp.concatenate(blocks, axis=0).astype(BF16)
        wcat = jnp.concatenate([w1_ref[e, :k_half], w1_ref[e, k_half:]], axis=1).astype(BF16)
        per = jnp.dot(x, wcat, preferred_element_type=F32)
        hid = per.shape[1] // 2
        shift_ref[...] = per[:, hid:]
        h = per[:n_main, :hid] + shift_ref[pl.ds(n_heads, n_main), :]
        h = jax.nn.gelu(h + bias_ref[e:e + 1, :])
        y_ref[...] = jnp.dot(h.astype(BF16), w2_ref[e].astype(BF16), preferred_element_type=F32)
        for g in range(n_heads):
            out_ref[e * n_heads + g] = y_ref[pl.ds(g, n_main // n_heads, stride=n_heads), :].astype(out_ref.dtype)


def compress(kv, layer, page_table, w1, bias, w2, *, n_seq, n_chunks, n_pages):
    paged = page_table is not None
    nc = kv.shape[2]
    n_main = n_pages * nc
    steps = n_chunks // n_main
    pages_per_seq = steps * n_pages
    rows_per_tok = 2 * N_KV_HEADS

    def nxt(s):
        return jnp.minimum((s + 1) * n_pages, pages_per_seq - 1)

    if paged:
        def page_map(p):
            return lambda b, s, pt: (layer, pt[b, s * n_pages + p], 0, 0, 0, 0)

        def next_map(b, s, pt):
            return (layer, pt[b, nxt(s)], 0, 0, 0, 0)

        def fixed(*idx):
            return lambda b, s, pt: idx

        out_map = lambda b, s, pt: (b, 0, s, 0)
    else:
        def page_map(p):
            return lambda b, s: (layer, b * pages_per_seq + s * n_pages + p, 0, 0, 0, 0)

        def next_map(b, s):
            return (layer, b * pages_per_seq + nxt(s), 0, 0, 0, 0)

        def fixed(*idx):
            return lambda b, s: idx

        out_map = lambda b, s: (b, 0, s, 0)

    tile = (CMP_STRIDE, rows_per_tok, HEAD_DIM)
    in_specs = [pl.BlockSpec((None, None, nc) + tile, page_map(p)) for p in range(n_pages)]
    in_specs += [
        pl.BlockSpec((None, None, 1) + tile, next_map),
        pl.BlockSpec((None, 2, CMP_BLOCK * HEAD_DIM, HEAD_DIM), fixed(layer, 0, 0, 0)),
        pl.BlockSpec((2, HEAD_DIM), fixed(0, 0)),
        pl.BlockSpec((None, 2, HEAD_DIM, HEAD_DIM), fixed(layer, 0, 0, 0)),
    ]
    kern = functools.partial(_compress_kernel, n_pages=n_pages, paged=paged)
    grid_spec = pltpu.PrefetchScalarGridSpec(
        num_scalar_prefetch=1 if paged else 0,
        grid=(n_seq, steps),
        in_specs=in_specs,
        out_specs=pl.BlockSpec((None, rows_per_tok, n_main, HEAD_DIM), out_map),
        scratch_shapes=[pltpu.VMEM((n_main * N_KV_HEADS + SUBLANES, HEAD_DIM), F32),
                        pltpu.VMEM((n_main * N_KV_HEADS, HEAD_DIM), F32)],
    )
    args = ([page_table] if paged else []) + [kv] * (n_pages + 1) + [w1, bias, w2]
    return pl.pallas_call(
        kern,
        grid_spec=grid_spec,
        out_shape=jax.ShapeDtypeStruct((n_seq, rows_per_tok, n_chunks, HEAD_DIM), BF16),
        compiler_params=_params("parallel", "arbitrary"),
        name="nsa_compress",
    )(*args)


def _group_rows(q):
    return jnp.concatenate([q[:, h * HEAD_DIM:(h + 1) * HEAD_DIM] for h in range(GQA_GROUP)], axis=0)


def _cmp_kernel(q_ref, k_ref, v_ref, cov_ref, gate_ref, o_ref, val_ref, *, tq, pos0):
    qi = pl.program_id(2)
    q4 = _group_rows(q_ref[...])
    s = lax.dot_general(q4, k_ref[...], _NT, preferred_element_type=F32)
    row = lax.broadcasted_iota(jnp.int32, s.shape, 0)
    col = lax.broadcasted_iota(jnp.int32, s.shape, 1)
    qpos = pos0 + qi * tq + jnp.bitwise_and(row, tq - 1)
    valid = col * CMP_STRIDE + (CMP_BLOCK - 1) <= qpos
    s = jnp.where(valid, s, NEG_INF)
    m = jnp.max(s, axis=1, keepdims=True)
    e = jnp.where(valid, jnp.exp2(s - m), 0.0)
    l = jnp.sum(e, axis=1, keepdims=True)
    p = e / jnp.where(l > 0.0, l, 1.0)
    o4 = jnp.dot(p.astype(BF16), v_ref[...], preferred_element_type=F32)
    for h in range(GQA_GROUP):
        o_ref[:, h * HEAD_DIM:(h + 1) * HEAD_DIM] = o4[h * tq:(h + 1) * tq] * gate_ref[:, h:h + 1]
    imp = p[0:tq] + p[tq:2 * tq] + p[2 * tq:3 * tq] + p[3 * tq:4 * tq]
    hi = imp.astype(BF16)
    r1 = imp - hi.astype(F32)
    mid = r1.astype(BF16)
    lo = (r1 - mid.astype(F32)).astype(BF16)
    cov = cov_ref[...]
    score = (lax.dot_general(cov, hi, _NT, preferred_element_type=F32)
             + lax.dot_general(cov, mid, _NT, preferred_element_type=F32)
             + lax.dot_general(cov, lo, _NT, preferred_element_type=F32))
    jb = lax.broadcasted_iota(jnp.int32, score.shape, 0)
    qp = pos0 + qi * tq + lax.broadcasted_iota(jnp.int32, score.shape, 1)
    cur = jnp.right_shift(qp, SEL_BLOCK.bit_length() - 1)
    visible = jb <= cur
    forced = (jb == 0) | (visible & (jb > cur - N_LOCAL))
    val_ref[...] = jnp.where(forced, FORCE_SCORE, jnp.where(visible, score, -FORCE_SCORE))


def cmp_attention(q, kc, cover_t, gates_t, *, n_seq, seq, tq, pos0):
    nq = seq // tq
    ncp = kc.shape[2]
    nbv = cover_t.shape[0]
    gw = GQA_GROUP * HEAD_DIM
    kern = functools.partial(_cmp_kernel, tq=tq, pos0=pos0)
    if tq % LANES == 0:
        val_spec = pl.BlockSpec((nbv, tq), lambda b, g, i: (0, (b * N_KV_HEADS + g) * nq + i))
        val_shape = jax.ShapeDtypeStruct((nbv, n_seq * N_KV_HEADS * seq), F32)
    else:
        val_spec = pl.BlockSpec((None, None, nbv, tq), lambda b, g, i: (b, g, 0, i))
        val_shape = jax.ShapeDtypeStruct((n_seq, N_KV_HEADS, nbv, seq), F32)
    return pl.pallas_call(
        kern,
        grid=(n_seq, N_KV_HEADS, nq),
        in_specs=[
            pl.BlockSpec((tq, gw), lambda b, g, i: (b * nq + i, g)),
            pl.BlockSpec((None, None, ncp, HEAD_DIM), lambda b, g, i: (b, g, 0, 0)),
            pl.BlockSpec((None, None, ncp, HEAD_DIM), lambda b, g, i: (b, N_KV_HEADS + g, 0, 0)),
            pl.BlockSpec((nbv, ncp), lambda b, g, i: (0, 0)),
            pl.BlockSpec((None, None, tq, GQA_GROUP), lambda b, g, i: (0, g, b * nq + i, 0)),
        ],
        out_specs=[pl.BlockSpec((tq, gw), lambda b, g, i: (b * nq + i, g)), val_spec],
        out_shape=[jax.ShapeDtypeStruct((n_seq * seq, N_HEADS * HEAD_DIM), F32), val_shape],
        compiler_params=_params("parallel", "parallel", "parallel"),
        name="nsa_cmp_attention",
    )(q, kc, kc, cover_t, gates_t)


def _cover_t(n_chunks, n_blk, nbv):
    n_cmp = n_chunks - CMP_BLOCK // CMP_STRIDE + 1
    c0 = np.arange(n_chunks) * CMP_STRIDE
    j0 = np.arange(nbv) * SEL_BLOCK
    cov = (c0[None, :] < j0[:, None] + SEL_BLOCK) & (c0[None, :] + CMP_BLOCK > j0[:, None])
    cov &= (np.arange(n_chunks)[None, :] < n_cmp) & (np.arange(nbv)[:, None] < n_blk)
    return jnp.asarray(cov.astype(np.float32), dtype=BF16)


def _rank_kernel(val_ref, out_ref, *, n_blk, n_sel, causal_seq):
    val = val_ref[...]
    tc = val.shape[1]
    jb = lax.broadcasted_iota(jnp.int32, val.shape, 0)
    if causal_seq is None:
        n_iter = n_blk
    else:
        pos0 = (pl.program_id(0) % (causal_seq // tc)) * tc
        n_iter = jnp.minimum(n_blk, (pos0 + tc - 1) // SEL_BLOCK + 1)

    def body(i, rank):
        row = val_ref[pl.ds(i, 1), :]
        beats = (row > val) | ((row == val) & (i < jb))
        return rank + jnp.where(beats, 1, 0)

    rank = lax.fori_loop(0, n_iter, body, jnp.zeros(val.shape, jnp.int32))
    chosen = (rank < n_sel) & (val > -0.5 * FORCE_SCORE)
    out_ref[...] = jnp.where(chosen, 0.0, NEG_INF).astype(out_ref.dtype)


def select_bias(val, *, n_blk, tc, causal_seq=None):
    nbv, ncols = val.shape
    kern = functools.partial(_rank_kernel, n_blk=n_blk, n_sel=min(N_SEL, n_blk), causal_seq=causal_seq)
    return pl.pallas_call(
        kern,
        grid=(ncols // tc,),
        in_specs=[pl.BlockSpec((nbv, tc), lambda i: (0, i))],
        out_specs=pl.BlockSpec((nbv, tc), lambda i: (0, i)),
        out_shape=jax.ShapeDtypeStruct((nbv, ncols), BF16),
        compiler_params=_params("parallel"),
        name="nsa_select_rank",
    )(val)


_STEP_FIRST, _STEP_LAST, _STEP_EDGE = 1, 2, 4


def _flash_steps(seq, tq, tk, window):
    qi_l, kt_l, fl_l = [], [], []
    for qi in range(seq // tq):
        q0, q1 = qi * tq, qi * tq + tq - 1
        kts = [kt for kt in range(seq // tk)
               if kt * tk <= q1 and (window is None or q0 - (kt * tk + tk - 1) < window)]
        for n, kt in enumerate(kts):
            k0, k1 = kt * tk, kt * tk + tk - 1
            all_visible = q0 - k1 >= 0 and (window is None or q1 - k0 < window)
            flags = (_STEP_FIRST if n == 0 else 0) | (_STEP_LAST if n == len(kts) - 1 else 0)
            flags |= 0 if all_visible else _STEP_EDGE
            qi_l.append(qi)
            kt_l.append(kt)
            fl_l.append(flags)
    return (jnp.asarray(np.array(qi_l, np.int32)), jnp.asarray(np.array(kt_l, np.int32)),
            jnp.asarray(np.array(fl_l, np.int32)))


def _flash_kernel(*refs, tq, tk, use_bias, window, out_dtype):
    qi_ref, kt_ref, flag_ref = refs[:3]
    if use_bias:
        qt_ref, bias_ref, k_ref, vt_ref, e_ref, gate_ref, prev_ref, o_ref, qa_ref, m_ref, acc_ref = refs[3:]
    else:
        qt_ref, k_ref, vt_ref, gate_ref, prev_ref, o_ref, qa_ref, m_ref, acc_ref = refs[3:]
    s = pl.program_id(2)
    qi = qi_ref[s]
    kt = kt_ref[s]
    flags = flag_ref[s]
    first = jnp.bitwise_and(flags, _STEP_FIRST) != 0
    last = jnp.bitwise_and(flags, _STEP_LAST) != 0
    edge = jnp.bitwise_and(flags, _STEP_EDGE) != 0

    @pl.when(first)
    def _():
        for h in range(GQA_GROUP):
            qa_ref[0:HEAD_DIM, h * tq:(h + 1) * tq] = qt_ref[h * HEAD_DIM:(h + 1) * HEAD_DIM, :]
            if use_bias:
                nbv = bias_ref.shape[0]
                qa_ref[HEAD_DIM:HEAD_DIM + nbv, h * tq:(h + 1) * tq] = bias_ref[...]
                if HEAD_DIM + nbv < qa_ref.shape[0]:
                    qa_ref[HEAD_DIM + nbv:, h * tq:(h + 1) * tq] = jnp.zeros(
                        (qa_ref.shape[0] - HEAD_DIM - nbv, tq), BF16)
        m_ref[...] = jnp.full(m_ref.shape, NEG_INF, F32)
        acc_ref[...] = jnp.zeros(acc_ref.shape, F32)

    def step(masked):
        if use_bias:
            ka = jnp.concatenate([k_ref[...], e_ref[...]], axis=1)
        else:
            ka = k_ref[...]
        st = jnp.dot(ka, qa_ref[...], preferred_element_type=F32)
        if masked:
            kpos = kt * tk + lax.broadcasted_iota(jnp.int32, st.shape, 0)
            qpos = qi * tq + jnp.bitwise_and(lax.broadcasted_iota(jnp.int32, st.shape, 1), tq - 1)
            d = qpos - kpos
            ok = d >= 0
            if window is not None:
                ok = ok & (d < window)
            st = jnp.where(ok, st, NEG_INF)
        m_old = m_ref[...]
        m_new = jnp.maximum(m_old, jnp.max(st, axis=0, keepdims=True))
        alpha = jnp.exp2(m_old - m_new)
        p = jnp.exp2(st - m_new).astype(BF16)
        va = jnp.concatenate([vt_ref[...], jnp.ones((acc_ref.shape[0] - HEAD_DIM, tk), BF16)], axis=0)
        acc_ref[...] = alpha * acc_ref[...] + jnp.dot(va, p, preferred_element_type=F32)
        m_ref[...] = m_new

    @pl.when(edge)
    def _():
        step(True)

    @pl.when(jnp.logical_not(edge))
    def _():
        step(False)

    @pl.when(last)
    def _():
        acc = acc_ref[...]
        ot = acc[:HEAD_DIM] / acc[HEAD_DIM:HEAD_DIM + 1]
        for h in range(GQA_GROUP):
            sl = slice(h * HEAD_DIM, (h + 1) * HEAD_DIM)
            o_h = ot[:, h * tq:(h + 1) * tq].T
            o_ref[:, sl] = (prev_ref[:, sl] + o_h * gate_ref[:, h:h + 1]).astype(out_dtype)


def flash_attention(qt, k, vt, branch, gates_t, prev, *, n_seq, seq, tq, tk, bias=None, onehot=None, window=None,
                    out_dtype=F32):
    use_bias = bias is not None
    nq = seq // tq
    nkt = seq // tk
    gw = GQA_GROUP * HEAD_DIM
    qi_tab, kt_tab, flag_tab = _flash_steps(seq, tq, tk, window)

    o_spec = pl.BlockSpec((tq, gw), lambda b, g, s, qi, kt, fl: (b * nq + qi[s], g))
    qt_spec = pl.BlockSpec((gw, tq), lambda b, g, s, qi, kt, fl: (g, b * nq + qi[s]))
    k_spec = pl.BlockSpec((None, None, tk, HEAD_DIM), lambda b, g, s, qi, kt, fl: (branch, g, b * nkt + kt[s], 0))
    vt_spec = pl.BlockSpec((None, None, HEAD_DIM, tk), lambda b, g, s, qi, kt, fl: (branch, g, 0, b * nkt + kt[s]))
    gate_spec = pl.BlockSpec((None, None, tq, GQA_GROUP), lambda b, g, s, qi, kt, fl: (branch, g, b * nq + qi[s], 0))
    in_specs = [qt_spec]
    args = [qt]
    qa_rows = HEAD_DIM
    if use_bias:
        nbv = bias.shape[0]
        nbp = onehot.shape[1]
        qa_rows += nbp
        in_specs.append(
            pl.BlockSpec((nbv, tq), lambda b, g, s, qi, kt, fl: (0, (b * N_KV_HEADS + g) * nq + qi[s])))
        args.append(bias)
    in_specs += [k_spec, vt_spec]
    args += [k, vt]
    if use_bias:
        in_specs.append(pl.BlockSpec((tk, nbp), lambda b, g, s, qi, kt, fl: (kt[s], 0)))
        args.append(onehot)
    in_specs += [gate_spec, o_spec]
    args += [gates_t, prev]
    kern = functools.partial(_flash_kernel, tq=tq, tk=tk, use_bias=use_bias, window=window, out_dtype=out_dtype)
    cols = GQA_GROUP * tq
    grid_spec = pltpu.PrefetchScalarGridSpec(
        num_scalar_prefetch=3,
        grid=(n_seq, N_KV_HEADS, qi_tab.shape[0]),
        in_specs=in_specs,
        out_specs=o_spec,
        scratch_shapes=[
            pltpu.VMEM((qa_rows, cols), BF16),
            pltpu.VMEM((1, cols), F32),
            pltpu.VMEM((HEAD_DIM + 2 * SUBLANES, cols), F32),
        ],
    )
    return pl.pallas_call(
        kern,
        grid_spec=grid_spec,
        out_shape=jax.ShapeDtypeStruct((n_seq * seq, N_HEADS * HEAD_DIM), out_dtype),
        compiler_params=_params("parallel", "parallel", "arbitrary"),
        name="nsa_sel_attention" if use_bias else "nsa_win_attention",
    )(qi_tab, kt_tab, flag_tab, *args)


def _block_onehot(n_keys, nbp):
    e = (np.arange(n_keys)[:, None] // SEL_BLOCK) == np.arange(nbp)[None, :]
    return jnp.asarray(e.astype(np.float32), dtype=BF16)


def _sel_sample_kernel(*refs, n_pages, nbp, past_len, nsteps, tq):
    logical_ref, count_ref = refs[1:3]
    qt_ref, bias_ref, new_ref = refs[3:6]
    page_refs = refs[6:6 + n_pages]
    gate_ref, prev_ref, o_ref, m_ref, acc_ref = refs[6 + n_pages:]
    b = pl.program_id(0)
    step = pl.program_id(1)
    page_rows = page_refs[0].shape[0]
    rows_per_tok = 2 * N_KV_HEADS
    page_tokens = page_rows // rows_per_tok
    blocks_per_page = page_tokens // SEL_BLOCK
    n_needed = count_ref[b]
    cols = qt_ref.shape[2]

    def scores(g, k, key_bias):
        return jnp.dot(k, qt_ref[g], preferred_element_type=F32) + key_bias

    def block_bias(g, first_block, n_blocks, valid):
        rows = bias_ref[g, pl.ds(first_block, n_blocks), :]
        rows = jnp.where(valid, rows, NEG_INF)
        return jnp.concatenate(
            [jnp.broadcast_to(rows[j:j + 1], (SEL_BLOCK, cols)) for j in range(n_blocks)], axis=0)

    def weighted_values(p, v):
        va = jnp.concatenate([v, jnp.ones(v.shape, BF16)], axis=1)
        return lax.dot_general(p, va, _TN, preferred_element_type=F32)

    @pl.when(step == 0)
    def _():
        for g in range(N_KV_HEADS):
            kn = new_ref[pl.ds(g, tq, stride=2 * N_KV_HEADS), :].astype(BF16)
            vn = new_ref[pl.ds(N_KV_HEADS + g, tq, stride=2 * N_KV_HEADS), :].astype(BF16)
            new_bias = block_bias(g, past_len // SEL_BLOCK, 1, True)[:tq]
            st = scores(g, kn, new_bias)
            key = lax.broadcasted_iota(jnp.int32, st.shape, 0)
            tok = jnp.bitwise_and(lax.broadcasted_iota(jnp.int32, st.shape, 1), tq - 1)
            st = jnp.where(key <= tok, st, NEG_INF)
            m = jnp.max(st, axis=0, keepdims=True)
            p = jnp.exp2(st - m).astype(BF16)
            m_ref[g] = m
            acc_ref[g] = weighted_values(p, vn)

    @pl.when(step * n_pages < n_needed)
    def _():
        slots = [step * n_pages + p for p in range(n_pages)]
        first_blocks = [logical_ref[b, slot] * blocks_per_page for slot in slots]
        for g in range(N_KV_HEADS):
            k = jnp.concatenate(
                [page_refs[p][pl.ds(g, page_tokens, stride=rows_per_tok), :] for p in range(n_pages)],
                axis=0).astype(BF16)
            v = jnp.concatenate(
                [page_refs[p][pl.ds(N_KV_HEADS + g, page_tokens, stride=rows_per_tok), :] for p in range(n_pages)],
                axis=0).astype(BF16)
            key_bias = jnp.concatenate(
                [block_bias(g, first_blocks[p], blocks_per_page, slots[p] < n_needed) for p in range(n_pages)],
                axis=0)
            st = scores(g, k, key_bias)
            m_old = m_ref[g]
            m_new = jnp.maximum(m_old, jnp.max(st, axis=0, keepdims=True))
            alpha = jnp.exp2(m_old - m_new)
            p = jnp.exp2(st - m_new).astype(BF16)
            n_cols = alpha.shape[1]
            eye = (lax.broadcasted_iota(jnp.int32, (n_cols, n_cols), 0)
                   == lax.broadcasted_iota(jnp.int32, (n_cols, n_cols), 1))
            alpha_col = jnp.sum(jnp.where(eye, alpha, 0.0), axis=1, keepdims=True)
            acc_ref[g] = alpha_col * acc_ref[g] + weighted_values(p, v)
            m_ref[g] = m_new

    @pl.when(step == nsteps - 1)
    def _():
        for g in range(N_KV_HEADS):
            acc = acc_ref[g]
            o4 = acc[:, :HEAD_DIM] / acc[:, HEAD_DIM:HEAD_DIM + 1]
            for h in range(GQA_GROUP):
                hh = g * GQA_GROUP + h
                sl = slice(hh * HEAD_DIM, (hh + 1) * HEAD_DIM)
                o_ref[:, sl] = prev_ref[:, sl] + o4[h * tq:(h + 1) * tq] * gate_ref[g, :, h:h + 1]


def sel_attention_sample(q, bias, needed, kv_new, cache, layer, page_table, gates_t, prev, *, n_pages_step,
                         past_len):
    n_seq, pages = page_table.shape
    tq = SAMPLE_PAD
    nbp = bias.shape[-1]
    page_rows = cache.shape[2]
    nsteps = pages // n_pages_step
    dm = N_HEADS * HEAD_DIM
    order = jnp.argsort(jnp.logical_not(needed), axis=1, stable=True).astype(jnp.int32)
    count = jnp.sum(needed, axis=1).astype(jnp.int32)
    slot = jnp.minimum(jnp.arange(pages, dtype=jnp.int32)[None, :], count[:, None] - 1)
    logical = jnp.take_along_axis(order, slot, axis=1)
    physical = jnp.take_along_axis(page_table, logical, axis=1)
    row_spec = pl.BlockSpec((tq, dm), lambda b, s, *_: (b, 0))
    assert past_len % SEL_BLOCK + tq <= SEL_BLOCK
    cols = GQA_GROUP * tq
    q_t = q.reshape(n_seq, tq, N_KV_HEADS, GQA_GROUP, HEAD_DIM).transpose(0, 2, 4, 3, 1)
    q_t = q_t.reshape(n_seq, N_KV_HEADS, HEAD_DIM, cols)
    bias_t = jnp.tile(bias.transpose(0, 1, 3, 2), (1, 1, 1, GQA_GROUP)).astype(F32)

    def page_map(p):
        return lambda b, s, phys, *_: (layer, phys[b, s * n_pages_step + p], 0, 0)

    in_specs = [
        pl.BlockSpec((None, N_KV_HEADS, HEAD_DIM, cols), lambda b, s, *_: (b, 0, 0, 0)),
        pl.BlockSpec((None, N_KV_HEADS, nbp, cols), lambda b, s, *_: (b, 0, 0, 0)),
        pl.BlockSpec((tq * 2 * N_KV_HEADS, HEAD_DIM), lambda b, s, *_: (b, 0)),
    ]
    in_specs += [pl.BlockSpec((None, None, page_rows, HEAD_DIM), page_map(p)) for p in range(n_pages_step)]
    in_specs += [
        pl.BlockSpec((None, N_KV_HEADS, tq, GQA_GROUP), lambda b, s, *_: (1, 0, b, 0)),
        row_spec,
    ]
    kern = functools.partial(_sel_sample_kernel, n_pages=n_pages_step, nbp=nbp, past_len=past_len, nsteps=nsteps,
                             tq=tq)
    grid_spec = pltpu.PrefetchScalarGridSpec(
        num_scalar_prefetch=3,
        grid=(n_seq, nsteps),
        in_specs=in_specs,
        out_specs=row_spec,
        scratch_shapes=[
            pltpu.VMEM((N_KV_HEADS, 1, cols), F32),
            pltpu.VMEM((N_KV_HEADS, cols, 2 * HEAD_DIM), F32),
        ],
    )
    return pl.pallas_call(
        kern,
        grid_spec=grid_spec,
        out_shape=jax.ShapeDtypeStruct((n_seq * tq, dm), F32),
        compiler_params=_params("parallel", "arbitrary"),
        name="nsa_sel_attention_sample",
    )(physical, logical, count, q_t, bias_t, kv_new, *([cache] * n_pages_step), gates_t, prev)


def _win_sample_kernel(q_ref, buf_ref, new_ref, gate_ref, prev_ref, o_ref, *, tq, out_dtype):
    rows_per_tok = 2 * N_KV_HEADS
    w_buf = buf_ref.shape[0] // rows_per_tok
    gw = GQA_GROUP * HEAD_DIM
    for g in range(N_KV_HEADS):
        q4 = _group_rows(q_ref[:, g * gw:(g + 1) * gw])
        kb = buf_ref[pl.ds(g, w_buf, stride=rows_per_tok), :].astype(BF16)
        vb = buf_ref[pl.ds(N_KV_HEADS + g, w_buf, stride=rows_per_tok), :].astype(BF16)
        kn = new_ref[pl.ds(g, tq, stride=rows_per_tok), :].astype(BF16)
        vn = new_ref[pl.ds(N_KV_HEADS + g, tq, stride=rows_per_tok), :].astype(BF16)
        sb = lax.dot_general(q4, kb, _NT, preferred_element_type=F32)
        sn = lax.dot_general(q4, kn, _NT, preferred_element_type=F32)
        tb = jnp.bitwise_and(lax.broadcasted_iota(jnp.int32, sb.shape, 0), tq - 1)
        rb = lax.broadcasted_iota(jnp.int32, sb.shape, 1)
        sb = jnp.where(tb + w_buf - rb < WINDOW, sb, NEG_INF)
        tn = jnp.bitwise_and(lax.broadcasted_iota(jnp.int32, sn.shape, 0), tq - 1)
        rn = lax.broadcasted_iota(jnp.int32, sn.shape, 1)
        sn = jnp.where(rn <= tn, sn, NEG_INF)
        m = jnp.maximum(jnp.max(sb, axis=1, keepdims=True), jnp.max(sn, axis=1, keepdims=True))
        pb = jnp.exp2(sb - m)
        pn = jnp.exp2(sn - m)
        l = jnp.sum(pb, axis=1, keepdims=True) + jnp.sum(pn, axis=1, keepdims=True)
        o4 = (jnp.dot(pb.astype(BF16), vb, preferred_element_type=F32)
              + jnp.dot(pn.astype(BF16), vn, preferred_element_type=F32)) / l
        for h in range(GQA_GROUP):
            hh = g * GQA_GROUP + h
            sl = slice(hh * HEAD_DIM, (hh + 1) * HEAD_DIM)
            o_ref[:, sl] = (prev_ref[:, sl] + o4[h * tq:(h + 1) * tq] * gate_ref[g, :, h:h + 1]).astype(out_dtype)


def win_attention_sample(q, win_buf, layer, kv_new, gates_t, prev, *, out_dtype):
    n_seq, buf_rows = win_buf.shape[1], win_buf.shape[2]
    tq = SAMPLE_PAD
    q_spec = pl.BlockSpec((tq, N_HEADS * HEAD_DIM), lambda b: (b, 0))
    kern = functools.partial(_win_sample_kernel, tq=tq, out_dtype=out_dtype)
    return pl.pallas_call(
        kern,
        grid=(n_seq,),
        in_specs=[
            q_spec,
            pl.BlockSpec((None, None, buf_rows, HEAD_DIM), lambda b: (layer, b, 0, 0)),
            pl.BlockSpec((tq * 2 * N_KV_HEADS, HEAD_DIM), lambda b: (b, 0)),
            pl.BlockSpec((None, N_KV_HEADS, tq, GQA_GROUP), lambda b: (2, 0, b, 0)),
            q_spec,
        ],
        out_specs=q_spec,
        out_shape=jax.ShapeDtypeStruct((n_seq * tq, N_HEADS * HEAD_DIM), out_dtype),
        compiler_params=_params("parallel"),
        name="nsa_win_attention_sample",
    )(q, win_buf, kv_new, gates_t, prev)


def _ret_kernel(q_ref, k_ref, v_ref, g_ref, gn_ref, dec_ref, qd_ref, kd_ref, gc_ref, s0_ref, o_ref, sout_ref,
                state_ref, *, n_chunks):
    c = pl.program_id(2)

    @pl.when(c == 0)
    def _():
        state_ref[...] = s0_ref[...]

    for h in range(state_ref.shape[0]):
        ksl = slice(h * RET_KDIM, (h + 1) * RET_KDIM)
        vsl = slice(h * RET_VDIM, (h + 1) * RET_VDIM)
        q = q_ref[:, ksl]
        kf = k_ref[:, ksl]
        v = v_ref[:, vsl]
        state = state_ref[h]
        s = lax.dot_general(q, kf.astype(BF16), _NT, preferred_element_type=F32) * dec_ref[h]
        inner = jnp.dot(s.astype(BF16), v, preferred_element_type=F32)
        cross = jnp.dot(q, state.astype(BF16), preferred_element_type=F32) * qd_ref[h]
        o = inner + cross
        kd = (kf * kd_ref[h]).astype(BF16)
        state_ref[h] = gc_ref[h] * state + lax.dot_general(kd, v, _TN, preferred_element_type=F32)
        mu = jnp.mean(o, axis=-1, keepdims=True)
        d = o - mu
        var = jnp.mean(d * d, axis=-1, keepdims=True)
        on = d * lax.rsqrt(var + GN_EPS) * gn_ref[:, vsl]
        g = g_ref[:, vsl]
        o_ref[:, vsl] = (g * jax.nn.sigmoid(g) * on).astype(o_ref.dtype)

    @pl.when(c == n_chunks - 1)
    def _():
        sout_ref[...] = state_ref[...]


def _ret_tables(chunk, n_tok):
    log_g = jnp.log(1.0 - 2.0 ** (-5.0 - jnp.arange(RET_HEADS, dtype=F32)))
    i = jnp.arange(chunk, dtype=F32)
    rel = i[:, None] - i[None, :]
    decay = jnp.where(rel >= 0, jnp.exp(jnp.maximum(rel, 0.0)[None] * log_g[:, None, None]), 0.0)
    q_dec = jnp.exp((i[None, :] + 1.0) * log_g[:, None])[..., None]
    k_dec = jnp.where(i[None, :] < n_tok, jnp.exp((n_tok - 1.0 - i)[None, :] * log_g[:, None]), 0.0)[..., None]
    g_c = jnp.exp(n_tok * log_g)[:, None, None]
    return decay, q_dec, k_dec, g_c


def retention(q, k, v, g, gn, layer, state0, *, n_seq, seq, chunk, n_tok):
    nc = seq // chunk
    hps = RET_HEADS_PER_STEP
    decay, q_dec, k_dec, g_c = _ret_tables(chunk, n_tok)
    kern = functools.partial(_ret_kernel, n_chunks=nc)
    tok_k = pl.BlockSpec((chunk, hps * RET_KDIM), lambda b, h, c: (b * nc + c, h))
    tok_v = pl.BlockSpec((chunk, hps * RET_VDIM), lambda b, h, c: (b * nc + c, h))
    st_spec = pl.BlockSpec((None, hps, RET_KDIM, RET_VDIM), lambda b, h, c: (b, h, 0, 0))
    return pl.pallas_call(
        kern,
        grid=(n_seq, RET_HEADS // hps, nc),
        in_specs=[
            tok_k, tok_k, tok_v, tok_v,
            pl.BlockSpec((None, 1, hps * RET_VDIM), lambda b, h, c: (layer, 0, h)),
            pl.BlockSpec((hps, chunk, chunk), lambda b, h, c: (h, 0, 0)),
            pl.BlockSpec((hps, chunk, 1), lambda b, h, c: (h, 0, 0)),
            pl.BlockSpec((hps, chunk, 1), lambda b, h, c: (h, 0, 0)),
            pl.BlockSpec((hps, 1, 1), lambda b, h, c: (h, 0, 0)),
            st_spec,
        ],
        out_specs=[tok_v, st_spec],
        out_shape=[
            jax.ShapeDtypeStruct((n_seq * seq, RET_HEADS * RET_VDIM), BF16),
            jax.ShapeDtypeStruct((n_seq, RET_HEADS, RET_KDIM, RET_VDIM), F32),
        ],
        scratch_shapes=[pltpu.VMEM((hps, RET_KDIM, RET_VDIM), F32)],
        compiler_params=_params("parallel", "parallel", "arbitrary"),
        name="retention_chunk",
    )(q, k, v, g, gn, decay, q_dec, k_dec, g_c, state0)


def _project_nsa(xn, w_in, layer, cos, sin_signed, *, tm):
    m, d = xn.shape
    nq = N_HEADS * HEAD_DIM
    half = N_KV_HEADS * HEAD_DIM
    rows_per_tok = 2 * N_KV_HEADS
    tabs_per_seq = cos.shape[0] // tm
    tab_spec = pl.BlockSpec((tm, HEAD_DIM), lambda i, j, k: (i % tabs_per_seq, 0))
    tn_q = 512
    q = matmul(
        [xn], w_in, layer, col_block0=0, n_cols=nq, tm=tm, tn=tn_q, tk=d,
        extras=[cos, sin_signed], extra_specs=[tab_spec, tab_spec],
        out_shapes=[jax.ShapeDtypeStruct((m, nq), BF16), jax.ShapeDtypeStruct((nq, m), BF16)],
        out_specs=[pl.BlockSpec((tm, tn_q), lambda i, j, k: (i, j)), pl.BlockSpec((tn_q, tm), lambda i, j, k: (j, i))],
        epilogue=functools.partial(_ep_rope_q, scale=HEAD_DIM ** -0.5 * LOG2_E), name="nsa_proj_q")
    kv = matmul(
        [xn], w_in, layer, col_block0=nq // half, n_cols=3 * KV_ROW, tm=tm, tn=half, tk=d,
        extras=[cos, sin_signed], extra_specs=[tab_spec, tab_spec],
        out_shapes=[jax.ShapeDtypeStruct((m * rows_per_tok, HEAD_DIM), F32)] * 3
        + [jax.ShapeDtypeStruct((3, N_KV_HEADS, m, HEAD_DIM), BF16),
           jax.ShapeDtypeStruct((3, N_KV_HEADS, HEAD_DIM, m), BF16)],
        out_specs=[pl.BlockSpec((tm * rows_per_tok, HEAD_DIM), lambda i, j, k: (i, 0))] * 3
        + [pl.BlockSpec((None, N_KV_HEADS, tm, HEAD_DIM), lambda i, j, k: (j // 2, 0, i, 0)),
           pl.BlockSpec((None, N_KV_HEADS, HEAD_DIM, tm), lambda i, j, k: (j // 2, 0, 0, i))],
        epilogue=_ep_nsa_kv, name="nsa_proj_kv", column_steps_ordered=True)
    kv = (kv[:3], kv[3], kv[4])
    n_gate = 3 * N_HEADS
    w_gate = jnp.pad(w_in[layer, :, nq + 3 * KV_ROW:], ((0, 0), (0, LANES - n_gate)))[None]
    gates = matmul(
        [xn], w_gate, 0, col_block0=0, n_cols=LANES, tm=tm, tn=LANES, tk=d, extras=[], extra_specs=[],
        out_shapes=[jax.ShapeDtypeStruct((m, LANES), F32)],
        out_specs=[pl.BlockSpec((tm, LANES), lambda i, j, k: (i, 0))],
        epilogue=_ep_sigmoid, name="nsa_proj_gate")[0]
    gates_t = gates[:, :n_gate].reshape(m, 3, N_KV_HEADS, GQA_GROUP).transpose(1, 2, 0, 3)
    return q, kv, gates_t


def _cmp_bias(w1, pos, b1, layer):
    k_full = CMP_BLOCK * HEAD_DIM
    pos_rows = jnp.zeros((2, SUBLANES, k_full), F32).at[:, 0].set(pos[layer].reshape(2, k_full))
    terms = []
    for e in range(2):
        t = matmul(
            [pos_rows[e]], w1.reshape(-1, k_full, HEAD_DIM), layer * 2 + e, col_block0=0, n_cols=HEAD_DIM,
            tm=SUBLANES, tn=HEAD_DIM, tk=k_full, extras=[], extra_specs=[],
            out_shapes=[jax.ShapeDtypeStruct((SUBLANES, HEAD_DIM), F32)],
            out_specs=[pl.BlockSpec((SUBLANES, HEAD_DIM), lambda i, j, k: (0, 0))],
            epilogue=_ep_cast, name="nsa_cmp_pos_term")[0]
        terms.append(t[0])
    return jnp.stack(terms) + b1[layer]


def _mlp(h, norm_g, w1, w2, layer, *, tm):
    m, d = h.shape
    d_ff = w1.shape[-1]
    xn = rmsnorm(h, norm_g, BF16, min(tm, 256))
    tn1 = 1024
    a = matmul(
        [xn], w1, layer, col_block0=0, n_cols=d_ff, tm=tm, tn=tn1, tk=d, extras=[], extra_specs=[],
        out_shapes=[jax.ShapeDtypeStruct((m, d_ff), BF16)],
        out_specs=[pl.BlockSpec((tm, tn1), lambda i, j, k: (i, j))],
        epilogue=_ep_relu2, name="mlp_up")[0]
    tn2 = 1024
    tm2 = tm
    return matmul(
        [a], w2, layer, col_block0=0, n_cols=d, tm=tm2, tn=tn2, tk=1024, extras=[h],
        extra_specs=[pl.BlockSpec((tm2, tn2), lambda i, j, k: (i, j))],
        out_shapes=[jax.ShapeDtypeStruct((m, d), F32)],
        out_specs=[pl.BlockSpec((tm2, tn2), lambda i, j, k: (i, j))],
        epilogue=_ep_residual, name="mlp_down")[0]


def _out_proj(o, w_out, layer, h, *, tm, name):
    m, d = h.shape
    tn = 512
    tm = min(tm, X_TILE_BYTES // (o.shape[1] * o.dtype.itemsize))
    return matmul(
        [o], w_out, layer, col_block0=0, n_cols=d, tm=tm, tn=tn, tk=o.shape[1], extras=[h],
        extra_specs=[pl.BlockSpec((tm, tn), lambda i, j, k: (i, j))],
        out_shapes=[jax.ShapeDtypeStruct((m, d), F32)],
        out_specs=[pl.BlockSpec((tm, tn), lambda i, j, k: (i, j))],
        epilogue=_ep_residual, name=name)[0]


def _nsa_tables(pos):
    cos, sin = _rope_tables(pos, HEAD_DIM // 2)
    return jnp.concatenate([cos, cos], axis=1), jnp.concatenate([-sin, sin], axis=1)


def _nsa_prompt(h, xn, layer, w_in, w_out, cmp_w1, cmp_bias, cmp_w2, *, n_seq, seq):
    tm = PROMPT_ROW_TILE
    cos, sin_signed = _nsa_tables(jnp.arange(seq))
    (q, qt), (kv_rows, k_bf, vt_bf), gates_t = _project_nsa(xn, w_in, layer, cos, sin_signed, tm=tm // 2)
    n_chunks = seq // CMP_STRIDE
    chunks_per_step = 128
    kv_chunks = kv_rows[0].reshape(1, (n_seq * n_chunks) // chunks_per_step, chunks_per_step, CMP_STRIDE,
                                   2 * N_KV_HEADS, HEAD_DIM)
    kc = compress(kv_chunks, 0, None, cmp_w1, cmp_bias, cmp_w2, n_seq=n_seq, n_chunks=n_chunks, n_pages=1)
    n_blk = seq // SEL_BLOCK
    cover_t = _cover_t(n_chunks, n_blk, n_blk)
    o1, val = cmp_attention(q, kc, cover_t, gates_t, n_seq=n_seq, seq=seq, tq=256, pos0=0)
    bias_t = select_bias(val, n_blk=n_blk, tc=512, causal_seq=seq)
    nbp = pl.cdiv(n_blk, LANES) * LANES
    o2 = flash_attention(qt, k_bf, vt_bf, 1, gates_t, o1, n_seq=n_seq, seq=seq, tq=512, tk=512, bias=bias_t,
                         onehot=_block_onehot(seq, nbp))
    o3 = flash_attention(qt, k_bf, vt_bf, 2, gates_t, o2, n_seq=n_seq, seq=seq, tq=256, tk=256, window=WINDOW,
                         out_dtype=BF16)
    h = _out_proj(o3, w_out, layer, h, tm=tm, name="nsa_out_proj")
    return h, kv_rows


def _nsa_sample(h, xn, layer, w_in, w_out, cmp_w1, cmp_bias, cmp_w2, cache_cmp, cache_sel, win_buf, page_table,
                *, n_seq, n_tok, past_len):
    tq = SAMPLE_PAD
    m = n_seq * tq
    cos, sin_signed = _nsa_tables(jnp.tile(past_len + jnp.arange(tq), n_seq))
    (q, _), (kv_rows, _, _), gates_t = _project_nsa(xn, w_in, layer, cos, sin_signed, tm=m)
    n_chunks = past_len // CMP_STRIDE
    kc = compress(cache_cmp, layer, page_table, cmp_w1, cmp_bias, cmp_w2, n_seq=n_seq, n_chunks=n_chunks,
                  n_pages=16)
    n_blk = pl.cdiv(past_len + n_tok, SEL_BLOCK)
    nbv = pl.cdiv(n_blk, SUBLANES) * SUBLANES
    cover_t = _cover_t(n_chunks, n_blk, nbv)
    o1, val = cmp_attention(q, kc, cover_t, gates_t, n_seq=n_seq, seq=tq, tq=tq, pos0=past_len)
    ncols = n_seq * N_KV_HEADS * n_tok
    bias_t = select_bias(val[..., :n_tok].transpose(2, 0, 1, 3).reshape(nbv, ncols), n_blk=n_blk, tc=ncols)
    blocks_per_page = cache_sel.shape[2] // (2 * N_KV_HEADS * SEL_BLOCK)
    n_pages = page_table.shape[1]
    chosen = (bias_t == 0).reshape(nbv, n_seq, N_KV_HEADS * n_tok).any(axis=2)
    needed = chosen[:n_pages * blocks_per_page].reshape(n_pages, blocks_per_page, n_seq).any(axis=1).T
    nbp = pl.cdiv(n_blk + 1, LANES) * LANES
    bias = bias_t.reshape(nbv, n_seq, N_KV_HEADS, n_tok).transpose(1, 2, 3, 0)
    bias = jnp.pad(bias, ((0, 0), (0, 0), (0, 0), (0, nbp - nbv)), constant_values=NEG_INF)
    bias = jnp.pad(bias, ((0, 0), (0, 0), (0, tq - n_tok), (0, 0)))
    o2 = sel_attention_sample(q, bias, needed, kv_rows[1], cache_sel, layer, page_table, gates_t, o1,
                              n_pages_step=16, past_len=past_len)
    o3 = win_attention_sample(q, win_buf, layer, kv_rows[2], gates_t, o2, out_dtype=BF16)
    h = _out_proj(o3, w_out, layer, h, tm=m, name="nsa_out_proj_sample")
    return h, kv_rows


def _ret_layer(h, xn, layer, w_in, gn, w_out, state0, pos, *, n_seq, seq, chunk, n_tok, tm):
    m, d = xn.shape
    nk = RET_HEADS * RET_KDIM
    nv = RET_HEADS * RET_VDIM
    cos, sin = _rope_tables(pos, RET_KDIM // 2)
    tabs_per_seq = cos.shape[0] // tm
    tab_spec = pl.BlockSpec((tm, LANES), lambda i, j, k: (i % tabs_per_seq, 0))
    tn = 512

    def proj(col0, n_cols, dtype, epilogue, extras, extra_specs, name):
        return matmul(
            [xn], w_in, layer, col_block0=col0 // tn, n_cols=n_cols, tm=tm, tn=tn, tk=d,
            extras=extras, extra_specs=extra_specs,
            out_shapes=[jax.ShapeDtypeStruct((m, n_cols), dtype)],
            out_specs=[pl.BlockSpec((tm, tn), lambda i, j, k: (i, j))],
            epilogue=epilogue, name=name)[0]

    q = proj(0, nk, BF16, functools.partial(_ep_rope_ret, scale=1.0), [cos, sin], [tab_spec, tab_spec], "ret_proj_q")
    k = proj(nk, nk, F32, functools.partial(_ep_rope_ret, scale=RET_KDIM ** -0.5), [cos, sin],
             [tab_spec, tab_spec], "ret_proj_k")
    v = proj(2 * nk, nv, BF16, _ep_cast, [], [], "ret_proj_v")
    g = proj(2 * nk + nv, nv, F32, _ep_cast, [], [], "ret_proj_g")
    gated, state = retention(q, k, v, g, gn.reshape(gn.shape[0], 1, nv), layer, state0, n_seq=n_seq, seq=seq,
                             chunk=chunk, n_tok=n_tok)
    h = _out_proj(gated, w_out, layer, h, tm=tm, name="ret_out_proj")
    return h, state


def kernel(x_prompt, x_sample, cache_cmp_kv, cache_sel_kv, state_win_kv, state_ret, page_table, norm_mix, norm_ffn,
           norm_final, nsa_w_in, nsa_w_out, nsa_cmp_pos, nsa_cmp_w1, nsa_cmp_b1, nsa_cmp_w2, ret_w_in, ret_gn,
           ret_w_out, ffn_w1, ffn_w2):
    n_seq_p, seq_p, d = x_prompt.shape
    n_seq_s, n_tok_s, _ = x_sample.shape
    depth = norm_mix.shape[0]
    n_nsa = cache_sel_kv.shape[0]
    pool, page_rows = cache_sel_kv.shape[1], cache_sel_kv.shape[2]
    past_len = page_table.shape[1] * page_rows
    assert past_len % CMP_STRIDE == 0 and n_tok_s < CMP_STRIDE and n_tok_s <= SAMPLE_PAD
    assert state_win_kv.shape[2] == WINDOW and seq_p >= WINDOW

    hp = x_prompt.reshape(n_seq_p * seq_p, d)
    hs = jnp.pad(x_sample, ((0, 0), (0, SAMPLE_PAD - n_tok_s), (0, 0))).reshape(n_seq_s * SAMPLE_PAD, d)
    m_s = n_seq_s * SAMPLE_PAD
    rows_per_tok = 2 * N_KV_HEADS
    cache_cmp = cache_cmp_kv.reshape(n_nsa, pool, page_rows // CMP_STRIDE, CMP_STRIDE, rows_per_tok, HEAD_DIM)
    cache_sel = cache_sel_kv.reshape(n_nsa, pool, page_rows * rows_per_tok, HEAD_DIM)
    win_buf = state_win_kv.reshape(n_nsa, n_seq_s, WINDOW * rows_per_tok, HEAD_DIM)
    cmp_w1 = nsa_cmp_w1.reshape(n_nsa, 2, CMP_BLOCK * HEAD_DIM, HEAD_DIM)
    kv_shape = (2, N_KV_HEADS, HEAD_DIM)
    ffn_w2 = ffn_w2.astype(BF16)
    ret_w_out = ret_w_out.astype(BF16)

    cmp_p, cmp_s, sel_p, sel_s, win_p, win_s, ret_p, ret_s = [], [], [], [], [], [], [], []
    for layer in range(depth):
        xp = rmsnorm(hp, norm_mix[layer], BF16, 256)
        xs = rmsnorm(hs, norm_mix[layer], BF16, m_s)
        if layer % 2 == 0:
            a = layer // 2
            cmp_bias = _cmp_bias(cmp_w1, nsa_cmp_pos, nsa_cmp_b1, a)
            hp, kv_p = _nsa_prompt(hp, xp, a, nsa_w_in, nsa_w_out, cmp_w1, cmp_bias, nsa_cmp_w2, n_seq=n_seq_p,
                                   seq=seq_p)
            hs, kv_s = _nsa_sample(hs, xs, a, nsa_w_in, nsa_w_out, cmp_w1, cmp_bias, nsa_cmp_w2, cache_cmp,
                                   cache_sel, win_buf, page_table, n_seq=n_seq_s, n_tok=n_tok_s, past_len=past_len)
            kv_p = [r.reshape((n_seq_p, seq_p) + kv_shape) for r in kv_p]
            kv_s = [r.reshape((n_seq_s, SAMPLE_PAD) + kv_shape)[:, :n_tok_s] for r in kv_s]
            cmp_p.append(kv_p[0])
            sel_p.append(kv_p[1])
            win_p.append(kv_p[2][:, seq_p - WINDOW:])
            cmp_s.append(kv_s[0])
            sel_s.append(kv_s[1])
            win_s.append(jnp.concatenate([state_win_kv[a][:, n_tok_s:], kv_s[2]], axis=1))
        else:
            r = layer // 2
            zeros = jnp.zeros((n_seq_p, RET_HEADS, RET_KDIM, RET_VDIM), F32)
            hp, st_p = _ret_layer(hp, xp, r, ret_w_in, ret_gn, ret_w_out, zeros, jnp.arange(seq_p), n_seq=n_seq_p,
                                  seq=seq_p, chunk=RET_CHUNK, n_tok=RET_CHUNK, tm=PROMPT_ROW_TILE)
            hs, st_s = _ret_layer(hs, xs, r, ret_w_in, ret_gn, ret_w_out, state_ret[r],
                                  jnp.tile(past_len + jnp.arange(SAMPLE_PAD), n_seq_s), n_seq=n_seq_s,
                                  seq=SAMPLE_PAD, chunk=SAMPLE_PAD,
                                  n_tok=n_tok_s, tm=m_s)
            ret_p.append(st_p)
            ret_s.append(st_s)
        hp = _mlp(hp, norm_ffn[layer], ffn_w1, ffn_w2, layer, tm=PROMPT_ROW_TILE)
        hs = _mlp(hs, norm_ffn[layer], ffn_w1, ffn_w2, layer, tm=m_s)
    y_prompt = rmsnorm(hp, norm_final, F32, 256).reshape(n_seq_p, seq_p, d)
    y_sample = rmsnorm(hs, norm_final, F32, m_s).reshape(n_seq_s, SAMPLE_PAD, d)[:, :n_tok_s]
    return (y_prompt, y_sample, jnp.stack(cmp_p), jnp.stack(cmp_s), jnp.stack(sel_p), jnp.stack(sel_s),
            jnp.stack(win_p), jnp.stack(win_s), jnp.stack(ret_p), jnp.stack(ret_s))
```

```python
import functools

import numpy as np
import jax
import jax.numpy as jnp
from jax import lax
from jax.experimental import pallas as pl
from jax.experimental.pallas import tpu as pltpu

F32 = jnp.float32
BF16 = jnp.bfloat16

N_HEADS = 16
HEAD_DIM = 128
N_KV_HEADS = 4
GQA_GROUP = N_HEADS // N_KV_HEADS
KV_ROW = 2 * N_KV_HEADS * HEAD_DIM
CMP_BLOCK = 32
CMP_STRIDE = 16
SEL_BLOCK = 64
N_SEL = 16
N_LOCAL = 2
WINDOW = 512
ROPE_THETA = 10000.0
RET_HEADS = 8
RET_KDIM = 256
RET_VDIM = 512
RET_CHUNK = 128
RET_HEADS_PER_STEP = 4
RMS_EPS = 1e-6
GN_EPS = 1e-5
NEG_INF = -1e30
FORCE_SCORE = 1e9
LOG2_E = 1.4426950408889634

SUBLANES = 8
LANES = 128
VMEM_LIMIT_BYTES = 56 * 1024 * 1024
SAMPLE_PAD = 16
PROMPT_ROW_TILE = 2048

_NT = (((1,), (1,)), ((), ()))
_TN = (((0,), (0,)), ((), ()))


def _params(*semantics):
    return pltpu.CompilerParams(dimension_semantics=semantics, vmem_limit_bytes=VMEM_LIMIT_BYTES)


def _rmsnorm_kernel(x_ref, g_ref, o_ref):
    x = x_ref[...]
    ms = jnp.mean(x * x, axis=-1, keepdims=True)
    o_ref[...] = (x * lax.rsqrt(ms + RMS_EPS) * g_ref[...]).astype(o_ref.dtype)


def rmsnorm(x, g, out_dtype, tm, row_block0=0, n_blocks=None):
    m, d = x.shape
    if n_blocks is None:
        n_blocks = m // tm
    return pl.pallas_call(
        _rmsnorm_kernel,
        grid=(n_blocks,),
        in_specs=[pl.BlockSpec((tm, d), lambda i: (i + row_block0, 0)), pl.BlockSpec((1, d), lambda i: (0, 0))],
        out_specs=pl.BlockSpec((tm, d), lambda i: (i, 0)),
        out_shape=jax.ShapeDtypeStruct((n_blocks * tm, d), out_dtype),
        compiler_params=_params("parallel"),
        name="rmsnorm",
    )(x, g.reshape(1, d))


def _mm_kernel(*refs, n_x, n_extra, n_out, nk, epilogue):
    x_refs = refs[:n_x]
    w_ref = refs[n_x]
    extra = refs[n_x + 1:n_x + 1 + n_extra]
    outs = refs[n_x + 1 + n_extra:n_x + 1 + n_extra + n_out]
    x = x_refs[0][...]
    for r in x_refs[1:]:
        x = x + r[...]
    part = jnp.dot(x.astype(BF16), w_ref[...].astype(BF16), preferred_element_type=F32)
    if nk == 1:
        epilogue(part, extra, outs)
        return
    k = pl.program_id(2)
    if epilogue is _ep_residual_norm:
        @pl.when(k == 0)
        def _():
            outs[0][...] = extra[0][...] + part

        @pl.when(k > 0)
        def _():
            outs[0][...] += part

        @pl.when(k == nk - 1)
        def _():
            _write_norm(outs[0][...], extra, outs)
        return
    acc_ref = refs[-1]

    @pl.when(k == 0)
    def _():
        acc_ref[...] = part

    @pl.when(k > 0)
    def _():
        acc_ref[...] += part

    @pl.when(k == nk - 1)
    def _():
        epilogue(acc_ref[...], extra, outs)


def matmul(xs, w, layer, *, col_block0, n_cols, tm, tn, tk, extras, extra_specs, out_shapes, out_specs,
           epilogue, name, column_steps_ordered=False):
    m, kdim = xs[0].shape
    nk = kdim // tk
    grid = (m // tm, pl.cdiv(n_cols, tn), nk)
    in_specs = [pl.BlockSpec((tm, tk), lambda i, j, k: (i, k)) for _ in xs]
    in_specs.append(pl.BlockSpec((None, tk, tn), lambda i, j, k: (layer, k, j + col_block0)))
    in_specs.extend(extra_specs)
    scratch = [pltpu.VMEM((tm, tn), F32)] if nk > 1 and epilogue is not _ep_residual_norm else []
    kern = functools.partial(_mm_kernel, n_x=len(xs), n_extra=len(extras), n_out=len(out_shapes), nk=nk,
                             epilogue=epilogue)
    return pl.pallas_call(
        kern,
        grid=grid,
        in_specs=in_specs,
        out_specs=out_specs,
        out_shape=out_shapes,
        scratch_shapes=scratch,
        compiler_params=_params("parallel", "arbitrary" if column_steps_ordered else "parallel", "arbitrary"),
        name=name,
    )(*xs, w, *extras)


def _rope_half_lane(x, cos, sin_signed):
    return x * cos + pltpu.roll(x, HEAD_DIM // 2, 1) * sin_signed


def _ep_rope_q(part, extra, outs, *, scale):
    cos = extra[0][...]
    sin = extra[1][...]
    for h in range(part.shape[1] // HEAD_DIM):
        sl = slice(h * HEAD_DIM, (h + 1) * HEAD_DIM)
        r = _rope_half_lane(part[:, sl], cos, sin) * scale
        outs[0][:, sl] = r.astype(outs[0].dtype)
        outs[1][sl, :] = r.T.astype(outs[1].dtype)


def _ep_nsa_kv(part, extra, outs):
    tm = part.shape[0]
    rows_per_tok = 2 * N_KV_HEADS
    j = pl.program_id(1)

    def heads():
        return [part[:, h * HEAD_DIM:(h + 1) * HEAD_DIM] for h in range(N_KV_HEADS)]

    for br in range(3):
        @pl.when(j == 2 * br)
        def _(br=br):
            cos = extra[0][...]
            sin = extra[1][...]
            for h, x in enumerate(heads()):
                k = _rope_half_lane(x, cos, sin)
                outs[br][pl.ds(h, tm, stride=rows_per_tok), :] = k
                outs[3][h] = k.astype(BF16)

        @pl.when(j == 2 * br + 1)
        def _(br=br):
            for h, v in enumerate(heads()):
                outs[br][pl.ds(N_KV_HEADS + h, tm, stride=rows_per_tok), :] = v
                outs[4][h] = v.T.astype(BF16)


def _ep_sigmoid(part, extra, outs):
    outs[0][...] = jax.nn.sigmoid(part)


def _write_norm(h, extra, outs):
    if len(extra) < 2:
        return
    ms = jnp.mean(h * h, axis=-1, keepdims=True)
    y = h * lax.rsqrt(ms + RMS_EPS) * extra[1][...]
    outs[-1][...] = y.astype(outs[-1].dtype)


def _ep_residual_norm(part, extra, outs):
    h = extra[0][...] + part
    outs[0][...] = h
    _write_norm(h, extra, outs)


def _ep_relu2(part, extra, outs):
    h = jnp.maximum(part, 0.0)
    outs[0][...] = (h * h).astype(outs[0].dtype)


def _ep_cast(part, extra, outs):
    outs[0][...] = part.astype(outs[0].dtype)


def _ep_rope_ret(part, extra, outs, *, scale):
    cos = extra[0][...]
    sin = extra[1][...]
    for h in range(part.shape[1] // RET_KDIM):
        a = slice(h * RET_KDIM, h * RET_KDIM + LANES)
        b = slice(h * RET_KDIM + LANES, (h + 1) * RET_KDIM)
        x1 = part[:, a]
        x2 = part[:, b]
        outs[0][:, a] = ((x1 * cos - x2 * sin) * scale).astype(outs[0].dtype)
        outs[0][:, b] = ((x2 * cos + x1 * sin) * scale).astype(outs[0].dtype)


def _rope_tables(pos, half):
    inv = ROPE_THETA ** (-jnp.arange(half, dtype=F32) / half)
    ang = pos.astype(F32)[:, None] * inv[None, :]
    return jnp.cos(ang), jnp.sin(ang)


def _compress_kernel(*refs, n_pages, paged):
    if paged:
        refs = refs[1:]
    page_refs = refs[:n_pages]
    next_ref, w1_ref, bias_ref, w2_ref, out_ref, shift_ref, y_ref = refs[n_pages:]
    nc = page_refs[0].shape[0]
    n_heads = N_KV_HEADS
    n_main = n_pages * nc * n_heads
    k_half = CMP_STRIDE * HEAD_DIM
    tiles = [[page_refs[p][:, s] for s in range(CMP_STRIDE)] for p in range(n_pages)]
    next_tiles = [next_ref[0, s] for s in range(CMP_STRIDE)]
    for e in range(2):
        hsl = slice(e * n_heads, (e + 1) * n_heads)
        blocks = [
            jnp.concatenate([tiles[p][s][:, hsl, :].reshape(nc * n_heads, HEAD_DIM) for s in range(CMP_STRIDE)],
                            axis=1)
            for p in range(n_pages)]
        blocks.append(jnp.concatenate([next_tiles[s][hsl, :] for s in range(CMP_STRIDE)], axis=1))
        blocks.append(jnp.zeros((SUBLANES - n_heads, CMP_STRIDE * HEAD_DIM), F32))
        x = jnp.concatenate(blocks, axis=0).astype(BF16)
        wcat = jnp.concatenate([w1_ref[e, :k_half], w1_ref[e, k_half:]], axis=1).astype(BF16)
        per = jnp.dot(x, wcat, preferred_element_type=F32)
        hid = per.shape[1] // 2
        shift_ref[...] = per[:, hid:]
        h = per[:n_main, :hid] + shift_ref[pl.ds(n_heads, n_main), :]
        h = jax.nn.gelu(h + bias_ref[e:e + 1, :])
        y_ref[...] = jnp.dot(h.astype(BF16), w2_ref[e].astype(BF16), preferred_element_type=F32)
        for g in range(n_heads):
            out_ref[e * n_heads + g] = y_ref[pl.ds(g, n_main // n_heads, stride=n_heads), :].astype(out_ref.dtype)


def compress(kv, layer, page_table, w1, bias, w2, *, n_seq, n_chunks, n_pages):
    paged = page_table is not None
    nc = kv.shape[2]
    n_main = n_pages * nc
    steps = n_chunks // n_main
    pages_per_seq = steps * n_pages
    rows_per_tok = 2 * N_KV_HEADS

    def nxt(s):
        return jnp.minimum((s + 1) * n_pages, pages_per_seq - 1)

    if paged:
        def page_map(p):
            return lambda b, s, pt: (layer, pt[b, s * n_pages + p], 0, 0, 0, 0)

        def next_map(b, s, pt):
            return (layer, pt[b, nxt(s)], 0, 0, 0, 0)

        def fixed(*idx):
            return lambda b, s, pt: idx

        out_map = lambda b, s, pt: (b, 0, s, 0)
    else:
        def page_map(p):
            return lambda b, s: (layer, b * pages_per_seq + s * n_pages + p, 0, 0, 0, 0)

        def next_map(b, s):
            return (layer, b * pages_per_seq + nxt(s), 0, 0, 0, 0)

        def fixed(*idx):
            return lambda b, s: idx

        out_map = lambda b, s: (b, 0, s, 0)

    tile = (CMP_STRIDE, rows_per_tok, HEAD_DIM)
    in_specs = [pl.BlockSpec((None, None, nc) + tile, page_map(p)) for p in range(n_pages)]
    in_specs += [
        pl.BlockSpec((None, None, 1) + tile, next_map),
        pl.BlockSpec((None, 2, CMP_BLOCK * HEAD_DIM, HEAD_DIM), fixed(layer, 0, 0, 0)),
        pl.BlockSpec((2, HEAD_DIM), fixed(0, 0)),
        pl.BlockSpec((None, 2, HEAD_DIM, HEAD_DIM), fixed(layer, 0, 0, 0)),
    ]
    kern = functools.partial(_compress_kernel, n_pages=n_pages, paged=paged)
    grid_spec = pltpu.PrefetchScalarGridSpec(
        num_scalar_prefetch=1 if paged else 0,
        grid=(n_seq, steps),
        in_specs=in_specs,
        out_specs=pl.BlockSpec((None, rows_per_tok, n_main, HEAD_DIM), out_map),
        scratch_shapes=[pltpu.VMEM((n_main * N_KV_HEADS + SUBLANES, HEAD_DIM), F32),
                        pltpu.VMEM((n_main * N_KV_HEADS, HEAD_DIM), F32)],
    )
    args = ([page_table] if paged else []) + [kv] * (n_pages + 1) + [w1, bias, w2]
    return pl.pallas_call(
        kern,
        grid_spec=grid_spec,
        out_shape=jax.ShapeDtypeStruct((n_seq, rows_per_tok, n_chunks, HEAD_DIM), BF16),
        compiler_params=_params("parallel", "arbitrary"),
        name="nsa_compress",
    )(*args)


def _group_rows(q):
    return jnp.concatenate([q[:, h * HEAD_DIM:(h + 1) * HEAD_DIM] for h in range(GQA_GROUP)], axis=0)


def _cmp_kernel(q_ref, k_ref, v_ref, cov_ref, gate_ref, o_ref, val_ref, *, tq, pos0):
    qi = pl.program_id(2)
    q4 = _group_rows(q_ref[...])
    s = lax.dot_general(q4, k_ref[...], _NT, preferred_element_type=F32)
    row = lax.broadcasted_iota(jnp.int32, s.shape, 0)
    col = lax.broadcasted_iota(jnp.int32, s.shape, 1)
    qpos = pos0 + qi * tq + jnp.bitwise_and(row, tq - 1)
    valid = col * CMP_STRIDE + (CMP_BLOCK - 1) <= qpos
    s = jnp.where(valid, s, NEG_INF)
    m = jnp.max(s, axis=1, keepdims=True)
    e = jnp.where(valid, jnp.exp2(s - m), 0.0)
    l = jnp.sum(e, axis=1, keepdims=True)
    p = e / jnp.where(l > 0.0, l, 1.0)
    o4 = jnp.dot(p.astype(BF16), v_ref[...], preferred_element_type=F32)
    for h in range(GQA_GROUP):
        o_ref[:, h * HEAD_DIM:(h + 1) * HEAD_DIM] = o4[h * tq:(h + 1) * tq] * gate_ref[:, h:h + 1]
    imp = p[0:tq] + p[tq:2 * tq] + p[2 * tq:3 * tq] + p[3 * tq:4 * tq]
    hi = imp.astype(BF16)
    r1 = imp - hi.astype(F32)
    mid = r1.astype(BF16)
    lo = (r1 - mid.astype(F32)).astype(BF16)
    cov = cov_ref[...]
    score = (lax.dot_general(cov, hi, _NT, preferred_element_type=F32)
             + lax.dot_general(cov, mid, _NT, preferred_element_type=F32)
             + lax.dot_general(cov, lo, _NT, preferred_element_type=F32))
    jb = lax.broadcasted_iota(jnp.int32, score.shape, 0)
    qp = pos0 + qi * tq + lax.broadcasted_iota(jnp.int32, score.shape, 1)
    cur = jnp.right_shift(qp, SEL_BLOCK.bit_length() - 1)
    visible = jb <= cur
    forced = (jb == 0) | (visible & (jb > cur - N_LOCAL))
    val_ref[...] = jnp.where(forced, FORCE_SCORE, jnp.where(visible, score, -FORCE_SCORE))


def cmp_attention(q, kc, cover_t, gates_t, *, n_seq, seq, tq, pos0):
    nq = seq // tq
    ncp = kc.shape[2]
    nbv = cover_t.shape[0]
    gw = GQA_GROUP * HEAD_DIM
    kern = functools.partial(_cmp_kernel, tq=tq, pos0=pos0)
    if tq % LANES == 0:
        val_spec = pl.BlockSpec((nbv, tq), lambda b, g, i: (0, (b * N_KV_HEADS + g) * nq + i))
        val_shape = jax.ShapeDtypeStruct((nbv, n_seq * N_KV_HEADS * seq), F32)
    else:
        val_spec = pl.BlockSpec((None, None, nbv, tq), lambda b, g, i: (b, g, 0, i))
        val_shape = jax.ShapeDtypeStruct((n_seq, N_KV_HEADS, nbv, seq), F32)
    return pl.pallas_call(
        kern,
        grid=(n_seq, N_KV_HEADS, nq),
        in_specs=[
            pl.BlockSpec((tq, gw), lambda b, g, i: (b * nq + i, g)),
            pl.BlockSpec((None, None, ncp, HEAD_DIM), lambda b, g, i: (b, g, 0, 0)),
            pl.BlockSpec((None, None, ncp, HEAD_DIM), lambda b, g, i: (b, N_KV_HEADS + g, 0, 0)),
            pl.BlockSpec((nbv, ncp), lambda b, g, i: (0, 0)),
            pl.BlockSpec((None, None, tq, GQA_GROUP), lambda b, g, i: (0, g, b * nq + i, 0)),
        ],
        out_specs=[pl.BlockSpec((tq, gw), lambda b, g, i: (b * nq + i, g)), val_spec],
        out_shape=[jax.ShapeDtypeStruct((n_seq * seq, N_HEADS * HEAD_DIM), F32), val_shape],
        compiler_params=_params("parallel", "parallel", "parallel"),
        name="nsa_cmp_attention",
    )(q, kc, kc, cover_t, gates_t)


def _cover_t(n_chunks, n_blk, nbv):
    n_cmp = n_chunks - CMP_BLOCK // CMP_STRIDE + 1
    c0 = np.arange(n_chunks) * CMP_STRIDE
    j0 = np.arange(nbv) * SEL_BLOCK
    cov = (c0[None, :] < j0[:, None] + SEL_BLOCK) & (c0[None, :] + CMP_BLOCK > j0[:, None])
    cov &= (np.arange(n_chunks)[None, :] < n_cmp) & (np.arange(nbv)[:, None] < n_blk)
    return jnp.asarray(cov.astype(np.float32), dtype=BF16)


def _rank_kernel(val_ref, out_ref, *, n_blk, n_sel, causal_seq):
    val = val_ref[...]
    tc = val.shape[1]
    jb = lax.broadcasted_iota(jnp.int32, val.shape, 0)
    if causal_seq is None:
        n_iter = n_blk
    else:
        pos0 = (pl.program_id(0) % (causal_seq // tc)) * tc
        n_iter = jnp.minimum(n_blk, (pos0 + tc - 1) // SEL_BLOCK + 1)

    def body(i, rank):
        row = val_ref[pl.ds(i, 1), :]
        beats = (row > val) | ((row == val) & (i < jb))
        return rank + jnp.where(beats, 1, 0)

    rank = lax.fori_loop(0, n_iter, body, jnp.zeros(val.shape, jnp.int32))
    chosen = (rank < n_sel) & (val > -0.5 * FORCE_SCORE)
    out_ref[...] = jnp.where(chosen, 0.0, NEG_INF).astype(out_ref.dtype)


def select_bias(val, *, n_blk, tc, causal_seq=None):
    nbv, ncols = val.shape
    kern = functools.partial(_rank_kernel, n_blk=n_blk, n_sel=min(N_SEL, n_blk), causal_seq=causal_seq)
    return pl.pallas_call(
        kern,
        grid=(ncols // tc,),
        in_specs=[pl.BlockSpec((nbv, tc), lambda i: (0, i))],
        out_specs=pl.BlockSpec((nbv, tc), lambda i: (0, i)),
        out_shape=jax.ShapeDtypeStruct((nbv, ncols), BF16),
        compiler_params=_params("parallel"),
        name="nsa_select_rank",
    )(val)


_STEP_FIRST, _STEP_LAST, _STEP_EDGE = 1, 2, 4


def _flash_steps(seq, tq, tk, window):
    qi_l, kt_l, fl_l = [], [], []
    for qi in range(seq // tq):
        q0, q1 = qi * tq, qi * tq + tq - 1
        kts = [kt for kt in range(seq // tk)
               if kt * tk <= q1 and (window is None or q0 - (kt * tk + tk - 1) < window)]
        for n, kt in enumerate(kts):
            k0, k1 = kt * tk, kt * tk + tk - 1
            all_visible = q0 - k1 >= 0 and (window is None or q1 - k0 < window)
            flags = (_STEP_FIRST if n == 0 else 0) | (_STEP_LAST if n == len(kts) - 1 else 0)
            flags |= 0 if all_visible else _STEP_EDGE
            qi_l.append(qi)
            kt_l.append(kt)
            fl_l.append(flags)
    return (jnp.asarray(np.array(qi_l, np.int32)), jnp.asarray(np.array(kt_l, np.int32)),
            jnp.asarray(np.array(fl_l, np.int32)))


def _flash_kernel(*refs, tq, tk, use_bias, window, out_dtype):
    qi_ref, kt_ref, flag_ref = refs[:3]
    if use_bias:
        qt_ref, bias_ref, k_ref, vt_ref, e_ref, gate_ref, prev_ref, o_ref, qa_ref, m_ref, acc_ref = refs[3:]
    else:
        qt_ref, k_ref, vt_ref, gate_ref, prev_ref, o_ref, qa_ref, m_ref, acc_ref = refs[3:]
    s = pl.program_id(2)
    qi = qi_ref[s]
    kt = kt_ref[s]
    flags = flag_ref[s]
    first = jnp.bitwise_and(flags, _STEP_FIRST) != 0
    last = jnp.bitwise_and(flags, _STEP_LAST) != 0
    edge = jnp.bitwise_and(flags, _STEP_EDGE) != 0

    @pl.when(first)
    def _():
        for h in range(GQA_GROUP):
            qa_ref[0:HEAD_DIM, h * tq:(h + 1) * tq] = qt_ref[h * HEAD_DIM:(h + 1) * HEAD_DIM, :]
            if use_bias:
                nbv = bias_ref.shape[0]
                qa_ref[HEAD_DIM:HEAD_DIM + nbv, h * tq:(h + 1) * tq] = bias_ref[...]
                if HEAD_DIM + nbv < qa_ref.shape[0]:
                    qa_ref[HEAD_DIM + nbv:, h * tq:(h + 1) * tq] = jnp.zeros(
                        (qa_ref.shape[0] - HEAD_DIM - nbv, tq), BF16)
        m_ref[...] = jnp.full(m_ref.shape, NEG_INF, F32)
        acc_ref[...] = jnp.zeros(acc_ref.shape, F32)

    def step(masked):
        if use_bias:
            ka = jnp.concatenate([k_ref[...], e_ref[...]], axis=1)
        else:
            ka = k_ref[...]
        st = jnp.dot(ka, qa_ref[...], preferred_element_type=F32)
        if masked:
            kpos = kt * tk + lax.broadcasted_iota(jnp.int32, st.shape, 0)
            qpos = qi * tq + jnp.bitwise_and(lax.broadcasted_iota(jnp.int32, st.shape, 1), tq - 1)
            d = qpos - kpos
            ok = d >= 0
            if window is not None:
                ok = ok & (d < window)
            st = jnp.where(ok, st, NEG_INF)
        m_old = m_ref[...]
        m_new = jnp.maximum(m_old, jnp.max(st, axis=0, keepdims=True))
        alpha = jnp.exp2(m_old - m_new)
        p = jnp.exp2(st - m_new).astype(BF16)
        va = jnp.concatenate([vt_ref[...], jnp.ones((acc_ref.shape[0] - HEAD_DIM, tk), BF16)], axis=0)
        acc_ref[...] = alpha * acc_ref[...] + jnp.dot(va, p, preferred_element_type=F32)
        m_ref[...] = m_new

    @pl.when(edge)
    def _():
        step(True)

    @pl.when(jnp.logical_not(edge))
    def _():
        step(False)

    @pl.when(last)
    def _():
        acc = acc_ref[...]
        ot = acc[:HEAD_DIM] / acc[HEAD_DIM:HEAD_DIM + 1]
        for h in range(GQA_GROUP):
            sl = slice(h * HEAD_DIM, (h + 1) * HEAD_DIM)
            o_h = ot[:, h * tq:(h + 1) * tq].T
            o_ref[:, sl] = (prev_ref[:, sl] + o_h * gate_ref[:, h:h + 1]).astype(out_dtype)


def flash_attention(qt, k, vt, branch, gates_t, prev, *, n_seq, seq, tq, tk, bias=None, onehot=None, window=None,
                    out_dtype=F32):
    use_bias = bias is not None
    nq = seq // tq
    nkt = seq // tk
    gw = GQA_GROUP * HEAD_DIM
    qi_tab, kt_tab, flag_tab = _flash_steps(seq, tq, tk, window)

    o_spec = pl.BlockSpec((tq, gw), lambda b, g, s, qi, kt, fl: (b * nq + qi[s], g))
    qt_spec = pl.BlockSpec((gw, tq), lambda b, g, s, qi, kt, fl: (g, b * nq + qi[s]))
    k_spec = pl.BlockSpec((None, None, tk, HEAD_DIM), lambda b, g, s, qi, kt, fl: (branch, g, b * nkt + kt[s], 0))
    vt_spec = pl.BlockSpec((None, None, HEAD_DIM, tk), lambda b, g, s, qi, kt, fl: (branch, g, 0, b * nkt + kt[s]))
    gate_spec = pl.BlockSpec((None, None, tq, GQA_GROUP), lambda b, g, s, qi, kt, fl: (branch, g, b * nq + qi[s], 0))
    in_specs = [qt_spec]
    args = [qt]
    qa_rows = HEAD_DIM
    if use_bias:
        nbv = bias.shape[0]
        nbp = onehot.shape[1]
        qa_rows += nbp
        in_specs.append(
            pl.BlockSpec((nbv, tq), lambda b, g, s, qi, kt, fl: (0, (b * N_KV_HEADS + g) * nq + qi[s])))
        args.append(bias)
    in_specs += [k_spec, vt_spec]
    args += [k, vt]
    if use_bias:
        in_specs.append(pl.BlockSpec((tk, nbp), lambda b, g, s, qi, kt, fl: (kt[s], 0)))
        args.append(onehot)
    in_specs += [gate_spec, o_spec]
    args += [gates_t, prev]
    kern = functools.partial(_flash_kernel, tq=tq, tk=tk, use_bias=use_bias, window=window, out_dtype=out_dtype)
    cols = GQA_GROUP * tq
    grid_spec = pltpu.PrefetchScalarGridSpec(
        num_scalar_prefetch=3,
        grid=(n_seq, N_KV_HEADS, qi_tab.shape[0]),
        in_specs=in_specs,
        out_specs=o_spec,
        scratch_shapes=[
            pltpu.VMEM((qa_rows, cols), BF16),
            pltpu.VMEM((1, cols), F32),
            pltpu.VMEM((HEAD_DIM + 2 * SUBLANES, cols), F32),
        ],
    )
    return pl.pallas_call(
        kern,
        grid_spec=grid_spec,
        out_shape=jax.ShapeDtypeStruct((n_seq * seq, N_HEADS * HEAD_DIM), out_dtype),
        compiler_params=_params("parallel", "parallel", "arbitrary"),
        name="nsa_sel_attention" if use_bias else "nsa_win_attention",
    )(qi_tab, kt_tab, flag_tab, *args)


def _block_onehot(n_keys, nbp):
    e = (np.arange(n_keys)[:, None] // SEL_BLOCK) == np.arange(nbp)[None, :]
    return jnp.asarray(e.astype(np.float32), dtype=BF16)


def _sel_sample_kernel(*refs, n_pages, nbp, past_len, nsteps, tq):
    logical_ref, count_ref = refs[1:3]
    qt_ref, bias_ref, new_ref = refs[3:6]
    page_refs = refs[6:6 + n_pages]
    gate_ref, prev_ref, o_ref, m_ref, acc_ref = refs[6 + n_pages:]
    b = pl.program_id(0)
    step = pl.program_id(1)
    page_rows = page_refs[0].shape[0]
    rows_per_tok = 2 * N_KV_HEADS
    page_tokens = page_rows // rows_per_tok
    blocks_per_page = page_tokens // SEL_BLOCK
    n_needed = count_ref[b]
    cols = qt_ref.shape[2]

    def scores(g, k, key_bias):
        return jnp.dot(k, qt_ref[g], preferred_element_type=F32) + key_bias

    def block_bias(g, first_block, n_blocks, valid):
        rows = bias_ref[g, pl.ds(first_block, n_blocks), :]
        rows = jnp.where(valid, rows, NEG_INF)
        return jnp.concatenate(
            [jnp.broadcast_to(rows[j:j + 1], (SEL_BLOCK, cols)) for j in range(n_blocks)], axis=0)

    def weighted_values(p, v):
        va = jnp.concatenate([v, jnp.ones(v.shape, BF16)], axis=1)
        return lax.dot_general(p, va, _TN, preferred_element_type=F32)

    @pl.when(step == 0)
    def _():
        for g in range(N_KV_HEADS):
            kn = new_ref[pl.ds(g, tq, stride=2 * N_KV_HEADS), :].astype(BF16)
            vn = new_ref[pl.ds(N_KV_HEADS + g, tq, stride=2 * N_KV_HEADS), :].astype(BF16)
            new_bias = block_bias(g, past_len // SEL_BLOCK, 1, True)[:tq]
            st = scores(g, kn, new_bias)
            key = lax.broadcasted_iota(jnp.int32, st.shape, 0)
            tok = jnp.bitwise_and(lax.broadcasted_iota(jnp.int32, st.shape, 1), tq - 1)
            st = jnp.where(key <= tok, st, NEG_INF)
            m = jnp.max(st, axis=0, keepdims=True)
            p = jnp.exp2(st - m).astype(BF16)
            m_ref[g] = m
            acc_ref[g] = weighted_values(p, vn)

    @pl.when(step * n_pages < n_needed)
    def _():
        slots = [step * n_pages + p for p in range(n_pages)]
        first_blocks = [logical_ref[b, slot] * blocks_per_page for slot in slots]
        for g in range(N_KV_HEADS):
            k = jnp.concatenate(
                [page_refs[p][pl.ds(g, page_tokens, stride=rows_per_tok), :] for p in range(n_pages)],
                axis=0).astype(BF16)
            v = jnp.concatenate(
                [page_refs[p][pl.ds(N_KV_HEADS + g, page_tokens, stride=rows_per_tok), :] for p in range(n_pages)],
                axis=0).astype(BF16)
            key_bias = jnp.concatenate(
                [block_bias(g, first_blocks[p], blocks_per_page, slots[p] < n_needed) for p in range(n_pages)],
                axis=0)
            st = scores(g, k, key_bias)
            m_old = m_ref[g]
            m_new = jnp.maximum(m_old, jnp.max(st, axis=0, keepdims=True))
            alpha = jnp.exp2(m_old - m_new)
            p = jnp.exp2(st - m_new).astype(BF16)
            n_cols = alpha.shape[1]
            eye = (lax.broadcasted_iota(jnp.int32, (n_cols, n_cols), 0)
                   == lax.broadcasted_iota(jnp.int32, (n_cols, n_cols), 1))
            alpha_col = jnp.sum(jnp.where(eye, alpha, 0.0), axis=1, keepdims=True)
            acc_ref[g] = alpha_col * acc_ref[g] + weighted_values(p, v)
            m_ref[g] = m_new

    @pl.when(step == nsteps - 1)
    def _():
        for g in range(N_KV_HEADS):
            acc = acc_ref[g]
            o4 = acc[:, :HEAD_DIM] / acc[:, HEAD_DIM:HEAD_DIM + 1]
            for h in range(GQA_GROUP):
                hh = g * GQA_GROUP + h
                sl = slice(hh * HEAD_DIM, (hh + 1) * HEAD_DIM)
                o_ref[:, sl] = prev_ref[:, sl] + o4[h * tq:(h + 1) * tq] * gate_ref[g, :, h:h + 1]


def sel_attention_sample(q, bias, needed, kv_new, cache, layer, page_table, gates_t, prev, *, n_pages_step,
                         past_len):
    n_seq, pages = page_table.shape
    tq = SAMPLE_PAD
    nbp = bias.shape[-1]
    page_rows = cache.shape[2]
    nsteps = pages // n_pages_step
    dm = N_HEADS * HEAD_DIM
    order = jnp.argsort(jnp.logical_not(needed), axis=1, stable=True).astype(jnp.int32)
    count = jnp.sum(needed, axis=1).astype(jnp.int32)
    slot = jnp.minimum(jnp.arange(pages, dtype=jnp.int32)[None, :], count[:, None] - 1)
    logical = jnp.take_along_axis(order, slot, axis=1)
    physical = jnp.take_along_axis(page_table, logical, axis=1)
    row_spec = pl.BlockSpec((tq, dm), lambda b, s, *_: (b, 0))
    assert past_len % SEL_BLOCK + tq <= SEL_BLOCK
    cols = GQA_GROUP * tq
    q_t = q.reshape(n_seq, tq, N_KV_HEADS, GQA_GROUP, HEAD_DIM).transpose(0, 2, 4, 3, 1)
    q_t = q_t.reshape(n_seq, N_KV_HEADS, HEAD_DIM, cols)
    bias_t = jnp.tile(bias.transpose(0, 1, 3, 2), (1, 1, 1, GQA_GROUP)).astype(F32)

    def page_map(p):
        return lambda b, s, phys, *_: (layer, phys[b, s * n_pages_step + p], 0, 0)

    in_specs = [
        pl.BlockSpec((None, N_KV_HEADS, HEAD_DIM, cols), lambda b, s, *_: (b, 0, 0, 0)),
        pl.BlockSpec((None, N_KV_HEADS, nbp, cols), lambda b, s, *_: (b, 0, 0, 0)),
        pl.BlockSpec((tq * 2 * N_KV_HEADS, HEAD_DIM), lambda b, s, *_: (b, 0)),
    ]
    in_specs += [pl.BlockSpec((None, None, page_rows, HEAD_DIM), page_map(p)) for p in range(n_pages_step)]
    in_specs += [
        pl.BlockSpec((None, N_KV_HEADS, tq, GQA_GROUP), lambda b, s, *_: (1, 0, b, 0)),
        row_spec,
    ]
    kern = functools.partial(_sel_sample_kernel, n_pages=n_pages_step, nbp=nbp, past_len=past_len, nsteps=nsteps,
                             tq=tq)
    grid_spec = pltpu.PrefetchScalarGridSpec(
        num_scalar_prefetch=3,
        grid=(n_seq, nsteps),
        in_specs=in_specs,
        out_specs=row_spec,
        scratch_shapes=[
            pltpu.VMEM((N_KV_HEADS, 1, cols), F32),
            pltpu.VMEM((N_KV_HEADS, cols, 2 * HEAD_DIM), F32),
        ],
    )
    return pl.pallas_call(
        kern,
        grid_spec=grid_spec,
        out_shape=jax.ShapeDtypeStruct((n_seq * tq, dm), F32),
        compiler_params=_params("parallel", "arbitrary"),
        name="nsa_sel_attention_sample",
    )(physical, logical, count, q_t, bias_t, kv_new, *([cache] * n_pages_step), gates_t, prev)


def _win_sample_kernel(q_ref, buf_ref, new_ref, gate_ref, prev_ref, o_ref, *, tq, out_dtype):
    rows_per_tok = 2 * N_KV_HEADS
    w_buf = buf_ref.shape[0] // rows_per_tok
    gw = GQA_GROUP * HEAD_DIM
    for g in range(N_KV_HEADS):
        q4 = _group_rows(q_ref[:, g * gw:(g + 1) * gw])
        kb = buf_ref[pl.ds(g, w_buf, stride=rows_per_tok), :].astype(BF16)
        vb = buf_ref[pl.ds(N_KV_HEADS + g, w_buf, stride=rows_per_tok), :].astype(BF16)
        kn = new_ref[pl.ds(g, tq, stride=rows_per_tok), :].astype(BF16)
        vn = new_ref[pl.ds(N_KV_HEADS + g, tq, stride=rows_per_tok), :].astype(BF16)
        sb = lax.dot_general(q4, kb, _NT, preferred_element_type=F32)
        sn = lax.dot_general(q4, kn, _NT, preferred_element_type=F32)
        tb = jnp.bitwise_and(lax.broadcasted_iota(jnp.int32, sb.shape, 0), tq - 1)
        rb = lax.broadcasted_iota(jnp.int32, sb.shape, 1)
        sb = jnp.where(tb + w_buf - rb < WINDOW, sb, NEG_INF)
        tn = jnp.bitwise_and(lax.broadcasted_iota(jnp.int32, sn.shape, 0), tq - 1)
        rn = lax.broadcasted_iota(jnp.int32, sn.shape, 1)
        sn = jnp.where(rn <= tn, sn, NEG_INF)
        m = jnp.maximum(jnp.max(sb, axis=1, keepdims=True), jnp.max(sn, axis=1, keepdims=True))
        pb = jnp.exp2(sb - m)
        pn = jnp.exp2(sn - m)
        l = jnp.sum(pb, axis=1, keepdims=True) + jnp.sum(pn, axis=1, keepdims=True)
        o4 = (jnp.dot(pb.astype(BF16), vb, preferred_element_type=F32)
              + jnp.dot(pn.astype(BF16), vn, preferred_element_type=F32)) / l
        for h in range(GQA_GROUP):
            hh = g * GQA_GROUP + h
            sl = slice(hh * HEAD_DIM, (hh + 1) * HEAD_DIM)
            o_ref[:, sl] = (prev_ref[:, sl] + o4[h * tq:(h + 1) * tq] * gate_ref[g, :, h:h + 1]).astype(out_dtype)


def win_attention_sample(q, win_buf, layer, kv_new, gates_t, prev, *, out_dtype):
    n_seq, buf_rows = win_buf.shape[1], win_buf.shape[2]
    tq = SAMPLE_PAD
    q_spec = pl.BlockSpec((tq, N_HEADS * HEAD_DIM), lambda b: (b, 0))
    kern = functools.partial(_win_sample_kernel, tq=tq, out_dtype=out_dtype)
    return pl.pallas_call(
        kern,
        grid=(n_seq,),
        in_specs=[
            q_spec,
            pl.BlockSpec((None, None, buf_rows, HEAD_DIM), lambda b: (layer, b, 0, 0)),
            pl.BlockSpec((tq * 2 * N_KV_HEADS, HEAD_DIM), lambda b: (b, 0)),
            pl.BlockSpec((None, N_KV_HEADS, tq, GQA_GROUP), lambda b: (2, 0, b, 0)),
            q_spec,
        ],
        out_specs=q_spec,
        out_shape=jax.ShapeDtypeStruct((n_seq * tq, N_HEADS * HEAD_DIM), out_dtype),
        compiler_params=_params("parallel"),
        name="nsa_win_attention_sample",
    )(q, win_buf, kv_new, gates_t, prev)


def _ret_kernel(q_ref, k_ref, v_ref, g_ref, gn_ref, dec_ref, qd_ref, kd_ref, gc_ref, s0_ref, o_ref, sout_ref,
                state_ref, *, n_chunks):
    c = pl.program_id(2)

    @pl.when(c == 0)
    def _():
        state_ref[...] = s0_ref[...]

    for h in range(state_ref.shape[0]):
        ksl = slice(h * RET_KDIM, (h + 1) * RET_KDIM)
        vsl = slice(h * RET_VDIM, (h + 1) * RET_VDIM)
        q = q_ref[:, ksl]
        kf = k_ref[:, ksl]
        v = v_ref[:, vsl]
        state = state_ref[h]
        s = lax.dot_general(q, kf.astype(BF16), _NT, preferred_element_type=F32) * dec_ref[h]
        inner = jnp.dot(s.astype(BF16), v, preferred_element_type=F32)
        cross = jnp.dot(q, state.astype(BF16), preferred_element_type=F32) * qd_ref[h]
        o = inner + cross
        kd = (kf * kd_ref[h]).astype(BF16)
        state_ref[h] = gc_ref[h] * state + lax.dot_general(kd, v, _TN, preferred_element_type=F32)
        mu = jnp.mean(o, axis=-1, keepdims=True)
        d = o - mu
        var = jnp.mean(d * d, axis=-1, keepdims=True)
        on = d * lax.rsqrt(var + GN_EPS) * gn_ref[:, vsl]
        g = g_ref[:, vsl]
        o_ref[:, vsl] = (g * jax.nn.sigmoid(g) * on).astype(o_ref.dtype)

    @pl.when(c == n_chunks - 1)
    def _():
        sout_ref[...] = state_ref[...]


def _ret_tables(chunk, n_tok):
    log_g = jnp.log(1.0 - 2.0 ** (-5.0 - jnp.arange(RET_HEADS, dtype=F32)))
    i = jnp.arange(chunk, dtype=F32)
    rel = i[:, None] - i[None, :]
    decay = jnp.where(rel >= 0, jnp.exp(jnp.maximum(rel, 0.0)[None] * log_g[:, None, None]), 0.0)
    q_dec = jnp.exp((i[None, :] + 1.0) * log_g[:, None])[..., None]
    k_dec = jnp.where(i[None, :] < n_tok, jnp.exp((n_tok - 1.0 - i)[None, :] * log_g[:, None]), 0.0)[..., None]
    g_c = jnp.exp(n_tok * log_g)[:, None, None]
    return decay, q_dec, k_dec, g_c


def retention(q, k, v, g, gn, layer, state0, *, n_seq, seq, chunk, n_tok):
    nc = seq // chunk
    hps = RET_HEADS_PER_STEP
    decay, q_dec, k_dec, g_c = _ret_tables(chunk, n_tok)
    kern = functools.partial(_ret_kernel, n_chunks=nc)
    tok_k = pl.BlockSpec((chunk, hps * RET_KDIM), lambda b, h, c: (b * nc + c, h))
    tok_v = pl.BlockSpec((chunk, hps * RET_VDIM), lambda b, h, c: (b * nc + c, h))
    st_spec = pl.BlockSpec((None, hps, RET_KDIM, RET_VDIM), lambda b, h, c: (b, h, 0, 0))
    return pl.pallas_call(
        kern,
        grid=(n_seq, RET_HEADS // hps, nc),
        in_specs=[
            tok_k, tok_k, tok_v, tok_v,
            pl.BlockSpec((None, 1, hps * RET_VDIM), lambda b, h, c: (layer, 0, h)),
            pl.BlockSpec((hps, chunk, chunk), lambda b, h, c: (h, 0, 0)),
            pl.BlockSpec((hps, chunk, 1), lambda b, h, c: (h, 0, 0)),
            pl.BlockSpec((hps, chunk, 1), lambda b, h, c: (h, 0, 0)),
            pl.BlockSpec((hps, 1, 1), lambda b, h, c: (h, 0, 0)),
            st_spec,
        ],
        out_specs=[tok_v, st_spec],
        out_shape=[
            jax.ShapeDtypeStruct((n_seq * seq, RET_HEADS * RET_VDIM), BF16),
            jax.ShapeDtypeStruct((n_seq, RET_HEADS, RET_KDIM, RET_VDIM), F32),
        ],
        scratch_shapes=[pltpu.VMEM((hps, RET_KDIM, RET_VDIM), F32)],
        compiler_params=_params("parallel", "parallel", "arbitrary"),
        name="retention_chunk",
    )(q, k, v, g, gn, decay, q_dec, k_dec, g_c, state0)


def _project_nsa(xn, w_in, layer, cos, sin_signed, *, tm):
    m, d = xn.shape
    nq = N_HEADS * HEAD_DIM
    half = N_KV_HEADS * HEAD_DIM
    rows_per_tok = 2 * N_KV_HEADS
    tabs_per_seq = cos.shape[0] // tm
    tab_spec = pl.BlockSpec((tm, HEAD_DIM), lambda i, j, k: (i % tabs_per_seq, 0))
    tn_q = 512
    q = matmul(
        [xn], w_in, layer, col_block0=0, n_cols=nq, tm=tm, tn=tn_q, tk=d,
        extras=[cos, sin_signed], extra_specs=[tab_spec, tab_spec],
        out_shapes=[jax.ShapeDtypeStruct((m, nq), BF16), jax.ShapeDtypeStruct((nq, m), BF16)],
        out_specs=[pl.BlockSpec((tm, tn_q), lambda i, j, k: (i, j)), pl.BlockSpec((tn_q, tm), lambda i, j, k: (j, i))],
        epilogue=functools.partial(_ep_rope_q, scale=HEAD_DIM ** -0.5 * LOG2_E), name="nsa_proj_q")
    kv = matmul(
        [xn], w_in, layer, col_block0=nq // half, n_cols=3 * KV_ROW, tm=tm, tn=half, tk=d,
        extras=[cos, sin_signed], extra_specs=[tab_spec, tab_spec],
        out_shapes=[jax.ShapeDtypeStruct((m * rows_per_tok, HEAD_DIM), F32)] * 3
        + [jax.ShapeDtypeStruct((3, N_KV_HEADS, m, HEAD_DIM), BF16),
           jax.ShapeDtypeStruct((3, N_KV_HEADS, HEAD_DIM, m), BF16)],
        out_specs=[pl.BlockSpec((tm * rows_per_tok, HEAD_DIM), lambda i, j, k: (i, 0))] * 3
        + [pl.BlockSpec((None, N_KV_HEADS, tm, HEAD_DIM), lambda i, j, k: (j // 2, 0, i, 0)),
           pl.BlockSpec((None, N_KV_HEADS, HEAD_DIM, tm), lambda i, j, k: (j // 2, 0, 0, i))],
        epilogue=_ep_nsa_kv, name="nsa_proj_kv", column_steps_ordered=True)
    kv = (kv[:3], kv[3], kv[4])
    n_gate = 3 * N_HEADS
    w_gate = jnp.pad(w_in[layer, :, nq + 3 * KV_ROW:], ((0, 0), (0, LANES - n_gate)))[None]
    gates = matmul(
        [xn], w_gate, 0, col_block0=0, n_cols=LANES, tm=tm, tn=LANES, tk=d, extras=[], extra_specs=[],
        out_shapes=[jax.ShapeDtypeStruct((m, LANES), F32)],
        out_specs=[pl.BlockSpec((tm, LANES), lambda i, j, k: (i, 0))],
        epilogue=_ep_sigmoid, name="nsa_proj_gate")[0]
    gates_t = gates[:, :n_gate].reshape(m, 3, N_KV_HEADS, GQA_GROUP).transpose(1, 2, 0, 3)
    return q, kv, gates_t


def _cmp_bias(w1, pos, b1, layer):
    k_full = CMP_BLOCK * HEAD_DIM
    pos_rows = jnp.zeros((2, SUBLANES, k_full), F32).at[:, 0].set(pos[layer].reshape(2, k_full))
    terms = []
    for e in range(2):
        t = matmul(
            [pos_rows[e]], w1.reshape(-1, k_full, HEAD_DIM), layer * 2 + e, col_block0=0, n_cols=HEAD_DIM,
            tm=SUBLANES, tn=HEAD_DIM, tk=k_full, extras=[], extra_specs=[],
            out_shapes=[jax.ShapeDtypeStruct((SUBLANES, HEAD_DIM), F32)],
            out_specs=[pl.BlockSpec((SUBLANES, HEAD_DIM), lambda i, j, k: (0, 0))],
            epilogue=_ep_cast, name="nsa_cmp_pos_term")[0]
        terms.append(t[0])
    return jnp.stack(terms) + b1[layer]


def _mlp_up(xn, w1, layer, *, tm):
    m, d = xn.shape
    d_ff = w1.shape[-1]
    tn = 1024
    return matmul(
        [xn], w1, layer, col_block0=0, n_cols=d_ff, tm=tm, tn=tn, tk=d, extras=[], extra_specs=[],
        out_shapes=[jax.ShapeDtypeStruct((m, d_ff), BF16)],
        out_specs=[pl.BlockSpec((tm, tn), lambda i, j, k: (i, j))],
        epilogue=_ep_relu2, name="mlp_up")[0]


def _mixer_out_proj(x, w, layer, h, norm_g, *, name):
    m, d = h.shape
    kdim = x.shape[1]
    w_bytes = 2 * kdim * d * w.dtype.itemsize
    per_row = 2 * kdim * x.dtype.itemsize + d * (4 * 4 + 2 * 2 + 4)
    tm = min(m, (VMEM_LIMIT_BYTES * 7 // 8 - w_bytes) // per_row // LANES * LANES)
    while m % tm:
        tm -= LANES
    row = pl.BlockSpec((tm, d), lambda i, j, k: (i, 0))
    return matmul(
        [x], w, layer, col_block0=0, n_cols=d, tm=tm, tn=d, tk=kdim,
        extras=[h, norm_g.reshape(1, d)], extra_specs=[row, pl.BlockSpec((1, d), lambda i, j, k: (0, 0))],
        out_shapes=[jax.ShapeDtypeStruct((m, d), F32), jax.ShapeDtypeStruct((m, d), BF16)], out_specs=[row, row],
        epilogue=_ep_residual_norm, name=name)


def _mlp_down(a, w2, layer, h, *, tm):
    m, d = h.shape
    tn = 1024
    spec = pl.BlockSpec((tm, tn), lambda i, j, k: (i, j))
    return matmul(
        [a], w2, layer, col_block0=0, n_cols=d, tm=tm, tn=tn, tk=1024, extras=[h], extra_specs=[spec],
        out_shapes=[jax.ShapeDtypeStruct((m, d), F32)], out_specs=[spec], epilogue=_ep_residual_norm,
        name="mlp_down")[0]


def _nsa_tables(pos):
    cos, sin = _rope_tables(pos, HEAD_DIM // 2)
    return jnp.concatenate([cos, cos], axis=1), jnp.concatenate([-sin, sin], axis=1)


def _nsa_prompt(xn, layer, w_in, cmp_w1, cmp_bias, cmp_w2, *, n_seq, seq):
    tm = PROMPT_ROW_TILE
    cos, sin_signed = _nsa_tables(jnp.arange(seq))
    (q, qt), (kv_rows, k_bf, vt_bf), gates_t = _project_nsa(xn, w_in, layer, cos, sin_signed, tm=tm // 2)
    n_chunks = seq // CMP_STRIDE
    chunks_per_step = 128
    kv_chunks = kv_rows[0].reshape(1, (n_seq * n_chunks) // chunks_per_step, chunks_per_step, CMP_STRIDE,
                                   2 * N_KV_HEADS, HEAD_DIM)
    kc = compress(kv_chunks, 0, None, cmp_w1, cmp_bias, cmp_w2, n_seq=n_seq, n_chunks=n_chunks, n_pages=1)
    n_blk = seq // SEL_BLOCK
    cover_t = _cover_t(n_chunks, n_blk, n_blk)
    o1, val = cmp_attention(q, kc, cover_t, gates_t, n_seq=n_seq, seq=seq, tq=256, pos0=0)
    bias_t = select_bias(val, n_blk=n_blk, tc=512, causal_seq=seq)
    nbp = pl.cdiv(n_blk, LANES) * LANES
    o2 = flash_attention(qt, k_bf, vt_bf, 1, gates_t, o1, n_seq=n_seq, seq=seq, tq=512, tk=512, bias=bias_t,
                         onehot=_block_onehot(seq, nbp))
    o3 = flash_attention(qt, k_bf, vt_bf, 2, gates_t, o2, n_seq=n_seq, seq=seq, tq=256, tk=256, window=WINDOW,
                         out_dtype=BF16)
    return o3, kv_rows


def _nsa_sample(xn, layer, w_in, cmp_w1, cmp_bias, cmp_w2, cache_cmp, cache_sel, win_buf, page_table,
                *, n_seq, n_tok, past_len):
    tq = SAMPLE_PAD
    m = n_seq * tq
    cos, sin_signed = _nsa_tables(jnp.tile(past_len + jnp.arange(tq), n_seq))
    (q, _), (kv_rows, _, _), gates_t = _project_nsa(xn, w_in, layer, cos, sin_signed, tm=m)
    n_chunks = past_len // CMP_STRIDE
    kc = compress(cache_cmp, layer, page_table, cmp_w1, cmp_bias, cmp_w2, n_seq=n_seq, n_chunks=n_chunks,
                  n_pages=16)
    n_blk = pl.cdiv(past_len + n_tok, SEL_BLOCK)
    nbv = pl.cdiv(n_blk, SUBLANES) * SUBLANES
    cover_t = _cover_t(n_chunks, n_blk, nbv)
    o1, val = cmp_attention(q, kc, cover_t, gates_t, n_seq=n_seq, seq=tq, tq=tq, pos0=past_len)
    ncols = n_seq * N_KV_HEADS * n_tok
    bias_t = select_bias(val[..., :n_tok].transpose(2, 0, 1, 3).reshape(nbv, ncols), n_blk=n_blk, tc=ncols)
    blocks_per_page = cache_sel.shape[2] // (2 * N_KV_HEADS * SEL_BLOCK)
    n_pages = page_table.shape[1]
    chosen = (bias_t == 0).reshape(nbv, n_seq, N_KV_HEADS * n_tok).any(axis=2)
    needed = chosen[:n_pages * blocks_per_page].reshape(n_pages, blocks_per_page, n_seq).any(axis=1).T
    nbp = pl.cdiv(n_blk + 1, LANES) * LANES
    bias = bias_t.reshape(nbv, n_seq, N_KV_HEADS, n_tok).transpose(1, 2, 3, 0)
    bias = jnp.pad(bias, ((0, 0), (0, 0), (0, 0), (0, nbp - nbv)), constant_values=NEG_INF)
    bias = jnp.pad(bias, ((0, 0), (0, 0), (0, tq - n_tok), (0, 0)))
    o2 = sel_attention_sample(q, bias, needed, kv_rows[1], cache_sel, layer, page_table, gates_t, o1,
                              n_pages_step=16, past_len=past_len)
    o3 = win_attention_sample(q, win_buf, layer, kv_rows[2], gates_t, o2, out_dtype=BF16)
    return o3, kv_rows


def _ret_layer(xn, layer, w_in, gn, state0, pos, *, n_seq, seq, chunk, n_tok, tm):
    m, d = xn.shape
    nk = RET_HEADS * RET_KDIM
    nv = RET_HEADS * RET_VDIM
    cos, sin = _rope_tables(pos, RET_KDIM // 2)
    tabs_per_seq = cos.shape[0] // tm
    tab_spec = pl.BlockSpec((tm, LANES), lambda i, j, k: (i % tabs_per_seq, 0))
    tn = 512

    def proj(col0, n_cols, dtype, epilogue, extras, extra_specs, name):
        return matmul(
            [xn], w_in, layer, col_block0=col0 // tn, n_cols=n_cols, tm=tm, tn=tn, tk=d,
            extras=extras, extra_specs=extra_specs,
            out_shapes=[jax.ShapeDtypeStruct((m, n_cols), dtype)],
            out_specs=[pl.BlockSpec((tm, tn), lambda i, j, k: (i, j))],
            epilogue=epilogue, name=name)[0]

    q = proj(0, nk, BF16, functools.partial(_ep_rope_ret, scale=1.0), [cos, sin], [tab_spec, tab_spec], "ret_proj_q")
    k = proj(nk, nk, F32, functools.partial(_ep_rope_ret, scale=RET_KDIM ** -0.5), [cos, sin],
             [tab_spec, tab_spec], "ret_proj_k")
    v = proj(2 * nk, nv, BF16, _ep_cast, [], [], "ret_proj_v")
    g = proj(2 * nk + nv, nv, F32, _ep_cast, [], [], "ret_proj_g")
    return retention(q, k, v, g, gn.reshape(gn.shape[0], 1, nv), layer, state0, n_seq=n_seq, seq=seq, chunk=chunk,
                     n_tok=n_tok)


def kernel(x_prompt, x_sample, cache_cmp_kv, cache_sel_kv, state_win_kv, state_ret, page_table, norm_mix, norm_ffn,
           norm_final, nsa_w_in, nsa_w_out, nsa_cmp_pos, nsa_cmp_w1, nsa_cmp_b1, nsa_cmp_w2, ret_w_in, ret_gn,
           ret_w_out, ffn_w1, ffn_w2):
    n_seq_p, seq_p, d = x_prompt.shape
    n_seq_s, n_tok_s, _ = x_sample.shape
    depth = norm_mix.shape[0]
    n_nsa = cache_sel_kv.shape[0]
    pool, page_rows = cache_sel_kv.shape[1], cache_sel_kv.shape[2]
    past_len = page_table.shape[1] * page_rows
    assert past_len % CMP_STRIDE == 0 and n_tok_s < CMP_STRIDE and n_tok_s <= SAMPLE_PAD
    assert state_win_kv.shape[2] == WINDOW and seq_p >= WINDOW

    hp = x_prompt.reshape(n_seq_p * seq_p, d)
    hs = jnp.pad(x_sample, ((0, 0), (0, SAMPLE_PAD - n_tok_s), (0, 0))).reshape(n_seq_s * SAMPLE_PAD, d)
    m_s = n_seq_s * SAMPLE_PAD
    rows_per_tok = 2 * N_KV_HEADS
    cache_cmp = cache_cmp_kv.reshape(n_nsa, pool, page_rows // CMP_STRIDE, CMP_STRIDE, rows_per_tok, HEAD_DIM)
    cache_sel = cache_sel_kv.reshape(n_nsa, pool, page_rows * rows_per_tok, HEAD_DIM)
    win_buf = state_win_kv.reshape(n_nsa, n_seq_s, WINDOW * rows_per_tok, HEAD_DIM)
    cmp_w1 = nsa_cmp_w1.reshape(n_nsa, 2, CMP_BLOCK * HEAD_DIM, HEAD_DIM)
    kv_shape = (2, N_KV_HEADS, HEAD_DIM)
    ffn_w2 = ffn_w2.astype(BF16)
    ret_w_out = ret_w_out.astype(BF16)
    nsa_w_out = nsa_w_out.astype(BF16)

    cmp_p, cmp_s, sel_p, sel_s, win_p, win_s, ret_p, ret_s = [], [], [], [], [], [], [], []
    xp = rmsnorm(hp, norm_mix[0], BF16, 256)
    xs = rmsnorm(hs, norm_mix[0], BF16, m_s)
    for layer in range(depth):
        if layer % 2 == 0:
            a = layer // 2
            cmp_bias = _cmp_bias(cmp_w1, nsa_cmp_pos, nsa_cmp_b1, a)
            op, kv_p = _nsa_prompt(xp, a, nsa_w_in, cmp_w1, cmp_bias, nsa_cmp_w2, n_seq=n_seq_p, seq=seq_p)
            os_, kv_s = _nsa_sample(xs, a, nsa_w_in, cmp_w1, cmp_bias, nsa_cmp_w2, cache_cmp, cache_sel, win_buf,
                                    page_table, n_seq=n_seq_s, n_tok=n_tok_s, past_len=past_len)
            w_out, w_idx = nsa_w_out, a
            kv_p = [r.reshape((n_seq_p, seq_p) + kv_shape) for r in kv_p]
            kv_s = [r.reshape((n_seq_s, SAMPLE_PAD) + kv_shape)[:, :n_tok_s] for r in kv_s]
            cmp_p.append(kv_p[0])
            sel_p.append(kv_p[1])
            win_p.append(kv_p[2][:, seq_p - WINDOW:])
            cmp_s.append(kv_s[0])
            sel_s.append(kv_s[1])
            win_s.append(jnp.concatenate([state_win_kv[a][:, n_tok_s:], kv_s[2]], axis=1))
        else:
            r = layer // 2
            zeros = jnp.zeros((n_seq_p, RET_HEADS, RET_KDIM, RET_VDIM), F32)
            op, st_p = _ret_layer(xp, r, ret_w_in, ret_gn, zeros, jnp.arange(seq_p), n_seq=n_seq_p, seq=seq_p,
                                  chunk=RET_CHUNK, n_tok=RET_CHUNK, tm=PROMPT_ROW_TILE)
            os_, st_s = _ret_layer(xs, r, ret_w_in, ret_gn, state_ret[r],
                                   jnp.tile(past_len + jnp.arange(SAMPLE_PAD), n_seq_s), n_seq=n_seq_s,
                                   seq=SAMPLE_PAD, chunk=SAMPLE_PAD, n_tok=n_tok_s, tm=m_s)
            ret_p.append(st_p)
            ret_s.append(st_s)
            w_out, w_idx = ret_w_out, r
        hp, xp = _mixer_out_proj(op, w_out, w_idx, hp, norm_ffn[layer], name="mixer_out_proj")
        hs, xs = _mixer_out_proj(os_, w_out, w_idx, hs, norm_ffn[layer], name="mixer_out_proj_sample")
        hp = _mlp_down(_mlp_up(xp, ffn_w1, layer, tm=PROMPT_ROW_TILE), ffn_w2, layer, hp, tm=PROMPT_ROW_TILE)
        hs = _mlp_down(_mlp_up(xs, ffn_w1, layer, tm=m_s), ffn_w2, layer, hs, tm=m_s)
        if layer + 1 < depth:
            xp = rmsnorm(hp, norm_mix[layer + 1], BF16, 256)
            xs = rmsnorm(hs, norm_mix[layer + 1], BF16, m_s)
    y_prompt = rmsnorm(hp, norm_final, F32, 256).reshape(n_seq_p, seq_p, d)
    y_sample = rmsnorm(hs, norm_final, F32, m_s).reshape(n_seq_s, SAMPLE_PAD, d)[:, :n_tok_s]
    return (y_prompt, y_sample, jnp.stack(cmp_p), jnp.stack(cmp_s), jnp.stack(sel_p), jnp.stack(sel_s),
            jnp.stack(win_p), jnp.stack(win_s), jnp.stack(ret_p), jnp.stack(ret_s))
```

```python
import functools

import numpy as np
import jax
import jax.numpy as jnp
from jax import lax
from jax.experimental import pallas as pl
from jax.experimental.pallas import tpu as pltpu

F32 = jnp.float32
BF16 = jnp.bfloat16

N_HEADS = 16
HEAD_DIM = 128
N_KV_HEADS = 4
GQA_GROUP = N_HEADS // N_KV_HEADS
KV_ROW = 2 * N_KV_HEADS * HEAD_DIM
CMP_BLOCK = 32
CMP_STRIDE = 16
SEL_BLOCK = 64
N_SEL = 16
N_LOCAL = 2
WINDOW = 512
ROPE_THETA = 10000.0
RET_HEADS = 8
RET_KDIM = 256
RET_VDIM = 512
RET_CHUNK = 128
RET_HEADS_PER_STEP = 8
RMS_EPS = 1e-6
GN_EPS = 1e-5
NEG_INF = -1e30
FORCE_SCORE = 1e9
LOG2_E = 1.4426950408889634

SUBLANES = 8
LANES = 128
VMEM_LIMIT_BYTES = 56 * 1024 * 1024
SAMPLE_PAD = 16
PROMPT_ROW_TILE = 2048

_NT = (((1,), (1,)), ((), ()))
_TN = (((0,), (0,)), ((), ()))


def _params(*semantics):
    return pltpu.CompilerParams(dimension_semantics=semantics, vmem_limit_bytes=VMEM_LIMIT_BYTES)


def _rmsnorm_kernel(x_ref, g_ref, o_ref):
    x = x_ref[...]
    ms = jnp.mean(x * x, axis=-1, keepdims=True)
    o_ref[...] = (x * lax.rsqrt(ms + RMS_EPS) * g_ref[...]).astype(o_ref.dtype)


def rmsnorm(x, g, out_dtype, tm, row_block0=0, n_blocks=None):
    m, d = x.shape
    if n_blocks is None:
        n_blocks = m // tm
    return pl.pallas_call(
        _rmsnorm_kernel,
        grid=(n_blocks,),
        in_specs=[pl.BlockSpec((tm, d), lambda i: (i + row_block0, 0)), pl.BlockSpec((1, d), lambda i: (0, 0))],
        out_specs=pl.BlockSpec((tm, d), lambda i: (i, 0)),
        out_shape=jax.ShapeDtypeStruct((n_blocks * tm, d), out_dtype),
        compiler_params=_params("parallel"),
        name="rmsnorm",
    )(x, g.reshape(1, d))


def _mm_kernel(*refs, n_x, n_extra, n_out, nk, epilogue):
    x_refs = refs[:n_x]
    w_ref = refs[n_x]
    extra = refs[n_x + 1:n_x + 1 + n_extra]
    outs = refs[n_x + 1 + n_extra:n_x + 1 + n_extra + n_out]
    x = x_refs[0][...]
    for r in x_refs[1:]:
        x = x + r[...]
    part = jnp.dot(x.astype(BF16), w_ref[...].astype(BF16), preferred_element_type=F32)
    if nk == 1:
        epilogue(part, extra, outs)
        return
    k = pl.program_id(2)
    if epilogue is _ep_residual_norm:
        @pl.when(k == 0)
        def _():
            outs[0][...] = extra[0][...] + part

        @pl.when(k > 0)
        def _():
            outs[0][...] += part

        @pl.when(k == nk - 1)
        def _():
            _write_norm(outs[0][...], extra, outs)
        return
    acc_ref = refs[-1]

    @pl.when(k == 0)
    def _():
        acc_ref[...] = part

    @pl.when(k > 0)
    def _():
        acc_ref[...] += part

    @pl.when(k == nk - 1)
    def _():
        epilogue(acc_ref[...], extra, outs)


def matmul(xs, w, layer, *, col_block0, n_cols, tm, tn, tk, extras, extra_specs, out_shapes, out_specs,
           epilogue, name, column_steps_ordered=False):
    m, kdim = xs[0].shape
    nk = kdim // tk
    grid = (m // tm, pl.cdiv(n_cols, tn), nk)
    in_specs = [pl.BlockSpec((tm, tk), lambda i, j, k: (i, k)) for _ in xs]
    in_specs.append(pl.BlockSpec((None, tk, tn), lambda i, j, k: (layer, k, j + col_block0)))
    in_specs.extend(extra_specs)
    scratch = [pltpu.VMEM((tm, tn), F32)] if nk > 1 and epilogue is not _ep_residual_norm else []
    kern = functools.partial(_mm_kernel, n_x=len(xs), n_extra=len(extras), n_out=len(out_shapes), nk=nk,
                             epilogue=epilogue)
    return pl.pallas_call(
        kern,
        grid=grid,
        in_specs=in_specs,
        out_specs=out_specs,
        out_shape=out_shapes,
        scratch_shapes=scratch,
        compiler_params=_params("parallel", "arbitrary" if column_steps_ordered else "parallel", "arbitrary"),
        name=name,
    )(*xs, w, *extras)


def _rope_half_lane(x, cos, sin_signed):
    return x * cos + pltpu.roll(x, HEAD_DIM // 2, 1) * sin_signed


def _ep_rope_q(part, extra, outs, *, scale):
    cos = extra[0][...]
    sin = extra[1][...]
    for h in range(part.shape[1] // HEAD_DIM):
        sl = slice(h * HEAD_DIM, (h + 1) * HEAD_DIM)
        r = _rope_half_lane(part[:, sl], cos, sin) * scale
        outs[0][:, sl] = r.astype(outs[0].dtype)
        outs[1][sl, :] = r.T.astype(outs[1].dtype)


def _ep_nsa_kv(part, extra, outs):
    tm = part.shape[0]
    rows_per_tok = 2 * N_KV_HEADS
    j = pl.program_id(1)

    def heads():
        return [part[:, h * HEAD_DIM:(h + 1) * HEAD_DIM] for h in range(N_KV_HEADS)]

    for br in range(3):
        @pl.when(j == 2 * br)
        def _(br=br):
            cos = extra[0][...]
            sin = extra[1][...]
            for h, x in enumerate(heads()):
                k = _rope_half_lane(x, cos, sin)
                outs[br][pl.ds(h, tm, stride=rows_per_tok), :] = k
                outs[3][h] = k.astype(BF16)

        @pl.when(j == 2 * br + 1)
        def _(br=br):
            for h, v in enumerate(heads()):
                outs[br][pl.ds(N_KV_HEADS + h, tm, stride=rows_per_tok), :] = v
                outs[4][h] = v.T.astype(BF16)


def _ep_gates(part, extra, outs):
    g = jax.nn.sigmoid(part)
    for br in range(3):
        for kvh in range(N_KV_HEADS):
            c = br * N_HEADS + kvh * GQA_GROUP
            outs[0][br, kvh] = g[:, c:c + GQA_GROUP]


def _write_norm(h, extra, outs):
    if len(extra) < 2:
        return
    ms = jnp.mean(h * h, axis=-1, keepdims=True)
    y = h * lax.rsqrt(ms + RMS_EPS) * extra[1][...]
    outs[-1][...] = y.astype(outs[-1].dtype)


def _ep_residual_norm(part, extra, outs):
    h = extra[0][...] + part
    outs[0][...] = h
    _write_norm(h, extra, outs)


def _ep_relu2(part, extra, outs):
    h = jnp.maximum(part, 0.0)
    outs[0][...] = (h * h).astype(outs[0].dtype)


def _ep_cast(part, extra, outs):
    outs[0][...] = part.astype(outs[0].dtype)


def _ep_rope_ret(part, extra, outs, *, scale):
    cos = extra[0][...]
    sin = extra[1][...]
    for h in range(part.shape[1] // RET_KDIM):
        a = slice(h * RET_KDIM, h * RET_KDIM + LANES)
        b = slice(h * RET_KDIM + LANES, (h + 1) * RET_KDIM)
        x1 = part[:, a]
        x2 = part[:, b]
        outs[0][:, a] = ((x1 * cos - x2 * sin) * scale).astype(outs[0].dtype)
        outs[0][:, b] = ((x2 * cos + x1 * sin) * scale).astype(outs[0].dtype)


def _rope_tables(pos, half):
    inv = ROPE_THETA ** (-jnp.arange(half, dtype=F32) / half)
    ang = pos.astype(F32)[:, None] * inv[None, :]
    return jnp.cos(ang), jnp.sin(ang)


def _compress_kernel(*refs, n_pages, paged):
    if paged:
        refs = refs[1:]
    page_refs = refs[:n_pages]
    next_ref, w1_ref, bias_ref, w2_ref, out_ref, shift_ref, y_ref = refs[n_pages:]
    nc = page_refs[0].shape[0]
    n_heads = N_KV_HEADS
    n_main = n_pages * nc * n_heads
    k_half = CMP_STRIDE * HEAD_DIM
    tiles = [[page_refs[p][:, s] for s in range(CMP_STRIDE)] for p in range(n_pages)]
    next_tiles = [next_ref[0, s] for s in range(CMP_STRIDE)]
    for e in range(2):
        hsl = slice(e * n_heads, (e + 1) * n_heads)
        blocks = [
            jnp.concatenate([tiles[p][s][:, hsl, :].reshape(nc * n_heads, HEAD_DIM) for s in range(CMP_STRIDE)],
                            axis=1)
            for p in range(n_pages)]
        blocks.append(jnp.concatenate([next_tiles[s][hsl, :] for s in range(CMP_STRIDE)], axis=1))
        blocks.append(jnp.zeros((SUBLANES - n_heads, CMP_STRIDE * HEAD_DIM), F32))
        x = jnp.concatenate(blocks, axis=0).astype(BF16)
        wcat = jnp.concatenate([w1_ref[e, :k_half], w1_ref[e, k_half:]], axis=1).astype(BF16)
        per = jnp.dot(x, wcat, preferred_element_type=F32)
        hid = per.shape[1] // 2
        shift_ref[...] = per[:, hid:]
        h = per[:n_main, :hid] + shift_ref[pl.ds(n_heads, n_main), :]
        h = jax.nn.gelu(h + bias_ref[e:e + 1, :])
        y_ref[...] = jnp.dot(h.astype(BF16), w2_ref[e].astype(BF16), preferred_element_type=F32)
        for g in range(n_heads):
            out_ref[e * n_heads + g] = y_ref[pl.ds(g, n_main // n_heads, stride=n_heads), :].astype(out_ref.dtype)


def compress(kv, layer, page_table, w1, bias, w2, *, n_seq, n_chunks, n_pages):
    paged = page_table is not None
    nc = kv.shape[2]
    n_main = n_pages * nc
    steps = n_chunks // n_main
    pages_per_seq = steps * n_pages
    rows_per_tok = 2 * N_KV_HEADS

    def nxt(s):
        return jnp.minimum((s + 1) * n_pages, pages_per_seq - 1)

    if paged:
        def page_map(p):
            return lambda b, s, pt: (layer, pt[b, s * n_pages + p], 0, 0, 0, 0)

        def next_map(b, s, pt):
            return (layer, pt[b, nxt(s)], 0, 0, 0, 0)

        def fixed(*idx):
            return lambda b, s, pt: idx

        out_map = lambda b, s, pt: (b, 0, s, 0)
    else:
        def page_map(p):
            return lambda b, s: (layer, b * pages_per_seq + s * n_pages + p, 0, 0, 0, 0)

        def next_map(b, s):
            return (layer, b * pages_per_seq + nxt(s), 0, 0, 0, 0)

        def fixed(*idx):
            return lambda b, s: idx

        out_map = lambda b, s: (b, 0, s, 0)

    tile = (CMP_STRIDE, rows_per_tok, HEAD_DIM)
    in_specs = [pl.BlockSpec((None, None, nc) + tile, page_map(p)) for p in range(n_pages)]
    in_specs += [
        pl.BlockSpec((None, None, 1) + tile, next_map),
        pl.BlockSpec((None, 2, CMP_BLOCK * HEAD_DIM, HEAD_DIM), fixed(layer, 0, 0, 0)),
        pl.BlockSpec((2, HEAD_DIM), fixed(0, 0)),
        pl.BlockSpec((None, 2, HEAD_DIM, HEAD_DIM), fixed(layer, 0, 0, 0)),
    ]
    kern = functools.partial(_compress_kernel, n_pages=n_pages, paged=paged)
    grid_spec = pltpu.PrefetchScalarGridSpec(
        num_scalar_prefetch=1 if paged else 0,
        grid=(n_seq, steps),
        in_specs=in_specs,
        out_specs=pl.BlockSpec((None, rows_per_tok, n_main, HEAD_DIM), out_map),
        scratch_shapes=[pltpu.VMEM((n_main * N_KV_HEADS + SUBLANES, HEAD_DIM), F32),
                        pltpu.VMEM((n_main * N_KV_HEADS, HEAD_DIM), F32)],
    )
    args = ([page_table] if paged else []) + [kv] * (n_pages + 1) + [w1, bias, w2]
    return pl.pallas_call(
        kern,
        grid_spec=grid_spec,
        out_shape=jax.ShapeDtypeStruct((n_seq, rows_per_tok, n_chunks, HEAD_DIM), BF16),
        compiler_params=_params("parallel", "arbitrary"),
        name="nsa_compress",
    )(*args)


def _group_rows(q):
    return jnp.concatenate([q[:, h * HEAD_DIM:(h + 1) * HEAD_DIM] for h in range(GQA_GROUP)], axis=0)


def _cmp_kernel(q_ref, k_ref, v_ref, cov_ref, gate_ref, o_ref, val_ref, *, tq, pos0):
    qi = pl.program_id(2)
    q4 = _group_rows(q_ref[...])
    s = lax.dot_general(q4, k_ref[...], _NT, preferred_element_type=F32)
    row = lax.broadcasted_iota(jnp.int32, s.shape, 0)
    col = lax.broadcasted_iota(jnp.int32, s.shape, 1)
    qpos = pos0 + qi * tq + jnp.bitwise_and(row, tq - 1)
    valid = col * CMP_STRIDE + (CMP_BLOCK - 1) <= qpos
    s = jnp.where(valid, s, NEG_INF)
    m = jnp.max(s, axis=1, keepdims=True)
    e = jnp.where(valid, jnp.exp2(s - m), 0.0)
    l = jnp.sum(e, axis=1, keepdims=True)
    p = e / jnp.where(l > 0.0, l, 1.0)
    o4 = jnp.dot(p.astype(BF16), v_ref[...], preferred_element_type=F32)
    for h in range(GQA_GROUP):
        o_ref[:, h * HEAD_DIM:(h + 1) * HEAD_DIM] = o4[h * tq:(h + 1) * tq] * gate_ref[:, h:h + 1]
    imp = p[0:tq] + p[tq:2 * tq] + p[2 * tq:3 * tq] + p[3 * tq:4 * tq]
    hi = imp.astype(BF16)
    r1 = imp - hi.astype(F32)
    mid = r1.astype(BF16)
    lo = (r1 - mid.astype(F32)).astype(BF16)
    cov = cov_ref[...]
    score = (lax.dot_general(cov, hi, _NT, preferred_element_type=F32)
             + lax.dot_general(cov, mid, _NT, preferred_element_type=F32)
             + lax.dot_general(cov, lo, _NT, preferred_element_type=F32))
    jb = lax.broadcasted_iota(jnp.int32, score.shape, 0)
    qp = pos0 + qi * tq + lax.broadcasted_iota(jnp.int32, score.shape, 1)
    cur = jnp.right_shift(qp, SEL_BLOCK.bit_length() - 1)
    visible = jb <= cur
    forced = (jb == 0) | (visible & (jb > cur - N_LOCAL))
    val_ref[...] = jnp.where(forced, FORCE_SCORE, jnp.where(visible, score, -FORCE_SCORE))


def cmp_attention(q, kc, cover_t, gates_t, *, n_seq, seq, tq, pos0):
    nq = seq // tq
    ncp = kc.shape[2]
    nbv = cover_t.shape[0]
    gw = GQA_GROUP * HEAD_DIM
    kern = functools.partial(_cmp_kernel, tq=tq, pos0=pos0)
    if tq % LANES == 0:
        val_spec = pl.BlockSpec((nbv, tq), lambda b, g, i: (0, (b * N_KV_HEADS + g) * nq + i))
        val_shape = jax.ShapeDtypeStruct((nbv, n_seq * N_KV_HEADS * seq), F32)
    else:
        val_spec = pl.BlockSpec((None, None, nbv, tq), lambda b, g, i: (b, g, 0, i))
        val_shape = jax.ShapeDtypeStruct((n_seq, N_KV_HEADS, nbv, seq), F32)
    return pl.pallas_call(
        kern,
        grid=(n_seq, N_KV_HEADS, nq),
        in_specs=[
            pl.BlockSpec((tq, gw), lambda b, g, i: (b * nq + i, g)),
            pl.BlockSpec((None, None, ncp, HEAD_DIM), lambda b, g, i: (b, g, 0, 0)),
            pl.BlockSpec((None, None, ncp, HEAD_DIM), lambda b, g, i: (b, N_KV_HEADS + g, 0, 0)),
            pl.BlockSpec((nbv, ncp), lambda b, g, i: (0, 0)),
            pl.BlockSpec((None, None, tq, GQA_GROUP), lambda b, g, i: (0, g, b * nq + i, 0)),
        ],
        out_specs=[pl.BlockSpec((tq, gw), lambda b, g, i: (b * nq + i, g)), val_spec],
        out_shape=[jax.ShapeDtypeStruct((n_seq * seq, N_HEADS * HEAD_DIM), F32), val_shape],
        compiler_params=_params("parallel", "parallel", "parallel"),
        name="nsa_cmp_attention",
    )(q, kc, kc, cover_t, gates_t)


def _cover_t(n_chunks, n_blk, nbv):
    n_cmp = n_chunks - CMP_BLOCK // CMP_STRIDE + 1
    c0 = np.arange(n_chunks) * CMP_STRIDE
    j0 = np.arange(nbv) * SEL_BLOCK
    cov = (c0[None, :] < j0[:, None] + SEL_BLOCK) & (c0[None, :] + CMP_BLOCK > j0[:, None])
    cov &= (np.arange(n_chunks)[None, :] < n_cmp) & (np.arange(nbv)[:, None] < n_blk)
    return jnp.asarray(cov.astype(np.float32), dtype=BF16)


def _rank_kernel(val_ref, out_ref, *, n_blk, n_sel, causal_seq):
    val = val_ref[...]
    tc = val.shape[1]
    jb = lax.broadcasted_iota(jnp.int32, val.shape, 0)
    if causal_seq is None:
        n_iter = n_blk
    else:
        pos0 = (pl.program_id(0) % (causal_seq // tc)) * tc
        n_iter = jnp.minimum(n_blk, (pos0 + tc - 1) // SEL_BLOCK + 1)

    def body(i, rank):
        row = val_ref[pl.ds(i, 1), :]
        beats = (row > val) | ((row == val) & (i < jb))
        return rank + jnp.where(beats, 1, 0)

    rank = lax.fori_loop(0, n_iter, body, jnp.zeros(val.shape, jnp.int32))
    chosen = (rank < n_sel) & (val > -0.5 * FORCE_SCORE)
    out_ref[...] = jnp.where(chosen, 0.0, NEG_INF).astype(out_ref.dtype)


def select_bias(val, *, n_blk, tc, causal_seq=None):
    nbv, ncols = val.shape
    kern = functools.partial(_rank_kernel, n_blk=n_blk, n_sel=min(N_SEL, n_blk), causal_seq=causal_seq)
    return pl.pallas_call(
        kern,
        grid=(ncols // tc,),
        in_specs=[pl.BlockSpec((nbv, tc), lambda i: (0, i))],
        out_specs=pl.BlockSpec((nbv, tc), lambda i: (0, i)),
        out_shape=jax.ShapeDtypeStruct((nbv, ncols), BF16),
        compiler_params=_params("parallel"),
        name="nsa_select_rank",
    )(val)


_STEP_FIRST, _STEP_LAST, _STEP_EDGE = 1, 2, 4


def _flash_steps(seq, tq, tk, window):
    qi_l, kt_l, fl_l = [], [], []
    for qi in range(seq // tq):
        q0, q1 = qi * tq, qi * tq + tq - 1
        kts = [kt for kt in range(seq // tk)
               if kt * tk <= q1 and (window is None or q0 - (kt * tk + tk - 1) < window)]
        for n, kt in enumerate(kts):
            k0, k1 = kt * tk, kt * tk + tk - 1
            all_visible = q0 - k1 >= 0 and (window is None or q1 - k0 < window)
            flags = (_STEP_FIRST if n == 0 else 0) | (_STEP_LAST if n == len(kts) - 1 else 0)
            flags |= 0 if all_visible else _STEP_EDGE
            qi_l.append(qi)
            kt_l.append(kt)
            fl_l.append(flags)
    return (jnp.asarray(np.array(qi_l, np.int32)), jnp.asarray(np.array(kt_l, np.int32)),
            jnp.asarray(np.array(fl_l, np.int32)))


def _flash_kernel(*refs, tq, tk, use_bias, window, out_dtype):
    qi_ref, kt_ref, flag_ref = refs[:3]
    if use_bias:
        qt_ref, bias_ref, k_ref, vt_ref, e_ref, gate_ref, prev_ref, o_ref, qa_ref, m_ref, acc_ref = refs[3:]
    else:
        qt_ref, k_ref, vt_ref, gate_ref, prev_ref, o_ref, qa_ref, m_ref, acc_ref = refs[3:]
    s = pl.program_id(2)
    qi = qi_ref[s]
    kt = kt_ref[s]
    flags = flag_ref[s]
    first = jnp.bitwise_and(flags, _STEP_FIRST) != 0
    last = jnp.bitwise_and(flags, _STEP_LAST) != 0
    edge = jnp.bitwise_and(flags, _STEP_EDGE) != 0

    @pl.when(first)
    def _():
        for h in range(GQA_GROUP):
            qa_ref[0:HEAD_DIM, h * tq:(h + 1) * tq] = qt_ref[h * HEAD_DIM:(h + 1) * HEAD_DIM, :]
            if use_bias:
                nbv = bias_ref.shape[0]
                qa_ref[HEAD_DIM:HEAD_DIM + nbv, h * tq:(h + 1) * tq] = bias_ref[...]
                if HEAD_DIM + nbv < qa_ref.shape[0]:
                    qa_ref[HEAD_DIM + nbv:, h * tq:(h + 1) * tq] = jnp.zeros(
                        (qa_ref.shape[0] - HEAD_DIM - nbv, tq), BF16)
        m_ref[...] = jnp.full(m_ref.shape, NEG_INF, F32)
        acc_ref[...] = jnp.zeros(acc_ref.shape, F32)

    def step(masked):
        if use_bias:
            ka = jnp.concatenate([k_ref[...], e_ref[...]], axis=1)
        else:
            ka = k_ref[...]
        st = jnp.dot(ka, qa_ref[...], preferred_element_type=F32)
        if masked:
            kpos = kt * tk + lax.broadcasted_iota(jnp.int32, st.shape, 0)
            qpos = qi * tq + jnp.bitwise_and(lax.broadcasted_iota(jnp.int32, st.shape, 1), tq - 1)
            d = qpos - kpos
            ok = d >= 0
            if window is not None:
                ok = ok & (d < window)
            st = jnp.where(ok, st, NEG_INF)
        m_old = m_ref[...]
        m_new = jnp.maximum(m_old, jnp.max(st, axis=0, keepdims=True))
        alpha = jnp.exp2(m_old - m_new)
        p = jnp.exp2(st - m_new).astype(BF16)
        va = jnp.concatenate([vt_ref[...], jnp.ones((acc_ref.shape[0] - HEAD_DIM, tk), BF16)], axis=0)
        acc_ref[...] = alpha * acc_ref[...] + jnp.dot(va, p, preferred_element_type=F32)
        m_ref[...] = m_new

    @pl.when(edge)
    def _():
        step(True)

    @pl.when(jnp.logical_not(edge))
    def _():
        step(False)

    @pl.when(last)
    def _():
        acc = acc_ref[...]
        ot = acc[:HEAD_DIM] / acc[HEAD_DIM:HEAD_DIM + 1]
        for h in range(GQA_GROUP):
            sl = slice(h * HEAD_DIM, (h + 1) * HEAD_DIM)
            o_h = ot[:, h * tq:(h + 1) * tq].T
            o_ref[:, sl] = (prev_ref[:, sl] + o_h * gate_ref[:, h:h + 1]).astype(out_dtype)


def flash_attention(qt, k, vt, branch, gates_t, prev, *, n_seq, seq, tq, tk, bias=None, onehot=None, window=None,
                    out_dtype=F32):
    use_bias = bias is not None
    nq = seq // tq
    nkt = seq // tk
    gw = GQA_GROUP * HEAD_DIM
    qi_tab, kt_tab, flag_tab = _flash_steps(seq, tq, tk, window)

    o_spec = pl.BlockSpec((tq, gw), lambda b, g, s, qi, kt, fl: (b * nq + qi[s], g))
    qt_spec = pl.BlockSpec((gw, tq), lambda b, g, s, qi, kt, fl: (g, b * nq + qi[s]))
    k_spec = pl.BlockSpec((None, None, tk, HEAD_DIM), lambda b, g, s, qi, kt, fl: (branch, g, b * nkt + kt[s], 0))
    vt_spec = pl.BlockSpec((None, None, HEAD_DIM, tk), lambda b, g, s, qi, kt, fl: (branch, g, 0, b * nkt + kt[s]))
    gate_spec = pl.BlockSpec((None, None, tq, GQA_GROUP), lambda b, g, s, qi, kt, fl: (branch, g, b * nq + qi[s], 0))
    in_specs = [qt_spec]
    args = [qt]
    qa_rows = HEAD_DIM
    if use_bias:
        nbv = bias.shape[0]
        nbp = onehot.shape[1]
        qa_rows += nbp
        in_specs.append(
            pl.BlockSpec((nbv, tq), lambda b, g, s, qi, kt, fl: (0, (b * N_KV_HEADS + g) * nq + qi[s])))
        args.append(bias)
    in_specs += [k_spec, vt_spec]
    args += [k, vt]
    if use_bias:
        in_specs.append(pl.BlockSpec((tk, nbp), lambda b, g, s, qi, kt, fl: (kt[s], 0)))
        args.append(onehot)
    in_specs += [gate_spec, o_spec]
    args += [gates_t, prev]
    kern = functools.partial(_flash_kernel, tq=tq, tk=tk, use_bias=use_bias, window=window, out_dtype=out_dtype)
    cols = GQA_GROUP * tq
    grid_spec = pltpu.PrefetchScalarGridSpec(
        num_scalar_prefetch=3,
        grid=(n_seq, N_KV_HEADS, qi_tab.shape[0]),
        in_specs=in_specs,
        out_specs=o_spec,
        scratch_shapes=[
            pltpu.VMEM((qa_rows, cols), BF16),
            pltpu.VMEM((1, cols), F32),
            pltpu.VMEM((HEAD_DIM + 2 * SUBLANES, cols), F32),
        ],
    )
    return pl.pallas_call(
        kern,
        grid_spec=grid_spec,
        out_shape=jax.ShapeDtypeStruct((n_seq * seq, N_HEADS * HEAD_DIM), out_dtype),
        compiler_params=_params("parallel", "parallel", "arbitrary"),
        name="nsa_sel_attention" if use_bias else "nsa_win_attention",
    )(qi_tab, kt_tab, flag_tab, *args)


def _block_onehot(n_keys, nbp):
    e = (np.arange(n_keys)[:, None] // SEL_BLOCK) == np.arange(nbp)[None, :]
    return jnp.asarray(e.astype(np.float32), dtype=BF16)


def _sel_sample_kernel(*refs, n_pages, nbp, past_len, nsteps, tq):
    logical_ref, count_ref = refs[1:3]
    qt_ref, bias_ref, new_ref = refs[3:6]
    page_refs = refs[6:6 + n_pages]
    gate_ref, prev_ref, o_ref, m_ref, acc_ref = refs[6 + n_pages:]
    b = pl.program_id(0)
    step = pl.program_id(1)
    page_rows = page_refs[0].shape[0]
    rows_per_tok = 2 * N_KV_HEADS
    page_tokens = page_rows // rows_per_tok
    blocks_per_page = page_tokens // SEL_BLOCK
    n_needed = count_ref[b]
    cols = qt_ref.shape[2]

    def scores(g, k, key_bias):
        return jnp.dot(k, qt_ref[g], preferred_element_type=F32) + key_bias

    def block_bias(g, first_block, n_blocks, valid):
        rows = bias_ref[g, pl.ds(first_block, n_blocks), :]
        rows = jnp.where(valid, rows, NEG_INF)
        return jnp.concatenate(
            [jnp.broadcast_to(rows[j:j + 1], (SEL_BLOCK, cols)) for j in range(n_blocks)], axis=0)

    def weighted_values(p, v):
        va = jnp.concatenate([v, jnp.ones(v.shape, BF16)], axis=1)
        return lax.dot_general(p, va, _TN, preferred_element_type=F32)

    @pl.when(step == 0)
    def _():
        for g in range(N_KV_HEADS):
            kn = new_ref[pl.ds(g, tq, stride=2 * N_KV_HEADS), :].astype(BF16)
            vn = new_ref[pl.ds(N_KV_HEADS + g, tq, stride=2 * N_KV_HEADS), :].astype(BF16)
            new_bias = block_bias(g, past_len // SEL_BLOCK, 1, True)[:tq]
            st = scores(g, kn, new_bias)
            key = lax.broadcasted_iota(jnp.int32, st.shape, 0)
            tok = jnp.bitwise_and(lax.broadcasted_iota(jnp.int32, st.shape, 1), tq - 1)
            st = jnp.where(key <= tok, st, NEG_INF)
            m = jnp.max(st, axis=0, keepdims=True)
            p = jnp.exp2(st - m).astype(BF16)
            m_ref[g] = m
            acc_ref[g] = weighted_values(p, vn)

    @pl.when(step * n_pages < n_needed)
    def _():
        slots = [step * n_pages + p for p in range(n_pages)]
        first_blocks = [logical_ref[b, slot] * blocks_per_page for slot in slots]
        for g in range(N_KV_HEADS):
            k = jnp.concatenate(
                [page_refs[p][pl.ds(g, page_tokens, stride=rows_per_tok), :] for p in range(n_pages)],
                axis=0).astype(BF16)
            v = jnp.concatenate(
                [page_refs[p][pl.ds(N_KV_HEADS + g, page_tokens, stride=rows_per_tok), :] for p in range(n_pages)],
                axis=0).astype(BF16)
            key_bias = jnp.concatenate(
                [block_bias(g, first_blocks[p], blocks_per_page, slots[p] < n_needed) for p in range(n_pages)],
                axis=0)
            st = scores(g, k, key_bias)
            m_old = m_ref[g]
            m_new = jnp.maximum(m_old, jnp.max(st, axis=0, keepdims=True))
            alpha = jnp.exp2(m_old - m_new)
            p = jnp.exp2(st - m_new).astype(BF16)
            n_cols = alpha.shape[1]
            eye = (lax.broadcasted_iota(jnp.int32, (n_cols, n_cols), 0)
                   == lax.broadcasted_iota(jnp.int32, (n_cols, n_cols), 1))
            alpha_col = jnp.sum(jnp.where(eye, alpha, 0.0), axis=1, keepdims=True)
            acc_ref[g] = alpha_col * acc_ref[g] + weighted_values(p, v)
            m_ref[g] = m_new

    @pl.when(step == nsteps - 1)
    def _():
        for g in range(N_KV_HEADS):
            acc = acc_ref[g]
            o4 = acc[:, :HEAD_DIM] / acc[:, HEAD_DIM:HEAD_DIM + 1]
            for h in range(GQA_GROUP):
                hh = g * GQA_GROUP + h
                sl = slice(hh * HEAD_DIM, (hh + 1) * HEAD_DIM)
                o_ref[:, sl] = prev_ref[:, sl] + o4[h * tq:(h + 1) * tq] * gate_ref[g, :, h:h + 1]


def sel_attention_sample(q, bias, needed, kv_new, cache, layer, page_table, gates_t, prev, *, n_pages_step,
                         past_len):
    n_seq, pages = page_table.shape
    tq = SAMPLE_PAD
    nbp = bias.shape[-1]
    page_rows = cache.shape[2]
    nsteps = pages // n_pages_step
    dm = N_HEADS * HEAD_DIM
    order = jnp.argsort(jnp.logical_not(needed), axis=1, stable=True).astype(jnp.int32)
    count = jnp.sum(needed, axis=1).astype(jnp.int32)
    slot = jnp.minimum(jnp.arange(pages, dtype=jnp.int32)[None, :], count[:, None] - 1)
    logical = jnp.take_along_axis(order, slot, axis=1)
    physical = jnp.take_along_axis(page_table, logical, axis=1)
    row_spec = pl.BlockSpec((tq, dm), lambda b, s, *_: (b, 0))
    assert past_len % SEL_BLOCK + tq <= SEL_BLOCK
    cols = GQA_GROUP * tq
    q_t = q.reshape(n_seq, tq, N_KV_HEADS, GQA_GROUP, HEAD_DIM).transpose(0, 2, 4, 3, 1)
    q_t = q_t.reshape(n_seq, N_KV_HEADS, HEAD_DIM, cols)
    bias_t = jnp.tile(bias.transpose(0, 1, 3, 2), (1, 1, 1, GQA_GROUP)).astype(F32)

    def page_map(p):
        return lambda b, s, phys, *_: (layer, phys[b, s * n_pages_step + p], 0, 0)

    in_specs = [
        pl.BlockSpec((None, N_KV_HEADS, HEAD_DIM, cols), lambda b, s, *_: (b, 0, 0, 0)),
        pl.BlockSpec((None, N_KV_HEADS, nbp, cols), lambda b, s, *_: (b, 0, 0, 0)),
        pl.BlockSpec((tq * 2 * N_KV_HEADS, HEAD_DIM), lambda b, s, *_: (b, 0)),
    ]
    in_specs += [pl.BlockSpec((None, None, page_rows, HEAD_DIM), page_map(p)) for p in range(n_pages_step)]
    in_specs += [
        pl.BlockSpec((None, N_KV_HEADS, tq, GQA_GROUP), lambda b, s, *_: (1, 0, b, 0)),
        row_spec,
    ]
    kern = functools.partial(_sel_sample_kernel, n_pages=n_pages_step, nbp=nbp, past_len=past_len, nsteps=nsteps,
                             tq=tq)
    grid_spec = pltpu.PrefetchScalarGridSpec(
        num_scalar_prefetch=3,
        grid=(n_seq, nsteps),
        in_specs=in_specs,
        out_specs=row_spec,
        scratch_shapes=[
            pltpu.VMEM((N_KV_HEADS, 1, cols), F32),
            pltpu.VMEM((N_KV_HEADS, cols, 2 * HEAD_DIM), F32),
        ],
    )
    return pl.pallas_call(
        kern,
        grid_spec=grid_spec,
        out_shape=jax.ShapeDtypeStruct((n_seq * tq, dm), F32),
        compiler_params=_params("parallel", "arbitrary"),
        name="nsa_sel_attention_sample",
    )(physical, logical, count, q_t, bias_t, kv_new, *([cache] * n_pages_step), gates_t, prev)


def _win_sample_kernel(q_ref, buf_ref, new_ref, gate_ref, prev_ref, o_ref, *, tq, out_dtype):
    rows_per_tok = 2 * N_KV_HEADS
    w_buf = buf_ref.shape[0] // rows_per_tok
    gw = GQA_GROUP * HEAD_DIM
    for g in range(N_KV_HEADS):
        q4 = _group_rows(q_ref[:, g * gw:(g + 1) * gw])
        kb = buf_ref[pl.ds(g, w_buf, stride=rows_per_tok), :].astype(BF16)
        vb = buf_ref[pl.ds(N_KV_HEADS + g, w_buf, stride=rows_per_tok), :].astype(BF16)
        kn = new_ref[pl.ds(g, tq, stride=rows_per_tok), :].astype(BF16)
        vn = new_ref[pl.ds(N_KV_HEADS + g, tq, stride=rows_per_tok), :].astype(BF16)
        sb = lax.dot_general(q4, kb, _NT, preferred_element_type=F32)
        sn = lax.dot_general(q4, kn, _NT, preferred_element_type=F32)
        tb = jnp.bitwise_and(lax.broadcasted_iota(jnp.int32, sb.shape, 0), tq - 1)
        rb = lax.broadcasted_iota(jnp.int32, sb.shape, 1)
        sb = jnp.where(tb + w_buf - rb < WINDOW, sb, NEG_INF)
        tn = jnp.bitwise_and(lax.broadcasted_iota(jnp.int32, sn.shape, 0), tq - 1)
        rn = lax.broadcasted_iota(jnp.int32, sn.shape, 1)
        sn = jnp.where(rn <= tn, sn, NEG_INF)
        m = jnp.maximum(jnp.max(sb, axis=1, keepdims=True), jnp.max(sn, axis=1, keepdims=True))
        pb = jnp.exp2(sb - m)
        pn = jnp.exp2(sn - m)
        l = jnp.sum(pb, axis=1, keepdims=True) + jnp.sum(pn, axis=1, keepdims=True)
        o4 = (jnp.dot(pb.astype(BF16), vb, preferred_element_type=F32)
              + jnp.dot(pn.astype(BF16), vn, preferred_element_type=F32)) / l
        for h in range(GQA_GROUP):
            hh = g * GQA_GROUP + h
            sl = slice(hh * HEAD_DIM, (hh + 1) * HEAD_DIM)
            o_ref[:, sl] = (prev_ref[:, sl] + o4[h * tq:(h + 1) * tq] * gate_ref[g, :, h:h + 1]).astype(out_dtype)


def win_attention_sample(q, win_buf, layer, kv_new, gates_t, prev, *, out_dtype):
    n_seq, buf_rows = win_buf.shape[1], win_buf.shape[2]
    tq = SAMPLE_PAD
    q_spec = pl.BlockSpec((tq, N_HEADS * HEAD_DIM), lambda b: (b, 0))
    kern = functools.partial(_win_sample_kernel, tq=tq, out_dtype=out_dtype)
    return pl.pallas_call(
        kern,
        grid=(n_seq,),
        in_specs=[
            q_spec,
            pl.BlockSpec((None, None, buf_rows, HEAD_DIM), lambda b: (layer, b, 0, 0)),
            pl.BlockSpec((tq * 2 * N_KV_HEADS, HEAD_DIM), lambda b: (b, 0)),
            pl.BlockSpec((None, N_KV_HEADS, tq, GQA_GROUP), lambda b: (2, 0, b, 0)),
            q_spec,
        ],
        out_specs=q_spec,
        out_shape=jax.ShapeDtypeStruct((n_seq * tq, N_HEADS * HEAD_DIM), out_dtype),
        compiler_params=_params("parallel"),
        name="nsa_win_attention_sample",
    )(q, win_buf, kv_new, gates_t, prev)


def _ret_kernel(q_ref, k_ref, v_ref, g_ref, gn_ref, dec_ref, qd_ref, kd_ref, gc_ref, s0_ref, o_ref, sout_ref,
                state_ref, *, n_chunks):
    c = pl.program_id(2)

    @pl.when(c == 0)
    def _():
        state_ref[...] = s0_ref[...]

    for h in range(state_ref.shape[0]):
        ksl = slice(h * RET_KDIM, (h + 1) * RET_KDIM)
        vsl = slice(h * RET_VDIM, (h + 1) * RET_VDIM)
        q = q_ref[:, ksl]
        kf = k_ref[:, ksl]
        v = v_ref[:, vsl]
        state = state_ref[h]
        s = lax.dot_general(q, kf.astype(BF16), _NT, preferred_element_type=F32) * dec_ref[h]
        inner = jnp.dot(s.astype(BF16), v, preferred_element_type=F32)
        cross = jnp.dot(q, state.astype(BF16), preferred_element_type=F32) * qd_ref[h]
        o = inner + cross
        kd = (kf * kd_ref[h]).astype(BF16)
        state_ref[h] = gc_ref[h] * state + lax.dot_general(kd, v, _TN, preferred_element_type=F32)
        mu = jnp.mean(o, axis=-1, keepdims=True)
        d = o - mu
        var = jnp.mean(d * d, axis=-1, keepdims=True)
        on = d * lax.rsqrt(var + GN_EPS) * gn_ref[:, vsl]
        g = g_ref[:, vsl]
        o_ref[:, vsl] = (g * jax.nn.sigmoid(g) * on).astype(o_ref.dtype)

    @pl.when(c == n_chunks - 1)
    def _():
        sout_ref[...] = state_ref[...]


def _ret_tables(chunk, n_tok):
    log_g = jnp.log(1.0 - 2.0 ** (-5.0 - jnp.arange(RET_HEADS, dtype=F32)))
    i = jnp.arange(chunk, dtype=F32)
    rel = i[:, None] - i[None, :]
    decay = jnp.where(rel >= 0, jnp.exp(jnp.maximum(rel, 0.0)[None] * log_g[:, None, None]), 0.0)
    q_dec = jnp.exp((i[None, :] + 1.0) * log_g[:, None])[..., None]
    k_dec = jnp.where(i[None, :] < n_tok, jnp.exp((n_tok - 1.0 - i)[None, :] * log_g[:, None]), 0.0)[..., None]
    g_c = jnp.exp(n_tok * log_g)[:, None, None]
    return decay, q_dec, k_dec, g_c


def retention(q, k, v, g, gn, layer, state0, *, n_seq, seq, chunk, n_tok):
    nc = seq // chunk
    hps = RET_HEADS_PER_STEP
    decay, q_dec, k_dec, g_c = _ret_tables(chunk, n_tok)
    kern = functools.partial(_ret_kernel, n_chunks=nc)
    tok_k = pl.BlockSpec((chunk, hps * RET_KDIM), lambda b, h, c: (b * nc + c, h))
    tok_v = pl.BlockSpec((chunk, hps * RET_VDIM), lambda b, h, c: (b * nc + c, h))
    st_spec = pl.BlockSpec((None, hps, RET_KDIM, RET_VDIM), lambda b, h, c: (b, h, 0, 0))
    return pl.pallas_call(
        kern,
        grid=(n_seq, RET_HEADS // hps, nc),
        in_specs=[
            tok_k, tok_k, tok_v, tok_v,
            pl.BlockSpec((None, 1, hps * RET_VDIM), lambda b, h, c: (layer, 0, h)),
            pl.BlockSpec((hps, chunk, chunk), lambda b, h, c: (h, 0, 0)),
            pl.BlockSpec((hps, chunk, 1), lambda b, h, c: (h, 0, 0)),
            pl.BlockSpec((hps, chunk, 1), lambda b, h, c: (h, 0, 0)),
            pl.BlockSpec((hps, 1, 1), lambda b, h, c: (h, 0, 0)),
            st_spec,
        ],
        out_specs=[tok_v, st_spec],
        out_shape=[
            jax.ShapeDtypeStruct((n_seq * seq, RET_HEADS * RET_VDIM), BF16),
            jax.ShapeDtypeStruct((n_seq, RET_HEADS, RET_KDIM, RET_VDIM), F32),
        ],
        scratch_shapes=[pltpu.VMEM((hps, RET_KDIM, RET_VDIM), F32)],
        compiler_params=_params("parallel", "parallel", "arbitrary"),
        name="retention_chunk",
    )(q, k, v, g, gn, decay, q_dec, k_dec, g_c, state0)


def _project_nsa(xn, w_in, layer, cos, sin_signed, *, tm):
    m, d = xn.shape
    nq = N_HEADS * HEAD_DIM
    half = N_KV_HEADS * HEAD_DIM
    rows_per_tok = 2 * N_KV_HEADS
    tabs_per_seq = cos.shape[0] // tm
    tab_spec = pl.BlockSpec((tm, HEAD_DIM), lambda i, j, k: (i % tabs_per_seq, 0))
    tn_q = 512
    q = matmul(
        [xn], w_in, layer, col_block0=0, n_cols=nq, tm=tm, tn=tn_q, tk=d,
        extras=[cos, sin_signed], extra_specs=[tab_spec, tab_spec],
        out_shapes=[jax.ShapeDtypeStruct((m, nq), BF16), jax.ShapeDtypeStruct((nq, m), BF16)],
        out_specs=[pl.BlockSpec((tm, tn_q), lambda i, j, k: (i, j)), pl.BlockSpec((tn_q, tm), lambda i, j, k: (j, i))],
        epilogue=functools.partial(_ep_rope_q, scale=HEAD_DIM ** -0.5 * LOG2_E), name="nsa_proj_q")
    kv = matmul(
        [xn], w_in, layer, col_block0=nq // half, n_cols=3 * KV_ROW, tm=tm, tn=half, tk=d,
        extras=[cos, sin_signed], extra_specs=[tab_spec, tab_spec],
        out_shapes=[jax.ShapeDtypeStruct((m * rows_per_tok, HEAD_DIM), F32)] * 3
        + [jax.ShapeDtypeStruct((3, N_KV_HEADS, m, HEAD_DIM), BF16),
           jax.ShapeDtypeStruct((3, N_KV_HEADS, HEAD_DIM, m), BF16)],
        out_specs=[pl.BlockSpec((tm * rows_per_tok, HEAD_DIM), lambda i, j, k: (i, 0))] * 3
        + [pl.BlockSpec((None, N_KV_HEADS, tm, HEAD_DIM), lambda i, j, k: (j // 2, 0, i, 0)),
           pl.BlockSpec((None, N_KV_HEADS, HEAD_DIM, tm), lambda i, j, k: (j // 2, 0, 0, i))],
        epilogue=_ep_nsa_kv, name="nsa_proj_kv", column_steps_ordered=True)
    kv = (kv[:3], kv[3], kv[4])
    n_gate = 3 * N_HEADS
    w_gate = jnp.pad(w_in[layer, :, nq + 3 * KV_ROW:], ((0, 0), (0, LANES - n_gate)))[None]
    gates_t = matmul(
        [xn], w_gate, 0, col_block0=0, n_cols=LANES, tm=tm, tn=LANES, tk=d, extras=[], extra_specs=[],
        out_shapes=[jax.ShapeDtypeStruct((3, N_KV_HEADS, m, GQA_GROUP), F32)],
        out_specs=[pl.BlockSpec((3, N_KV_HEADS, tm, GQA_GROUP), lambda i, j, k: (0, 0, i, 0))],
        epilogue=_ep_gates, name="nsa_proj_gate")[0]
    return q, kv, gates_t


def _cmp_bias(w1, pos, b1, layer):
    k_full = CMP_BLOCK * HEAD_DIM
    pos_rows = jnp.zeros((2, SUBLANES, k_full), F32).at[:, 0].set(pos[layer].reshape(2, k_full))
    terms = []
    for e in range(2):
        t = matmul(
            [pos_rows[e]], w1.reshape(-1, k_full, HEAD_DIM), layer * 2 + e, col_block0=0, n_cols=HEAD_DIM,
            tm=SUBLANES, tn=HEAD_DIM, tk=k_full, extras=[], extra_specs=[],
            out_shapes=[jax.ShapeDtypeStruct((SUBLANES, HEAD_DIM), F32)],
            out_specs=[pl.BlockSpec((SUBLANES, HEAD_DIM), lambda i, j, k: (0, 0))],
            epilogue=_ep_cast, name="nsa_cmp_pos_term")[0]
        terms.append(t[0])
    return jnp.stack(terms) + b1[layer]


def _mlp_up(xn, w1, layer, *, tm):
    m, d = xn.shape
    d_ff = w1.shape[-1]
    tn = 1024
    return matmul(
        [xn], w1, layer, col_block0=0, n_cols=d_ff, tm=tm, tn=tn, tk=d, extras=[], extra_specs=[],
        out_shapes=[jax.ShapeDtypeStruct((m, d_ff), BF16)],
        out_specs=[pl.BlockSpec((tm, tn), lambda i, j, k: (i, j))],
        epilogue=_ep_relu2, name="mlp_up")[0]


def _mixer_out_proj(x, w, layer, h, norm_g, *, name):
    m, d = h.shape
    kdim = x.shape[1]
    w_bytes = 2 * kdim * d * w.dtype.itemsize
    per_row = 2 * kdim * x.dtype.itemsize + d * (4 * 4 + 2 * 2 + 4)
    tm = min(m, (VMEM_LIMIT_BYTES * 7 // 8 - w_bytes) // per_row // LANES * LANES)
    while m % tm:
        tm -= LANES
    row = pl.BlockSpec((tm, d), lambda i, j, k: (i, 0))
    return matmul(
        [x], w, layer, col_block0=0, n_cols=d, tm=tm, tn=d, tk=kdim,
        extras=[h, norm_g.reshape(1, d)], extra_specs=[row, pl.BlockSpec((1, d), lambda i, j, k: (0, 0))],
        out_shapes=[jax.ShapeDtypeStruct((m, d), F32), jax.ShapeDtypeStruct((m, d), BF16)], out_specs=[row, row],
        epilogue=_ep_residual_norm, name=name)


def _mlp_down(a, w2, layer, h, *, tm):
    m, d = h.shape
    tn = 1024
    tm = min(tm, 1024)
    spec = pl.BlockSpec((tm, tn), lambda i, j, k: (i, j))
    return matmul(
        [a], w2, layer, col_block0=0, n_cols=d, tm=tm, tn=tn, tk=2048, extras=[h], extra_specs=[spec],
        out_shapes=[jax.ShapeDtypeStruct((m, d), F32)], out_specs=[spec], epilogue=_ep_residual_norm,
        name="mlp_down")[0]


def _nsa_tables(pos):
    cos, sin = _rope_tables(pos, HEAD_DIM // 2)
    return jnp.concatenate([cos, cos], axis=1), jnp.concatenate([-sin, sin], axis=1)


def _nsa_prompt(xn, layer, w_in, cmp_w1, cmp_bias, cmp_w2, *, n_seq, seq):
    tm = PROMPT_ROW_TILE
    cos, sin_signed = _nsa_tables(jnp.arange(seq))
    (q, qt), (kv_rows, k_bf, vt_bf), gates_t = _project_nsa(xn, w_in, layer, cos, sin_signed, tm=tm // 2)
    n_chunks = seq // CMP_STRIDE
    chunks_per_step = 128
    kv_chunks = kv_rows[0].reshape(1, (n_seq * n_chunks) // chunks_per_step, chunks_per_step, CMP_STRIDE,
                                   2 * N_KV_HEADS, HEAD_DIM)
    kc = compress(kv_chunks, 0, None, cmp_w1, cmp_bias, cmp_w2, n_seq=n_seq, n_chunks=n_chunks, n_pages=1)
    n_blk = seq // SEL_BLOCK
    cover_t = _cover_t(n_chunks, n_blk, n_blk)
    o1, val = cmp_attention(q, kc, cover_t, gates_t, n_seq=n_seq, seq=seq, tq=512, pos0=0)
    bias_t = select_bias(val, n_blk=n_blk, tc=1024, causal_seq=seq)
    nbp = pl.cdiv(n_blk, LANES) * LANES
    o2 = flash_attention(qt, k_bf, vt_bf, 1, gates_t, o1, n_seq=n_seq, seq=seq, tq=512, tk=512, bias=bias_t,
                         onehot=_block_onehot(seq, nbp))
    o3 = flash_attention(qt, k_bf, vt_bf, 2, gates_t, o2, n_seq=n_seq, seq=seq, tq=256, tk=256, window=WINDOW,
                         out_dtype=BF16)
    return o3, kv_rows


def _nsa_sample(xn, layer, w_in, cmp_w1, cmp_bias, cmp_w2, cache_cmp, cache_sel, win_buf, page_table,
                *, n_seq, n_tok, past_len):
    tq = SAMPLE_PAD
    m = n_seq * tq
    cos, sin_signed = _nsa_tables(jnp.tile(past_len + jnp.arange(tq), n_seq))
    (q, _), (kv_rows, _, _), gates_t = _project_nsa(xn, w_in, layer, cos, sin_signed, tm=m)
    n_chunks = past_len // CMP_STRIDE
    kc = compress(cache_cmp, layer, page_table, cmp_w1, cmp_bias, cmp_w2, n_seq=n_seq, n_chunks=n_chunks,
                  n_pages=16)
    n_blk = pl.cdiv(past_len + n_tok, SEL_BLOCK)
    nbv = pl.cdiv(n_blk, SUBLANES) * SUBLANES
    cover_t = _cover_t(n_chunks, n_blk, nbv)
    o1, val = cmp_attention(q, kc, cover_t, gates_t, n_seq=n_seq, seq=tq, tq=tq, pos0=past_len)
    ncols = n_seq * N_KV_HEADS * n_tok
    bias_t = select_bias(val[..., :n_tok].transpose(2, 0, 1, 3).reshape(nbv, ncols), n_blk=n_blk, tc=ncols)
    blocks_per_page = cache_sel.shape[2] // (2 * N_KV_HEADS * SEL_BLOCK)
    n_pages = page_table.shape[1]
    chosen = (bias_t == 0).reshape(nbv, n_seq, N_KV_HEADS * n_tok).any(axis=2)
    needed = chosen[:n_pages * blocks_per_page].reshape(n_pages, blocks_per_page, n_seq).any(axis=1).T
    nbp = pl.cdiv(n_blk + 1, LANES) * LANES
    bias = bias_t.reshape(nbv, n_seq, N_KV_HEADS, n_tok).transpose(1, 2, 3, 0)
    bias = jnp.pad(bias, ((0, 0), (0, 0), (0, 0), (0, nbp - nbv)), constant_values=NEG_INF)
    bias = jnp.pad(bias, ((0, 0), (0, 0), (0, tq - n_tok), (0, 0)))
    o2 = sel_attention_sample(q, bias, needed, kv_rows[1], cache_sel, layer, page_table, gates_t, o1,
                              n_pages_step=16, past_len=past_len)
    o3 = win_attention_sample(q, win_buf, layer, kv_rows[2], gates_t, o2, out_dtype=BF16)
    return o3, kv_rows


def _ret_layer(xn, layer, w_in, gn, state0, pos, *, n_seq, seq, chunk, n_tok, tm):
    m, d = xn.shape
    nk = RET_HEADS * RET_KDIM
    nv = RET_HEADS * RET_VDIM
    cos, sin = _rope_tables(pos, RET_KDIM // 2)
    tabs_per_seq = cos.shape[0] // tm
    tab_spec = pl.BlockSpec((tm, LANES), lambda i, j, k: (i % tabs_per_seq, 0))
    tn = 512

    def proj(col0, n_cols, dtype, epilogue, extras, extra_specs, name):
        return matmul(
            [xn], w_in, layer, col_block0=col0 // tn, n_cols=n_cols, tm=tm, tn=tn, tk=d,
            extras=extras, extra_specs=extra_specs,
            out_shapes=[jax.ShapeDtypeStruct((m, n_cols), dtype)],
            out_specs=[pl.BlockSpec((tm, tn), lambda i, j, k: (i, j))],
            epilogue=epilogue, name=name)[0]

    q = proj(0, nk, BF16, functools.partial(_ep_rope_ret, scale=1.0), [cos, sin], [tab_spec, tab_spec], "ret_proj_q")
    k = proj(nk, nk, F32, functools.partial(_ep_rope_ret, scale=RET_KDIM ** -0.5), [cos, sin],
             [tab_spec, tab_spec], "ret_proj_k")
    v = proj(2 * nk, nv, BF16, _ep_cast, [], [], "ret_proj_v")
    g = proj(2 * nk + nv, nv, F32, _ep_cast, [], [], "ret_proj_g")
    return retention(q, k, v, g, gn.reshape(gn.shape[0], 1, nv), layer, state0, n_seq=n_seq, seq=seq, chunk=chunk,
                     n_tok=n_tok)


def kernel(x_prompt, x_sample, cache_cmp_kv, cache_sel_kv, state_win_kv, state_ret, page_table, norm_mix, norm_ffn,
           norm_final, nsa_w_in, nsa_w_out, nsa_cmp_pos, nsa_cmp_w1, nsa_cmp_b1, nsa_cmp_w2, ret_w_in, ret_gn,
           ret_w_out, ffn_w1, ffn_w2):
    n_seq_p, seq_p, d = x_prompt.shape
    n_seq_s, n_tok_s, _ = x_sample.shape
    depth = norm_mix.shape[0]
    n_nsa = cache_sel_kv.shape[0]
    pool, page_rows = cache_sel_kv.shape[1], cache_sel_kv.shape[2]
    past_len = page_table.shape[1] * page_rows
    assert past_len % CMP_STRIDE == 0 and n_tok_s < CMP_STRIDE and n_tok_s <= SAMPLE_PAD
    assert state_win_kv.shape[2] == WINDOW and seq_p >= WINDOW

    hp = x_prompt.reshape(n_seq_p * seq_p, d)
    hs = jnp.pad(x_sample, ((0, 0), (0, SAMPLE_PAD - n_tok_s), (0, 0))).reshape(n_seq_s * SAMPLE_PAD, d)
    m_s = n_seq_s * SAMPLE_PAD
    rows_per_tok = 2 * N_KV_HEADS
    cache_cmp = cache_cmp_kv.reshape(n_nsa, pool, page_rows // CMP_STRIDE, CMP_STRIDE, rows_per_tok, HEAD_DIM)
    cache_sel = cache_sel_kv.reshape(n_nsa, pool, page_rows * rows_per_tok, HEAD_DIM)
    win_buf = state_win_kv.reshape(n_nsa, n_seq_s, WINDOW * rows_per_tok, HEAD_DIM)
    cmp_w1 = nsa_cmp_w1.reshape(n_nsa, 2, CMP_BLOCK * HEAD_DIM, HEAD_DIM)
    kv_shape = (2, N_KV_HEADS, HEAD_DIM)
    ffn_w2 = ffn_w2.astype(BF16)
    ret_w_out = ret_w_out.astype(BF16)
    nsa_w_out = nsa_w_out.astype(BF16)

    cmp_p, cmp_s, sel_p, sel_s, win_p, win_s, ret_p, ret_s = [], [], [], [], [], [], [], []
    xp = rmsnorm(hp, norm_mix[0], BF16, 256)
    xs = rmsnorm(hs, norm_mix[0], BF16, m_s)
    for layer in range(depth):
        if layer % 2 == 0:
            a = layer // 2
            cmp_bias = _cmp_bias(cmp_w1, nsa_cmp_pos, nsa_cmp_b1, a)
            op, kv_p = _nsa_prompt(xp, a, nsa_w_in, cmp_w1, cmp_bias, nsa_cmp_w2, n_seq=n_seq_p, seq=seq_p)
            os_, kv_s = _nsa_sample(xs, a, nsa_w_in, cmp_w1, cmp_bias, nsa_cmp_w2, cache_cmp, cache_sel, win_buf,
                                    page_table, n_seq=n_seq_s, n_tok=n_tok_s, past_len=past_len)
            w_out, w_idx = nsa_w_out, a
            kv_p = [r.reshape((n_seq_p, seq_p) + kv_shape) for r in kv_p]
            kv_s = [r.reshape((n_seq_s, SAMPLE_PAD) + kv_shape)[:, :n_tok_s] for r in kv_s]
            cmp_p.append(kv_p[0])
            sel_p.append(kv_p[1])
            win_p.append(kv_p[2][:, seq_p - WINDOW:])
            cmp_s.append(kv_s[0])
            sel_s.append(kv_s[1])
            win_s.append(jnp.concatenate([state_win_kv[a][:, n_tok_s:], kv_s[2]], axis=1))
        else:
            r = layer // 2
            zeros = jnp.zeros((n_seq_p, RET_HEADS, RET_KDIM, RET_VDIM), F32)
            op, st_p = _ret_layer(xp, r, ret_w_in, ret_gn, zeros, jnp.arange(seq_p), n_seq=n_seq_p, seq=seq_p,
                                  chunk=RET_CHUNK, n_tok=RET_CHUNK, tm=PROMPT_ROW_TILE)
            os_, st_s = _ret_layer(xs, r, ret_w_in, ret_gn, state_ret[r],
                                   jnp.tile(past_len + jnp.arange(SAMPLE_PAD), n_seq_s), n_seq=n_seq_s,
                                   seq=SAMPLE_PAD, chunk=SAMPLE_PAD, n_tok=n_tok_s, tm=m_s)
            ret_p.append(st_p)
            ret_s.append(st_s)
            w_out, w_idx = ret_w_out, r
        hp, xp = _mixer_out_proj(op, w_out, w_idx, hp, norm_ffn[layer], name="mixer_out_proj")
        hs, xs = _mixer_out_proj(os_, w_out, w_idx, hs, norm_ffn[layer], name="mixer_out_proj_sample")
        hp = _mlp_down(_mlp_up(xp, ffn_w1, layer, tm=PROMPT_ROW_TILE), ffn_w2, layer, hp, tm=PROMPT_ROW_TILE)
        hs = _mlp_down(_mlp_up(xs, ffn_w1, layer, tm=m_s), ffn_w2, layer, hs, tm=m_s)
        if layer + 1 < depth:
            xp = rmsnorm(hp, norm_mix[layer + 1], BF16, 256)
            xs = rmsnorm(hs, norm_mix[layer + 1], BF16, m_s)
    y_prompt = rmsnorm(hp, norm_final, F32, 256).reshape(n_seq_p, seq_p, d)
    y_sample = rmsnorm(hs, norm_final, F32, m_s).reshape(n_seq_s, SAMPLE_PAD, d)[:, :n_tok_s]
    return (y_prompt, y_sample, jnp.stack(cmp_p), jnp.stack(cmp_s), jnp.stack(sel_p), jnp.stack(sel_s),
            jnp.stack(win_p), jnp.stack(win_s), jnp.stack(ret_p), jnp.stack(ret_s))
```

```python
import functools

import numpy as np
import jax
import jax.numpy as jnp
from jax import lax
from jax.experimental import pallas as pl
from jax.experimental.pallas import tpu as pltpu

F32 = jnp.float32
BF16 = jnp.bfloat16

N_HEADS = 16
HEAD_DIM = 128
N_KV_HEADS = 4
GQA_GROUP = N_HEADS // N_KV_HEADS
KV_ROW = 2 * N_KV_HEADS * HEAD_DIM
CMP_BLOCK = 32
CMP_STRIDE = 16
SEL_BLOCK = 64
N_SEL = 16
N_LOCAL = 2
WINDOW = 512
ROPE_THETA = 10000.0
RET_HEADS = 8
RET_KDIM = 256
RET_VDIM = 512
RET_CHUNK = 128
RET_HEADS_PER_STEP = 8
RMS_EPS = 1e-6
GN_EPS = 1e-5
NEG_INF = -1e30
FORCE_SCORE = 1e9
LOG2_E = 1.4426950408889634

SUBLANES = 8
LANES = 128
VMEM_LIMIT_BYTES = 56 * 1024 * 1024
SAMPLE_PAD = 16
PROMPT_ROW_TILE = 2048

_NT = (((1,), (1,)), ((), ()))
_TN = (((0,), (0,)), ((), ()))


def _params(*semantics):
    return pltpu.CompilerParams(dimension_semantics=semantics, vmem_limit_bytes=VMEM_LIMIT_BYTES)


def _rmsnorm_kernel(x_ref, g_ref, o_ref):
    x = x_ref[...]
    ms = jnp.mean(x * x, axis=-1, keepdims=True)
    o_ref[...] = (x * lax.rsqrt(ms + RMS_EPS) * g_ref[...]).astype(o_ref.dtype)


def rmsnorm(x, g, out_dtype, tm, row_block0=0, n_blocks=None):
    m, d = x.shape
    if n_blocks is None:
        n_blocks = m // tm
    return pl.pallas_call(
        _rmsnorm_kernel,
        grid=(n_blocks,),
        in_specs=[pl.BlockSpec((tm, d), lambda i: (i + row_block0, 0)), pl.BlockSpec((1, d), lambda i: (0, 0))],
        out_specs=pl.BlockSpec((tm, d), lambda i: (i, 0)),
        out_shape=jax.ShapeDtypeStruct((n_blocks * tm, d), out_dtype),
        compiler_params=_params("parallel"),
        name="rmsnorm",
    )(x, g.reshape(1, d))


def _mm_kernel(*refs, n_x, n_extra, n_out, nk, epilogue):
    x_refs = refs[:n_x]
    w_ref = refs[n_x]
    extra = refs[n_x + 1:n_x + 1 + n_extra]
    outs = refs[n_x + 1 + n_extra:n_x + 1 + n_extra + n_out]
    x = x_refs[0][...]
    for r in x_refs[1:]:
        x = x + r[...]
    part = jnp.dot(x.astype(BF16), w_ref[...].astype(BF16), preferred_element_type=F32)
    if nk == 1:
        epilogue(part, extra, outs)
        return
    k = pl.program_id(2)
    if epilogue is _ep_residual_norm:
        @pl.when(k == 0)
        def _():
            outs[0][...] = extra[0][...] + part

        @pl.when(k > 0)
        def _():
            outs[0][...] += part

        @pl.when(k == nk - 1)
        def _():
            _write_norm(outs[0][...], extra, outs)
        return
    acc_ref = refs[-1]

    @pl.when(k == 0)
    def _():
        acc_ref[...] = part

    @pl.when(k > 0)
    def _():
        acc_ref[...] += part

    @pl.when(k == nk - 1)
    def _():
        epilogue(acc_ref[...], extra, outs)


def matmul(xs, w, layer, *, col_block0, n_cols, tm, tn, tk, extras, extra_specs, out_shapes, out_specs,
           epilogue, name, column_steps_ordered=False):
    m, kdim = xs[0].shape
    nk = kdim // tk
    grid = (m // tm, pl.cdiv(n_cols, tn), nk)
    in_specs = [pl.BlockSpec((tm, tk), lambda i, j, k: (i, k)) for _ in xs]
    in_specs.append(pl.BlockSpec((None, tk, tn), lambda i, j, k: (layer, k, j + col_block0)))
    in_specs.extend(extra_specs)
    scratch = [pltpu.VMEM((tm, tn), F32)] if nk > 1 and epilogue is not _ep_residual_norm else []
    kern = functools.partial(_mm_kernel, n_x=len(xs), n_extra=len(extras), n_out=len(out_shapes), nk=nk,
                             epilogue=epilogue)
    return pl.pallas_call(
        kern,
        grid=grid,
        in_specs=in_specs,
        out_specs=out_specs,
        out_shape=out_shapes,
        scratch_shapes=scratch,
        compiler_params=_params("parallel", "arbitrary" if column_steps_ordered else "parallel", "arbitrary"),
        name=name,
    )(*xs, w, *extras)


def _rope_half_lane(x, cos, sin_signed):
    return x * cos + pltpu.roll(x, HEAD_DIM // 2, 1) * sin_signed


def _ep_rope_q(part, extra, outs, *, scale):
    cos = extra[0][...]
    sin = extra[1][...]
    for h in range(part.shape[1] // HEAD_DIM):
        sl = slice(h * HEAD_DIM, (h + 1) * HEAD_DIM)
        r = _rope_half_lane(part[:, sl], cos, sin) * scale
        outs[0][:, sl] = r.astype(outs[0].dtype)
        outs[1][sl, :] = r.T.astype(outs[1].dtype)


def _ep_nsa_kv(part, extra, outs):
    tm = part.shape[0]
    rows_per_tok = 2 * N_KV_HEADS
    j = pl.program_id(1)

    def heads():
        return [part[:, h * HEAD_DIM:(h + 1) * HEAD_DIM] for h in range(N_KV_HEADS)]

    for br in range(3):
        @pl.when(j == 2 * br)
        def _(br=br):
            cos = extra[0][...]
            sin = extra[1][...]
            for h, x in enumerate(heads()):
                k = _rope_half_lane(x, cos, sin)
                outs[br][pl.ds(h, tm, stride=rows_per_tok), :] = k
                outs[3][h] = k.astype(BF16)

        @pl.when(j == 2 * br + 1)
        def _(br=br):
            for h, v in enumerate(heads()):
                outs[br][pl.ds(N_KV_HEADS + h, tm, stride=rows_per_tok), :] = v
                outs[4][h] = v.T.astype(BF16)


def _ep_gates(part, extra, outs):
    g = jax.nn.sigmoid(part)
    for br in range(3):
        for kvh in range(N_KV_HEADS):
            c = br * N_HEADS + kvh * GQA_GROUP
            outs[0][br, kvh] = g[:, c:c + GQA_GROUP]


def _write_norm(h, extra, outs):
    if len(extra) < 2:
        return
    ms = jnp.mean(h * h, axis=-1, keepdims=True)
    y = h * lax.rsqrt(ms + RMS_EPS) * extra[1][...]
    outs[-1][...] = y.astype(outs[-1].dtype)


def _ep_residual_norm(part, extra, outs):
    h = extra[0][...] + part
    outs[0][...] = h
    _write_norm(h, extra, outs)


def _ep_relu2(part, extra, outs):
    h = jnp.maximum(part, 0.0)
    outs[0][...] = (h * h).astype(outs[0].dtype)


def _ep_cast(part, extra, outs):
    outs[0][...] = part.astype(outs[0].dtype)


def _ep_rope_ret(part, extra, outs, *, scale):
    cos = extra[0][...]
    sin = extra[1][...]
    for h in range(part.shape[1] // RET_KDIM):
        a = slice(h * RET_KDIM, h * RET_KDIM + LANES)
        b = slice(h * RET_KDIM + LANES, (h + 1) * RET_KDIM)
        x1 = part[:, a]
        x2 = part[:, b]
        outs[0][:, a] = ((x1 * cos - x2 * sin) * scale).astype(outs[0].dtype)
        outs[0][:, b] = ((x2 * cos + x1 * sin) * scale).astype(outs[0].dtype)


def _rope_tables(pos, half):
    inv = ROPE_THETA ** (-jnp.arange(half, dtype=F32) / half)
    ang = pos.astype(F32)[:, None] * inv[None, :]
    return jnp.cos(ang), jnp.sin(ang)


def _compress_kernel(*refs, n_pages, paged):
    if paged:
        refs = refs[1:]
    page_refs = refs[:n_pages]
    next_ref, w1_ref, bias_ref, w2_ref, out_ref, shift_ref, y_ref = refs[n_pages:]
    nc = page_refs[0].shape[0]
    n_heads = N_KV_HEADS
    n_main = n_pages * nc * n_heads
    k_half = CMP_STRIDE * HEAD_DIM
    tiles = [[page_refs[p][:, s] for s in range(CMP_STRIDE)] for p in range(n_pages)]
    next_tiles = [next_ref[0, s] for s in range(CMP_STRIDE)]
    for e in range(2):
        hsl = slice(e * n_heads, (e + 1) * n_heads)
        blocks = [
            jnp.concatenate([tiles[p][s][:, hsl, :].reshape(nc * n_heads, HEAD_DIM) for s in range(CMP_STRIDE)],
                            axis=1)
            for p in range(n_pages)]
        blocks.append(jnp.concatenate([next_tiles[s][hsl, :] for s in range(CMP_STRIDE)], axis=1))
        blocks.append(jnp.zeros((SUBLANES - n_heads, CMP_STRIDE * HEAD_DIM), F32))
        x = jnp.concatenate(blocks, axis=0).astype(BF16)
        wcat = jnp.concatenate([w1_ref[e, :k_half], w1_ref[e, k_half:]], axis=1).astype(BF16)
        per = jnp.dot(x, wcat, preferred_element_type=F32)
        hid = per.shape[1] // 2
        shift_ref[...] = per[:, hid:]
        h = per[:n_main, :hid] + shift_ref[pl.ds(n_heads, n_main), :]
        h = jax.nn.gelu(h + bias_ref[e:e + 1, :])
        y_ref[...] = jnp.dot(h.astype(BF16), w2_ref[e].astype(BF16), preferred_element_type=F32)
        for g in range(n_heads):
            out_ref[e * n_heads + g] = y_ref[pl.ds(g, n_main // n_heads, stride=n_heads), :].astype(out_ref.dtype)


def compress(kv, layer, page_table, w1, bias, w2, *, n_seq, n_chunks, n_pages):
    paged = page_table is not None
    nc = kv.shape[2]
    n_main = n_pages * nc
    steps = n_chunks // n_main
    pages_per_seq = steps * n_pages
    rows_per_tok = 2 * N_KV_HEADS

    def nxt(s):
        return jnp.minimum((s + 1) * n_pages, pages_per_seq - 1)

    if paged:
        def page_map(p):
            return lambda b, s, pt: (layer, pt[b, s * n_pages + p], 0, 0, 0, 0)

        def next_map(b, s, pt):
            return (layer, pt[b, nxt(s)], 0, 0, 0, 0)

        def fixed(*idx):
            return lambda b, s, pt: idx

        out_map = lambda b, s, pt: (b, 0, s, 0)
    else:
        def page_map(p):
            return lambda b, s: (layer, b * pages_per_seq + s * n_pages + p, 0, 0, 0, 0)

        def next_map(b, s):
            return (layer, b * pages_per_seq + nxt(s), 0, 0, 0, 0)

        def fixed(*idx):
            return lambda b, s: idx

        out_map = lambda b, s: (b, 0, s, 0)

    tile = (CMP_STRIDE, rows_per_tok, HEAD_DIM)
    in_specs = [pl.BlockSpec((None, None, nc) + tile, page_map(p)) for p in range(n_pages)]
    in_specs += [
        pl.BlockSpec((None, None, 1) + tile, next_map),
        pl.BlockSpec((None, 2, CMP_BLOCK * HEAD_DIM, HEAD_DIM), fixed(layer, 0, 0, 0)),
        pl.BlockSpec((2, HEAD_DIM), fixed(0, 0)),
        pl.BlockSpec((None, 2, HEAD_DIM, HEAD_DIM), fixed(layer, 0, 0, 0)),
    ]
    kern = functools.partial(_compress_kernel, n_pages=n_pages, paged=paged)
    grid_spec = pltpu.PrefetchScalarGridSpec(
        num_scalar_prefetch=1 if paged else 0,
        grid=(n_seq, steps),
        in_specs=in_specs,
        out_specs=pl.BlockSpec((None, rows_per_tok, n_main, HEAD_DIM), out_map),
        scratch_shapes=[pltpu.VMEM((n_main * N_KV_HEADS + SUBLANES, HEAD_DIM), F32),
                        pltpu.VMEM((n_main * N_KV_HEADS, HEAD_DIM), F32)],
    )
    args = ([page_table] if paged else []) + [kv] * (n_pages + 1) + [w1, bias, w2]
    return pl.pallas_call(
        kern,
        grid_spec=grid_spec,
        out_shape=jax.ShapeDtypeStruct((n_seq, rows_per_tok, n_chunks, HEAD_DIM), BF16),
        compiler_params=_params("parallel", "arbitrary"),
        name="nsa_compress",
    )(*args)


def _group_rows(q):
    return jnp.concatenate([q[:, h * HEAD_DIM:(h + 1) * HEAD_DIM] for h in range(GQA_GROUP)], axis=0)


def _cmp_kernel(q_ref, k_ref, v_ref, cov_ref, gate_ref, o_ref, val_ref, *, tq, pos0):
    qi = pl.program_id(2)
    q4 = _group_rows(q_ref[...])
    s = lax.dot_general(q4, k_ref[...], _NT, preferred_element_type=F32)
    row = lax.broadcasted_iota(jnp.int32, s.shape, 0)
    col = lax.broadcasted_iota(jnp.int32, s.shape, 1)
    qpos = pos0 + qi * tq + jnp.bitwise_and(row, tq - 1)
    valid = col * CMP_STRIDE + (CMP_BLOCK - 1) <= qpos
    s = jnp.where(valid, s, NEG_INF)
    m = jnp.max(s, axis=1, keepdims=True)
    e = jnp.where(valid, jnp.exp2(s - m), 0.0)
    l = jnp.sum(e, axis=1, keepdims=True)
    p = e / jnp.where(l > 0.0, l, 1.0)
    o4 = jnp.dot(p.astype(BF16), v_ref[...], preferred_element_type=F32)
    for h in range(GQA_GROUP):
        o_ref[:, h * HEAD_DIM:(h + 1) * HEAD_DIM] = o4[h * tq:(h + 1) * tq] * gate_ref[:, h:h + 1]
    imp = p[0:tq] + p[tq:2 * tq] + p[2 * tq:3 * tq] + p[3 * tq:4 * tq]
    hi = imp.astype(BF16)
    r1 = imp - hi.astype(F32)
    mid = r1.astype(BF16)
    lo = (r1 - mid.astype(F32)).astype(BF16)
    cov = cov_ref[...]
    score = (lax.dot_general(cov, hi, _NT, preferred_element_type=F32)
             + lax.dot_general(cov, mid, _NT, preferred_element_type=F32)
             + lax.dot_general(cov, lo, _NT, preferred_element_type=F32))
    jb = lax.broadcasted_iota(jnp.int32, score.shape, 0)
    qp = pos0 + qi * tq + lax.broadcasted_iota(jnp.int32, score.shape, 1)
    cur = jnp.right_shift(qp, SEL_BLOCK.bit_length() - 1)
    visible = jb <= cur
    forced = (jb == 0) | (visible & (jb > cur - N_LOCAL))
    val_ref[...] = jnp.where(forced, FORCE_SCORE, jnp.where(visible, score, -FORCE_SCORE))


def cmp_attention(q, kc, cover_t, gates_t, *, n_seq, seq, tq, pos0):
    nq = seq // tq
    ncp = kc.shape[2]
    nbv = cover_t.shape[0]
    gw = GQA_GROUP * HEAD_DIM
    kern = functools.partial(_cmp_kernel, tq=tq, pos0=pos0)
    if tq % LANES == 0:
        val_spec = pl.BlockSpec((nbv, tq), lambda b, g, i: (0, (b * N_KV_HEADS + g) * nq + i))
        val_shape = jax.ShapeDtypeStruct((nbv, n_seq * N_KV_HEADS * seq), F32)
    else:
        val_spec = pl.BlockSpec((None, None, nbv, tq), lambda b, g, i: (b, g, 0, i))
        val_shape = jax.ShapeDtypeStruct((n_seq, N_KV_HEADS, nbv, seq), F32)
    return pl.pallas_call(
        kern,
        grid=(n_seq, N_KV_HEADS, nq),
        in_specs=[
            pl.BlockSpec((tq, gw), lambda b, g, i: (b * nq + i, g)),
            pl.BlockSpec((None, None, ncp, HEAD_DIM), lambda b, g, i: (b, g, 0, 0)),
            pl.BlockSpec((None, None, ncp, HEAD_DIM), lambda b, g, i: (b, N_KV_HEADS + g, 0, 0)),
            pl.BlockSpec((nbv, ncp), lambda b, g, i: (0, 0)),
            pl.BlockSpec((None, None, tq, GQA_GROUP), lambda b, g, i: (0, g, b * nq + i, 0)),
        ],
        out_specs=[pl.BlockSpec((tq, gw), lambda b, g, i: (b * nq + i, g)), val_spec],
        out_shape=[jax.ShapeDtypeStruct((n_seq * seq, N_HEADS * HEAD_DIM), F32), val_shape],
        compiler_params=_params("parallel", "parallel", "parallel"),
        name="nsa_cmp_attention",
    )(q, kc, kc, cover_t, gates_t)


def _cover_t(n_chunks, n_blk, nbv):
    n_cmp = n_chunks - CMP_BLOCK // CMP_STRIDE + 1
    c0 = np.arange(n_chunks) * CMP_STRIDE
    j0 = np.arange(nbv) * SEL_BLOCK
    cov = (c0[None, :] < j0[:, None] + SEL_BLOCK) & (c0[None, :] + CMP_BLOCK > j0[:, None])
    cov &= (np.arange(n_chunks)[None, :] < n_cmp) & (np.arange(nbv)[:, None] < n_blk)
    return jnp.asarray(cov.astype(np.float32), dtype=BF16)


def _rank_kernel(val_ref, out_ref, *, n_blk, n_sel, causal_seq):
    val = val_ref[...]
    tc = val.shape[1]
    jb = lax.broadcasted_iota(jnp.int32, val.shape, 0)
    if causal_seq is None:
        n_iter = n_blk
    else:
        pos0 = (pl.program_id(0) % (causal_seq // tc)) * tc
        n_iter = jnp.minimum(n_blk, (pos0 + tc - 1) // SEL_BLOCK + 1)

    def body(i, rank):
        row = val_ref[pl.ds(i, 1), :]
        beats = (row > val) | ((row == val) & (i < jb))
        return rank + jnp.where(beats, 1, 0)

    rank = lax.fori_loop(0, n_iter, body, jnp.zeros(val.shape, jnp.int32))
    chosen = (rank < n_sel) & (val > -0.5 * FORCE_SCORE)
    out_ref[...] = jnp.where(chosen, 0.0, NEG_INF).astype(out_ref.dtype)


def select_bias(val, *, n_blk, tc, causal_seq=None):
    nbv, ncols = val.shape
    kern = functools.partial(_rank_kernel, n_blk=n_blk, n_sel=min(N_SEL, n_blk), causal_seq=causal_seq)
    return pl.pallas_call(
        kern,
        grid=(ncols // tc,),
        in_specs=[pl.BlockSpec((nbv, tc), lambda i: (0, i))],
        out_specs=pl.BlockSpec((nbv, tc), lambda i: (0, i)),
        out_shape=jax.ShapeDtypeStruct((nbv, ncols), BF16),
        compiler_params=_params("parallel"),
        name="nsa_select_rank",
    )(val)


_STEP_FIRST, _STEP_LAST, _STEP_EDGE = 1, 2, 4


def _flash_steps(seq, tq, tk, window):
    qi_l, kt_l, fl_l = [], [], []
    for qi in range(seq // tq):
        q0, q1 = qi * tq, qi * tq + tq - 1
        kts = [kt for kt in range(seq // tk)
               if kt * tk <= q1 and (window is None or q0 - (kt * tk + tk - 1) < window)]
        for n, kt in enumerate(kts):
            k0, k1 = kt * tk, kt * tk + tk - 1
            all_visible = q0 - k1 >= 0 and (window is None or q1 - k0 < window)
            flags = (_STEP_FIRST if n == 0 else 0) | (_STEP_LAST if n == len(kts) - 1 else 0)
            flags |= 0 if all_visible else _STEP_EDGE
            qi_l.append(qi)
            kt_l.append(kt)
            fl_l.append(flags)
    return (jnp.asarray(np.array(qi_l, np.int32)), jnp.asarray(np.array(kt_l, np.int32)),
            jnp.asarray(np.array(fl_l, np.int32)))


def _flash_kernel(*refs, tq, tk, use_bias, window, out_dtype):
    qi_ref, kt_ref, flag_ref = refs[:3]
    if use_bias:
        qt_ref, bias_ref, k_ref, vt_ref, e_ref, gate_ref, prev_ref, o_ref, qa_ref, m_ref, acc_ref = refs[3:]
    else:
        qt_ref, k_ref, vt_ref, gate_ref, prev_ref, o_ref, qa_ref, m_ref, acc_ref = refs[3:]
    s = pl.program_id(2)
    qi = qi_ref[s]
    kt = kt_ref[s]
    flags = flag_ref[s]
    first = jnp.bitwise_and(flags, _STEP_FIRST) != 0
    last = jnp.bitwise_and(flags, _STEP_LAST) != 0
    edge = jnp.bitwise_and(flags, _STEP_EDGE) != 0

    @pl.when(first)
    def _():
        for h in range(GQA_GROUP):
            qa_ref[0:HEAD_DIM, h * tq:(h + 1) * tq] = qt_ref[h * HEAD_DIM:(h + 1) * HEAD_DIM, :]
            if use_bias:
                nbv = bias_ref.shape[0]
                qa_ref[HEAD_DIM:HEAD_DIM + nbv, h * tq:(h + 1) * tq] = bias_ref[...]
                if HEAD_DIM + nbv < qa_ref.shape[0]:
                    qa_ref[HEAD_DIM + nbv:, h * tq:(h + 1) * tq] = jnp.zeros(
                        (qa_ref.shape[0] - HEAD_DIM - nbv, tq), BF16)
        m_ref[...] = jnp.full(m_ref.shape, NEG_INF, F32)
        acc_ref[...] = jnp.zeros(acc_ref.shape, F32)

    def step(masked):
        if use_bias:
            ka = jnp.concatenate([k_ref[...], e_ref[...]], axis=1)
        else:
            ka = k_ref[...]
        st = jnp.dot(ka, qa_ref[...], preferred_element_type=F32)
        if masked:
            kpos = kt * tk + lax.broadcasted_iota(jnp.int32, st.shape, 0)
            qpos = qi * tq + jnp.bitwise_and(lax.broadcasted_iota(jnp.int32, st.shape, 1), tq - 1)
            d = qpos - kpos
            ok = d >= 0
            if window is not None:
                ok = ok & (d < window)
            st = jnp.where(ok, st, NEG_INF)
        m_old = m_ref[...]
        m_new = jnp.maximum(m_old, jnp.max(st, axis=0, keepdims=True))
        alpha = jnp.exp2(m_old - m_new)
        p = jnp.exp2(st - m_new).astype(BF16)
        va = jnp.concatenate([vt_ref[...], jnp.ones((acc_ref.shape[0] - HEAD_DIM, tk), BF16)], axis=0)
        acc_ref[...] = alpha * acc_ref[...] + jnp.dot(va, p, preferred_element_type=F32)
        m_ref[...] = m_new

    @pl.when(edge)
    def _():
        step(True)

    @pl.when(jnp.logical_not(edge))
    def _():
        step(False)

    @pl.when(last)
    def _():
        acc = acc_ref[...]
        ot = acc[:HEAD_DIM] / acc[HEAD_DIM:HEAD_DIM + 1]
        for h in range(GQA_GROUP):
            sl = slice(h * HEAD_DIM, (h + 1) * HEAD_DIM)
            o_h = ot[:, h * tq:(h + 1) * tq].T
            o_ref[:, sl] = (prev_ref[:, sl] + o_h * gate_ref[:, h:h + 1]).astype(out_dtype)


def flash_attention(qt, k, vt, branch, gates_t, prev, *, n_seq, seq, tq, tk, bias=None, onehot=None, window=None,
                    out_dtype=F32):
    use_bias = bias is not None
    nq = seq // tq
    nkt = seq // tk
    gw = GQA_GROUP * HEAD_DIM
    qi_tab, kt_tab, flag_tab = _flash_steps(seq, tq, tk, window)

    o_spec = pl.BlockSpec((tq, gw), lambda b, g, s, qi, kt, fl: (b * nq + qi[s], g))
    qt_spec = pl.BlockSpec((gw, tq), lambda b, g, s, qi, kt, fl: (g, b * nq + qi[s]))
    k_spec = pl.BlockSpec((None, None, tk, HEAD_DIM), lambda b, g, s, qi, kt, fl: (branch, g, b * nkt + kt[s], 0))
    vt_spec = pl.BlockSpec((None, None, HEAD_DIM, tk), lambda b, g, s, qi, kt, fl: (branch, g, 0, b * nkt + kt[s]))
    gate_spec = pl.BlockSpec((None, None, tq, GQA_GROUP), lambda b, g, s, qi, kt, fl: (branch, g, b * nq + qi[s], 0))
    in_specs = [qt_spec]
    args = [qt]
    qa_rows = HEAD_DIM
    if use_bias:
        nbv = bias.shape[0]
        nbp = onehot.shape[1]
        qa_rows += nbp
        in_specs.append(
            pl.BlockSpec((nbv, tq), lambda b, g, s, qi, kt, fl: (0, (b * N_KV_HEADS + g) * nq + qi[s])))
        args.append(bias)
    in_specs += [k_spec, vt_spec]
    args += [k, vt]
    if use_bias:
        in_specs.append(pl.BlockSpec((tk, nbp), lambda b, g, s, qi, kt, fl: (kt[s], 0)))
        args.append(onehot)
    in_specs += [gate_spec, o_spec]
    args += [gates_t, prev]
    kern = functools.partial(_flash_kernel, tq=tq, tk=tk, use_bias=use_bias, window=window, out_dtype=out_dtype)
    cols = GQA_GROUP * tq
    grid_spec = pltpu.PrefetchScalarGridSpec(
        num_scalar_prefetch=3,
        grid=(n_seq, N_KV_HEADS, qi_tab.shape[0]),
        in_specs=in_specs,
        out_specs=o_spec,
        scratch_shapes=[
            pltpu.VMEM((qa_rows, cols), BF16),
            pltpu.VMEM((1, cols), F32),
            pltpu.VMEM((HEAD_DIM + 2 * SUBLANES, cols), F32),
        ],
    )
    return pl.pallas_call(
        kern,
        grid_spec=grid_spec,
        out_shape=jax.ShapeDtypeStruct((n_seq * seq, N_HEADS * HEAD_DIM), out_dtype),
        compiler_params=_params("parallel", "parallel", "arbitrary"),
        name="nsa_sel_attention" if use_bias else "nsa_win_attention",
    )(qi_tab, kt_tab, flag_tab, *args)


def _win_kernel(*refs, tq, n_tiles, window, out_dtype):
    qt_ref = refs[0]
    k_refs = refs[1:1 + n_tiles]
    vt_refs = refs[1 + n_tiles:1 + 2 * n_tiles]
    gate_ref, prev_ref, o_ref = refs[1 + 2 * n_tiles:]
    qi = pl.program_id(2)
    qa = jnp.concatenate([qt_ref[h * HEAD_DIM:(h + 1) * HEAD_DIM, :] for h in range(GQA_GROUP)], axis=1)
    sts = []
    for j in range(n_tiles):
        kt = qi - (n_tiles - 1) + j
        st = jnp.dot(k_refs[j][...], qa, preferred_element_type=F32)
        if j == 0 or j == n_tiles - 1:
            kpos = kt * tq + lax.broadcasted_iota(jnp.int32, st.shape, 0)
            qpos = qi * tq + jnp.bitwise_and(lax.broadcasted_iota(jnp.int32, st.shape, 1), tq - 1)
            d = qpos - kpos
            ok = (d >= 0) & (d < window) & (kpos >= 0)
            st = jnp.where(ok, st, NEG_INF)
        else:
            st = jnp.where(kt >= 0, st, NEG_INF)
        sts.append(st)
    st = jnp.concatenate(sts, axis=0)
    m = jnp.max(st, axis=0, keepdims=True)
    p = jnp.exp2(st - m).astype(BF16)
    vt = jnp.concatenate([r[...] for r in vt_refs], axis=1)
    va = jnp.concatenate([vt, jnp.ones((2 * SUBLANES, vt.shape[1]), BF16)], axis=0)
    acc = jnp.dot(va, p, preferred_element_type=F32)
    ot = acc[:HEAD_DIM] / acc[HEAD_DIM:HEAD_DIM + 1]
    for h in range(GQA_GROUP):
        sl = slice(h * HEAD_DIM, (h + 1) * HEAD_DIM)
        o_h = ot[:, h * tq:(h + 1) * tq].T
        o_ref[:, sl] = (prev_ref[:, sl] + o_h * gate_ref[:, h:h + 1]).astype(out_dtype)


def window_attention(qt, k, vt, branch, gates_t, prev, *, n_seq, seq, tq, window, out_dtype):
    nq = seq // tq
    n_tiles = window // tq + 1
    gw = GQA_GROUP * HEAD_DIM

    def kt_of(i, j):
        return jnp.maximum(i - (n_tiles - 1) + j, 0)

    o_spec = pl.BlockSpec((tq, gw), lambda b, g, i: (b * nq + i, g))
    in_specs = [pl.BlockSpec((gw, tq), lambda b, g, i: (g, b * nq + i))]
    in_specs += [pl.BlockSpec((None, None, tq, HEAD_DIM), lambda b, g, i, j=j: (branch, g, b * nq + kt_of(i, j), 0))
                 for j in range(n_tiles)]
    in_specs += [pl.BlockSpec((None, None, HEAD_DIM, tq), lambda b, g, i, j=j: (branch, g, 0, b * nq + kt_of(i, j)))
                 for j in range(n_tiles)]
    in_specs += [pl.BlockSpec((None, None, tq, GQA_GROUP), lambda b, g, i: (branch, g, b * nq + i, 0)), o_spec]
    kern = functools.partial(_win_kernel, tq=tq, n_tiles=n_tiles, window=window, out_dtype=out_dtype)
    return pl.pallas_call(
        kern,
        grid=(n_seq, N_KV_HEADS, nq),
        in_specs=in_specs,
        out_specs=o_spec,
        out_shape=jax.ShapeDtypeStruct((n_seq * seq, N_HEADS * HEAD_DIM), out_dtype),
        compiler_params=_params("parallel", "parallel", "parallel"),
        name="nsa_win_attention",
    )(qt, *([k] * n_tiles), *([vt] * n_tiles), gates_t, prev)


def _block_onehot(n_keys, nbp):
    e = (np.arange(n_keys)[:, None] // SEL_BLOCK) == np.arange(nbp)[None, :]
    return jnp.asarray(e.astype(np.float32), dtype=BF16)


def _sel_sample_kernel(*refs, n_pages, nbp, past_len, nsteps, tq):
    logical_ref, count_ref = refs[1:3]
    qt_ref, bias_ref, new_ref = refs[3:6]
    page_refs = refs[6:6 + n_pages]
    gate_ref, prev_ref, o_ref, m_ref, acc_ref = refs[6 + n_pages:]
    b = pl.program_id(0)
    step = pl.program_id(1)
    page_rows = page_refs[0].shape[0]
    rows_per_tok = 2 * N_KV_HEADS
    page_tokens = page_rows // rows_per_tok
    blocks_per_page = page_tokens // SEL_BLOCK
    n_needed = count_ref[b]
    cols = qt_ref.shape[2]

    def scores(g, k, key_bias):
        return jnp.dot(k, qt_ref[g], preferred_element_type=F32) + key_bias

    def block_bias(g, first_block, n_blocks, valid):
        rows = bias_ref[g, pl.ds(first_block, n_blocks), :]
        rows = jnp.where(valid, rows, NEG_INF)
        return jnp.concatenate(
            [jnp.broadcast_to(rows[j:j + 1], (SEL_BLOCK, cols)) for j in range(n_blocks)], axis=0)

    def weighted_values(p, v):
        va = jnp.concatenate([v, jnp.ones(v.shape, BF16)], axis=1)
        return lax.dot_general(p, va, _TN, preferred_element_type=F32)

    @pl.when(step == 0)
    def _():
        for g in range(N_KV_HEADS):
            kn = new_ref[pl.ds(g, tq, stride=2 * N_KV_HEADS), :].astype(BF16)
            vn = new_ref[pl.ds(N_KV_HEADS + g, tq, stride=2 * N_KV_HEADS), :].astype(BF16)
            new_bias = block_bias(g, past_len // SEL_BLOCK, 1, True)[:tq]
            st = scores(g, kn, new_bias)
            key = lax.broadcasted_iota(jnp.int32, st.shape, 0)
            tok = jnp.bitwise_and(lax.broadcasted_iota(jnp.int32, st.shape, 1), tq - 1)
            st = jnp.where(key <= tok, st, NEG_INF)
            m = jnp.max(st, axis=0, keepdims=True)
            p = jnp.exp2(st - m).astype(BF16)
            m_ref[g] = m
            acc_ref[g] = weighted_values(p, vn)

    @pl.when(step * n_pages < n_needed)
    def _():
        slots = [step * n_pages + p for p in range(n_pages)]
        first_blocks = [logical_ref[b, slot] * blocks_per_page for slot in slots]
        for g in range(N_KV_HEADS):
            k = jnp.concatenate(
                [page_refs[p][pl.ds(g, page_tokens, stride=rows_per_tok), :] for p in range(n_pages)],
                axis=0).astype(BF16)
            v = jnp.concatenate(
                [page_refs[p][pl.ds(N_KV_HEADS + g, page_tokens, stride=rows_per_tok), :] for p in range(n_pages)],
                axis=0).astype(BF16)
            key_bias = jnp.concatenate(
                [block_bias(g, first_blocks[p], blocks_per_page, slots[p] < n_needed) for p in range(n_pages)],
                axis=0)
            st = scores(g, k, key_bias)
            m_old = m_ref[g]
            m_new = jnp.maximum(m_old, jnp.max(st, axis=0, keepdims=True))
            alpha = jnp.exp2(m_old - m_new)
            p = jnp.exp2(st - m_new).astype(BF16)
            n_cols = alpha.shape[1]
            eye = (lax.broadcasted_iota(jnp.int32, (n_cols, n_cols), 0)
                   == lax.broadcasted_iota(jnp.int32, (n_cols, n_cols), 1))
            alpha_col = jnp.sum(jnp.where(eye, alpha, 0.0), axis=1, keepdims=True)
            acc_ref[g] = alpha_col * acc_ref[g] + weighted_values(p, v)
            m_ref[g] = m_new

    @pl.when(step == nsteps - 1)
    def _():
        for g in range(N_KV_HEADS):
            acc = acc_ref[g]
            o4 = acc[:, :HEAD_DIM] / acc[:, HEAD_DIM:HEAD_DIM + 1]
            for h in range(GQA_GROUP):
                hh = g * GQA_GROUP + h
                sl = slice(hh * HEAD_DIM, (hh + 1) * HEAD_DIM)
                o_ref[:, sl] = prev_ref[:, sl] + o4[h * tq:(h + 1) * tq] * gate_ref[g, :, h:h + 1]


def sel_attention_sample(q, bias, needed, kv_new, cache, layer, page_table, gates_t, prev, *, n_pages_step,
                         past_len):
    n_seq, pages = page_table.shape
    tq = SAMPLE_PAD
    nbp = bias.shape[-1]
    page_rows = cache.shape[2]
    nsteps = pages // n_pages_step
    dm = N_HEADS * HEAD_DIM
    order = jnp.argsort(jnp.logical_not(needed), axis=1, stable=True).astype(jnp.int32)
    count = jnp.sum(needed, axis=1).astype(jnp.int32)
    slot = jnp.minimum(jnp.arange(pages, dtype=jnp.int32)[None, :], count[:, None] - 1)
    logical = jnp.take_along_axis(order, slot, axis=1)
    physical = jnp.take_along_axis(page_table, logical, axis=1)
    row_spec = pl.BlockSpec((tq, dm), lambda b, s, *_: (b, 0))
    assert past_len % SEL_BLOCK + tq <= SEL_BLOCK
    cols = GQA_GROUP * tq
    q_t = q.reshape(n_seq, tq, N_KV_HEADS, GQA_GROUP, HEAD_DIM).transpose(0, 2, 4, 3, 1)
    q_t = q_t.reshape(n_seq, N_KV_HEADS, HEAD_DIM, cols)
    bias_t = jnp.tile(bias.transpose(0, 1, 3, 2), (1, 1, 1, GQA_GROUP)).astype(F32)

    def page_map(p):
        return lambda b, s, phys, *_: (layer, phys[b, s * n_pages_step + p], 0, 0)

    in_specs = [
        pl.BlockSpec((None, N_KV_HEADS, HEAD_DIM, cols), lambda b, s, *_: (b, 0, 0, 0)),
        pl.BlockSpec((None, N_KV_HEADS, nbp, cols), lambda b, s, *_: (b, 0, 0, 0)),
        pl.BlockSpec((tq * 2 * N_KV_HEADS, HEAD_DIM), lambda b, s, *_: (b, 0)),
    ]
    in_specs += [pl.BlockSpec((None, None, page_rows, HEAD_DIM), page_map(p)) for p in range(n_pages_step)]
    in_specs += [
        pl.BlockSpec((None, N_KV_HEADS, tq, GQA_GROUP), lambda b, s, *_: (1, 0, b, 0)),
        row_spec,
    ]
    kern = functools.partial(_sel_sample_kernel, n_pages=n_pages_step, nbp=nbp, past_len=past_len, nsteps=nsteps,
                             tq=tq)
    grid_spec = pltpu.PrefetchScalarGridSpec(
        num_scalar_prefetch=3,
        grid=(n_seq, nsteps),
        in_specs=in_specs,
        out_specs=row_spec,
        scratch_shapes=[
            pltpu.VMEM((N_KV_HEADS, 1, cols), F32),
            pltpu.VMEM((N_KV_HEADS, cols, 2 * HEAD_DIM), F32),
        ],
    )
    return pl.pallas_call(
        kern,
        grid_spec=grid_spec,
        out_shape=jax.ShapeDtypeStruct((n_seq * tq, dm), F32),
        compiler_params=_params("parallel", "arbitrary"),
        name="nsa_sel_attention_sample",
    )(physical, logical, count, q_t, bias_t, kv_new, *([cache] * n_pages_step), gates_t, prev)


def _win_sample_kernel(q_ref, buf_ref, new_ref, gate_ref, prev_ref, o_ref, *, tq, out_dtype):
    rows_per_tok = 2 * N_KV_HEADS
    w_buf = buf_ref.shape[0] // rows_per_tok
    gw = GQA_GROUP * HEAD_DIM
    for g in range(N_KV_HEADS):
        q4 = _group_rows(q_ref[:, g * gw:(g + 1) * gw])
        kb = buf_ref[pl.ds(g, w_buf, stride=rows_per_tok), :].astype(BF16)
        vb = buf_ref[pl.ds(N_KV_HEADS + g, w_buf, stride=rows_per_tok), :].astype(BF16)
        kn = new_ref[pl.ds(g, tq, stride=rows_per_tok), :].astype(BF16)
        vn = new_ref[pl.ds(N_KV_HEADS + g, tq, stride=rows_per_tok), :].astype(BF16)
        sb = lax.dot_general(q4, kb, _NT, preferred_element_type=F32)
        sn = lax.dot_general(q4, kn, _NT, preferred_element_type=F32)
        tb = jnp.bitwise_and(lax.broadcasted_iota(jnp.int32, sb.shape, 0), tq - 1)
        rb = lax.broadcasted_iota(jnp.int32, sb.shape, 1)
        sb = jnp.where(tb + w_buf - rb < WINDOW, sb, NEG_INF)
        tn = jnp.bitwise_and(lax.broadcasted_iota(jnp.int32, sn.shape, 0), tq - 1)
        rn = lax.broadcasted_iota(jnp.int32, sn.shape, 1)
        sn = jnp.where(rn <= tn, sn, NEG_INF)
        m = jnp.maximum(jnp.max(sb, axis=1, keepdims=True), jnp.max(sn, axis=1, keepdims=True))
        pb = jnp.exp2(sb - m)
        pn = jnp.exp2(sn - m)
        l = jnp.sum(pb, axis=1, keepdims=True) + jnp.sum(pn, axis=1, keepdims=True)
        o4 = (jnp.dot(pb.astype(BF16), vb, preferred_element_type=F32)
              + jnp.dot(pn.astype(BF16), vn, preferred_element_type=F32)) / l
        for h in range(GQA_GROUP):
            hh = g * GQA_GROUP + h
            sl = slice(hh * HEAD_DIM, (hh + 1) * HEAD_DIM)
            o_ref[:, sl] = (prev_ref[:, sl] + o4[h * tq:(h + 1) * tq] * gate_ref[g, :, h:h + 1]).astype(out_dtype)


def win_attention_sample(q, win_buf, layer, kv_new, gates_t, prev, *, out_dtype):
    n_seq, buf_rows = win_buf.shape[1], win_buf.shape[2]
    tq = SAMPLE_PAD
    q_spec = pl.BlockSpec((tq, N_HEADS * HEAD_DIM), lambda b: (b, 0))
    kern = functools.partial(_win_sample_kernel, tq=tq, out_dtype=out_dtype)
    return pl.pallas_call(
        kern,
        grid=(n_seq,),
        in_specs=[
            q_spec,
            pl.BlockSpec((None, None, buf_rows, HEAD_DIM), lambda b: (layer, b, 0, 0)),
            pl.BlockSpec((tq * 2 * N_KV_HEADS, HEAD_DIM), lambda b: (b, 0)),
            pl.BlockSpec((None, N_KV_HEADS, tq, GQA_GROUP), lambda b: (2, 0, b, 0)),
            q_spec,
        ],
        out_specs=q_spec,
        out_shape=jax.ShapeDtypeStruct((n_seq * tq, N_HEADS * HEAD_DIM), out_dtype),
        compiler_params=_params("parallel"),
        name="nsa_win_attention_sample",
    )(q, win_buf, kv_new, gates_t, prev)


def _ret_kernel(q_ref, k_ref, v_ref, g_ref, gn_ref, dec_ref, qd_ref, kd_ref, gc_ref, s0_ref, o_ref, sout_ref,
                state_ref, *, n_chunks):
    c = pl.program_id(2)

    @pl.when(c == 0)
    def _():
        state_ref[...] = s0_ref[...]

    for h in range(state_ref.shape[0]):
        ksl = slice(h * RET_KDIM, (h + 1) * RET_KDIM)
        vsl = slice(h * RET_VDIM, (h + 1) * RET_VDIM)
        q = q_ref[:, ksl]
        kf = k_ref[:, ksl]
        v = v_ref[:, vsl]
        state = state_ref[h]
        s = lax.dot_general(q, kf.astype(BF16), _NT, preferred_element_type=F32) * dec_ref[h]
        inner = jnp.dot(s.astype(BF16), v, preferred_element_type=F32)
        cross = jnp.dot(q, state.astype(BF16), preferred_element_type=F32) * qd_ref[h]
        o = inner + cross
        kd = (kf * kd_ref[h]).astype(BF16)
        state_ref[h] = gc_ref[h] * state + lax.dot_general(kd, v, _TN, preferred_element_type=F32)
        mu = jnp.mean(o, axis=-1, keepdims=True)
        d = o - mu
        var = jnp.mean(d * d, axis=-1, keepdims=True)
        on = d * lax.rsqrt(var + GN_EPS) * gn_ref[:, vsl]
        g = g_ref[:, vsl]
        o_ref[:, vsl] = (g * jax.nn.sigmoid(g) * on).astype(o_ref.dtype)

    @pl.when(c == n_chunks - 1)
    def _():
        sout_ref[...] = state_ref[...]


def _ret_tables(chunk, n_tok):
    log_g = jnp.log(1.0 - 2.0 ** (-5.0 - jnp.arange(RET_HEADS, dtype=F32)))
    i = jnp.arange(chunk, dtype=F32)
    rel = i[:, None] - i[None, :]
    decay = jnp.where(rel >= 0, jnp.exp(jnp.maximum(rel, 0.0)[None] * log_g[:, None, None]), 0.0)
    q_dec = jnp.exp((i[None, :] + 1.0) * log_g[:, None])[..., None]
    k_dec = jnp.where(i[None, :] < n_tok, jnp.exp((n_tok - 1.0 - i)[None, :] * log_g[:, None]), 0.0)[..., None]
    g_c = jnp.exp(n_tok * log_g)[:, None, None]
    return decay, q_dec, k_dec, g_c


def retention(q, k, v, g, gn, layer, state0, *, n_seq, seq, chunk, n_tok):
    nc = seq // chunk
    hps = RET_HEADS_PER_STEP
    decay, q_dec, k_dec, g_c = _ret_tables(chunk, n_tok)
    kern = functools.partial(_ret_kernel, n_chunks=nc)
    tok_k = pl.BlockSpec((chunk, hps * RET_KDIM), lambda b, h, c: (b * nc + c, h))
    tok_v = pl.BlockSpec((chunk, hps * RET_VDIM), lambda b, h, c: (b * nc + c, h))
    st_spec = pl.BlockSpec((None, hps, RET_KDIM, RET_VDIM), lambda b, h, c: (b, h, 0, 0))
    return pl.pallas_call(
        kern,
        grid=(n_seq, RET_HEADS // hps, nc),
        in_specs=[
            tok_k, tok_k, tok_v, tok_v,
            pl.BlockSpec((None, 1, hps * RET_VDIM), lambda b, h, c: (layer, 0, h)),
            pl.BlockSpec((hps, chunk, chunk), lambda b, h, c: (h, 0, 0)),
            pl.BlockSpec((hps, chunk, 1), lambda b, h, c: (h, 0, 0)),
            pl.BlockSpec((hps, chunk, 1), lambda b, h, c: (h, 0, 0)),
            pl.BlockSpec((hps, 1, 1), lambda b, h, c: (h, 0, 0)),
            st_spec,
        ],
        out_specs=[tok_v, st_spec],
        out_shape=[
            jax.ShapeDtypeStruct((n_seq * seq, RET_HEADS * RET_VDIM), BF16),
            jax.ShapeDtypeStruct((n_seq, RET_HEADS, RET_KDIM, RET_VDIM), F32),
        ],
        scratch_shapes=[pltpu.VMEM((hps, RET_KDIM, RET_VDIM), F32)],
        compiler_params=_params("parallel", "parallel", "arbitrary"),
        name="retention_chunk",
    )(q, k, v, g, gn, decay, q_dec, k_dec, g_c, state0)


def _project_nsa(xn, w_in, layer, cos, sin_signed, *, tm):
    m, d = xn.shape
    nq = N_HEADS * HEAD_DIM
    half = N_KV_HEADS * HEAD_DIM
    rows_per_tok = 2 * N_KV_HEADS
    tabs_per_seq = cos.shape[0] // tm
    tab_spec = pl.BlockSpec((tm, HEAD_DIM), lambda i, j, k: (i % tabs_per_seq, 0))
    tn_q = 512
    q = matmul(
        [xn], w_in, layer, col_block0=0, n_cols=nq, tm=tm, tn=tn_q, tk=d,
        extras=[cos, sin_signed], extra_specs=[tab_spec, tab_spec],
        out_shapes=[jax.ShapeDtypeStruct((m, nq), BF16), jax.ShapeDtypeStruct((nq, m), BF16)],
        out_specs=[pl.BlockSpec((tm, tn_q), lambda i, j, k: (i, j)), pl.BlockSpec((tn_q, tm), lambda i, j, k: (j, i))],
        epilogue=functools.partial(_ep_rope_q, scale=HEAD_DIM ** -0.5 * LOG2_E), name="nsa_proj_q")
    kv = matmul(
        [xn], w_in, layer, col_block0=nq // half, n_cols=3 * KV_ROW, tm=tm, tn=half, tk=d,
        extras=[cos, sin_signed], extra_specs=[tab_spec, tab_spec],
        out_shapes=[jax.ShapeDtypeStruct((m * rows_per_tok, HEAD_DIM), F32)] * 3
        + [jax.ShapeDtypeStruct((3, N_KV_HEADS, m, HEAD_DIM), BF16),
           jax.ShapeDtypeStruct((3, N_KV_HEADS, HEAD_DIM, m), BF16)],
        out_specs=[pl.BlockSpec((tm * rows_per_tok, HEAD_DIM), lambda i, j, k: (i, 0))] * 3
        + [pl.BlockSpec((None, N_KV_HEADS, tm, HEAD_DIM), lambda i, j, k: (j // 2, 0, i, 0)),
           pl.BlockSpec((None, N_KV_HEADS, HEAD_DIM, tm), lambda i, j, k: (j // 2, 0, 0, i))],
        epilogue=_ep_nsa_kv, name="nsa_proj_kv", column_steps_ordered=True)
    kv = (kv[:3], kv[3], kv[4])
    n_gate = 3 * N_HEADS
    w_gate = jnp.pad(w_in[layer, :, nq + 3 * KV_ROW:], ((0, 0), (0, LANES - n_gate)))[None]
    gates_t = matmul(
        [xn], w_gate, 0, col_block0=0, n_cols=LANES, tm=tm, tn=LANES, tk=d, extras=[], extra_specs=[],
        out_shapes=[jax.ShapeDtypeStruct((3, N_KV_HEADS, m, GQA_GROUP), F32)],
        out_specs=[pl.BlockSpec((3, N_KV_HEADS, tm, GQA_GROUP), lambda i, j, k: (0, 0, i, 0))],
        epilogue=_ep_gates, name="nsa_proj_gate")[0]
    return q, kv, gates_t


def _cmp_bias(w1, pos, b1, layer):
    k_full = CMP_BLOCK * HEAD_DIM
    pos_rows = jnp.zeros((2, SUBLANES, k_full), F32).at[:, 0].set(pos[layer].reshape(2, k_full))
    terms = []
    for e in range(2):
        t = matmul(
            [pos_rows[e]], w1.reshape(-1, k_full, HEAD_DIM), layer * 2 + e, col_block0=0, n_cols=HEAD_DIM,
            tm=SUBLANES, tn=HEAD_DIM, tk=k_full, extras=[], extra_specs=[],
            out_shapes=[jax.ShapeDtypeStruct((SUBLANES, HEAD_DIM), F32)],
            out_specs=[pl.BlockSpec((SUBLANES, HEAD_DIM), lambda i, j, k: (0, 0))],
            epilogue=_ep_cast, name="nsa_cmp_pos_term")[0]
        terms.append(t[0])
    return jnp.stack(terms) + b1[layer]


def _mlp_up(xn, w1, layer, *, tm):
    m, d = xn.shape
    d_ff = w1.shape[-1]
    tn = 1024
    return matmul(
        [xn], w1, layer, col_block0=0, n_cols=d_ff, tm=tm, tn=tn, tk=d, extras=[], extra_specs=[],
        out_shapes=[jax.ShapeDtypeStruct((m, d_ff), BF16)],
        out_specs=[pl.BlockSpec((tm, tn), lambda i, j, k: (i, j))],
        epilogue=_ep_relu2, name="mlp_up")[0]


def _mixer_out_proj(x, w, layer, h, norm_g, *, name):
    m, d = h.shape
    kdim = x.shape[1]
    w_bytes = 2 * kdim * d * w.dtype.itemsize
    per_row = 2 * kdim * x.dtype.itemsize + d * (4 * 4 + 2 * 2 + 4)
    tm = min(m, (VMEM_LIMIT_BYTES * 7 // 8 - w_bytes) // per_row // LANES * LANES)
    while m % tm:
        tm -= LANES
    row = pl.BlockSpec((tm, d), lambda i, j, k: (i, 0))
    return matmul(
        [x], w, layer, col_block0=0, n_cols=d, tm=tm, tn=d, tk=kdim,
        extras=[h, norm_g.reshape(1, d)], extra_specs=[row, pl.BlockSpec((1, d), lambda i, j, k: (0, 0))],
        out_shapes=[jax.ShapeDtypeStruct((m, d), F32), jax.ShapeDtypeStruct((m, d), BF16)], out_specs=[row, row],
        epilogue=_ep_residual_norm, name=name)


def _mlp_down(a, w2, layer, h, *, tm):
    m, d = h.shape
    tn = 1024
    tm = min(tm, 1024)
    spec = pl.BlockSpec((tm, tn), lambda i, j, k: (i, j))
    return matmul(
        [a], w2, layer, col_block0=0, n_cols=d, tm=tm, tn=tn, tk=2048, extras=[h], extra_specs=[spec],
        out_shapes=[jax.ShapeDtypeStruct((m, d), F32)], out_specs=[spec], epilogue=_ep_residual_norm,
        name="mlp_down")[0]


def _nsa_tables(pos):
    cos, sin = _rope_tables(pos, HEAD_DIM // 2)
    return jnp.concatenate([cos, cos], axis=1), jnp.concatenate([-sin, sin], axis=1)


def _nsa_prompt(xn, layer, w_in, cmp_w1, cmp_bias, cmp_w2, *, n_seq, seq):
    tm = PROMPT_ROW_TILE
    cos, sin_signed = _nsa_tables(jnp.arange(seq))
    (q, qt), (kv_rows, k_bf, vt_bf), gates_t = _project_nsa(xn, w_in, layer, cos, sin_signed, tm=tm // 2)
    n_chunks = seq // CMP_STRIDE
    chunks_per_step = 128
    kv_chunks = kv_rows[0].reshape(1, (n_seq * n_chunks) // chunks_per_step, chunks_per_step, CMP_STRIDE,
                                   2 * N_KV_HEADS, HEAD_DIM)
    kc = compress(kv_chunks, 0, None, cmp_w1, cmp_bias, cmp_w2, n_seq=n_seq, n_chunks=n_chunks, n_pages=1)
    n_blk = seq // SEL_BLOCK
    cover_t = _cover_t(n_chunks, n_blk, n_blk)
    o1, val = cmp_attention(q, kc, cover_t, gates_t, n_seq=n_seq, seq=seq, tq=512, pos0=0)
    bias_t = select_bias(val, n_blk=n_blk, tc=1024, causal_seq=seq)
    nbp = pl.cdiv(n_blk, LANES) * LANES
    o2 = flash_attention(qt, k_bf, vt_bf, 1, gates_t, o1, n_seq=n_seq, seq=seq, tq=512, tk=512, bias=bias_t,
                         onehot=_block_onehot(seq, nbp))
    o3 = window_attention(qt, k_bf, vt_bf, 2, gates_t, o2, n_seq=n_seq, seq=seq, tq=256, window=WINDOW,
                          out_dtype=BF16)
    return o3, kv_rows


def _nsa_sample(xn, layer, w_in, cmp_w1, cmp_bias, cmp_w2, cache_cmp, cache_sel, win_buf, page_table,
                *, n_seq, n_tok, past_len):
    tq = SAMPLE_PAD
    m = n_seq * tq
    cos, sin_signed = _nsa_tables(jnp.tile(past_len + jnp.arange(tq), n_seq))
    (q, _), (kv_rows, _, _), gates_t = _project_nsa(xn, w_in, layer, cos, sin_signed, tm=m)
    n_chunks = past_len // CMP_STRIDE
    kc = compress(cache_cmp, layer, page_table, cmp_w1, cmp_bias, cmp_w2, n_seq=n_seq, n_chunks=n_chunks,
                  n_pages=16)
    n_blk = pl.cdiv(past_len + n_tok, SEL_BLOCK)
    nbv = pl.cdiv(n_blk, SUBLANES) * SUBLANES
    cover_t = _cover_t(n_chunks, n_blk, nbv)
    o1, val = cmp_attention(q, kc, cover_t, gates_t, n_seq=n_seq, seq=tq, tq=tq, pos0=past_len)
    ncols = n_seq * N_KV_HEADS * n_tok
    bias_t = select_bias(val[..., :n_tok].transpose(2, 0, 1, 3).reshape(nbv, ncols), n_blk=n_blk, tc=ncols)
    blocks_per_page = cache_sel.shape[2] // (2 * N_KV_HEADS * SEL_BLOCK)
    n_pages = page_table.shape[1]
    chosen = (bias_t == 0).reshape(nbv, n_seq, N_KV_HEADS * n_tok).any(axis=2)
    needed = chosen[:n_pages * blocks_per_page].reshape(n_pages, blocks_per_page, n_seq).any(axis=1).T
    nbp = pl.cdiv(n_blk + 1, LANES) * LANES
    bias = bias_t.reshape(nbv, n_seq, N_KV_HEADS, n_tok).transpose(1, 2, 3, 0)
    bias = jnp.pad(bias, ((0, 0), (0, 0), (0, 0), (0, nbp - nbv)), constant_values=NEG_INF)
    bias = jnp.pad(bias, ((0, 0), (0, 0), (0, tq - n_tok), (0, 0)))
    o2 = sel_attention_sample(q, bias, needed, kv_rows[1], cache_sel, layer, page_table, gates_t, o1,
                              n_pages_step=16, past_len=past_len)
    o3 = win_attention_sample(q, win_buf, layer, kv_rows[2], gates_t, o2, out_dtype=BF16)
    return o3, kv_rows


def _ret_layer(xn, layer, w_in, gn, state0, pos, *, n_seq, seq, chunk, n_tok, tm):
    m, d = xn.shape
    nk = RET_HEADS * RET_KDIM
    nv = RET_HEADS * RET_VDIM
    cos, sin = _rope_tables(pos, RET_KDIM // 2)
    tabs_per_seq = cos.shape[0] // tm
    tab_spec = pl.BlockSpec((tm, LANES), lambda i, j, k: (i % tabs_per_seq, 0))
    tn = 512

    def proj(col0, n_cols, dtype, epilogue, extras, extra_specs, name):
        return matmul(
            [xn], w_in, layer, col_block0=col0 // tn, n_cols=n_cols, tm=tm, tn=tn, tk=d,
            extras=extras, extra_specs=extra_specs,
            out_shapes=[jax.ShapeDtypeStruct((m, n_cols), dtype)],
            out_specs=[pl.BlockSpec((tm, tn), lambda i, j, k: (i, j))],
            epilogue=epilogue, name=name)[0]

    q = proj(0, nk, BF16, functools.partial(_ep_rope_ret, scale=1.0), [cos, sin], [tab_spec, tab_spec], "ret_proj_q")
    k = proj(nk, nk, F32, functools.partial(_ep_rope_ret, scale=RET_KDIM ** -0.5), [cos, sin],
             [tab_spec, tab_spec], "ret_proj_k")
    v = proj(2 * nk, nv, BF16, _ep_cast, [], [], "ret_proj_v")
    g = proj(2 * nk + nv, nv, F32, _ep_cast, [], [], "ret_proj_g")
    return retention(q, k, v, g, gn.reshape(gn.shape[0], 1, nv), layer, state0, n_seq=n_seq, seq=seq, chunk=chunk,
                     n_tok=n_tok)


def kernel(x_prompt, x_sample, cache_cmp_kv, cache_sel_kv, state_win_kv, state_ret, page_table, norm_mix, norm_ffn,
           norm_final, nsa_w_in, nsa_w_out, nsa_cmp_pos, nsa_cmp_w1, nsa_cmp_b1, nsa_cmp_w2, ret_w_in, ret_gn,
           ret_w_out, ffn_w1, ffn_w2):
    n_seq_p, seq_p, d = x_prompt.shape
    n_seq_s, n_tok_s, _ = x_sample.shape
    depth = norm_mix.shape[0]
    n_nsa = cache_sel_kv.shape[0]
    pool, page_rows = cache_sel_kv.shape[1], cache_sel_kv.shape[2]
    past_len = page_table.shape[1] * page_rows
    assert past_len % CMP_STRIDE == 0 and n_tok_s < CMP_STRIDE and n_tok_s <= SAMPLE_PAD
    assert state_win_kv.shape[2] == WINDOW and seq_p >= WINDOW

    hp = x_prompt.reshape(n_seq_p * seq_p, d)
    hs = jnp.pad(x_sample, ((0, 0), (0, SAMPLE_PAD - n_tok_s), (0, 0))).reshape(n_seq_s * SAMPLE_PAD, d)
    m_s = n_seq_s * SAMPLE_PAD
    rows_per_tok = 2 * N_KV_HEADS
    cache_cmp = cache_cmp_kv.reshape(n_nsa, pool, page_rows // CMP_STRIDE, CMP_STRIDE, rows_per_tok, HEAD_DIM)
    cache_sel = cache_sel_kv.reshape(n_nsa, pool, page_rows * rows_per_tok, HEAD_DIM)
    win_buf = state_win_kv.reshape(n_nsa, n_seq_s, WINDOW * rows_per_tok, HEAD_DIM)
    cmp_w1 = nsa_cmp_w1.reshape(n_nsa, 2, CMP_BLOCK * HEAD_DIM, HEAD_DIM)
    kv_shape = (2, N_KV_HEADS, HEAD_DIM)
    ffn_w2 = ffn_w2.astype(BF16)
    ret_w_out = ret_w_out.astype(BF16)
    nsa_w_out = nsa_w_out.astype(BF16)

    cmp_p, cmp_s, sel_p, sel_s, win_p, win_s, ret_p, ret_s = [], [], [], [], [], [], [], []
    xp = rmsnorm(hp, norm_mix[0], BF16, 256)
    xs = rmsnorm(hs, norm_mix[0], BF16, m_s)
    for layer in range(depth):
        if layer % 2 == 0:
            a = layer // 2
            cmp_bias = _cmp_bias(cmp_w1, nsa_cmp_pos, nsa_cmp_b1, a)
            op, kv_p = _nsa_prompt(xp, a, nsa_w_in, cmp_w1, cmp_bias, nsa_cmp_w2, n_seq=n_seq_p, seq=seq_p)
            os_, kv_s = _nsa_sample(xs, a, nsa_w_in, cmp_w1, cmp_bias, nsa_cmp_w2, cache_cmp, cache_sel, win_buf,
                                    page_table, n_seq=n_seq_s, n_tok=n_tok_s, past_len=past_len)
            w_out, w_idx = nsa_w_out, a
            kv_p = [r.reshape((n_seq_p, seq_p) + kv_shape) for r in kv_p]
            kv_s = [r.reshape((n_seq_s, SAMPLE_PAD) + kv_shape)[:, :n_tok_s] for r in kv_s]
            cmp_p.append(kv_p[0])
            sel_p.append(kv_p[1])
            win_p.append(kv_p[2][:, seq_p - WINDOW:])
            cmp_s.append(kv_s[0])
            sel_s.append(kv_s[1])
            win_s.append(jnp.concatenate([state_win_kv[a][:, n_tok_s:], kv_s[2]], axis=1))
        else:
            r = layer // 2
            zeros = jnp.zeros((n_seq_p, RET_HEADS, RET_KDIM, RET_VDIM), F32)
            op, st_p = _ret_layer(xp, r, ret_w_in, ret_gn, zeros, jnp.arange(seq_p), n_seq=n_seq_p, seq=seq_p,
                                  chunk=RET_CHUNK, n_tok=RET_CHUNK, tm=PROMPT_ROW_TILE)
            os_, st_s = _ret_layer(xs, r, ret_w_in, ret_gn, state_ret[r],
                                   jnp.tile(past_len + jnp.arange(SAMPLE_PAD), n_seq_s), n_seq=n_seq_s,
                                   seq=SAMPLE_PAD, chunk=SAMPLE_PAD, n_tok=n_tok_s, tm=m_s)
            ret_p.append(st_p)
            ret_s.append(st_s)
            w_out, w_idx = ret_w_out, r
        hp, xp = _mixer_out_proj(op, w_out, w_idx, hp, norm_ffn[layer], name="mixer_out_proj")
        hs, xs = _mixer_out_proj(os_, w_out, w_idx, hs, norm_ffn[layer], name="mixer_out_proj_sample")
        hp = _mlp_down(_mlp_up(xp, ffn_w1, layer, tm=PROMPT_ROW_TILE), ffn_w2, layer, hp, tm=PROMPT_ROW_TILE)
        hs = _mlp_down(_mlp_up(xs, ffn_w1, layer, tm=m_s), ffn_w2, layer, hs, tm=m_s)
        if layer + 1 < depth:
            xp = rmsnorm(hp, norm_mix[layer + 1], BF16, 256)
            xs = rmsnorm(hs, norm_mix[layer + 1], BF16, m_s)
    y_prompt = rmsnorm(hp, norm_final, F32, 256).reshape(n_seq_p, seq_p, d)
    y_sample = rmsnorm(hs, norm_final, F32, m_s).reshape(n_seq_s, SAMPLE_PAD, d)[:, :n_tok_s]
    return (y_prompt, y_sample, jnp.stack(cmp_p), jnp.stack(cmp_s), jnp.stack(sel_p), jnp.stack(sel_s),
            jnp.stack(win_p), jnp.stack(win_s), jnp.stack(ret_p), jnp.stack(ret_s))
```

```python
import functools

import numpy as np
import jax
import jax.numpy as jnp
from jax import lax
from jax.experimental import pallas as pl
from jax.experimental.pallas import tpu as pltpu

F32 = jnp.float32
BF16 = jnp.bfloat16

N_HEADS = 16
HEAD_DIM = 128
N_KV_HEADS = 4
GQA_GROUP = N_HEADS // N_KV_HEADS
KV_ROW = 2 * N_KV_HEADS * HEAD_DIM
CMP_BLOCK = 32
CMP_STRIDE = 16
SEL_BLOCK = 64
N_SEL = 16
N_LOCAL = 2
WINDOW = 512
ROPE_THETA = 10000.0
RET_HEADS = 8
RET_KDIM = 256
RET_VDIM = 512
RET_CHUNK = 128
RET_HEADS_PER_STEP = 8
RMS_EPS = 1e-6
GN_EPS = 1e-5
NEG_INF = -1e30
FORCE_SCORE = 1e9
LOG2_E = 1.4426950408889634

SUBLANES = 8
LANES = 128
VMEM_LIMIT_BYTES = 56 * 1024 * 1024
SAMPLE_PAD = 16
PROMPT_ROW_TILE = 2048

_NT = (((1,), (1,)), ((), ()))
_TN = (((0,), (0,)), ((), ()))


def _params(*semantics):
    return pltpu.CompilerParams(dimension_semantics=semantics, vmem_limit_bytes=VMEM_LIMIT_BYTES)


def _rmsnorm_kernel(x_ref, g_ref, o_ref):
    x = x_ref[...]
    ms = jnp.mean(x * x, axis=-1, keepdims=True)
    o_ref[...] = (x * lax.rsqrt(ms + RMS_EPS) * g_ref[...]).astype(o_ref.dtype)


def rmsnorm(x, g, out_dtype, tm, row_block0=0, n_blocks=None):
    m, d = x.shape
    if n_blocks is None:
        n_blocks = m // tm
    return pl.pallas_call(
        _rmsnorm_kernel,
        grid=(n_blocks,),
        in_specs=[pl.BlockSpec((tm, d), lambda i: (i + row_block0, 0)), pl.BlockSpec((1, d), lambda i: (0, 0))],
        out_specs=pl.BlockSpec((tm, d), lambda i: (i, 0)),
        out_shape=jax.ShapeDtypeStruct((n_blocks * tm, d), out_dtype),
        compiler_params=_params("parallel"),
        name="rmsnorm",
    )(x, g.reshape(1, d))


def _mm_kernel(*refs, n_x, n_extra, n_out, nk, epilogue):
    x_refs = refs[:n_x]
    w_ref = refs[n_x]
    extra = refs[n_x + 1:n_x + 1 + n_extra]
    outs = refs[n_x + 1 + n_extra:n_x + 1 + n_extra + n_out]
    x = x_refs[0][...]
    for r in x_refs[1:]:
        x = x + r[...]
    part = jnp.dot(x.astype(BF16), w_ref[...].astype(BF16), preferred_element_type=F32)
    if nk == 1:
        epilogue(part, extra, outs)
        return
    k = pl.program_id(2)
    if epilogue is _ep_residual_norm:
        @pl.when(k == 0)
        def _():
            outs[0][...] = extra[0][...] + part

        @pl.when(k > 0)
        def _():
            outs[0][...] += part

        @pl.when(k == nk - 1)
        def _():
            _write_norm(outs[0][...], extra, outs)
        return
    acc_ref = refs[-1]

    @pl.when(k == 0)
    def _():
        acc_ref[...] = part

    @pl.when(k > 0)
    def _():
        acc_ref[...] += part

    @pl.when(k == nk - 1)
    def _():
        epilogue(acc_ref[...], extra, outs)


def matmul(xs, w, layer, *, col_block0, n_cols, tm, tn, tk, extras, extra_specs, out_shapes, out_specs,
           epilogue, name, column_steps_ordered=False):
    m, kdim = xs[0].shape
    nk = kdim // tk
    grid = (m // tm, pl.cdiv(n_cols, tn), nk)
    in_specs = [pl.BlockSpec((tm, tk), lambda i, j, k: (i, k)) for _ in xs]
    in_specs.append(pl.BlockSpec((None, tk, tn), lambda i, j, k: (layer, k, j + col_block0)))
    in_specs.extend(extra_specs)
    scratch = [pltpu.VMEM((tm, tn), F32)] if nk > 1 and epilogue is not _ep_residual_norm else []
    kern = functools.partial(_mm_kernel, n_x=len(xs), n_extra=len(extras), n_out=len(out_shapes), nk=nk,
                             epilogue=epilogue)
    return pl.pallas_call(
        kern,
        grid=grid,
        in_specs=in_specs,
        out_specs=out_specs,
        out_shape=out_shapes,
        scratch_shapes=scratch,
        compiler_params=_params("parallel", "arbitrary" if column_steps_ordered else "parallel", "arbitrary"),
        name=name,
    )(*xs, w, *extras)


def _rope_half_lane(x, cos, sin_signed):
    return x * cos + pltpu.roll(x, HEAD_DIM // 2, 1) * sin_signed


def _ep_rope_q(part, extra, outs, *, scale):
    cos = extra[0][...]
    sin = extra[1][...]
    for h in range(part.shape[1] // HEAD_DIM):
        sl = slice(h * HEAD_DIM, (h + 1) * HEAD_DIM)
        r = _rope_half_lane(part[:, sl], cos, sin) * scale
        outs[0][:, sl] = r.astype(outs[0].dtype)
        outs[1][sl, :] = r.T.astype(outs[1].dtype)


def _ep_nsa_kv(part, extra, outs):
    tm = part.shape[0]
    rows_per_tok = 2 * N_KV_HEADS
    j = pl.program_id(1)

    def heads():
        return [part[:, h * HEAD_DIM:(h + 1) * HEAD_DIM] for h in range(N_KV_HEADS)]

    for br in range(3):
        @pl.when(j == 2 * br)
        def _(br=br):
            cos = extra[0][...]
            sin = extra[1][...]
            for h, x in enumerate(heads()):
                k = _rope_half_lane(x, cos, sin)
                outs[br][pl.ds(h, tm, stride=rows_per_tok), :] = k
                outs[3][h] = k.astype(BF16)

        @pl.when(j == 2 * br + 1)
        def _(br=br):
            for h, v in enumerate(heads()):
                outs[br][pl.ds(N_KV_HEADS + h, tm, stride=rows_per_tok), :] = v
                outs[4][h] = v.T.astype(BF16)


def _ep_gates(part, extra, outs):
    g = jax.nn.sigmoid(part)
    for br in range(3):
        for kvh in range(N_KV_HEADS):
            c = br * N_HEADS + kvh * GQA_GROUP
            outs[0][br, kvh] = g[:, c:c + GQA_GROUP]


def _write_norm(h, extra, outs):
    if len(extra) < 2:
        return
    ms = jnp.mean(h * h, axis=-1, keepdims=True)
    y = h * lax.rsqrt(ms + RMS_EPS) * extra[1][...]
    outs[-1][...] = y.astype(outs[-1].dtype)


def _ep_residual_norm(part, extra, outs):
    h = extra[0][...] + part
    outs[0][...] = h
    _write_norm(h, extra, outs)


def _ep_relu2(part, extra, outs):
    h = jnp.maximum(part, 0.0)
    outs[0][...] = (h * h).astype(outs[0].dtype)


def _ep_cast(part, extra, outs):
    outs[0][...] = part.astype(outs[0].dtype)


def _ep_rope_ret(part, extra, outs, *, scale):
    cos = extra[0][...]
    sin = extra[1][...]
    for h in range(part.shape[1] // RET_KDIM):
        a = slice(h * RET_KDIM, h * RET_KDIM + LANES)
        b = slice(h * RET_KDIM + LANES, (h + 1) * RET_KDIM)
        x1 = part[:, a]
        x2 = part[:, b]
        outs[0][:, a] = ((x1 * cos - x2 * sin) * scale).astype(outs[0].dtype)
        outs[0][:, b] = ((x2 * cos + x1 * sin) * scale).astype(outs[0].dtype)


def _rope_tables(pos, half):
    inv = ROPE_THETA ** (-jnp.arange(half, dtype=F32) / half)
    ang = pos.astype(F32)[:, None] * inv[None, :]
    return jnp.cos(ang), jnp.sin(ang)


def _compress_kernel(*refs, n_pages, paged):
    if paged:
        refs = refs[1:]
    page_refs = refs[:n_pages]
    next_ref, w1_ref, bias_ref, w2_ref, out_ref, shift_ref, y_ref = refs[n_pages:]
    nc = page_refs[0].shape[0]
    n_heads = N_KV_HEADS
    n_main = n_pages * nc * n_heads
    k_half = CMP_STRIDE * HEAD_DIM
    tiles = [[page_refs[p][:, s] for s in range(CMP_STRIDE)] for p in range(n_pages)]
    next_tiles = [next_ref[0, s] for s in range(CMP_STRIDE)]
    for e in range(2):
        hsl = slice(e * n_heads, (e + 1) * n_heads)
        blocks = [
            jnp.concatenate([tiles[p][s][:, hsl, :].reshape(nc * n_heads, HEAD_DIM) for s in range(CMP_STRIDE)],
                            axis=1)
            for p in range(n_pages)]
        blocks.append(jnp.concatenate([next_tiles[s][hsl, :] for s in range(CMP_STRIDE)], axis=1))
        blocks.append(jnp.zeros((SUBLANES - n_heads, CMP_STRIDE * HEAD_DIM), F32))
        x = jnp.concatenate(blocks, axis=0).astype(BF16)
        wcat = jnp.concatenate([w1_ref[e, :k_half], w1_ref[e, k_half:]], axis=1).astype(BF16)
        per = jnp.dot(x, wcat, preferred_element_type=F32)
        hid = per.shape[1] // 2
        shift_ref[...] = per[:, hid:]
        h = per[:n_main, :hid] + shift_ref[pl.ds(n_heads, n_main), :]
        h = jax.nn.gelu(h + bias_ref[e:e + 1, :])
        y_ref[...] = jnp.dot(h.astype(BF16), w2_ref[e].astype(BF16), preferred_element_type=F32)
        for g in range(n_heads):
            out_ref[e * n_heads + g] = y_ref[pl.ds(g, n_main // n_heads, stride=n_heads), :].astype(out_ref.dtype)


def compress(kv, layer, page_table, w1, bias, w2, *, n_seq, n_chunks, n_pages):
    paged = page_table is not None
    nc = kv.shape[2]
    n_main = n_pages * nc
    steps = n_chunks // n_main
    pages_per_seq = steps * n_pages
    rows_per_tok = 2 * N_KV_HEADS

    def nxt(s):
        return jnp.minimum((s + 1) * n_pages, pages_per_seq - 1)

    if paged:
        def page_map(p):
            return lambda b, s, pt: (layer, pt[b, s * n_pages + p], 0, 0, 0, 0)

        def next_map(b, s, pt):
            return (layer, pt[b, nxt(s)], 0, 0, 0, 0)

        def fixed(*idx):
            return lambda b, s, pt: idx

        out_map = lambda b, s, pt: (b, 0, s, 0)
    else:
        def page_map(p):
            return lambda b, s: (layer, b * pages_per_seq + s * n_pages + p, 0, 0, 0, 0)

        def next_map(b, s):
            return (layer, b * pages_per_seq + nxt(s), 0, 0, 0, 0)

        def fixed(*idx):
            return lambda b, s: idx

        out_map = lambda b, s: (b, 0, s, 0)

    tile = (CMP_STRIDE, rows_per_tok, HEAD_DIM)
    in_specs = [pl.BlockSpec((None, None, nc) + tile, page_map(p)) for p in range(n_pages)]
    in_specs += [
        pl.BlockSpec((None, None, 1) + tile, next_map),
        pl.BlockSpec((None, 2, CMP_BLOCK * HEAD_DIM, HEAD_DIM), fixed(layer, 0, 0, 0)),
        pl.BlockSpec((2, HEAD_DIM), fixed(0, 0)),
        pl.BlockSpec((None, 2, HEAD_DIM, HEAD_DIM), fixed(layer, 0, 0, 0)),
    ]
    kern = functools.partial(_compress_kernel, n_pages=n_pages, paged=paged)
    grid_spec = pltpu.PrefetchScalarGridSpec(
        num_scalar_prefetch=1 if paged else 0,
        grid=(n_seq, steps),
        in_specs=in_specs,
        out_specs=pl.BlockSpec((None, rows_per_tok, n_main, HEAD_DIM), out_map),
        scratch_shapes=[pltpu.VMEM((n_main * N_KV_HEADS + SUBLANES, HEAD_DIM), F32),
                        pltpu.VMEM((n_main * N_KV_HEADS, HEAD_DIM), F32)],
    )
    args = ([page_table] if paged else []) + [kv] * (n_pages + 1) + [w1, bias, w2]
    return pl.pallas_call(
        kern,
        grid_spec=grid_spec,
        out_shape=jax.ShapeDtypeStruct((n_seq, rows_per_tok, n_chunks, HEAD_DIM), BF16),
        compiler_params=_params("parallel", "arbitrary"),
        name="nsa_compress",
    )(*args)


def _group_rows(q):
    return jnp.concatenate([q[:, h * HEAD_DIM:(h + 1) * HEAD_DIM] for h in range(GQA_GROUP)], axis=0)


def _cmp_kernel(q_ref, k_ref, v_ref, cov_ref, gate_ref, o_ref, val_ref, *, tq, pos0):
    qi = pl.program_id(2)
    q4 = _group_rows(q_ref[...])
    s = lax.dot_general(q4, k_ref[...], _NT, preferred_element_type=F32)
    row = lax.broadcasted_iota(jnp.int32, s.shape, 0)
    col = lax.broadcasted_iota(jnp.int32, s.shape, 1)
    qpos = pos0 + qi * tq + jnp.bitwise_and(row, tq - 1)
    valid = col * CMP_STRIDE + (CMP_BLOCK - 1) <= qpos
    s = jnp.where(valid, s, NEG_INF)
    m = jnp.max(s, axis=1, keepdims=True)
    e = jnp.where(valid, jnp.exp2(s - m), 0.0)
    l = jnp.sum(e, axis=1, keepdims=True)
    p = e / jnp.where(l > 0.0, l, 1.0)
    o4 = jnp.dot(p.astype(BF16), v_ref[...], preferred_element_type=F32)
    for h in range(GQA_GROUP):
        o_ref[:, h * HEAD_DIM:(h + 1) * HEAD_DIM] = o4[h * tq:(h + 1) * tq] * gate_ref[:, h:h + 1]
    imp = p[0:tq] + p[tq:2 * tq] + p[2 * tq:3 * tq] + p[3 * tq:4 * tq]
    hi = imp.astype(BF16)
    r1 = imp - hi.astype(F32)
    mid = r1.astype(BF16)
    lo = (r1 - mid.astype(F32)).astype(BF16)
    cov = cov_ref[...]
    score = (lax.dot_general(cov, hi, _NT, preferred_element_type=F32)
             + lax.dot_general(cov, mid, _NT, preferred_element_type=F32)
             + lax.dot_general(cov, lo, _NT, preferred_element_type=F32))
    jb = lax.broadcasted_iota(jnp.int32, score.shape, 0)
    qp = pos0 + qi * tq + lax.broadcasted_iota(jnp.int32, score.shape, 1)
    cur = jnp.right_shift(qp, SEL_BLOCK.bit_length() - 1)
    visible = jb <= cur
    forced = (jb == 0) | (visible & (jb > cur - N_LOCAL))
    val_ref[...] = jnp.where(forced, FORCE_SCORE, jnp.where(visible, score, -FORCE_SCORE))


def cmp_attention(q, kc, cover_t, gates_t, *, n_seq, seq, tq, pos0):
    nq = seq // tq
    ncp = kc.shape[2]
    nbv = cover_t.shape[0]
    gw = GQA_GROUP * HEAD_DIM
    kern = functools.partial(_cmp_kernel, tq=tq, pos0=pos0)
    if tq % LANES == 0:
        val_spec = pl.BlockSpec((nbv, tq), lambda b, g, i: (0, (b * N_KV_HEADS + g) * nq + i))
        val_shape = jax.ShapeDtypeStruct((nbv, n_seq * N_KV_HEADS * seq), F32)
    else:
        val_spec = pl.BlockSpec((None, None, nbv, tq), lambda b, g, i: (b, g, 0, i))
        val_shape = jax.ShapeDtypeStruct((n_seq, N_KV_HEADS, nbv, seq), F32)
    return pl.pallas_call(
        kern,
        grid=(n_seq, N_KV_HEADS, nq),
        in_specs=[
            pl.BlockSpec((tq, gw), lambda b, g, i: (b * nq + i, g)),
            pl.BlockSpec((None, None, ncp, HEAD_DIM), lambda b, g, i: (b, g, 0, 0)),
            pl.BlockSpec((None, None, ncp, HEAD_DIM), lambda b, g, i: (b, N_KV_HEADS + g, 0, 0)),
            pl.BlockSpec((nbv, ncp), lambda b, g, i: (0, 0)),
            pl.BlockSpec((None, None, tq, GQA_GROUP), lambda b, g, i: (0, g, b * nq + i, 0)),
        ],
        out_specs=[pl.BlockSpec((tq, gw), lambda b, g, i: (b * nq + i, g)), val_spec],
        out_shape=[jax.ShapeDtypeStruct((n_seq * seq, N_HEADS * HEAD_DIM), F32), val_shape],
        compiler_params=_params("parallel", "parallel", "parallel"),
        name="nsa_cmp_attention",
    )(q, kc, kc, cover_t, gates_t)


def _cover_t(n_chunks, n_blk, nbv):
    n_cmp = n_chunks - CMP_BLOCK // CMP_STRIDE + 1
    c0 = np.arange(n_chunks) * CMP_STRIDE
    j0 = np.arange(nbv) * SEL_BLOCK
    cov = (c0[None, :] < j0[:, None] + SEL_BLOCK) & (c0[None, :] + CMP_BLOCK > j0[:, None])
    cov &= (np.arange(n_chunks)[None, :] < n_cmp) & (np.arange(nbv)[:, None] < n_blk)
    return jnp.asarray(cov.astype(np.float32), dtype=BF16)


def _rank_kernel(val_ref, out_ref, *, n_blk, n_sel, causal_seq):
    val = val_ref[...]
    tc = val.shape[1]
    jb = lax.broadcasted_iota(jnp.int32, val.shape, 0)
    if causal_seq is None:
        n_iter = n_blk
    else:
        pos0 = (pl.program_id(0) % (causal_seq // tc)) * tc
        n_iter = jnp.minimum(n_blk, (pos0 + tc - 1) // SEL_BLOCK + 1)

    def body(i, rank):
        row = val_ref[pl.ds(i, 1), :]
        beats = (row > val) | ((row == val) & (i < jb))
        return rank + jnp.where(beats, 1, 0)

    rank = lax.fori_loop(0, n_iter, body, jnp.zeros(val.shape, jnp.int32))
    chosen = (rank < n_sel) & (val > -0.5 * FORCE_SCORE)
    out_ref[...] = jnp.where(chosen, 0.0, NEG_INF).astype(out_ref.dtype)


def select_bias(val, *, n_blk, tc, causal_seq=None):
    nbv, ncols = val.shape
    kern = functools.partial(_rank_kernel, n_blk=n_blk, n_sel=min(N_SEL, n_blk), causal_seq=causal_seq)
    return pl.pallas_call(
        kern,
        grid=(ncols // tc,),
        in_specs=[pl.BlockSpec((nbv, tc), lambda i: (0, i))],
        out_specs=pl.BlockSpec((nbv, tc), lambda i: (0, i)),
        out_shape=jax.ShapeDtypeStruct((nbv, ncols), BF16),
        compiler_params=_params("parallel"),
        name="nsa_select_rank",
    )(val)


_STEP_FIRST, _STEP_LAST, _STEP_EDGE = 1, 2, 4


def _flash_steps(seq, tq, tk):
    qi_l, kt_l, fl_l = [], [], []
    for qi in range(seq // tq):
        q0, q1 = qi * tq, qi * tq + tq - 1
        kts = [kt for kt in range(seq // tk) if kt * tk <= q1]
        for n, kt in enumerate(kts):
            all_visible = q0 - (kt * tk + tk - 1) >= 0
            flags = (_STEP_FIRST if n == 0 else 0) | (_STEP_LAST if n == len(kts) - 1 else 0)
            flags |= 0 if all_visible else _STEP_EDGE
            qi_l.append(qi)
            kt_l.append(kt)
            fl_l.append(flags)
    return (jnp.asarray(np.array(qi_l, np.int32)), jnp.asarray(np.array(kt_l, np.int32)),
            jnp.asarray(np.array(fl_l, np.int32)))


def _flash_kernel(qi_ref, kt_ref, flag_ref, qt_ref, bias_ref, k_ref, vt_ref, e_ref, gate_ref, prev_ref, o_ref,
                  qa_ref, m_ref, acc_ref, *, tq, tk, out_dtype):
    s = pl.program_id(2)
    qi = qi_ref[s]
    kt = kt_ref[s]
    flags = flag_ref[s]
    first = jnp.bitwise_and(flags, _STEP_FIRST) != 0
    last = jnp.bitwise_and(flags, _STEP_LAST) != 0
    edge = jnp.bitwise_and(flags, _STEP_EDGE) != 0

    @pl.when(first)
    def _():
        nbv = bias_ref.shape[0]
        for h in range(GQA_GROUP):
            qa_ref[0:HEAD_DIM, h * tq:(h + 1) * tq] = qt_ref[h * HEAD_DIM:(h + 1) * HEAD_DIM, :]
            qa_ref[HEAD_DIM:HEAD_DIM + nbv, h * tq:(h + 1) * tq] = bias_ref[...]
            if HEAD_DIM + nbv < qa_ref.shape[0]:
                qa_ref[HEAD_DIM + nbv:, h * tq:(h + 1) * tq] = jnp.zeros((qa_ref.shape[0] - HEAD_DIM - nbv, tq), BF16)
        m_ref[...] = jnp.full(m_ref.shape, NEG_INF, F32)
        acc_ref[...] = jnp.zeros(acc_ref.shape, F32)

    def step(masked):
        ka = jnp.concatenate([k_ref[...], e_ref[...]], axis=1)
        st = jnp.dot(ka, qa_ref[...], preferred_element_type=F32)
        if masked:
            kpos = kt * tk + lax.broadcasted_iota(jnp.int32, st.shape, 0)
            qpos = qi * tq + jnp.bitwise_and(lax.broadcasted_iota(jnp.int32, st.shape, 1), tq - 1)
            st = jnp.where(qpos >= kpos, st, NEG_INF)
        m_old = m_ref[...]
        m_new = jnp.maximum(m_old, jnp.max(st, axis=0, keepdims=True))
        alpha = jnp.exp2(m_old - m_new)
        p = jnp.exp2(st - m_new).astype(BF16)
        va = jnp.concatenate([vt_ref[...], jnp.ones((acc_ref.shape[0] - HEAD_DIM, tk), BF16)], axis=0)
        acc_ref[...] = alpha * acc_ref[...] + jnp.dot(va, p, preferred_element_type=F32)
        m_ref[...] = m_new

    @pl.when(edge)
    def _():
        step(True)

    @pl.when(jnp.logical_not(edge))
    def _():
        step(False)

    @pl.when(last)
    def _():
        acc = acc_ref[...]
        ot = acc[:HEAD_DIM] / acc[HEAD_DIM:HEAD_DIM + 1]
        for h in range(GQA_GROUP):
            sl = slice(h * HEAD_DIM, (h + 1) * HEAD_DIM)
            o_h = ot[:, h * tq:(h + 1) * tq].T
            o_ref[:, sl] = (prev_ref[:, sl] + o_h * gate_ref[:, h:h + 1]).astype(out_dtype)


def selected_attention(qt, k, vt, branch, gates_t, prev, bias, onehot, *, n_seq, seq, tq, tk, out_dtype=F32):
    nq = seq // tq
    nkt = seq // tk
    gw = GQA_GROUP * HEAD_DIM
    qi_tab, kt_tab, flag_tab = _flash_steps(seq, tq, tk)
    nbv = bias.shape[0]
    nbp = onehot.shape[1]

    o_spec = pl.BlockSpec((tq, gw), lambda b, g, s, qi, kt, fl: (b * nq + qi[s], g))
    qt_spec = pl.BlockSpec((gw, tq), lambda b, g, s, qi, kt, fl: (g, b * nq + qi[s]))
    k_spec = pl.BlockSpec((None, None, tk, HEAD_DIM), lambda b, g, s, qi, kt, fl: (branch, g, b * nkt + kt[s], 0))
    vt_spec = pl.BlockSpec((None, None, HEAD_DIM, tk), lambda b, g, s, qi, kt, fl: (branch, g, 0, b * nkt + kt[s]))
    gate_spec = pl.BlockSpec((None, None, tq, GQA_GROUP), lambda b, g, s, qi, kt, fl: (branch, g, b * nq + qi[s], 0))
    bias_spec = pl.BlockSpec((nbv, tq), lambda b, g, s, qi, kt, fl: (0, (b * N_KV_HEADS + g) * nq + qi[s]))
    onehot_spec = pl.BlockSpec((tk, nbp), lambda b, g, s, qi, kt, fl: (kt[s], 0))
    in_specs = [qt_spec, bias_spec, k_spec, vt_spec, onehot_spec, gate_spec, o_spec]
    args = [qt, bias, k, vt, onehot, gates_t, prev]
    qa_rows = HEAD_DIM + nbp
    kern = functools.partial(_flash_kernel, tq=tq, tk=tk, out_dtype=out_dtype)
    cols = GQA_GROUP * tq
    grid_spec = pltpu.PrefetchScalarGridSpec(
        num_scalar_prefetch=3,
        grid=(n_seq, N_KV_HEADS, qi_tab.shape[0]),
        in_specs=in_specs,
        out_specs=o_spec,
        scratch_shapes=[
            pltpu.VMEM((qa_rows, cols), BF16),
            pltpu.VMEM((1, cols), F32),
            pltpu.VMEM((HEAD_DIM + 2 * SUBLANES, cols), F32),
        ],
    )
    return pl.pallas_call(
        kern,
        grid_spec=grid_spec,
        out_shape=jax.ShapeDtypeStruct((n_seq * seq, N_HEADS * HEAD_DIM), out_dtype),
        compiler_params=_params("parallel", "parallel", "arbitrary"),
        name="nsa_sel_attention",
    )(qi_tab, kt_tab, flag_tab, *args)


def _win_kernel(*refs, tq, n_tiles, window, out_dtype):
    qt_ref = refs[0]
    k_refs = refs[1:1 + n_tiles]
    vt_refs = refs[1 + n_tiles:1 + 2 * n_tiles]
    gate_ref, prev_ref, o_ref = refs[1 + 2 * n_tiles:]
    qi = pl.program_id(2)
    qa = jnp.concatenate([qt_ref[h * HEAD_DIM:(h + 1) * HEAD_DIM, :] for h in range(GQA_GROUP)], axis=1)
    sts = []
    for j in range(n_tiles):
        kt = qi - (n_tiles - 1) + j
        st = jnp.dot(k_refs[j][...], qa, preferred_element_type=F32)
        if j == 0 or j == n_tiles - 1:
            kpos = kt * tq + lax.broadcasted_iota(jnp.int32, st.shape, 0)
            qpos = qi * tq + jnp.bitwise_and(lax.broadcasted_iota(jnp.int32, st.shape, 1), tq - 1)
            d = qpos - kpos
            ok = (d >= 0) & (d < window) & (kpos >= 0)
            st = jnp.where(ok, st, NEG_INF)
        else:
            st = jnp.where(kt >= 0, st, NEG_INF)
        sts.append(st)
    st = jnp.concatenate(sts, axis=0)
    m = jnp.max(st, axis=0, keepdims=True)
    p = jnp.exp2(st - m).astype(BF16)
    vt = jnp.concatenate([r[...] for r in vt_refs], axis=1)
    va = jnp.concatenate([vt, jnp.ones((2 * SUBLANES, vt.shape[1]), BF16)], axis=0)
    acc = jnp.dot(va, p, preferred_element_type=F32)
    ot = acc[:HEAD_DIM] / acc[HEAD_DIM:HEAD_DIM + 1]
    for h in range(GQA_GROUP):
        sl = slice(h * HEAD_DIM, (h + 1) * HEAD_DIM)
        o_h = ot[:, h * tq:(h + 1) * tq].T
        o_ref[:, sl] = (prev_ref[:, sl] + o_h * gate_ref[:, h:h + 1]).astype(out_dtype)


def window_attention(qt, k, vt, branch, gates_t, prev, *, n_seq, seq, tq, window, out_dtype):
    nq = seq // tq
    n_tiles = window // tq + 1
    gw = GQA_GROUP * HEAD_DIM

    def kt_of(i, j):
        return jnp.maximum(i - (n_tiles - 1) + j, 0)

    o_spec = pl.BlockSpec((tq, gw), lambda b, g, i: (b * nq + i, g))
    in_specs = [pl.BlockSpec((gw, tq), lambda b, g, i: (g, b * nq + i))]
    in_specs += [pl.BlockSpec((None, None, tq, HEAD_DIM), lambda b, g, i, j=j: (branch, g, b * nq + kt_of(i, j), 0))
                 for j in range(n_tiles)]
    in_specs += [pl.BlockSpec((None, None, HEAD_DIM, tq), lambda b, g, i, j=j: (branch, g, 0, b * nq + kt_of(i, j)))
                 for j in range(n_tiles)]
    in_specs += [pl.BlockSpec((None, None, tq, GQA_GROUP), lambda b, g, i: (branch, g, b * nq + i, 0)), o_spec]
    kern = functools.partial(_win_kernel, tq=tq, n_tiles=n_tiles, window=window, out_dtype=out_dtype)
    return pl.pallas_call(
        kern,
        grid=(n_seq, N_KV_HEADS, nq),
        in_specs=in_specs,
        out_specs=o_spec,
        out_shape=jax.ShapeDtypeStruct((n_seq * seq, N_HEADS * HEAD_DIM), out_dtype),
        compiler_params=_params("parallel", "parallel", "parallel"),
        name="nsa_win_attention",
    )(qt, *([k] * n_tiles), *([vt] * n_tiles), gates_t, prev)


def _block_onehot(n_keys, nbp):
    e = (np.arange(n_keys)[:, None] // SEL_BLOCK) == np.arange(nbp)[None, :]
    return jnp.asarray(e.astype(np.float32), dtype=BF16)


def _sel_sample_kernel(*refs, n_pages, nbp, past_len, nsteps, tq):
    logical_ref, count_ref = refs[1:3]
    qt_ref, bias_ref, new_ref = refs[3:6]
    page_refs = refs[6:6 + n_pages]
    gate_ref, prev_ref, o_ref, m_ref, acc_ref = refs[6 + n_pages:]
    b = pl.program_id(0)
    step = pl.program_id(1)
    page_rows = page_refs[0].shape[0]
    rows_per_tok = 2 * N_KV_HEADS
    page_tokens = page_rows // rows_per_tok
    blocks_per_page = page_tokens // SEL_BLOCK
    n_needed = count_ref[b]
    cols = qt_ref.shape[2]

    def scores(g, k, key_bias):
        return jnp.dot(k, qt_ref[g], preferred_element_type=F32) + key_bias

    def block_bias(g, first_block, n_blocks, valid):
        rows = bias_ref[g, pl.ds(first_block, n_blocks), :]
        rows = jnp.where(valid, rows, NEG_INF)
        return jnp.concatenate(
            [jnp.broadcast_to(rows[j:j + 1], (SEL_BLOCK, cols)) for j in range(n_blocks)], axis=0)

    def weighted_values(p, v):
        va = jnp.concatenate([v, jnp.ones(v.shape, BF16)], axis=1)
        return lax.dot_general(p, va, _TN, preferred_element_type=F32)

    @pl.when(step == 0)
    def _():
        for g in range(N_KV_HEADS):
            kn = new_ref[pl.ds(g, tq, stride=2 * N_KV_HEADS), :].astype(BF16)
            vn = new_ref[pl.ds(N_KV_HEADS + g, tq, stride=2 * N_KV_HEADS), :].astype(BF16)
            new_bias = block_bias(g, past_len // SEL_BLOCK, 1, True)[:tq]
            st = scores(g, kn, new_bias)
            key = lax.broadcasted_iota(jnp.int32, st.shape, 0)
            tok = jnp.bitwise_and(lax.broadcasted_iota(jnp.int32, st.shape, 1), tq - 1)
            st = jnp.where(key <= tok, st, NEG_INF)
            m = jnp.max(st, axis=0, keepdims=True)
            p = jnp.exp2(st - m).astype(BF16)
            m_ref[g] = m
            acc_ref[g] = weighted_values(p, vn)

    @pl.when(step * n_pages < n_needed)
    def _():
        slots = [step * n_pages + p for p in range(n_pages)]
        first_blocks = [logical_ref[b, slot] * blocks_per_page for slot in slots]
        for g in range(N_KV_HEADS):
            k = jnp.concatenate(
                [page_refs[p][pl.ds(g, page_tokens, stride=rows_per_tok), :] for p in range(n_pages)],
                axis=0).astype(BF16)
            v = jnp.concatenate(
                [page_refs[p][pl.ds(N_KV_HEADS + g, page_tokens, stride=rows_per_tok), :] for p in range(n_pages)],
                axis=0).astype(BF16)
            key_bias = jnp.concatenate(
                [block_bias(g, first_blocks[p], blocks_per_page, slots[p] < n_needed) for p in range(n_pages)],
                axis=0)
            st = scores(g, k, key_bias)
            m_old = m_ref[g]
            m_new = jnp.maximum(m_old, jnp.max(st, axis=0, keepdims=True))
            alpha = jnp.exp2(m_old - m_new)
            p = jnp.exp2(st - m_new).astype(BF16)
            n_cols = alpha.shape[1]
            eye = (lax.broadcasted_iota(jnp.int32, (n_cols, n_cols), 0)
                   == lax.broadcasted_iota(jnp.int32, (n_cols, n_cols), 1))
            alpha_col = jnp.sum(jnp.where(eye, alpha, 0.0), axis=1, keepdims=True)
            acc_ref[g] = alpha_col * acc_ref[g] + weighted_values(p, v)
            m_ref[g] = m_new

    @pl.when(step == nsteps - 1)
    def _():
        for g in range(N_KV_HEADS):
            acc = acc_ref[g]
            o4 = acc[:, :HEAD_DIM] / acc[:, HEAD_DIM:HEAD_DIM + 1]
            for h in range(GQA_GROUP):
                hh = g * GQA_GROUP + h
                sl = slice(hh * HEAD_DIM, (hh + 1) * HEAD_DIM)
                o_ref[:, sl] = prev_ref[:, sl] + o4[h * tq:(h + 1) * tq] * gate_ref[g, :, h:h + 1]


def sel_attention_sample(q, bias, needed, kv_new, cache, layer, page_table, gates_t, prev, *, n_pages_step,
                         past_len):
    n_seq, pages = page_table.shape
    tq = SAMPLE_PAD
    nbp = bias.shape[-1]
    page_rows = cache.shape[2]
    nsteps = pages // n_pages_step
    dm = N_HEADS * HEAD_DIM
    order = jnp.argsort(jnp.logical_not(needed), axis=1, stable=True).astype(jnp.int32)
    count = jnp.sum(needed, axis=1).astype(jnp.int32)
    slot = jnp.minimum(jnp.arange(pages, dtype=jnp.int32)[None, :], count[:, None] - 1)
    logical = jnp.take_along_axis(order, slot, axis=1)
    physical = jnp.take_along_axis(page_table, logical, axis=1)
    row_spec = pl.BlockSpec((tq, dm), lambda b, s, *_: (b, 0))
    assert past_len % SEL_BLOCK + tq <= SEL_BLOCK
    cols = GQA_GROUP * tq
    q_t = q.reshape(n_seq, tq, N_KV_HEADS, GQA_GROUP, HEAD_DIM).transpose(0, 2, 4, 3, 1)
    q_t = q_t.reshape(n_seq, N_KV_HEADS, HEAD_DIM, cols)
    bias_t = jnp.tile(bias.transpose(0, 1, 3, 2), (1, 1, 1, GQA_GROUP)).astype(F32)

    def page_map(p):
        return lambda b, s, phys, *_: (layer, phys[b, s * n_pages_step + p], 0, 0)

    in_specs = [
        pl.BlockSpec((None, N_KV_HEADS, HEAD_DIM, cols), lambda b, s, *_: (b, 0, 0, 0)),
        pl.BlockSpec((None, N_KV_HEADS, nbp, cols), lambda b, s, *_: (b, 0, 0, 0)),
        pl.BlockSpec((tq * 2 * N_KV_HEADS, HEAD_DIM), lambda b, s, *_: (b, 0)),
    ]
    in_specs += [pl.BlockSpec((None, None, page_rows, HEAD_DIM), page_map(p)) for p in range(n_pages_step)]
    in_specs += [
        pl.BlockSpec((None, N_KV_HEADS, tq, GQA_GROUP), lambda b, s, *_: (1, 0, b, 0)),
        row_spec,
    ]
    kern = functools.partial(_sel_sample_kernel, n_pages=n_pages_step, nbp=nbp, past_len=past_len, nsteps=nsteps,
                             tq=tq)
    grid_spec = pltpu.PrefetchScalarGridSpec(
        num_scalar_prefetch=3,
        grid=(n_seq, nsteps),
        in_specs=in_specs,
        out_specs=row_spec,
        scratch_shapes=[
            pltpu.VMEM((N_KV_HEADS, 1, cols), F32),
            pltpu.VMEM((N_KV_HEADS, cols, 2 * HEAD_DIM), F32),
        ],
    )
    return pl.pallas_call(
        kern,
        grid_spec=grid_spec,
        out_shape=jax.ShapeDtypeStruct((n_seq * tq, dm), F32),
        compiler_params=_params("parallel", "arbitrary"),
        name="nsa_sel_attention_sample",
    )(physical, logical, count, q_t, bias_t, kv_new, *([cache] * n_pages_step), gates_t, prev)


def _win_sample_kernel(q_ref, buf_ref, new_ref, gate_ref, prev_ref, o_ref, *, tq, out_dtype):
    rows_per_tok = 2 * N_KV_HEADS
    w_buf = buf_ref.shape[0] // rows_per_tok
    gw = GQA_GROUP * HEAD_DIM
    for g in range(N_KV_HEADS):
        q4 = _group_rows(q_ref[:, g * gw:(g + 1) * gw])
        kb = buf_ref[pl.ds(g, w_buf, stride=rows_per_tok), :].astype(BF16)
        vb = buf_ref[pl.ds(N_KV_HEADS + g, w_buf, stride=rows_per_tok), :].astype(BF16)
        kn = new_ref[pl.ds(g, tq, stride=rows_per_tok), :].astype(BF16)
        vn = new_ref[pl.ds(N_KV_HEADS + g, tq, stride=rows_per_tok), :].astype(BF16)
        sb = lax.dot_general(q4, kb, _NT, preferred_element_type=F32)
        sn = lax.dot_general(q4, kn, _NT, preferred_element_type=F32)
        tb = jnp.bitwise_and(lax.broadcasted_iota(jnp.int32, sb.shape, 0), tq - 1)
        rb = lax.broadcasted_iota(jnp.int32, sb.shape, 1)
        sb = jnp.where(tb + w_buf - rb < WINDOW, sb, NEG_INF)
        tn = jnp.bitwise_and(lax.broadcasted_iota(jnp.int32, sn.shape, 0), tq - 1)
        rn = lax.broadcasted_iota(jnp.int32, sn.shape, 1)
        sn = jnp.where(rn <= tn, sn, NEG_INF)
        m = jnp.maximum(jnp.max(sb, axis=1, keepdims=True), jnp.max(sn, axis=1, keepdims=True))
        pb = jnp.exp2(sb - m)
        pn = jnp.exp2(sn - m)
        l = jnp.sum(pb, axis=1, keepdims=True) + jnp.sum(pn, axis=1, keepdims=True)
        o4 = (jnp.dot(pb.astype(BF16), vb, preferred_element_type=F32)
              + jnp.dot(pn.astype(BF16), vn, preferred_element_type=F32)) / l
        for h in range(GQA_GROUP):
            hh = g * GQA_GROUP + h
            sl = slice(hh * HEAD_DIM, (hh + 1) * HEAD_DIM)
            o_ref[:, sl] = (prev_ref[:, sl] + o4[h * tq:(h + 1) * tq] * gate_ref[g, :, h:h + 1]).astype(out_dtype)


def win_attention_sample(q, win_buf, layer, kv_new, gates_t, prev, *, out_dtype):
    n_seq, buf_rows = win_buf.shape[1], win_buf.shape[2]
    tq = SAMPLE_PAD
    q_spec = pl.BlockSpec((tq, N_HEADS * HEAD_DIM), lambda b: (b, 0))
    kern = functools.partial(_win_sample_kernel, tq=tq, out_dtype=out_dtype)
    return pl.pallas_call(
        kern,
        grid=(n_seq,),
        in_specs=[
            q_spec,
            pl.BlockSpec((None, None, buf_rows, HEAD_DIM), lambda b: (layer, b, 0, 0)),
            pl.BlockSpec((tq * 2 * N_KV_HEADS, HEAD_DIM), lambda b: (b, 0)),
            pl.BlockSpec((None, N_KV_HEADS, tq, GQA_GROUP), lambda b: (2, 0, b, 0)),
            q_spec,
        ],
        out_specs=q_spec,
        out_shape=jax.ShapeDtypeStruct((n_seq * tq, N_HEADS * HEAD_DIM), out_dtype),
        compiler_params=_params("parallel"),
        name="nsa_win_attention_sample",
    )(q, win_buf, kv_new, gates_t, prev)


def _ret_kernel(q_ref, k_ref, v_ref, g_ref, gn_ref, dec_ref, qd_ref, kd_ref, gc_ref, s0_ref, o_ref, sout_ref,
                state_ref, *, n_chunks):
    c = pl.program_id(2)

    @pl.when(c == 0)
    def _():
        state_ref[...] = s0_ref[...]

    for h in range(state_ref.shape[0]):
        ksl = slice(h * RET_KDIM, (h + 1) * RET_KDIM)
        vsl = slice(h * RET_VDIM, (h + 1) * RET_VDIM)
        q = q_ref[:, ksl]
        kf = k_ref[:, ksl]
        v = v_ref[:, vsl]
        state = state_ref[h]
        s = lax.dot_general(q, kf.astype(BF16), _NT, preferred_element_type=F32) * dec_ref[h]
        inner = jnp.dot(s.astype(BF16), v, preferred_element_type=F32)
        cross = jnp.dot(q, state.astype(BF16), preferred_element_type=F32) * qd_ref[h]
        o = inner + cross
        kd = (kf * kd_ref[h]).astype(BF16)
        state_ref[h] = gc_ref[h] * state + lax.dot_general(kd, v, _TN, preferred_element_type=F32)
        mu = jnp.mean(o, axis=-1, keepdims=True)
        d = o - mu
        var = jnp.mean(d * d, axis=-1, keepdims=True)
        on = d * lax.rsqrt(var + GN_EPS) * gn_ref[:, vsl]
        g = g_ref[:, vsl]
        o_ref[:, vsl] = (g * jax.nn.sigmoid(g) * on).astype(o_ref.dtype)

    @pl.when(c == n_chunks - 1)
    def _():
        sout_ref[...] = state_ref[...]


def _ret_tables(chunk, n_tok):
    log_g = jnp.log(1.0 - 2.0 ** (-5.0 - jnp.arange(RET_HEADS, dtype=F32)))
    i = jnp.arange(chunk, dtype=F32)
    rel = i[:, None] - i[None, :]
    decay = jnp.where(rel >= 0, jnp.exp(jnp.maximum(rel, 0.0)[None] * log_g[:, None, None]), 0.0)
    q_dec = jnp.exp((i[None, :] + 1.0) * log_g[:, None])[..., None]
    k_dec = jnp.where(i[None, :] < n_tok, jnp.exp((n_tok - 1.0 - i)[None, :] * log_g[:, None]), 0.0)[..., None]
    g_c = jnp.exp(n_tok * log_g)[:, None, None]
    return decay, q_dec, k_dec, g_c


def retention(q, k, v, g, gn, layer, state0, *, n_seq, seq, chunk, n_tok):
    nc = seq // chunk
    hps = RET_HEADS_PER_STEP
    decay, q_dec, k_dec, g_c = _ret_tables(chunk, n_tok)
    kern = functools.partial(_ret_kernel, n_chunks=nc)
    tok_k = pl.BlockSpec((chunk, hps * RET_KDIM), lambda b, h, c: (b * nc + c, h))
    tok_v = pl.BlockSpec((chunk, hps * RET_VDIM), lambda b, h, c: (b * nc + c, h))
    st_spec = pl.BlockSpec((None, hps, RET_KDIM, RET_VDIM), lambda b, h, c: (b, h, 0, 0))
    return pl.pallas_call(
        kern,
        grid=(n_seq, RET_HEADS // hps, nc),
        in_specs=[
            tok_k, tok_k, tok_v, tok_v,
            pl.BlockSpec((None, 1, hps * RET_VDIM), lambda b, h, c: (layer, 0, h)),
            pl.BlockSpec((hps, chunk, chunk), lambda b, h, c: (h, 0, 0)),
            pl.BlockSpec((hps, chunk, 1), lambda b, h, c: (h, 0, 0)),
            pl.BlockSpec((hps, chunk, 1), lambda b, h, c: (h, 0, 0)),
            pl.BlockSpec((hps, 1, 1), lambda b, h, c: (h, 0, 0)),
            st_spec,
        ],
        out_specs=[tok_v, st_spec],
        out_shape=[
            jax.ShapeDtypeStruct((n_seq * seq, RET_HEADS * RET_VDIM), BF16),
            jax.ShapeDtypeStruct((n_seq, RET_HEADS, RET_KDIM, RET_VDIM), F32),
        ],
        scratch_shapes=[pltpu.VMEM((hps, RET_KDIM, RET_VDIM), F32)],
        compiler_params=_params("parallel", "parallel", "arbitrary"),
        name="retention_chunk",
    )(q, k, v, g, gn, decay, q_dec, k_dec, g_c, state0)


def _project_nsa(xn, w_in, layer, cos, sin_signed, *, tm):
    m, d = xn.shape
    nq = N_HEADS * HEAD_DIM
    half = N_KV_HEADS * HEAD_DIM
    rows_per_tok = 2 * N_KV_HEADS
    tabs_per_seq = cos.shape[0] // tm
    tab_spec = pl.BlockSpec((tm, HEAD_DIM), lambda i, j, k: (i % tabs_per_seq, 0))
    tn_q = 512
    q = matmul(
        [xn], w_in, layer, col_block0=0, n_cols=nq, tm=tm, tn=tn_q, tk=d,
        extras=[cos, sin_signed], extra_specs=[tab_spec, tab_spec],
        out_shapes=[jax.ShapeDtypeStruct((m, nq), BF16), jax.ShapeDtypeStruct((nq, m), BF16)],
        out_specs=[pl.BlockSpec((tm, tn_q), lambda i, j, k: (i, j)), pl.BlockSpec((tn_q, tm), lambda i, j, k: (j, i))],
        epilogue=functools.partial(_ep_rope_q, scale=HEAD_DIM ** -0.5 * LOG2_E), name="nsa_proj_q")
    kv = matmul(
        [xn], w_in, layer, col_block0=nq // half, n_cols=3 * KV_ROW, tm=tm, tn=half, tk=d,
        extras=[cos, sin_signed], extra_specs=[tab_spec, tab_spec],
        out_shapes=[jax.ShapeDtypeStruct((m * rows_per_tok, HEAD_DIM), F32)] * 3
        + [jax.ShapeDtypeStruct((3, N_KV_HEADS, m, HEAD_DIM), BF16),
           jax.ShapeDtypeStruct((3, N_KV_HEADS, HEAD_DIM, m), BF16)],
        out_specs=[pl.BlockSpec((tm * rows_per_tok, HEAD_DIM), lambda i, j, k: (i, 0))] * 3
        + [pl.BlockSpec((None, N_KV_HEADS, tm, HEAD_DIM), lambda i, j, k: (j // 2, 0, i, 0)),
           pl.BlockSpec((None, N_KV_HEADS, HEAD_DIM, tm), lambda i, j, k: (j // 2, 0, 0, i))],
        epilogue=_ep_nsa_kv, name="nsa_proj_kv", column_steps_ordered=True)
    kv = (kv[:3], kv[3], kv[4])
    n_gate = 3 * N_HEADS
    w_gate = jnp.pad(w_in[layer, :, nq + 3 * KV_ROW:], ((0, 0), (0, LANES - n_gate)))[None]
    gates_t = matmul(
        [xn], w_gate, 0, col_block0=0, n_cols=LANES, tm=tm, tn=LANES, tk=d, extras=[], extra_specs=[],
        out_shapes=[jax.ShapeDtypeStruct((3, N_KV_HEADS, m, GQA_GROUP), F32)],
        out_specs=[pl.BlockSpec((3, N_KV_HEADS, tm, GQA_GROUP), lambda i, j, k: (0, 0, i, 0))],
        epilogue=_ep_gates, name="nsa_proj_gate")[0]
    return q, kv, gates_t


def _cmp_bias(w1, pos, b1, layer):
    k_full = CMP_BLOCK * HEAD_DIM
    pos_rows = jnp.zeros((2, SUBLANES, k_full), F32).at[:, 0].set(pos[layer].reshape(2, k_full))
    terms = []
    for e in range(2):
        t = matmul(
            [pos_rows[e]], w1.reshape(-1, k_full, HEAD_DIM), layer * 2 + e, col_block0=0, n_cols=HEAD_DIM,
            tm=SUBLANES, tn=HEAD_DIM, tk=k_full, extras=[], extra_specs=[],
            out_shapes=[jax.ShapeDtypeStruct((SUBLANES, HEAD_DIM), F32)],
            out_specs=[pl.BlockSpec((SUBLANES, HEAD_DIM), lambda i, j, k: (0, 0))],
            epilogue=_ep_cast, name="nsa_cmp_pos_term")[0]
        terms.append(t[0])
    return jnp.stack(terms) + b1[layer]


def _mlp_up(xn, w1, layer, *, tm):
    m, d = xn.shape
    d_ff = w1.shape[-1]
    tn = 1024
    return matmul(
        [xn], w1, layer, col_block0=0, n_cols=d_ff, tm=tm, tn=tn, tk=d, extras=[], extra_specs=[],
        out_shapes=[jax.ShapeDtypeStruct((m, d_ff), BF16)],
        out_specs=[pl.BlockSpec((tm, tn), lambda i, j, k: (i, j))],
        epilogue=_ep_relu2, name="mlp_up")[0]


def _mixer_out_proj(x, w, layer, h, norm_g, *, name):
    m, d = h.shape
    kdim = x.shape[1]
    w_bytes = 2 * kdim * d * w.dtype.itemsize
    per_row = 2 * kdim * x.dtype.itemsize + d * (4 * 4 + 2 * 2 + 4)
    tm = min(m, (VMEM_LIMIT_BYTES * 7 // 8 - w_bytes) // per_row // LANES * LANES)
    while m % tm:
        tm -= LANES
    row = pl.BlockSpec((tm, d), lambda i, j, k: (i, 0))
    return matmul(
        [x], w, layer, col_block0=0, n_cols=d, tm=tm, tn=d, tk=kdim,
        extras=[h, norm_g.reshape(1, d)], extra_specs=[row, pl.BlockSpec((1, d), lambda i, j, k: (0, 0))],
        out_shapes=[jax.ShapeDtypeStruct((m, d), F32), jax.ShapeDtypeStruct((m, d), BF16)], out_specs=[row, row],
        epilogue=_ep_residual_norm, name=name)


def _mlp_down(a, w2, layer, h, *, tm):
    m, d = h.shape
    tn = 1024
    tm = min(tm, 1024)
    tk = 2048 if m >= tm * 2 else a.shape[1]
    spec = pl.BlockSpec((tm, tn), lambda i, j, k: (i, j))
    return matmul(
        [a], w2, layer, col_block0=0, n_cols=d, tm=tm, tn=tn, tk=tk, extras=[h], extra_specs=[spec],
        out_shapes=[jax.ShapeDtypeStruct((m, d), F32)], out_specs=[spec], epilogue=_ep_residual_norm,
        name="mlp_down")[0]


def _nsa_tables(pos):
    cos, sin = _rope_tables(pos, HEAD_DIM // 2)
    return jnp.concatenate([cos, cos], axis=1), jnp.concatenate([-sin, sin], axis=1)


def _nsa_prompt(xn, layer, w_in, cmp_w1, cmp_bias, cmp_w2, *, n_seq, seq):
    tm = PROMPT_ROW_TILE
    cos, sin_signed = _nsa_tables(jnp.arange(seq))
    (q, qt), (kv_rows, k_bf, vt_bf), gates_t = _project_nsa(xn, w_in, layer, cos, sin_signed, tm=tm // 2)
    n_chunks = seq // CMP_STRIDE
    chunks_per_step = 128
    kv_chunks = kv_rows[0].reshape(1, (n_seq * n_chunks) // chunks_per_step, chunks_per_step, CMP_STRIDE,
                                   2 * N_KV_HEADS, HEAD_DIM)
    kc = compress(kv_chunks, 0, None, cmp_w1, cmp_bias, cmp_w2, n_seq=n_seq, n_chunks=n_chunks, n_pages=1)
    n_blk = seq // SEL_BLOCK
    cover_t = _cover_t(n_chunks, n_blk, n_blk)
    o1, val = cmp_attention(q, kc, cover_t, gates_t, n_seq=n_seq, seq=seq, tq=1024, pos0=0)
    bias_t = select_bias(val, n_blk=n_blk, tc=512, causal_seq=seq)
    nbp = pl.cdiv(n_blk, LANES) * LANES
    o2 = selected_attention(qt, k_bf, vt_bf, 1, gates_t, o1, bias_t, _block_onehot(seq, nbp), n_seq=n_seq, seq=seq,
                            tq=512, tk=512)
    o3 = window_attention(qt, k_bf, vt_bf, 2, gates_t, o2, n_seq=n_seq, seq=seq, tq=256, window=WINDOW,
                          out_dtype=BF16)
    return o3, kv_rows


def _nsa_sample(xn, layer, w_in, cmp_w1, cmp_bias, cmp_w2, cache_cmp, cache_sel, win_buf, page_table,
                *, n_seq, n_tok, past_len):
    tq = SAMPLE_PAD
    m = n_seq * tq
    cos, sin_signed = _nsa_tables(jnp.tile(past_len + jnp.arange(tq), n_seq))
    (q, _), (kv_rows, _, _), gates_t = _project_nsa(xn, w_in, layer, cos, sin_signed, tm=m)
    n_chunks = past_len // CMP_STRIDE
    kc = compress(cache_cmp, layer, page_table, cmp_w1, cmp_bias, cmp_w2, n_seq=n_seq, n_chunks=n_chunks,
                  n_pages=32)
    n_blk = pl.cdiv(past_len + n_tok, SEL_BLOCK)
    nbv = pl.cdiv(n_blk, SUBLANES) * SUBLANES
    cover_t = _cover_t(n_chunks, n_blk, nbv)
    o1, val = cmp_attention(q, kc, cover_t, gates_t, n_seq=n_seq, seq=tq, tq=tq, pos0=past_len)
    ncols = n_seq * N_KV_HEADS * n_tok
    bias_t = select_bias(val[..., :n_tok].transpose(2, 0, 1, 3).reshape(nbv, ncols), n_blk=n_blk, tc=ncols)
    blocks_per_page = cache_sel.shape[2] // (2 * N_KV_HEADS * SEL_BLOCK)
    n_pages = page_table.shape[1]
    chosen = (bias_t == 0).reshape(nbv, n_seq, N_KV_HEADS * n_tok).any(axis=2)
    needed = chosen[:n_pages * blocks_per_page].reshape(n_pages, blocks_per_page, n_seq).any(axis=1).T
    nbp = pl.cdiv(n_blk + 1, LANES) * LANES
    bias = bias_t.reshape(nbv, n_seq, N_KV_HEADS, n_tok).transpose(1, 2, 3, 0)
    bias = jnp.pad(bias, ((0, 0), (0, 0), (0, 0), (0, nbp - nbv)), constant_values=NEG_INF)
    bias = jnp.pad(bias, ((0, 0), (0, 0), (0, tq - n_tok), (0, 0)))
    o2 = sel_attention_sample(q, bias, needed, kv_rows[1], cache_sel, layer, page_table, gates_t, o1,
                              n_pages_step=32, past_len=past_len)
    o3 = win_attention_sample(q, win_buf, layer, kv_rows[2], gates_t, o2, out_dtype=BF16)
    return o3, kv_rows


def _ret_layer(xn, layer, w_in, gn, state0, pos, *, n_seq, seq, chunk, n_tok, tm):
    m, d = xn.shape
    nk = RET_HEADS * RET_KDIM
    nv = RET_HEADS * RET_VDIM
    cos, sin = _rope_tables(pos, RET_KDIM // 2)
    tabs_per_seq = cos.shape[0] // tm
    tab_spec = pl.BlockSpec((tm, LANES), lambda i, j, k: (i % tabs_per_seq, 0))
    tn = 512

    def proj(col0, n_cols, dtype, epilogue, extras, extra_specs, name):
        return matmul(
            [xn], w_in, layer, col_block0=col0 // tn, n_cols=n_cols, tm=tm, tn=tn, tk=d,
            extras=extras, extra_specs=extra_specs,
            out_shapes=[jax.ShapeDtypeStruct((m, n_cols), dtype)],
            out_specs=[pl.BlockSpec((tm, tn), lambda i, j, k: (i, j))],
            epilogue=epilogue, name=name)[0]

    q = proj(0, nk, BF16, functools.partial(_ep_rope_ret, scale=1.0), [cos, sin], [tab_spec, tab_spec], "ret_proj_q")
    k = proj(nk, nk, F32, functools.partial(_ep_rope_ret, scale=RET_KDIM ** -0.5), [cos, sin],
             [tab_spec, tab_spec], "ret_proj_k")
    v = proj(2 * nk, nv, BF16, _ep_cast, [], [], "ret_proj_v")
    g = proj(2 * nk + nv, nv, F32, _ep_cast, [], [], "ret_proj_g")
    return retention(q, k, v, g, gn.reshape(gn.shape[0], 1, nv), layer, state0, n_seq=n_seq, seq=seq, chunk=chunk,
                     n_tok=n_tok)


def kernel(x_prompt, x_sample, cache_cmp_kv, cache_sel_kv, state_win_kv, state_ret, page_table, norm_mix, norm_ffn,
           norm_final, nsa_w_in, nsa_w_out, nsa_cmp_pos, nsa_cmp_w1, nsa_cmp_b1, nsa_cmp_w2, ret_w_in, ret_gn,
           ret_w_out, ffn_w1, ffn_w2):
    n_seq_p, seq_p, d = x_prompt.shape
    n_seq_s, n_tok_s, _ = x_sample.shape
    depth = norm_mix.shape[0]
    n_nsa = cache_sel_kv.shape[0]
    pool, page_rows = cache_sel_kv.shape[1], cache_sel_kv.shape[2]
    past_len = page_table.shape[1] * page_rows
    assert past_len % CMP_STRIDE == 0 and n_tok_s < CMP_STRIDE and n_tok_s <= SAMPLE_PAD
    assert state_win_kv.shape[2] == WINDOW and seq_p >= WINDOW

    hp = x_prompt.reshape(n_seq_p * seq_p, d)
    hs = jnp.pad(x_sample, ((0, 0), (0, SAMPLE_PAD - n_tok_s), (0, 0))).reshape(n_seq_s * SAMPLE_PAD, d)
    m_s = n_seq_s * SAMPLE_PAD
    rows_per_tok = 2 * N_KV_HEADS
    cache_cmp = cache_cmp_kv.reshape(n_nsa, pool, page_rows // CMP_STRIDE, CMP_STRIDE, rows_per_tok, HEAD_DIM)
    cache_sel = cache_sel_kv.reshape(n_nsa, pool, page_rows * rows_per_tok, HEAD_DIM)
    win_buf = state_win_kv.reshape(n_nsa, n_seq_s, WINDOW * rows_per_tok, HEAD_DIM)
    cmp_w1 = nsa_cmp_w1.reshape(n_nsa, 2, CMP_BLOCK * HEAD_DIM, HEAD_DIM)
    kv_shape = (2, N_KV_HEADS, HEAD_DIM)
    ffn_w2 = ffn_w2.astype(BF16)
    ret_w_out = ret_w_out.astype(BF16)
    nsa_w_out = nsa_w_out.astype(BF16)

    cmp_p, cmp_s, sel_p, sel_s, win_p, win_s, ret_p, ret_s = [], [], [], [], [], [], [], []
    xp = rmsnorm(hp, norm_mix[0], BF16, 256)
    xs = rmsnorm(hs, norm_mix[0], BF16, m_s)
    for layer in range(depth):
        if layer % 2 == 0:
            a = layer // 2
            cmp_bias = _cmp_bias(cmp_w1, nsa_cmp_pos, nsa_cmp_b1, a)
            op, kv_p = _nsa_prompt(xp, a, nsa_w_in, cmp_w1, cmp_bias, nsa_cmp_w2, n_seq=n_seq_p, seq=seq_p)
            os_, kv_s = _nsa_sample(xs, a, nsa_w_in, cmp_w1, cmp_bias, nsa_cmp_w2, cache_cmp, cache_sel, win_buf,
                                    page_table, n_seq=n_seq_s, n_tok=n_tok_s, past_len=past_len)
            w_out, w_idx = nsa_w_out, a
            kv_p = [r.reshape((n_seq_p, seq_p) + kv_shape) for r in kv_p]
            kv_s = [r.reshape((n_seq_s, SAMPLE_PAD) + kv_shape)[:, :n_tok_s] for r in kv_s]
            cmp_p.append(kv_p[0])
            sel_p.append(kv_p[1])
            win_p.append(kv_p[2][:, seq_p - WINDOW:])
            cmp_s.append(kv_s[0])
            sel_s.append(kv_s[1])
            win_s.append(jnp.concatenate([state_win_kv[a][:, n_tok_s:], kv_s[2]], axis=1))
        else:
            r = layer // 2
            zeros = jnp.zeros((n_seq_p, RET_HEADS, RET_KDIM, RET_VDIM), F32)
            op, st_p = _ret_layer(xp, r, ret_w_in, ret_gn, zeros, jnp.arange(seq_p), n_seq=n_seq_p, seq=seq_p,
                                  chunk=RET_CHUNK, n_tok=RET_CHUNK, tm=PROMPT_ROW_TILE)
            os_, st_s = _ret_layer(xs, r, ret_w_in, ret_gn, state_ret[r],
                                   jnp.tile(past_len + jnp.arange(SAMPLE_PAD), n_seq_s), n_seq=n_seq_s,
                                   seq=SAMPLE_PAD, chunk=SAMPLE_PAD, n_tok=n_tok_s, tm=m_s)
            ret_p.append(st_p)
            ret_s.append(st_s)
            w_out, w_idx = ret_w_out, r
        hp, xp = _mixer_out_proj(op, w_out, w_idx, hp, norm_ffn[layer], name="mixer_out_proj")
        hs, xs = _mixer_out_proj(os_, w_out, w_idx, hs, norm_ffn[layer], name="mixer_out_proj_sample")
        hp = _mlp_down(_mlp_up(xp, ffn_w1, layer, tm=PROMPT_ROW_TILE), ffn_w2, layer, hp, tm=PROMPT_ROW_TILE)
        hs = _mlp_down(_mlp_up(xs, ffn_w1, layer, tm=m_s), ffn_w2, layer, hs, tm=m_s)
        if layer + 1 < depth:
            xp = rmsnorm(hp, norm_mix[layer + 1], BF16, 256)
            xs = rmsnorm(hs, norm_mix[layer + 1], BF16, m_s)
    y_prompt = rmsnorm(hp, norm_final, F32, 256).reshape(n_seq_p, seq_p, d)
    y_sample = rmsnorm(hs, norm_final, F32, m_s).reshape(n_seq_s, SAMPLE_PAD, d)[:, :n_tok_s]
    return (y_prompt, y_sample, jnp.stack(cmp_p), jnp.stack(cmp_s), jnp.stack(sel_p), jnp.stack(sel_s),
            jnp.stack(win_p), jnp.stack(win_s), jnp.stack(ret_p), jnp.stack(ret_s))
```

```python
import functools

import numpy as np
import jax
import jax.numpy as jnp
from jax import lax
from jax.experimental import pallas as pl
from jax.experimental.pallas import tpu as pltpu

F32 = jnp.float32
BF16 = jnp.bfloat16

N_HEADS = 16
HEAD_DIM = 128
N_KV_HEADS = 4
GQA_GROUP = N_HEADS // N_KV_HEADS
KV_ROW = 2 * N_KV_HEADS * HEAD_DIM
CMP_BLOCK = 32
CMP_STRIDE = 16
SEL_BLOCK = 64
N_SEL = 16
N_LOCAL = 2
WINDOW = 512
ROPE_THETA = 10000.0
RET_HEADS = 8
RET_KDIM = 256
RET_VDIM = 512
RET_CHUNK = 128
RET_HEADS_PER_STEP = 8
RMS_EPS = 1e-6
GN_EPS = 1e-5
NEG_INF = -1e30
FORCE_SCORE = 1e9
LOG2_E = 1.4426950408889634

SUBLANES = 8
LANES = 128
VMEM_LIMIT_BYTES = 56 * 1024 * 1024
SAMPLE_PAD = 16
PROMPT_ROW_TILE = 2048

_NT = (((1,), (1,)), ((), ()))
_TN = (((0,), (0,)), ((), ()))


def _params(*semantics):
    return pltpu.CompilerParams(dimension_semantics=semantics, vmem_limit_bytes=VMEM_LIMIT_BYTES)


def _rmsnorm_kernel(x_ref, g_ref, o_ref):
    x = x_ref[...]
    ms = jnp.mean(x * x, axis=-1, keepdims=True)
    o_ref[...] = (x * lax.rsqrt(ms + RMS_EPS) * g_ref[...]).astype(o_ref.dtype)


def rmsnorm(x, g, out_dtype, tm, row_block0=0, n_blocks=None):
    m, d = x.shape
    if n_blocks is None:
        n_blocks = m // tm
    return pl.pallas_call(
        _rmsnorm_kernel,
        grid=(n_blocks,),
        in_specs=[pl.BlockSpec((tm, d), lambda i: (i + row_block0, 0)), pl.BlockSpec((1, d), lambda i: (0, 0))],
        out_specs=pl.BlockSpec((tm, d), lambda i: (i, 0)),
        out_shape=jax.ShapeDtypeStruct((n_blocks * tm, d), out_dtype),
        compiler_params=_params("parallel"),
        name="rmsnorm",
    )(x, g.reshape(1, d))


def _mm_kernel(*refs, n_x, n_extra, n_out, nk, epilogue):
    x_refs = refs[:n_x]
    w_ref = refs[n_x]
    extra = refs[n_x + 1:n_x + 1 + n_extra]
    outs = refs[n_x + 1 + n_extra:n_x + 1 + n_extra + n_out]
    x = x_refs[0][...]
    for r in x_refs[1:]:
        x = x + r[...]
    part = jnp.dot(x.astype(BF16), w_ref[...].astype(BF16), preferred_element_type=F32)
    if nk == 1:
        epilogue(part, extra, outs)
        return
    k = pl.program_id(2)
    if epilogue is _ep_residual_norm:
        @pl.when(k == 0)
        def _():
            outs[0][...] = extra[0][...] + part

        @pl.when(k > 0)
        def _():
            outs[0][...] += part

        @pl.when(k == nk - 1)
        def _():
            _write_norm(outs[0][...], extra, outs)
        return
    acc_ref = refs[-1]

    @pl.when(k == 0)
    def _():
        acc_ref[...] = part

    @pl.when(k > 0)
    def _():
        acc_ref[...] += part

    @pl.when(k == nk - 1)
    def _():
        epilogue(acc_ref[...], extra, outs)


def matmul(xs, w, layer, *, col_block0, n_cols, tm, tn, tk, extras, extra_specs, out_shapes, out_specs,
           epilogue, name, column_steps_ordered=False):
    m, kdim = xs[0].shape
    nk = kdim // tk
    grid = (m // tm, pl.cdiv(n_cols, tn), nk)
    in_specs = [pl.BlockSpec((tm, tk), lambda i, j, k: (i, k)) for _ in xs]
    in_specs.append(pl.BlockSpec((None, tk, tn), lambda i, j, k: (layer, k, j + col_block0)))
    in_specs.extend(extra_specs)
    scratch = [pltpu.VMEM((tm, tn), F32)] if nk > 1 and epilogue is not _ep_residual_norm else []
    kern = functools.partial(_mm_kernel, n_x=len(xs), n_extra=len(extras), n_out=len(out_shapes), nk=nk,
                             epilogue=epilogue)
    return pl.pallas_call(
        kern,
        grid=grid,
        in_specs=in_specs,
        out_specs=out_specs,
        out_shape=out_shapes,
        scratch_shapes=scratch,
        compiler_params=_params("parallel", "arbitrary" if column_steps_ordered else "parallel", "arbitrary"),
        name=name,
    )(*xs, w, *extras)


def _rope_half_lane(x, cos, sin_signed):
    return x * cos + pltpu.roll(x, HEAD_DIM // 2, 1) * sin_signed


def _ep_rope_q(part, extra, outs, *, scale):
    cos = extra[0][...]
    sin = extra[1][...]
    for h in range(part.shape[1] // HEAD_DIM):
        sl = slice(h * HEAD_DIM, (h + 1) * HEAD_DIM)
        r = _rope_half_lane(part[:, sl], cos, sin) * scale
        outs[0][:, sl] = r.astype(outs[0].dtype)
        outs[1][sl, :] = r.T.astype(outs[1].dtype)


def _ep_nsa_kv(part, extra, outs):
    tm = part.shape[0]
    rows_per_tok = 2 * N_KV_HEADS
    j = pl.program_id(1)

    def heads():
        return [part[:, h * HEAD_DIM:(h + 1) * HEAD_DIM] for h in range(N_KV_HEADS)]

    for br in range(3):
        @pl.when(j == 2 * br)
        def _(br=br):
            cos = extra[0][...]
            sin = extra[1][...]
            for h, x in enumerate(heads()):
                k = _rope_half_lane(x, cos, sin)
                outs[br][pl.ds(h, tm, stride=rows_per_tok), :] = k
                outs[3][h] = k.astype(BF16)

        @pl.when(j == 2 * br + 1)
        def _(br=br):
            for h, v in enumerate(heads()):
                outs[br][pl.ds(N_KV_HEADS + h, tm, stride=rows_per_tok), :] = v
                outs[4][h] = v.T.astype(BF16)


def _ep_gates(part, extra, outs):
    g = jax.nn.sigmoid(part)
    for br in range(3):
        for kvh in range(N_KV_HEADS):
            c = br * N_HEADS + kvh * GQA_GROUP
            outs[0][br, kvh] = g[:, c:c + GQA_GROUP]


def _write_norm(h, extra, outs):
    if len(extra) < 2:
        return
    ms = jnp.mean(h * h, axis=-1, keepdims=True)
    y = h * lax.rsqrt(ms + RMS_EPS) * extra[1][...]
    outs[-1][...] = y.astype(outs[-1].dtype)


def _ep_residual_norm(part, extra, outs):
    h = extra[0][...] + part
    outs[0][...] = h
    _write_norm(h, extra, outs)


def _ep_relu2(part, extra, outs):
    h = jnp.maximum(part, 0.0)
    outs[0][...] = (h * h).astype(outs[0].dtype)


def _ep_cast(part, extra, outs):
    outs[0][...] = part.astype(outs[0].dtype)


def _ep_rope_ret(part, extra, outs, *, scale):
    cos = extra[0][...]
    sin = extra[1][...]
    for h in range(part.shape[1] // RET_KDIM):
        a = slice(h * RET_KDIM, h * RET_KDIM + LANES)
        b = slice(h * RET_KDIM + LANES, (h + 1) * RET_KDIM)
        x1 = part[:, a]
        x2 = part[:, b]
        outs[0][:, a] = ((x1 * cos - x2 * sin) * scale).astype(outs[0].dtype)
        outs[0][:, b] = ((x2 * cos + x1 * sin) * scale).astype(outs[0].dtype)


def _rope_tables(pos, half):
    inv = ROPE_THETA ** (-jnp.arange(half, dtype=F32) / half)
    ang = pos.astype(F32)[:, None] * inv[None, :]
    return jnp.cos(ang), jnp.sin(ang)


def _compress_kernel(*refs, n_pages, paged):
    if paged:
        refs = refs[1:]
    page_refs = refs[:n_pages]
    next_ref, w1_ref, bias_ref, w2_ref, out_ref, shift_ref, y_ref = refs[n_pages:]
    nc = page_refs[0].shape[0]
    n_heads = N_KV_HEADS
    n_main = n_pages * nc * n_heads
    k_half = CMP_STRIDE * HEAD_DIM
    tiles = [[page_refs[p][:, s] for s in range(CMP_STRIDE)] for p in range(n_pages)]
    next_tiles = [next_ref[0, s] for s in range(CMP_STRIDE)]
    for e in range(2):
        hsl = slice(e * n_heads, (e + 1) * n_heads)
        blocks = [
            jnp.concatenate([tiles[p][s][:, hsl, :].reshape(nc * n_heads, HEAD_DIM) for s in range(CMP_STRIDE)],
                            axis=1)
            for p in range(n_pages)]
        blocks.append(jnp.concatenate([next_tiles[s][hsl, :] for s in range(CMP_STRIDE)], axis=1))
        blocks.append(jnp.zeros((SUBLANES - n_heads, CMP_STRIDE * HEAD_DIM), F32))
        x = jnp.concatenate(blocks, axis=0).astype(BF16)
        wcat = jnp.concatenate([w1_ref[e, :k_half], w1_ref[e, k_half:]], axis=1).astype(BF16)
        per = jnp.dot(x, wcat, preferred_element_type=F32)
        hid = per.shape[1] // 2
        shift_ref[...] = per[:, hid:]
        h = per[:n_main, :hid] + shift_ref[pl.ds(n_heads, n_main), :]
        h = jax.nn.gelu(h + bias_ref[e:e + 1, :])
        y_ref[...] = jnp.dot(h.astype(BF16), w2_ref[e].astype(BF16), preferred_element_type=F32)
        for g in range(n_heads):
            out_ref[e * n_heads + g] = y_ref[pl.ds(g, n_main // n_heads, stride=n_heads), :].astype(out_ref.dtype)


def compress(kv, layer, page_table, w1, bias, w2, *, n_seq, n_chunks, n_pages):
    paged = page_table is not None
    nc = kv.shape[2]
    n_main = n_pages * nc
    steps = n_chunks // n_main
    pages_per_seq = steps * n_pages
    rows_per_tok = 2 * N_KV_HEADS

    def nxt(s):
        return jnp.minimum((s + 1) * n_pages, pages_per_seq - 1)

    if paged:
        def page_map(p):
            return lambda b, s, pt: (layer, pt[b, s * n_pages + p], 0, 0, 0, 0)

        def next_map(b, s, pt):
            return (layer, pt[b, nxt(s)], 0, 0, 0, 0)

        def fixed(*idx):
            return lambda b, s, pt: idx

        out_map = lambda b, s, pt: (b, 0, s, 0)
    else:
        def page_map(p):
            return lambda b, s: (layer, b * pages_per_seq + s * n_pages + p, 0, 0, 0, 0)

        def next_map(b, s):
            return (layer, b * pages_per_seq + nxt(s), 0, 0, 0, 0)

        def fixed(*idx):
            return lambda b, s: idx

        out_map = lambda b, s: (b, 0, s, 0)

    tile = (CMP_STRIDE, rows_per_tok, HEAD_DIM)
    in_specs = [pl.BlockSpec((None, None, nc) + tile, page_map(p)) for p in range(n_pages)]
    in_specs += [
        pl.BlockSpec((None, None, 1) + tile, next_map),
        pl.BlockSpec((None, 2, CMP_BLOCK * HEAD_DIM, HEAD_DIM), fixed(layer, 0, 0, 0)),
        pl.BlockSpec((2, HEAD_DIM), fixed(0, 0)),
        pl.BlockSpec((None, 2, HEAD_DIM, HEAD_DIM), fixed(layer, 0, 0, 0)),
    ]
    kern = functools.partial(_compress_kernel, n_pages=n_pages, paged=paged)
    grid_spec = pltpu.PrefetchScalarGridSpec(
        num_scalar_prefetch=1 if paged else 0,
        grid=(n_seq, steps),
        in_specs=in_specs,
        out_specs=pl.BlockSpec((None, rows_per_tok, n_main, HEAD_DIM), out_map),
        scratch_shapes=[pltpu.VMEM((n_main * N_KV_HEADS + SUBLANES, HEAD_DIM), F32),
                        pltpu.VMEM((n_main * N_KV_HEADS, HEAD_DIM), F32)],
    )
    args = ([page_table] if paged else []) + [kv] * (n_pages + 1) + [w1, bias, w2]
    return pl.pallas_call(
        kern,
        grid_spec=grid_spec,
        out_shape=jax.ShapeDtypeStruct((n_seq, rows_per_tok, n_chunks, HEAD_DIM), BF16),
        compiler_params=_params("parallel", "arbitrary"),
        name="nsa_compress",
    )(*args)


def _group_rows(q):
    return jnp.concatenate([q[:, h * HEAD_DIM:(h + 1) * HEAD_DIM] for h in range(GQA_GROUP)], axis=0)


def _cmp_kernel(q_ref, k_ref, v_ref, cov_ref, gate_ref, o_ref, val_ref, *, tq, pos0):
    qi = pl.program_id(2)
    q4 = _group_rows(q_ref[...])
    s = lax.dot_general(q4, k_ref[...], _NT, preferred_element_type=F32)
    row = lax.broadcasted_iota(jnp.int32, s.shape, 0)
    col = lax.broadcasted_iota(jnp.int32, s.shape, 1)
    qpos = pos0 + qi * tq + jnp.bitwise_and(row, tq - 1)
    valid = col * CMP_STRIDE + (CMP_BLOCK - 1) <= qpos
    s = jnp.where(valid, s, NEG_INF)
    m = jnp.max(s, axis=1, keepdims=True)
    e = jnp.where(valid, jnp.exp2(s - m), 0.0)
    l = jnp.sum(e, axis=1, keepdims=True)
    p = e / jnp.where(l > 0.0, l, 1.0)
    o4 = jnp.dot(p.astype(BF16), v_ref[...], preferred_element_type=F32)
    for h in range(GQA_GROUP):
        o_ref[:, h * HEAD_DIM:(h + 1) * HEAD_DIM] = o4[h * tq:(h + 1) * tq] * gate_ref[:, h:h + 1]
    imp = p[0:tq] + p[tq:2 * tq] + p[2 * tq:3 * tq] + p[3 * tq:4 * tq]
    hi = imp.astype(BF16)
    r1 = imp - hi.astype(F32)
    mid = r1.astype(BF16)
    lo = (r1 - mid.astype(F32)).astype(BF16)
    cov = cov_ref[...]
    score = (lax.dot_general(cov, hi, _NT, preferred_element_type=F32)
             + lax.dot_general(cov, mid, _NT, preferred_element_type=F32)
             + lax.dot_general(cov, lo, _NT, preferred_element_type=F32))
    jb = lax.broadcasted_iota(jnp.int32, score.shape, 0)
    qp = pos0 + qi * tq + lax.broadcasted_iota(jnp.int32, score.shape, 1)
    cur = jnp.right_shift(qp, SEL_BLOCK.bit_length() - 1)
    visible = jb <= cur
    forced = (jb == 0) | (visible & (jb > cur - N_LOCAL))
    val_ref[...] = jnp.where(forced, FORCE_SCORE, jnp.where(visible, score, -FORCE_SCORE))


def cmp_attention(q, kc, cover_t, gates_t, *, n_seq, seq, tq, pos0):
    nq = seq // tq
    ncp = kc.shape[2]
    nbv = cover_t.shape[0]
    gw = GQA_GROUP * HEAD_DIM
    kern = functools.partial(_cmp_kernel, tq=tq, pos0=pos0)
    if tq % LANES == 0:
        val_spec = pl.BlockSpec((nbv, tq), lambda b, g, i: (0, (b * N_KV_HEADS + g) * nq + i))
        val_shape = jax.ShapeDtypeStruct((nbv, n_seq * N_KV_HEADS * seq), F32)
    else:
        val_spec = pl.BlockSpec((None, None, nbv, tq), lambda b, g, i: (b, g, 0, i))
        val_shape = jax.ShapeDtypeStruct((n_seq, N_KV_HEADS, nbv, seq), F32)
    return pl.pallas_call(
        kern,
        grid=(n_seq, N_KV_HEADS, nq),
        in_specs=[
            pl.BlockSpec((tq, gw), lambda b, g, i: (b * nq + i, g)),
            pl.BlockSpec((None, None, ncp, HEAD_DIM), lambda b, g, i: (b, g, 0, 0)),
            pl.BlockSpec((None, None, ncp, HEAD_DIM), lambda b, g, i: (b, N_KV_HEADS + g, 0, 0)),
            pl.BlockSpec((nbv, ncp), lambda b, g, i: (0, 0)),
            pl.BlockSpec((None, None, tq, GQA_GROUP), lambda b, g, i: (0, g, b * nq + i, 0)),
        ],
        out_specs=[pl.BlockSpec((tq, gw), lambda b, g, i: (b * nq + i, g)), val_spec],
        out_shape=[jax.ShapeDtypeStruct((n_seq * seq, N_HEADS * HEAD_DIM), F32), val_shape],
        compiler_params=_params("parallel", "parallel", "parallel"),
        name="nsa_cmp_attention",
    )(q, kc, kc, cover_t, gates_t)


def _cover_t(n_chunks, n_blk, nbv):
    n_cmp = n_chunks - CMP_BLOCK // CMP_STRIDE + 1
    c0 = np.arange(n_chunks) * CMP_STRIDE
    j0 = np.arange(nbv) * SEL_BLOCK
    cov = (c0[None, :] < j0[:, None] + SEL_BLOCK) & (c0[None, :] + CMP_BLOCK > j0[:, None])
    cov &= (np.arange(n_chunks)[None, :] < n_cmp) & (np.arange(nbv)[:, None] < n_blk)
    return jnp.asarray(cov.astype(np.float32), dtype=BF16)


def _rank_kernel(val_ref, out_ref, *, n_blk, n_sel, causal_seq):
    val = val_ref[...]
    tc = val.shape[1]
    jb = lax.broadcasted_iota(jnp.int32, val.shape, 0)
    if causal_seq is None:
        n_iter = n_blk
    else:
        pos0 = (pl.program_id(0) % (causal_seq // tc)) * tc
        n_iter = jnp.minimum(n_blk, (pos0 + tc - 1) // SEL_BLOCK + 1)

    def body(i, rank):
        row = val_ref[pl.ds(i, 1), :]
        beats = (row > val) | ((row == val) & (i < jb))
        return rank + jnp.where(beats, 1, 0)

    rank = lax.fori_loop(0, n_iter, body, jnp.zeros(val.shape, jnp.int32))
    chosen = (rank < n_sel) & (val > -0.5 * FORCE_SCORE)
    out_ref[...] = jnp.where(chosen, 0.0, NEG_INF).astype(out_ref.dtype)


def select_bias(val, *, n_blk, tc, causal_seq=None):
    nbv, ncols = val.shape
    kern = functools.partial(_rank_kernel, n_blk=n_blk, n_sel=min(N_SEL, n_blk), causal_seq=causal_seq)
    return pl.pallas_call(
        kern,
        grid=(ncols // tc,),
        in_specs=[pl.BlockSpec((nbv, tc), lambda i: (0, i))],
        out_specs=pl.BlockSpec((nbv, tc), lambda i: (0, i)),
        out_shape=jax.ShapeDtypeStruct((nbv, ncols), BF16),
        compiler_params=_params("parallel"),
        name="nsa_select_rank",
    )(val)


_STEP_FIRST, _STEP_LAST, _STEP_EDGE = 1, 2, 4


def _flash_steps(seq, tq, tk):
    qi_l, kt_l, fl_l = [], [], []
    for qi in range(seq // tq):
        q0, q1 = qi * tq, qi * tq + tq - 1
        kts = [kt for kt in range(seq // tk) if kt * tk <= q1]
        for n, kt in enumerate(kts):
            all_visible = q0 - (kt * tk + tk - 1) >= 0
            flags = (_STEP_FIRST if n == 0 else 0) | (_STEP_LAST if n == len(kts) - 1 else 0)
            flags |= 0 if all_visible else _STEP_EDGE
            qi_l.append(qi)
            kt_l.append(kt)
            fl_l.append(flags)
    return (jnp.asarray(np.array(qi_l, np.int32)), jnp.asarray(np.array(kt_l, np.int32)),
            jnp.asarray(np.array(fl_l, np.int32)))


def _flash_kernel(qi_ref, kt_ref, flag_ref, qt_ref, bias_ref, k_ref, vt_ref, e_ref, gate_ref, prev_ref, o_ref,
                  qa_ref, m_ref, acc_ref, *, tq, tk, out_dtype):
    s = pl.program_id(2)
    qi = qi_ref[s]
    kt = kt_ref[s]
    flags = flag_ref[s]
    first = jnp.bitwise_and(flags, _STEP_FIRST) != 0
    last = jnp.bitwise_and(flags, _STEP_LAST) != 0
    edge = jnp.bitwise_and(flags, _STEP_EDGE) != 0

    @pl.when(first)
    def _():
        nbv = bias_ref.shape[0]
        for h in range(GQA_GROUP):
            qa_ref[0:HEAD_DIM, h * tq:(h + 1) * tq] = qt_ref[h * HEAD_DIM:(h + 1) * HEAD_DIM, :]
            qa_ref[HEAD_DIM:HEAD_DIM + nbv, h * tq:(h + 1) * tq] = bias_ref[...]
            if HEAD_DIM + nbv < qa_ref.shape[0]:
                qa_ref[HEAD_DIM + nbv:, h * tq:(h + 1) * tq] = jnp.zeros((qa_ref.shape[0] - HEAD_DIM - nbv, tq), BF16)
        m_ref[...] = jnp.full(m_ref.shape, NEG_INF, F32)
        acc_ref[...] = jnp.zeros(acc_ref.shape, F32)

    def step(masked):
        ka = jnp.concatenate([k_ref[...], e_ref[...]], axis=1)
        st = jnp.dot(ka, qa_ref[...], preferred_element_type=F32)
        if masked:
            kpos = kt * tk + lax.broadcasted_iota(jnp.int32, st.shape, 0)
            qpos = qi * tq + jnp.bitwise_and(lax.broadcasted_iota(jnp.int32, st.shape, 1), tq - 1)
            st = jnp.where(qpos >= kpos, st, NEG_INF)
        m_old = m_ref[...]
        m_new = jnp.maximum(m_old, jnp.max(st, axis=0, keepdims=True))
        alpha = jnp.exp2(m_old - m_new)
        p = jnp.exp2(st - m_new).astype(BF16)
        va = jnp.concatenate([vt_ref[...], jnp.ones((acc_ref.shape[0] - HEAD_DIM, tk), BF16)], axis=0)
        acc_ref[...] = alpha * acc_ref[...] + jnp.dot(va, p, preferred_element_type=F32)
        m_ref[...] = m_new

    @pl.when(edge)
    def _():
        step(True)

    @pl.when(jnp.logical_not(edge))
    def _():
        step(False)

    @pl.when(last)
    def _():
        acc = acc_ref[...]
        ot = acc[:HEAD_DIM] / acc[HEAD_DIM:HEAD_DIM + 1]
        for h in range(GQA_GROUP):
            sl = slice(h * HEAD_DIM, (h + 1) * HEAD_DIM)
            o_h = ot[:, h * tq:(h + 1) * tq].T
            o_ref[:, sl] = (prev_ref[:, sl] + o_h * gate_ref[:, h:h + 1]).astype(out_dtype)


def selected_attention(qt, k, vt, branch, gates_t, prev, bias, onehot, *, n_seq, seq, tq, tk, out_dtype=F32):
    nq = seq // tq
    nkt = seq // tk
    gw = GQA_GROUP * HEAD_DIM
    qi_tab, kt_tab, flag_tab = _flash_steps(seq, tq, tk)
    nbv = bias.shape[0]
    nbp = onehot.shape[1]

    o_spec = pl.BlockSpec((tq, gw), lambda b, g, s, qi, kt, fl: (b * nq + qi[s], g))
    qt_spec = pl.BlockSpec((gw, tq), lambda b, g, s, qi, kt, fl: (g, b * nq + qi[s]))
    k_spec = pl.BlockSpec((None, None, tk, HEAD_DIM), lambda b, g, s, qi, kt, fl: (branch, g, b * nkt + kt[s], 0))
    vt_spec = pl.BlockSpec((None, None, HEAD_DIM, tk), lambda b, g, s, qi, kt, fl: (branch, g, 0, b * nkt + kt[s]))
    gate_spec = pl.BlockSpec((None, None, tq, GQA_GROUP), lambda b, g, s, qi, kt, fl: (branch, g, b * nq + qi[s], 0))
    bias_spec = pl.BlockSpec((nbv, tq), lambda b, g, s, qi, kt, fl: (0, (b * N_KV_HEADS + g) * nq + qi[s]))
    onehot_spec = pl.BlockSpec((tk, nbp), lambda b, g, s, qi, kt, fl: (kt[s], 0))
    in_specs = [qt_spec, bias_spec, k_spec, vt_spec, onehot_spec, gate_spec, o_spec]
    args = [qt, bias, k, vt, onehot, gates_t, prev]
    qa_rows = HEAD_DIM + nbp
    kern = functools.partial(_flash_kernel, tq=tq, tk=tk, out_dtype=out_dtype)
    cols = GQA_GROUP * tq
    grid_spec = pltpu.PrefetchScalarGridSpec(
        num_scalar_prefetch=3,
        grid=(n_seq, N_KV_HEADS, qi_tab.shape[0]),
        in_specs=in_specs,
        out_specs=o_spec,
        scratch_shapes=[
            pltpu.VMEM((qa_rows, cols), BF16),
            pltpu.VMEM((1, cols), F32),
            pltpu.VMEM((HEAD_DIM + 2 * SUBLANES, cols), F32),
        ],
    )
    return pl.pallas_call(
        kern,
        grid_spec=grid_spec,
        out_shape=jax.ShapeDtypeStruct((n_seq * seq, N_HEADS * HEAD_DIM), out_dtype),
        compiler_params=_params("parallel", "parallel", "arbitrary"),
        name="nsa_sel_attention",
    )(qi_tab, kt_tab, flag_tab, *args)


def _win_kernel(*refs, tq, n_tiles, window, out_dtype):
    qt_ref = refs[0]
    k_refs = refs[1:1 + n_tiles]
    vt_refs = refs[1 + n_tiles:1 + 2 * n_tiles]
    gate_ref, prev_ref, o_ref = refs[1 + 2 * n_tiles:]
    qi = pl.program_id(2)
    qa = jnp.concatenate([qt_ref[h * HEAD_DIM:(h + 1) * HEAD_DIM, :] for h in range(GQA_GROUP)], axis=1)
    sts = []
    for j in range(n_tiles):
        kt = qi - (n_tiles - 1) + j
        st = jnp.dot(k_refs[j][...], qa, preferred_element_type=F32)
        if j == 0 or j == n_tiles - 1:
            kpos = kt * tq + lax.broadcasted_iota(jnp.int32, st.shape, 0)
            qpos = qi * tq + jnp.bitwise_and(lax.broadcasted_iota(jnp.int32, st.shape, 1), tq - 1)
            d = qpos - kpos
            ok = (d >= 0) & (d < window) & (kpos >= 0)
            st = jnp.where(ok, st, NEG_INF)
        else:
            st = jnp.where(kt >= 0, st, NEG_INF)
        sts.append(st)
    st = jnp.concatenate(sts, axis=0)
    m = jnp.max(st, axis=0, keepdims=True)
    p = jnp.exp2(st - m).astype(BF16)
    vt = jnp.concatenate([r[...] for r in vt_refs], axis=1)
    va = jnp.concatenate([vt, jnp.ones((2 * SUBLANES, vt.shape[1]), BF16)], axis=0)
    acc = jnp.dot(va, p, preferred_element_type=F32)
    ot = acc[:HEAD_DIM] / acc[HEAD_DIM:HEAD_DIM + 1]
    for h in range(GQA_GROUP):
        sl = slice(h * HEAD_DIM, (h + 1) * HEAD_DIM)
        o_h = ot[:, h * tq:(h + 1) * tq].T
        o_ref[:, sl] = (prev_ref[:, sl] + o_h * gate_ref[:, h:h + 1]).astype(out_dtype)


def window_attention(qt, k, vt, branch, gates_t, prev, *, n_seq, seq, tq, window, out_dtype):
    nq = seq // tq
    n_tiles = window // tq + 1
    gw = GQA_GROUP * HEAD_DIM

    def kt_of(i, j):
        return jnp.maximum(i - (n_tiles - 1) + j, 0)

    o_spec = pl.BlockSpec((tq, gw), lambda b, g, i: (b * nq + i, g))
    in_specs = [pl.BlockSpec((gw, tq), lambda b, g, i: (g, b * nq + i))]
    in_specs += [pl.BlockSpec((None, None, tq, HEAD_DIM), lambda b, g, i, j=j: (branch, g, b * nq + kt_of(i, j), 0))
                 for j in range(n_tiles)]
    in_specs += [pl.BlockSpec((None, None, HEAD_DIM, tq), lambda b, g, i, j=j: (branch, g, 0, b * nq + kt_of(i, j)))
                 for j in range(n_tiles)]
    in_specs += [pl.BlockSpec((None, None, tq, GQA_GROUP), lambda b, g, i: (branch, g, b * nq + i, 0)), o_spec]
    kern = functools.partial(_win_kernel, tq=tq, n_tiles=n_tiles, window=window, out_dtype=out_dtype)
    return pl.pallas_call(
        kern,
        grid=(n_seq, N_KV_HEADS, nq),
        in_specs=in_specs,
        out_specs=o_spec,
        out_shape=jax.ShapeDtypeStruct((n_seq * seq, N_HEADS * HEAD_DIM), out_dtype),
        compiler_params=_params("parallel", "parallel", "parallel"),
        name="nsa_win_attention",
    )(qt, *([k] * n_tiles), *([vt] * n_tiles), gates_t, prev)


def _block_onehot(n_keys, nbp):
    e = (np.arange(n_keys)[:, None] // SEL_BLOCK) == np.arange(nbp)[None, :]
    return jnp.asarray(e.astype(np.float32), dtype=BF16)


def _sel_sample_kernel(*refs, n_pages, nbp, past_len, nsteps, tq):
    logical_ref, count_ref = refs[1:3]
    qt_ref, bias_ref, new_ref = refs[3:6]
    page_refs = refs[6:6 + n_pages]
    gate_ref, prev_ref, o_ref, m_ref, acc_ref = refs[6 + n_pages:]
    b = pl.program_id(0)
    step = pl.program_id(1)
    page_rows = page_refs[0].shape[0]
    rows_per_tok = 2 * N_KV_HEADS
    page_tokens = page_rows // rows_per_tok
    blocks_per_page = page_tokens // SEL_BLOCK
    n_needed = count_ref[b]
    cols = qt_ref.shape[2]

    def scores(g, k, key_bias):
        return jnp.dot(k, qt_ref[g], preferred_element_type=F32) + key_bias

    def block_bias(g, first_block, n_blocks, valid):
        rows = bias_ref[g, pl.ds(first_block, n_blocks), :]
        rows = jnp.where(valid, rows, NEG_INF)
        return jnp.concatenate(
            [jnp.broadcast_to(rows[j:j + 1], (SEL_BLOCK, cols)) for j in range(n_blocks)], axis=0)

    def weighted_values(p, v):
        va = jnp.concatenate([v, jnp.ones(v.shape, BF16)], axis=1)
        return lax.dot_general(p, va, _TN, preferred_element_type=F32)

    @pl.when(step == 0)
    def _():
        for g in range(N_KV_HEADS):
            kn = new_ref[pl.ds(g, tq, stride=2 * N_KV_HEADS), :].astype(BF16)
            vn = new_ref[pl.ds(N_KV_HEADS + g, tq, stride=2 * N_KV_HEADS), :].astype(BF16)
            new_bias = block_bias(g, past_len // SEL_BLOCK, 1, True)[:tq]
            st = scores(g, kn, new_bias)
            key = lax.broadcasted_iota(jnp.int32, st.shape, 0)
            tok = jnp.bitwise_and(lax.broadcasted_iota(jnp.int32, st.shape, 1), tq - 1)
            st = jnp.where(key <= tok, st, NEG_INF)
            m = jnp.max(st, axis=0, keepdims=True)
            p = jnp.exp2(st - m).astype(BF16)
            m_ref[g] = m
            acc_ref[g] = weighted_values(p, vn)

    @pl.when(step * n_pages < n_needed)
    def _():
        slots = [step * n_pages + p for p in range(n_pages)]
        first_blocks = [logical_ref[b, slot] * blocks_per_page for slot in slots]
        for g in range(N_KV_HEADS):
            k = jnp.concatenate(
                [page_refs[p][pl.ds(g, page_tokens, stride=rows_per_tok), :] for p in range(n_pages)],
                axis=0).astype(BF16)
            v = jnp.concatenate(
                [page_refs[p][pl.ds(N_KV_HEADS + g, page_tokens, stride=rows_per_tok), :] for p in range(n_pages)],
                axis=0).astype(BF16)
            key_bias = jnp.concatenate(
                [block_bias(g, first_blocks[p], blocks_per_page, slots[p] < n_needed) for p in range(n_pages)],
                axis=0)
            st = scores(g, k, key_bias)
            m_old = m_ref[g]
            m_new = jnp.maximum(m_old, jnp.max(st, axis=0, keepdims=True))
            alpha = jnp.exp2(m_old - m_new)
            p = jnp.exp2(st - m_new).astype(BF16)
            n_cols = alpha.shape[1]
            eye = (lax.broadcasted_iota(jnp.int32, (n_cols, n_cols), 0)
                   == lax.broadcasted_iota(jnp.int32, (n_cols, n_cols), 1))
            alpha_col = jnp.sum(jnp.where(eye, alpha, 0.0), axis=1, keepdims=True)
            acc_ref[g] = alpha_col * acc_ref[g] + weighted_values(p, v)
            m_ref[g] = m_new

    @pl.when(step == nsteps - 1)
    def _():
        for g in range(N_KV_HEADS):
            acc = acc_ref[g]
            o4 = acc[:, :HEAD_DIM] / acc[:, HEAD_DIM:HEAD_DIM + 1]
            for h in range(GQA_GROUP):
                hh = g * GQA_GROUP + h
                sl = slice(hh * HEAD_DIM, (hh + 1) * HEAD_DIM)
                o_ref[:, sl] = prev_ref[:, sl] + o4[h * tq:(h + 1) * tq] * gate_ref[g, :, h:h + 1]


def sel_attention_sample(q, bias, needed, kv_new, cache, layer, page_table, gates_t, prev, *, n_pages_step,
                         past_len):
    n_seq, pages = page_table.shape
    tq = SAMPLE_PAD
    nbp = bias.shape[-1]
    page_rows = cache.shape[2]
    nsteps = pages // n_pages_step
    dm = N_HEADS * HEAD_DIM
    order = jnp.argsort(jnp.logical_not(needed), axis=1, stable=True).astype(jnp.int32)
    count = jnp.sum(needed, axis=1).astype(jnp.int32)
    slot = jnp.minimum(jnp.arange(pages, dtype=jnp.int32)[None, :], count[:, None] - 1)
    logical = jnp.take_along_axis(order, slot, axis=1)
    physical = jnp.take_along_axis(page_table, logical, axis=1)
    row_spec = pl.BlockSpec((tq, dm), lambda b, s, *_: (b, 0))
    assert past_len % SEL_BLOCK + tq <= SEL_BLOCK
    cols = GQA_GROUP * tq
    q_t = q.reshape(n_seq, tq, N_KV_HEADS, GQA_GROUP, HEAD_DIM).transpose(0, 2, 4, 3, 1)
    q_t = q_t.reshape(n_seq, N_KV_HEADS, HEAD_DIM, cols)
    bias_t = jnp.tile(bias.transpose(0, 1, 3, 2), (1, 1, 1, GQA_GROUP)).astype(F32)

    def page_map(p):
        return lambda b, s, phys, *_: (layer, phys[b, s * n_pages_step + p], 0, 0)

    in_specs = [
        pl.BlockSpec((None, N_KV_HEADS, HEAD_DIM, cols), lambda b, s, *_: (b, 0, 0, 0)),
        pl.BlockSpec((None, N_KV_HEADS, nbp, cols), lambda b, s, *_: (b, 0, 0, 0)),
        pl.BlockSpec((tq * 2 * N_KV_HEADS, HEAD_DIM), lambda b, s, *_: (b, 0)),
    ]
    in_specs += [pl.BlockSpec((None, None, page_rows, HEAD_DIM), page_map(p)) for p in range(n_pages_step)]
    in_specs += [
        pl.BlockSpec((None, N_KV_HEADS, tq, GQA_GROUP), lambda b, s, *_: (1, 0, b, 0)),
        row_spec,
    ]
    kern = functools.partial(_sel_sample_kernel, n_pages=n_pages_step, nbp=nbp, past_len=past_len, nsteps=nsteps,
                             tq=tq)
    grid_spec = pltpu.PrefetchScalarGridSpec(
        num_scalar_prefetch=3,
        grid=(n_seq, nsteps),
        in_specs=in_specs,
        out_specs=row_spec,
        scratch_shapes=[
            pltpu.VMEM((N_KV_HEADS, 1, cols), F32),
            pltpu.VMEM((N_KV_HEADS, cols, 2 * HEAD_DIM), F32),
        ],
    )
    return pl.pallas_call(
        kern,
        grid_spec=grid_spec,
        out_shape=jax.ShapeDtypeStruct((n_seq * tq, dm), F32),
        compiler_params=_params("parallel", "arbitrary"),
        name="nsa_sel_attention_sample",
    )(physical, logical, count, q_t, bias_t, kv_new, *([cache] * n_pages_step), gates_t, prev)


def _win_sample_kernel(q_ref, buf_ref, new_ref, gate_ref, prev_ref, o_ref, *, tq, out_dtype):
    rows_per_tok = 2 * N_KV_HEADS
    w_buf = buf_ref.shape[0] // rows_per_tok
    gw = GQA_GROUP * HEAD_DIM
    for g in range(N_KV_HEADS):
        q4 = _group_rows(q_ref[:, g * gw:(g + 1) * gw])
        kb = buf_ref[pl.ds(g, w_buf, stride=rows_per_tok), :].astype(BF16)
        vb = buf_ref[pl.ds(N_KV_HEADS + g, w_buf, stride=rows_per_tok), :].astype(BF16)
        kn = new_ref[pl.ds(g, tq, stride=rows_per_tok), :].astype(BF16)
        vn = new_ref[pl.ds(N_KV_HEADS + g, tq, stride=rows_per_tok), :].astype(BF16)
        sb = lax.dot_general(q4, kb, _NT, preferred_element_type=F32)
        sn = lax.dot_general(q4, kn, _NT, preferred_element_type=F32)
        tb = jnp.bitwise_and(lax.broadcasted_iota(jnp.int32, sb.shape, 0), tq - 1)
        rb = lax.broadcasted_iota(jnp.int32, sb.shape, 1)
        sb = jnp.where(tb + w_buf - rb < WINDOW, sb, NEG_INF)
        tn = jnp.bitwise_and(lax.broadcasted_iota(jnp.int32, sn.shape, 0), tq - 1)
        rn = lax.broadcasted_iota(jnp.int32, sn.shape, 1)
        sn = jnp.where(rn <= tn, sn, NEG_INF)
        m = jnp.maximum(jnp.max(sb, axis=1, keepdims=True), jnp.max(sn, axis=1, keepdims=True))
        pb = jnp.exp2(sb - m)
        pn = jnp.exp2(sn - m)
        l = jnp.sum(pb, axis=1, keepdims=True) + jnp.sum(pn, axis=1, keepdims=True)
        o4 = (jnp.dot(pb.astype(BF16), vb, preferred_element_type=F32)
              + jnp.dot(pn.astype(BF16), vn, preferred_element_type=F32)) / l
        for h in range(GQA_GROUP):
            hh = g * GQA_GROUP + h
            sl = slice(hh * HEAD_DIM, (hh + 1) * HEAD_DIM)
            o_ref[:, sl] = (prev_ref[:, sl] + o4[h * tq:(h + 1) * tq] * gate_ref[g, :, h:h + 1]).astype(out_dtype)


def win_attention_sample(q, win_buf, layer, kv_new, gates_t, prev, *, out_dtype):
    n_seq, buf_rows = win_buf.shape[1], win_buf.shape[2]
    tq = SAMPLE_PAD
    q_spec = pl.BlockSpec((tq, N_HEADS * HEAD_DIM), lambda b: (b, 0))
    kern = functools.partial(_win_sample_kernel, tq=tq, out_dtype=out_dtype)
    return pl.pallas_call(
        kern,
        grid=(n_seq,),
        in_specs=[
            q_spec,
            pl.BlockSpec((None, None, buf_rows, HEAD_DIM), lambda b: (layer, b, 0, 0)),
            pl.BlockSpec((tq * 2 * N_KV_HEADS, HEAD_DIM), lambda b: (b, 0)),
            pl.BlockSpec((None, N_KV_HEADS, tq, GQA_GROUP), lambda b: (2, 0, b, 0)),
            q_spec,
        ],
        out_specs=q_spec,
        out_shape=jax.ShapeDtypeStruct((n_seq * tq, N_HEADS * HEAD_DIM), out_dtype),
        compiler_params=_params("parallel"),
        name="nsa_win_attention_sample",
    )(q, win_buf, kv_new, gates_t, prev)


def _ret_kernel(q_ref, k_ref, v_ref, g_ref, gn_ref, dec_ref, qd_ref, kd_ref, gc_ref, s0_ref, o_ref, sout_ref,
                state_ref, *, n_chunks):
    c = pl.program_id(2)

    @pl.when(c == 0)
    def _():
        state_ref[...] = s0_ref[...]

    for h in range(state_ref.shape[0]):
        ksl = slice(h * RET_KDIM, (h + 1) * RET_KDIM)
        vsl = slice(h * RET_VDIM, (h + 1) * RET_VDIM)
        q = q_ref[:, ksl]
        kf = k_ref[:, ksl]
        v = v_ref[:, vsl]
        state = state_ref[h]
        s = lax.dot_general(q, kf.astype(BF16), _NT, preferred_element_type=F32) * dec_ref[h]
        inner = jnp.dot(s.astype(BF16), v, preferred_element_type=F32)
        cross = jnp.dot(q, state.astype(BF16), preferred_element_type=F32) * qd_ref[h]
        o = inner + cross
        kd = (kf * kd_ref[h]).astype(BF16)
        state_ref[h] = gc_ref[h] * state + lax.dot_general(kd, v, _TN, preferred_element_type=F32)
        mu = jnp.mean(o, axis=-1, keepdims=True)
        d = o - mu
        var = jnp.mean(d * d, axis=-1, keepdims=True)
        on = d * lax.rsqrt(var + GN_EPS) * gn_ref[:, vsl]
        g = g_ref[:, vsl]
        o_ref[:, vsl] = (g * jax.nn.sigmoid(g) * on).astype(o_ref.dtype)

    @pl.when(c == n_chunks - 1)
    def _():
        sout_ref[...] = state_ref[...]


def _ret_tables(chunk, n_tok):
    log_g = jnp.log(1.0 - 2.0 ** (-5.0 - jnp.arange(RET_HEADS, dtype=F32)))
    i = jnp.arange(chunk, dtype=F32)
    rel = i[:, None] - i[None, :]
    decay = jnp.where(rel >= 0, jnp.exp(jnp.maximum(rel, 0.0)[None] * log_g[:, None, None]), 0.0)
    q_dec = jnp.exp((i[None, :] + 1.0) * log_g[:, None])[..., None]
    k_dec = jnp.where(i[None, :] < n_tok, jnp.exp((n_tok - 1.0 - i)[None, :] * log_g[:, None]), 0.0)[..., None]
    g_c = jnp.exp(n_tok * log_g)[:, None, None]
    return decay, q_dec, k_dec, g_c


def retention(q, k, v, g, gn, layer, state0, *, n_seq, seq, chunk, n_tok):
    nc = seq // chunk
    hps = RET_HEADS_PER_STEP
    decay, q_dec, k_dec, g_c = _ret_tables(chunk, n_tok)
    kern = functools.partial(_ret_kernel, n_chunks=nc)
    tok_k = pl.BlockSpec((chunk, hps * RET_KDIM), lambda b, h, c: (b * nc + c, h))
    tok_v = pl.BlockSpec((chunk, hps * RET_VDIM), lambda b, h, c: (b * nc + c, h))
    st_spec = pl.BlockSpec((None, hps, RET_KDIM, RET_VDIM), lambda b, h, c: (b, h, 0, 0))
    return pl.pallas_call(
        kern,
        grid=(n_seq, RET_HEADS // hps, nc),
        in_specs=[
            tok_k, tok_k, tok_v, tok_v,
            pl.BlockSpec((None, 1, hps * RET_VDIM), lambda b, h, c: (layer, 0, h)),
            pl.BlockSpec((hps, chunk, chunk), lambda b, h, c: (h, 0, 0)),
            pl.BlockSpec((hps, chunk, 1), lambda b, h, c: (h, 0, 0)),
            pl.BlockSpec((hps, chunk, 1), lambda b, h, c: (h, 0, 0)),
            pl.BlockSpec((hps, 1, 1), lambda b, h, c: (h, 0, 0)),
            st_spec,
        ],
        out_specs=[tok_v, st_spec],
        out_shape=[
            jax.ShapeDtypeStruct((n_seq * seq, RET_HEADS * RET_VDIM), BF16),
            jax.ShapeDtypeStruct((n_seq, RET_HEADS, RET_KDIM, RET_VDIM), F32),
        ],
        scratch_shapes=[pltpu.VMEM((hps, RET_KDIM, RET_VDIM), F32)],
        compiler_params=_params("parallel", "parallel", "arbitrary"),
        name="retention_chunk",
    )(q, k, v, g, gn, decay, q_dec, k_dec, g_c, state0)


def _project_nsa(xn, w_in, layer, cos, sin_signed, *, tm):
    m, d = xn.shape
    nq = N_HEADS * HEAD_DIM
    half = N_KV_HEADS * HEAD_DIM
    rows_per_tok = 2 * N_KV_HEADS
    tabs_per_seq = cos.shape[0] // tm
    tab_spec = pl.BlockSpec((tm, HEAD_DIM), lambda i, j, k: (i % tabs_per_seq, 0))
    tn_q = 512
    q = matmul(
        [xn], w_in, layer, col_block0=0, n_cols=nq, tm=tm, tn=tn_q, tk=d,
        extras=[cos, sin_signed], extra_specs=[tab_spec, tab_spec],
        out_shapes=[jax.ShapeDtypeStruct((m, nq), BF16), jax.ShapeDtypeStruct((nq, m), BF16)],
        out_specs=[pl.BlockSpec((tm, tn_q), lambda i, j, k: (i, j)), pl.BlockSpec((tn_q, tm), lambda i, j, k: (j, i))],
        epilogue=functools.partial(_ep_rope_q, scale=HEAD_DIM ** -0.5 * LOG2_E), name="nsa_proj_q")
    kv = matmul(
        [xn], w_in, layer, col_block0=nq // half, n_cols=3 * KV_ROW, tm=tm, tn=half, tk=d,
        extras=[cos, sin_signed], extra_specs=[tab_spec, tab_spec],
        out_shapes=[jax.ShapeDtypeStruct((m * rows_per_tok, HEAD_DIM), F32)] * 3
        + [jax.ShapeDtypeStruct((3, N_KV_HEADS, m, HEAD_DIM), BF16),
           jax.ShapeDtypeStruct((3, N_KV_HEADS, HEAD_DIM, m), BF16)],
        out_specs=[pl.BlockSpec((tm * rows_per_tok, HEAD_DIM), lambda i, j, k: (i, 0))] * 3
        + [pl.BlockSpec((None, N_KV_HEADS, tm, HEAD_DIM), lambda i, j, k: (j // 2, 0, i, 0)),
           pl.BlockSpec((None, N_KV_HEADS, HEAD_DIM, tm), lambda i, j, k: (j // 2, 0, 0, i))],
        epilogue=_ep_nsa_kv, name="nsa_proj_kv", column_steps_ordered=True)
    kv = (kv[:3], kv[3], kv[4])
    n_gate = 3 * N_HEADS
    w_gate = jnp.pad(w_in[layer, :, nq + 3 * KV_ROW:], ((0, 0), (0, LANES - n_gate)))[None]
    gates_t = matmul(
        [xn], w_gate, 0, col_block0=0, n_cols=LANES, tm=tm, tn=LANES, tk=d, extras=[], extra_specs=[],
        out_shapes=[jax.ShapeDtypeStruct((3, N_KV_HEADS, m, GQA_GROUP), F32)],
        out_specs=[pl.BlockSpec((3, N_KV_HEADS, tm, GQA_GROUP), lambda i, j, k: (0, 0, i, 0))],
        epilogue=_ep_gates, name="nsa_proj_gate")[0]
    return q, kv, gates_t


def _cmp_bias(w1, pos, b1, layer):
    k_full = CMP_BLOCK * HEAD_DIM
    pos_rows = jnp.zeros((2, SUBLANES, k_full), F32).at[:, 0].set(pos[layer].reshape(2, k_full))
    terms = []
    for e in range(2):
        t = matmul(
            [pos_rows[e]], w1.reshape(-1, k_full, HEAD_DIM), layer * 2 + e, col_block0=0, n_cols=HEAD_DIM,
            tm=SUBLANES, tn=HEAD_DIM, tk=k_full, extras=[], extra_specs=[],
            out_shapes=[jax.ShapeDtypeStruct((SUBLANES, HEAD_DIM), F32)],
            out_specs=[pl.BlockSpec((SUBLANES, HEAD_DIM), lambda i, j, k: (0, 0))],
            epilogue=_ep_cast, name="nsa_cmp_pos_term")[0]
        terms.append(t[0])
    return jnp.stack(terms) + b1[layer]


def _mlp_up(xn, w1, layer, *, tm):
    m, d = xn.shape
    d_ff = w1.shape[-1]
    tn = 1024
    return matmul(
        [xn], w1, layer, col_block0=0, n_cols=d_ff, tm=tm, tn=tn, tk=d, extras=[], extra_specs=[],
        out_shapes=[jax.ShapeDtypeStruct((m, d_ff), BF16)],
        out_specs=[pl.BlockSpec((tm, tn), lambda i, j, k: (i, j))],
        epilogue=_ep_relu2, name="mlp_up")[0]


def _mixer_out_proj(x, w, layer, h, norm_g, *, name):
    m, d = h.shape
    kdim = x.shape[1]
    w_bytes = 2 * kdim * d * w.dtype.itemsize
    per_row = 2 * kdim * x.dtype.itemsize + d * (4 * 4 + 2 * 2 + 4)
    tm = min(m, (VMEM_LIMIT_BYTES * 7 // 8 - w_bytes) // per_row // LANES * LANES)
    while m % tm:
        tm -= LANES
    row = pl.BlockSpec((tm, d), lambda i, j, k: (i, 0))
    return matmul(
        [x], w, layer, col_block0=0, n_cols=d, tm=tm, tn=d, tk=kdim,
        extras=[h, norm_g.reshape(1, d)], extra_specs=[row, pl.BlockSpec((1, d), lambda i, j, k: (0, 0))],
        out_shapes=[jax.ShapeDtypeStruct((m, d), F32), jax.ShapeDtypeStruct((m, d), BF16)], out_specs=[row, row],
        epilogue=_ep_residual_norm, name=name)


def _mlp_down(a, w2, layer, h, *, tm):
    m, d = h.shape
    tn = 1024
    tm = min(tm, 1024)
    tk = 2048 if m >= tm * 2 else a.shape[1]
    spec = pl.BlockSpec((tm, tn), lambda i, j, k: (i, j))
    return matmul(
        [a], w2, layer, col_block0=0, n_cols=d, tm=tm, tn=tn, tk=tk, extras=[h], extra_specs=[spec],
        out_shapes=[jax.ShapeDtypeStruct((m, d), F32)], out_specs=[spec], epilogue=_ep_residual_norm,
        name="mlp_down")[0]


def _nsa_tables(pos):
    cos, sin = _rope_tables(pos, HEAD_DIM // 2)
    return jnp.concatenate([cos, cos], axis=1), jnp.concatenate([-sin, sin], axis=1)


def _nsa_prompt(xn, layer, w_in, cmp_w1, cmp_bias, cmp_w2, *, n_seq, seq):
    tm = PROMPT_ROW_TILE
    cos, sin_signed = _nsa_tables(jnp.arange(seq))
    (q, qt), (kv_rows, k_bf, vt_bf), gates_t = _project_nsa(xn, w_in, layer, cos, sin_signed, tm=tm // 2)
    n_chunks = seq // CMP_STRIDE
    chunks_per_step = 128
    kv_chunks = kv_rows[0].reshape(1, (n_seq * n_chunks) // chunks_per_step, chunks_per_step, CMP_STRIDE,
                                   2 * N_KV_HEADS, HEAD_DIM)
    kc = compress(kv_chunks, 0, None, cmp_w1, cmp_bias, cmp_w2, n_seq=n_seq, n_chunks=n_chunks, n_pages=1)
    n_blk = seq // SEL_BLOCK
    cover_t = _cover_t(n_chunks, n_blk, n_blk)
    o1, val = cmp_attention(q, kc, cover_t, gates_t, n_seq=n_seq, seq=seq, tq=1024, pos0=0)
    bias_t = select_bias(val, n_blk=n_blk, tc=512, causal_seq=seq)
    nbp = pl.cdiv(n_blk, LANES) * LANES
    o2 = selected_attention(qt, k_bf, vt_bf, 1, gates_t, o1, bias_t, _block_onehot(seq, nbp), n_seq=n_seq, seq=seq,
                            tq=512, tk=512)
    o3 = window_attention(qt, k_bf, vt_bf, 2, gates_t, o2, n_seq=n_seq, seq=seq, tq=256, window=WINDOW,
                          out_dtype=BF16)
    return o3, kv_rows


def _nsa_sample(xn, layer, w_in, cmp_w1, cmp_bias, cmp_w2, cache_cmp, cache_sel, win_buf, page_table,
                *, n_seq, n_tok, past_len):
    tq = SAMPLE_PAD
    m = n_seq * tq
    cos, sin_signed = _nsa_tables(jnp.tile(past_len + jnp.arange(tq), n_seq))
    (q, _), (kv_rows, _, _), gates_t = _project_nsa(xn, w_in, layer, cos, sin_signed, tm=m)
    n_chunks = past_len // CMP_STRIDE
    kc = compress(cache_cmp, layer, page_table, cmp_w1, cmp_bias, cmp_w2, n_seq=n_seq, n_chunks=n_chunks,
                  n_pages=32)
    n_blk = pl.cdiv(past_len + n_tok, SEL_BLOCK)
    nbv = pl.cdiv(n_blk, SUBLANES) * SUBLANES
    cover_t = _cover_t(n_chunks, n_blk, nbv)
    o1, val = cmp_attention(q, kc, cover_t, gates_t, n_seq=n_seq, seq=tq, tq=tq, pos0=past_len)
    ncols = n_seq * N_KV_HEADS * n_tok
    bias_t = select_bias(val[..., :n_tok].transpose(2, 0, 1, 3).reshape(nbv, ncols), n_blk=n_blk, tc=ncols)
    blocks_per_page = cache_sel.shape[2] // (2 * N_KV_HEADS * SEL_BLOCK)
    n_pages = page_table.shape[1]
    chosen = (bias_t == 0).reshape(nbv, n_seq, N_KV_HEADS * n_tok).any(axis=2)
    needed = chosen[:n_pages * blocks_per_page].reshape(n_pages, blocks_per_page, n_seq).any(axis=1).T
    nbp = pl.cdiv(n_blk + 1, LANES) * LANES
    bias = bias_t.reshape(nbv, n_seq, N_KV_HEADS, n_tok).transpose(1, 2, 3, 0)
    bias = jnp.pad(bias, ((0, 0), (0, 0), (0, 0), (0, nbp - nbv)), constant_values=NEG_INF)
    bias = jnp.pad(bias, ((0, 0), (0, 0), (0, tq - n_tok), (0, 0)))
    o2 = sel_attention_sample(q, bias, needed, kv_rows[1], cache_sel, layer, page_table, gates_t, o1,
                              n_pages_step=16, past_len=past_len)
    o3 = win_attention_sample(q, win_buf, layer, kv_rows[2], gates_t, o2, out_dtype=BF16)
    return o3, kv_rows


def _ret_layer(xn, layer, w_in, gn, state0, pos, *, n_seq, seq, chunk, n_tok, tm):
    m, d = xn.shape
    nk = RET_HEADS * RET_KDIM
    nv = RET_HEADS * RET_VDIM
    cos, sin = _rope_tables(pos, RET_KDIM // 2)
    tabs_per_seq = cos.shape[0] // tm
    tab_spec = pl.BlockSpec((tm, LANES), lambda i, j, k: (i % tabs_per_seq, 0))
    tn = 512

    def proj(col0, n_cols, dtype, epilogue, extras, extra_specs, name):
        return matmul(
            [xn], w_in, layer, col_block0=col0 // tn, n_cols=n_cols, tm=tm, tn=tn, tk=d,
            extras=extras, extra_specs=extra_specs,
            out_shapes=[jax.ShapeDtypeStruct((m, n_cols), dtype)],
            out_specs=[pl.BlockSpec((tm, tn), lambda i, j, k: (i, j))],
            epilogue=epilogue, name=name)[0]

    q = proj(0, nk, BF16, functools.partial(_ep_rope_ret, scale=1.0), [cos, sin], [tab_spec, tab_spec], "ret_proj_q")
    k = proj(nk, nk, F32, functools.partial(_ep_rope_ret, scale=RET_KDIM ** -0.5), [cos, sin],
             [tab_spec, tab_spec], "ret_proj_k")
    v = proj(2 * nk, nv, BF16, _ep_cast, [], [], "ret_proj_v")
    g = proj(2 * nk + nv, nv, F32, _ep_cast, [], [], "ret_proj_g")
    return retention(q, k, v, g, gn.reshape(gn.shape[0], 1, nv), layer, state0, n_seq=n_seq, seq=seq, chunk=chunk,
                     n_tok=n_tok)


def kernel(x_prompt, x_sample, cache_cmp_kv, cache_sel_kv, state_win_kv, state_ret, page_table, norm_mix, norm_ffn,
           norm_final, nsa_w_in, nsa_w_out, nsa_cmp_pos, nsa_cmp_w1, nsa_cmp_b1, nsa_cmp_w2, ret_w_in, ret_gn,
           ret_w_out, ffn_w1, ffn_w2):
    n_seq_p, seq_p, d = x_prompt.shape
    n_seq_s, n_tok_s, _ = x_sample.shape
    depth = norm_mix.shape[0]
    n_nsa = cache_sel_kv.shape[0]
    pool, page_rows = cache_sel_kv.shape[1], cache_sel_kv.shape[2]
    past_len = page_table.shape[1] * page_rows
    assert past_len % CMP_STRIDE == 0 and n_tok_s < CMP_STRIDE and n_tok_s <= SAMPLE_PAD
    assert state_win_kv.shape[2] == WINDOW and seq_p >= WINDOW

    hp = x_prompt.reshape(n_seq_p * seq_p, d)
    hs = jnp.pad(x_sample, ((0, 0), (0, SAMPLE_PAD - n_tok_s), (0, 0))).reshape(n_seq_s * SAMPLE_PAD, d)
    m_s = n_seq_s * SAMPLE_PAD
    rows_per_tok = 2 * N_KV_HEADS
    cache_cmp = cache_cmp_kv.reshape(n_nsa, pool, page_rows // CMP_STRIDE, CMP_STRIDE, rows_per_tok, HEAD_DIM)
    cache_sel = cache_sel_kv.reshape(n_nsa, pool, page_rows * rows_per_tok, HEAD_DIM)
    win_buf = state_win_kv.reshape(n_nsa, n_seq_s, WINDOW * rows_per_tok, HEAD_DIM)
    cmp_w1 = nsa_cmp_w1.reshape(n_nsa, 2, CMP_BLOCK * HEAD_DIM, HEAD_DIM)
    kv_shape = (2, N_KV_HEADS, HEAD_DIM)
    ffn_w2 = ffn_w2.astype(BF16)
    ret_w_out = ret_w_out.astype(BF16)
    nsa_w_out = nsa_w_out.astype(BF16)

    cmp_p, cmp_s, sel_p, sel_s, win_p, win_s, ret_p, ret_s = [], [], [], [], [], [], [], []
    xp = rmsnorm(hp, norm_mix[0], BF16, 256)
    xs = rmsnorm(hs, norm_mix[0], BF16, m_s)
    for layer in range(depth):
        if layer % 2 == 0:
            a = layer // 2
            cmp_bias = _cmp_bias(cmp_w1, nsa_cmp_pos, nsa_cmp_b1, a)
            op, kv_p = _nsa_prompt(xp, a, nsa_w_in, cmp_w1, cmp_bias, nsa_cmp_w2, n_seq=n_seq_p, seq=seq_p)
            os_, kv_s = _nsa_sample(xs, a, nsa_w_in, cmp_w1, cmp_bias, nsa_cmp_w2, cache_cmp, cache_sel, win_buf,
                                    page_table, n_seq=n_seq_s, n_tok=n_tok_s, past_len=past_len)
            w_out, w_idx = nsa_w_out, a
            kv_p = [r.reshape((n_seq_p, seq_p) + kv_shape) for r in kv_p]
            kv_s = [r.reshape((n_seq_s, SAMPLE_PAD) + kv_shape)[:, :n_tok_s] for r in kv_s]
            cmp_p.append(kv_p[0])
            sel_p.append(kv_p[1])
            win_p.append(kv_p[2][:, seq_p - WINDOW:])
            cmp_s.append(kv_s[0])
            sel_s.append(kv_s[1])
            win_s.append(jnp.concatenate([state_win_kv[a][:, n_tok_s:], kv_s[2]], axis=1))
        else:
            r = layer // 2
            zeros = jnp.zeros((n_seq_p, RET_HEADS, RET_KDIM, RET_VDIM), F32)
            op, st_p = _ret_layer(xp, r, ret_w_in, ret_gn, zeros, jnp.arange(seq_p), n_seq=n_seq_p, seq=seq_p,
                                  chunk=RET_CHUNK, n_tok=RET_CHUNK, tm=PROMPT_ROW_TILE)
            os_, st_s = _ret_layer(xs, r, ret_w_in, ret_gn, state_ret[r],
                                   jnp.tile(past_len + jnp.arange(SAMPLE_PAD), n_seq_s), n_seq=n_seq_s,
                                   seq=SAMPLE_PAD, chunk=SAMPLE_PAD, n_tok=n_tok_s, tm=m_s)
            ret_p.append(st_p)
            ret_s.append(st_s)
            w_out, w_idx = ret_w_out, r
        hp, xp = _mixer_out_proj(op, w_out, w_idx, hp, norm_ffn[layer], name="mixer_out_proj")
        hs, xs = _mixer_out_proj(os_, w_out, w_idx, hs, norm_ffn[layer], name="mixer_out_proj_sample")
        hp = _mlp_down(_mlp_up(xp, ffn_w1, layer, tm=PROMPT_ROW_TILE), ffn_w2, layer, hp, tm=PROMPT_ROW_TILE)
        hs = _mlp_down(_mlp_up(xs, ffn_w1, layer, tm=m_s), ffn_w2, layer, hs, tm=m_s)
        if layer + 1 < depth:
            xp = rmsnorm(hp, norm_mix[layer + 1], BF16, 256)
            xs = rmsnorm(hs, norm_mix[layer + 1], BF16, m_s)
    y_prompt = rmsnorm(hp, norm_final, F32, 256).reshape(n_seq_p, seq_p, d)
    y_sample = rmsnorm(hs, norm_final, F32, m_s).reshape(n_seq_s, SAMPLE_PAD, d)[:, :n_tok_s]
    return (y_prompt, y_sample, jnp.stack(cmp_p), jnp.stack(cmp_s), jnp.stack(sel_p), jnp.stack(sel_s),
            jnp.stack(win_p), jnp.stack(win_s), jnp.stack(ret_p), jnp.stack(ret_s))
```

```python
import functools

import numpy as np
import jax
import jax.numpy as jnp
from jax import lax
from jax.experimental import pallas as pl
from jax.experimental.pallas import tpu as pltpu

F32 = jnp.float32
BF16 = jnp.bfloat16

N_HEADS = 16
HEAD_DIM = 128
N_KV_HEADS = 4
GQA_GROUP = N_HEADS // N_KV_HEADS
KV_ROW = 2 * N_KV_HEADS * HEAD_DIM
CMP_BLOCK = 32
CMP_STRIDE = 16
SEL_BLOCK = 64
N_SEL = 16
N_LOCAL = 2
WINDOW = 512
ROPE_THETA = 10000.0
RET_HEADS = 8
RET_KDIM = 256
RET_VDIM = 512
RET_CHUNK = 256
RET_HEADS_PER_STEP = 4
RMS_EPS = 1e-6
GN_EPS = 1e-5
NEG_INF = -1e30
FORCE_SCORE = 1e9
LOG2_E = 1.4426950408889634

SUBLANES = 8
LANES = 128
VMEM_LIMIT_BYTES = 56 * 1024 * 1024
SAMPLE_PAD = 16
PROMPT_ROW_TILE = 2048

_NT = (((1,), (1,)), ((), ()))
_TN = (((0,), (0,)), ((), ()))


def _params(*semantics):
    return pltpu.CompilerParams(dimension_semantics=semantics, vmem_limit_bytes=VMEM_LIMIT_BYTES)


def _rmsnorm_kernel(x_ref, g_ref, o_ref):
    x = x_ref[...]
    ms = jnp.mean(x * x, axis=-1, keepdims=True)
    o_ref[...] = (x * lax.rsqrt(ms + RMS_EPS) * g_ref[...]).astype(o_ref.dtype)


def rmsnorm(x, g, out_dtype, tm, row_block0=0, n_blocks=None):
    m, d = x.shape
    if n_blocks is None:
        n_blocks = m // tm
    return pl.pallas_call(
        _rmsnorm_kernel,
        grid=(n_blocks,),
        in_specs=[pl.BlockSpec((tm, d), lambda i: (i + row_block0, 0)), pl.BlockSpec((1, d), lambda i: (0, 0))],
        out_specs=pl.BlockSpec((tm, d), lambda i: (i, 0)),
        out_shape=jax.ShapeDtypeStruct((n_blocks * tm, d), out_dtype),
        compiler_params=_params("parallel"),
        name="rmsnorm",
    )(x, g.reshape(1, d))


def _mm_kernel(*refs, n_x, n_extra, n_out, nk, epilogue):
    x_refs = refs[:n_x]
    w_ref = refs[n_x]
    extra = refs[n_x + 1:n_x + 1 + n_extra]
    outs = refs[n_x + 1 + n_extra:n_x + 1 + n_extra + n_out]
    x = x_refs[0][...]
    for r in x_refs[1:]:
        x = x + r[...]
    part = jnp.dot(x.astype(BF16), w_ref[...].astype(BF16), preferred_element_type=F32)
    if nk == 1:
        epilogue(part, extra, outs)
        return
    k = pl.program_id(2)
    if epilogue is _ep_residual_norm:
        @pl.when(k == 0)
        def _():
            outs[0][...] = extra[0][...] + part

        @pl.when(k > 0)
        def _():
            outs[0][...] += part

        @pl.when(k == nk - 1)
        def _():
            _write_norm(outs[0][...], extra, outs)
        return
    acc_ref = refs[-1]

    @pl.when(k == 0)
    def _():
        acc_ref[...] = part

    @pl.when(k > 0)
    def _():
        acc_ref[...] += part

    @pl.when(k == nk - 1)
    def _():
        epilogue(acc_ref[...], extra, outs)


def matmul(xs, w, layer, *, col_block0, n_cols, tm, tn, tk, extras, extra_specs, out_shapes, out_specs,
           epilogue, name, column_steps_ordered=False):
    m, kdim = xs[0].shape
    nk = kdim // tk
    grid = (m // tm, pl.cdiv(n_cols, tn), nk)
    in_specs = [pl.BlockSpec((tm, tk), lambda i, j, k: (i, k)) for _ in xs]
    in_specs.append(pl.BlockSpec((None, tk, tn), lambda i, j, k: (layer, k, j + col_block0)))
    in_specs.extend(extra_specs)
    scratch = [pltpu.VMEM((tm, tn), F32)] if nk > 1 and epilogue is not _ep_residual_norm else []
    kern = functools.partial(_mm_kernel, n_x=len(xs), n_extra=len(extras), n_out=len(out_shapes), nk=nk,
                             epilogue=epilogue)
    return pl.pallas_call(
        kern,
        grid=grid,
        in_specs=in_specs,
        out_specs=out_specs,
        out_shape=out_shapes,
        scratch_shapes=scratch,
        compiler_params=_params("parallel", "arbitrary" if column_steps_ordered else "parallel", "arbitrary"),
        name=name,
    )(*xs, w, *extras)


def _rope_half_lane(x, cos, sin_signed):
    return x * cos + pltpu.roll(x, HEAD_DIM // 2, 1) * sin_signed


def _ep_rope_q(part, extra, outs, *, scale):
    cos = extra[0][...]
    sin = extra[1][...]
    for h in range(part.shape[1] // HEAD_DIM):
        sl = slice(h * HEAD_DIM, (h + 1) * HEAD_DIM)
        r = _rope_half_lane(part[:, sl], cos, sin) * scale
        outs[0][:, sl] = r.astype(outs[0].dtype)
        outs[1][sl, :] = r.T.astype(outs[1].dtype)


def _ep_nsa_kv(part, extra, outs):
    tm = part.shape[0]
    rows_per_tok = 2 * N_KV_HEADS
    j = pl.program_id(1)

    def heads():
        return [part[:, h * HEAD_DIM:(h + 1) * HEAD_DIM] for h in range(N_KV_HEADS)]

    for br in range(3):
        @pl.when(j == 2 * br)
        def _(br=br):
            cos = extra[0][...]
            sin = extra[1][...]
            for h, x in enumerate(heads()):
                k = _rope_half_lane(x, cos, sin)
                outs[br][pl.ds(h, tm, stride=rows_per_tok), :] = k
                outs[3][h] = k.astype(BF16)

        @pl.when(j == 2 * br + 1)
        def _(br=br):
            for h, v in enumerate(heads()):
                outs[br][pl.ds(N_KV_HEADS + h, tm, stride=rows_per_tok), :] = v
                outs[4][h] = v.T.astype(BF16)


def _ep_gates(part, extra, outs):
    g = jax.nn.sigmoid(part)
    for br in range(3):
        for kvh in range(N_KV_HEADS):
            c = br * N_HEADS + kvh * GQA_GROUP
            outs[0][br, kvh] = g[:, c:c + GQA_GROUP]


def _write_norm(h, extra, outs):
    if len(extra) < 2:
        return
    ms = jnp.mean(h * h, axis=-1, keepdims=True)
    y = h * lax.rsqrt(ms + RMS_EPS) * extra[1][...]
    outs[-1][...] = y.astype(outs[-1].dtype)


def _ep_residual_norm(part, extra, outs):
    h = extra[0][...] + part
    outs[0][...] = h
    _write_norm(h, extra, outs)


def _ep_relu2(part, extra, outs):
    h = jnp.maximum(part, 0.0)
    outs[0][...] = (h * h).astype(outs[0].dtype)


def _ep_cast(part, extra, outs):
    outs[0][...] = part.astype(outs[0].dtype)


def _ep_rope_ret(part, extra, outs, *, scale):
    cos = extra[0][...]
    sin = extra[1][...]
    for h in range(part.shape[1] // RET_KDIM):
        a = slice(h * RET_KDIM, h * RET_KDIM + LANES)
        b = slice(h * RET_KDIM + LANES, (h + 1) * RET_KDIM)
        x1 = part[:, a]
        x2 = part[:, b]
        outs[0][:, a] = ((x1 * cos - x2 * sin) * scale).astype(outs[0].dtype)
        outs[0][:, b] = ((x2 * cos + x1 * sin) * scale).astype(outs[0].dtype)


def _rope_tables(pos, half):
    inv = ROPE_THETA ** (-jnp.arange(half, dtype=F32) / half)
    ang = pos.astype(F32)[:, None] * inv[None, :]
    return jnp.cos(ang), jnp.sin(ang)


def _compress_kernel(*refs, n_pages, paged):
    if paged:
        refs = refs[1:]
    page_refs = refs[:n_pages]
    next_ref, w1_ref, bias_ref, w2_ref, out_ref, shift_ref, y_ref = refs[n_pages:]
    nc = page_refs[0].shape[0]
    n_heads = N_KV_HEADS
    n_main = n_pages * nc * n_heads
    k_half = CMP_STRIDE * HEAD_DIM
    tiles = [[page_refs[p][:, s] for s in range(CMP_STRIDE)] for p in range(n_pages)]
    next_tiles = [next_ref[0, s] for s in range(CMP_STRIDE)]
    for e in range(2):
        hsl = slice(e * n_heads, (e + 1) * n_heads)
        blocks = [
            jnp.concatenate([tiles[p][s][:, hsl, :].reshape(nc * n_heads, HEAD_DIM) for s in range(CMP_STRIDE)],
                            axis=1)
            for p in range(n_pages)]
        blocks.append(jnp.concatenate([next_tiles[s][hsl, :] for s in range(CMP_STRIDE)], axis=1))
        blocks.append(jnp.zeros((SUBLANES - n_heads, CMP_STRIDE * HEAD_DIM), F32))
        x = jnp.concatenate(blocks, axis=0).astype(BF16)
        wcat = jnp.concatenate([w1_ref[e, :k_half], w1_ref[e, k_half:]], axis=1).astype(BF16)
        per = jnp.dot(x, wcat, preferred_element_type=F32)
        hid = per.shape[1] // 2
        shift_ref[...] = per[:, hid:]
        h = per[:n_main, :hid] + shift_ref[pl.ds(n_heads, n_main), :]
        h = jax.nn.gelu(h + bias_ref[e:e + 1, :])
        y_ref[...] = jnp.dot(h.astype(BF16), w2_ref[e].astype(BF16), preferred_element_type=F32)
        for g in range(n_heads):
            out_ref[e * n_heads + g] = y_ref[pl.ds(g, n_main // n_heads, stride=n_heads), :].astype(out_ref.dtype)


def compress(kv, layer, page_table, w1, bias, w2, *, n_seq, n_chunks, n_pages):
    paged = page_table is not None
    nc = kv.shape[2]
    n_main = n_pages * nc
    steps = n_chunks // n_main
    pages_per_seq = steps * n_pages
    rows_per_tok = 2 * N_KV_HEADS

    def nxt(s):
        return jnp.minimum((s + 1) * n_pages, pages_per_seq - 1)

    if paged:
        def page_map(p):
            return lambda b, s, pt: (layer, pt[b, s * n_pages + p], 0, 0, 0, 0)

        def next_map(b, s, pt):
            return (layer, pt[b, nxt(s)], 0, 0, 0, 0)

        def fixed(*idx):
            return lambda b, s, pt: idx

        out_map = lambda b, s, pt: (b, 0, s, 0)
    else:
        def page_map(p):
            return lambda b, s: (layer, b * pages_per_seq + s * n_pages + p, 0, 0, 0, 0)

        def next_map(b, s):
            return (layer, b * pages_per_seq + nxt(s), 0, 0, 0, 0)

        def fixed(*idx):
            return lambda b, s: idx

        out_map = lambda b, s: (b, 0, s, 0)

    tile = (CMP_STRIDE, rows_per_tok, HEAD_DIM)
    in_specs = [pl.BlockSpec((None, None, nc) + tile, page_map(p)) for p in range(n_pages)]
    in_specs += [
        pl.BlockSpec((None, None, 1) + tile, next_map),
        pl.BlockSpec((None, 2, CMP_BLOCK * HEAD_DIM, HEAD_DIM), fixed(layer, 0, 0, 0)),
        pl.BlockSpec((2, HEAD_DIM), fixed(0, 0)),
        pl.BlockSpec((None, 2, HEAD_DIM, HEAD_DIM), fixed(layer, 0, 0, 0)),
    ]
    kern = functools.partial(_compress_kernel, n_pages=n_pages, paged=paged)
    grid_spec = pltpu.PrefetchScalarGridSpec(
        num_scalar_prefetch=1 if paged else 0,
        grid=(n_seq, steps),
        in_specs=in_specs,
        out_specs=pl.BlockSpec((None, rows_per_tok, n_main, HEAD_DIM), out_map),
        scratch_shapes=[pltpu.VMEM((n_main * N_KV_HEADS + SUBLANES, HEAD_DIM), F32),
                        pltpu.VMEM((n_main * N_KV_HEADS, HEAD_DIM), F32)],
    )
    args = ([page_table] if paged else []) + [kv] * (n_pages + 1) + [w1, bias, w2]
    return pl.pallas_call(
        kern,
        grid_spec=grid_spec,
        out_shape=jax.ShapeDtypeStruct((n_seq, rows_per_tok, n_chunks, HEAD_DIM), BF16),
        compiler_params=_params("parallel", "arbitrary"),
        name="nsa_compress",
    )(*args)


def _group_rows(q):
    return jnp.concatenate([q[:, h * HEAD_DIM:(h + 1) * HEAD_DIM] for h in range(GQA_GROUP)], axis=0)


def _cmp_kernel(q_ref, k_ref, v_ref, cov_ref, gate_ref, o_ref, val_ref, *, tq, pos0):
    qi = pl.program_id(2)
    q4 = _group_rows(q_ref[...])
    s = lax.dot_general(q4, k_ref[...], _NT, preferred_element_type=F32)
    row = lax.broadcasted_iota(jnp.int32, s.shape, 0)
    col = lax.broadcasted_iota(jnp.int32, s.shape, 1)
    qpos = pos0 + qi * tq + jnp.bitwise_and(row, tq - 1)
    valid = col * CMP_STRIDE + (CMP_BLOCK - 1) <= qpos
    s = jnp.where(valid, s, NEG_INF)
    m = jnp.max(s, axis=1, keepdims=True)
    e = jnp.where(valid, jnp.exp2(s - m), 0.0)
    l = jnp.sum(e, axis=1, keepdims=True)
    p = e / jnp.where(l > 0.0, l, 1.0)
    o4 = jnp.dot(p.astype(BF16), v_ref[...], preferred_element_type=F32)
    for h in range(GQA_GROUP):
        o_ref[:, h * HEAD_DIM:(h + 1) * HEAD_DIM] = o4[h * tq:(h + 1) * tq] * gate_ref[:, h:h + 1]
    imp = p[0:tq] + p[tq:2 * tq] + p[2 * tq:3 * tq] + p[3 * tq:4 * tq]
    hi = imp.astype(BF16)
    r1 = imp - hi.astype(F32)
    mid = r1.astype(BF16)
    lo = (r1 - mid.astype(F32)).astype(BF16)
    cov = cov_ref[...]
    score = (lax.dot_general(cov, hi, _NT, preferred_element_type=F32)
             + lax.dot_general(cov, mid, _NT, preferred_element_type=F32)
             + lax.dot_general(cov, lo, _NT, preferred_element_type=F32))
    jb = lax.broadcasted_iota(jnp.int32, score.shape, 0)
    qp = pos0 + qi * tq + lax.broadcasted_iota(jnp.int32, score.shape, 1)
    cur = jnp.right_shift(qp, SEL_BLOCK.bit_length() - 1)
    visible = jb <= cur
    forced = (jb == 0) | (visible & (jb > cur - N_LOCAL))
    val_ref[...] = jnp.where(forced, FORCE_SCORE, jnp.where(visible, score, -FORCE_SCORE))


def cmp_attention(q, kc, cover_t, gates_t, *, n_seq, seq, tq, pos0):
    nq = seq // tq
    ncp = kc.shape[2]
    nbv = cover_t.shape[0]
    gw = GQA_GROUP * HEAD_DIM
    kern = functools.partial(_cmp_kernel, tq=tq, pos0=pos0)
    if tq % LANES == 0:
        val_spec = pl.BlockSpec((nbv, tq), lambda b, g, i: (0, (b * N_KV_HEADS + g) * nq + i))
        val_shape = jax.ShapeDtypeStruct((nbv, n_seq * N_KV_HEADS * seq), F32)
    else:
        val_spec = pl.BlockSpec((None, None, nbv, tq), lambda b, g, i: (b, g, 0, i))
        val_shape = jax.ShapeDtypeStruct((n_seq, N_KV_HEADS, nbv, seq), F32)
    return pl.pallas_call(
        kern,
        grid=(n_seq, N_KV_HEADS, nq),
        in_specs=[
            pl.BlockSpec((tq, gw), lambda b, g, i: (b * nq + i, g)),
            pl.BlockSpec((None, None, ncp, HEAD_DIM), lambda b, g, i: (b, g, 0, 0)),
            pl.BlockSpec((None, None, ncp, HEAD_DIM), lambda b, g, i: (b, N_KV_HEADS + g, 0, 0)),
            pl.BlockSpec((nbv, ncp), lambda b, g, i: (0, 0)),
            pl.BlockSpec((None, None, tq, GQA_GROUP), lambda b, g, i: (0, g, b * nq + i, 0)),
        ],
        out_specs=[pl.BlockSpec((tq, gw), lambda b, g, i: (b * nq + i, g)), val_spec],
        out_shape=[jax.ShapeDtypeStruct((n_seq * seq, N_HEADS * HEAD_DIM), F32), val_shape],
        compiler_params=_params("parallel", "parallel", "parallel"),
        name="nsa_cmp_attention",
    )(q, kc, kc, cover_t, gates_t)


def _cover_t(n_chunks, n_blk, nbv):
    n_cmp = n_chunks - CMP_BLOCK // CMP_STRIDE + 1
    c0 = np.arange(n_chunks) * CMP_STRIDE
    j0 = np.arange(nbv) * SEL_BLOCK
    cov = (c0[None, :] < j0[:, None] + SEL_BLOCK) & (c0[None, :] + CMP_BLOCK > j0[:, None])
    cov &= (np.arange(n_chunks)[None, :] < n_cmp) & (np.arange(nbv)[:, None] < n_blk)
    return jnp.asarray(cov.astype(np.float32), dtype=BF16)


def _rank_kernel(val_ref, out_ref, *, n_blk, n_sel, causal_seq):
    val = val_ref[...]
    tc = val.shape[1]
    jb = lax.broadcasted_iota(jnp.int32, val.shape, 0)
    if causal_seq is None:
        n_iter = n_blk
    else:
        pos0 = (pl.program_id(0) % (causal_seq // tc)) * tc
        n_iter = jnp.minimum(n_blk, (pos0 + tc - 1) // SEL_BLOCK + 1)

    def body(i, rank):
        row = val_ref[pl.ds(i, 1), :]
        beats = (row > val) | ((row == val) & (i < jb))
        return rank + jnp.where(beats, 1, 0)

    rank = lax.fori_loop(0, n_iter, body, jnp.zeros(val.shape, jnp.int32))
    chosen = (rank < n_sel) & (val > -0.5 * FORCE_SCORE)
    out_ref[...] = jnp.where(chosen, 0.0, NEG_INF).astype(out_ref.dtype)


def select_bias(val, *, n_blk, tc, causal_seq=None):
    nbv, ncols = val.shape
    kern = functools.partial(_rank_kernel, n_blk=n_blk, n_sel=min(N_SEL, n_blk), causal_seq=causal_seq)
    return pl.pallas_call(
        kern,
        grid=(ncols // tc,),
        in_specs=[pl.BlockSpec((nbv, tc), lambda i: (0, i))],
        out_specs=pl.BlockSpec((nbv, tc), lambda i: (0, i)),
        out_shape=jax.ShapeDtypeStruct((nbv, ncols), BF16),
        compiler_params=_params("parallel"),
        name="nsa_select_rank",
    )(val)


_STEP_FIRST, _STEP_LAST, _STEP_EDGE = 1, 2, 4


def _flash_steps(seq, tq, tk):
    qi_l, kt_l, fl_l = [], [], []
    for qi in range(seq // tq):
        q0, q1 = qi * tq, qi * tq + tq - 1
        kts = [kt for kt in range(seq // tk) if kt * tk <= q1]
        for n, kt in enumerate(kts):
            all_visible = q0 - (kt * tk + tk - 1) >= 0
            flags = (_STEP_FIRST if n == 0 else 0) | (_STEP_LAST if n == len(kts) - 1 else 0)
            flags |= 0 if all_visible else _STEP_EDGE
            qi_l.append(qi)
            kt_l.append(kt)
            fl_l.append(flags)
    return (jnp.asarray(np.array(qi_l, np.int32)), jnp.asarray(np.array(kt_l, np.int32)),
            jnp.asarray(np.array(fl_l, np.int32)))


def _flash_kernel(qi_ref, kt_ref, flag_ref, qt_ref, bias_ref, k_ref, vt_ref, e_ref, gate_ref, prev_ref, o_ref,
                  qa_ref, m_ref, acc_ref, *, tq, tk, out_dtype):
    s = pl.program_id(2)
    qi = qi_ref[s]
    kt = kt_ref[s]
    flags = flag_ref[s]
    first = jnp.bitwise_and(flags, _STEP_FIRST) != 0
    last = jnp.bitwise_and(flags, _STEP_LAST) != 0
    edge = jnp.bitwise_and(flags, _STEP_EDGE) != 0

    @pl.when(first)
    def _():
        nbv = bias_ref.shape[0]
        for h in range(GQA_GROUP):
            qa_ref[0:HEAD_DIM, h * tq:(h + 1) * tq] = qt_ref[h * HEAD_DIM:(h + 1) * HEAD_DIM, :]
            qa_ref[HEAD_DIM:HEAD_DIM + nbv, h * tq:(h + 1) * tq] = bias_ref[...]
            if HEAD_DIM + nbv < qa_ref.shape[0]:
                qa_ref[HEAD_DIM + nbv:, h * tq:(h + 1) * tq] = jnp.zeros((qa_ref.shape[0] - HEAD_DIM - nbv, tq), BF16)
        m_ref[...] = jnp.full(m_ref.shape, NEG_INF, F32)
        acc_ref[...] = jnp.zeros(acc_ref.shape, F32)

    def step(masked):
        ka = jnp.concatenate([k_ref[...], e_ref[...]], axis=1)
        st = jnp.dot(ka, qa_ref[...], preferred_element_type=F32)
        if masked:
            kpos = kt * tk + lax.broadcasted_iota(jnp.int32, st.shape, 0)
            qpos = qi * tq + jnp.bitwise_and(lax.broadcasted_iota(jnp.int32, st.shape, 1), tq - 1)
            st = jnp.where(qpos >= kpos, st, NEG_INF)
        m_old = m_ref[...]
        m_new = jnp.maximum(m_old, jnp.max(st, axis=0, keepdims=True))
        alpha = jnp.exp2(m_old - m_new)
        p = jnp.exp2(st - m_new).astype(BF16)
        va = jnp.concatenate([vt_ref[...], jnp.ones((acc_ref.shape[0] - HEAD_DIM, tk), BF16)], axis=0)
        acc_ref[...] = alpha * acc_ref[...] + jnp.dot(va, p, preferred_element_type=F32)
        m_ref[...] = m_new

    @pl.when(edge)
    def _():
        step(True)

    @pl.when(jnp.logical_not(edge))
    def _():
        step(False)

    @pl.when(last)
    def _():
        acc = acc_ref[...]
        ot = acc[:HEAD_DIM] / acc[HEAD_DIM:HEAD_DIM + 1]
        for h in range(GQA_GROUP):
            sl = slice(h * HEAD_DIM, (h + 1) * HEAD_DIM)
            o_h = ot[:, h * tq:(h + 1) * tq].T
            o_ref[:, sl] = (prev_ref[:, sl] + o_h * gate_ref[:, h:h + 1]).astype(out_dtype)


def selected_attention(qt, k, vt, branch, gates_t, prev, bias, onehot, *, n_seq, seq, tq, tk, out_dtype=F32):
    nq = seq // tq
    nkt = seq // tk
    gw = GQA_GROUP * HEAD_DIM
    qi_tab, kt_tab, flag_tab = _flash_steps(seq, tq, tk)
    nbv = bias.shape[0]
    nbp = onehot.shape[1]

    o_spec = pl.BlockSpec((tq, gw), lambda b, g, s, qi, kt, fl: (b * nq + qi[s], g))
    qt_spec = pl.BlockSpec((gw, tq), lambda b, g, s, qi, kt, fl: (g, b * nq + qi[s]))
    k_spec = pl.BlockSpec((None, None, tk, HEAD_DIM), lambda b, g, s, qi, kt, fl: (branch, g, b * nkt + kt[s], 0))
    vt_spec = pl.BlockSpec((None, None, HEAD_DIM, tk), lambda b, g, s, qi, kt, fl: (branch, g, 0, b * nkt + kt[s]))
    gate_spec = pl.BlockSpec((None, None, tq, GQA_GROUP), lambda b, g, s, qi, kt, fl: (branch, g, b * nq + qi[s], 0))
    bias_spec = pl.BlockSpec((nbv, tq), lambda b, g, s, qi, kt, fl: (0, (b * N_KV_HEADS + g) * nq + qi[s]))
    onehot_spec = pl.BlockSpec((tk, nbp), lambda b, g, s, qi, kt, fl: (kt[s], 0))
    in_specs = [qt_spec, bias_spec, k_spec, vt_spec, onehot_spec, gate_spec, o_spec]
    args = [qt, bias, k, vt, onehot, gates_t, prev]
    qa_rows = HEAD_DIM + nbp
    kern = functools.partial(_flash_kernel, tq=tq, tk=tk, out_dtype=out_dtype)
    cols = GQA_GROUP * tq
    grid_spec = pltpu.PrefetchScalarGridSpec(
        num_scalar_prefetch=3,
        grid=(n_seq, N_KV_HEADS, qi_tab.shape[0]),
        in_specs=in_specs,
        out_specs=o_spec,
        scratch_shapes=[
            pltpu.VMEM((qa_rows, cols), BF16),
            pltpu.VMEM((1, cols), F32),
            pltpu.VMEM((HEAD_DIM + 2 * SUBLANES, cols), F32),
        ],
    )
    return pl.pallas_call(
        kern,
        grid_spec=grid_spec,
        out_shape=jax.ShapeDtypeStruct((n_seq * seq, N_HEADS * HEAD_DIM), out_dtype),
        compiler_params=_params("parallel", "parallel", "arbitrary"),
        name="nsa_sel_attention",
    )(qi_tab, kt_tab, flag_tab, *args)


def _win_kernel(*refs, tq, n_tiles, window, out_dtype):
    qt_ref = refs[0]
    k_refs = refs[1:1 + n_tiles]
    vt_refs = refs[1 + n_tiles:1 + 2 * n_tiles]
    gate_ref, prev_ref, o_ref = refs[1 + 2 * n_tiles:]
    qi = pl.program_id(2)
    qa = jnp.concatenate([qt_ref[h * HEAD_DIM:(h + 1) * HEAD_DIM, :] for h in range(GQA_GROUP)], axis=1)
    sts = []
    for j in range(n_tiles):
        kt = qi - (n_tiles - 1) + j
        st = jnp.dot(k_refs[j][...], qa, preferred_element_type=F32)
        if j == 0 or j == n_tiles - 1:
            kpos = kt * tq + lax.broadcasted_iota(jnp.int32, st.shape, 0)
            qpos = qi * tq + jnp.bitwise_and(lax.broadcasted_iota(jnp.int32, st.shape, 1), tq - 1)
            d = qpos - kpos
            ok = (d >= 0) & (d < window) & (kpos >= 0)
            st = jnp.where(ok, st, NEG_INF)
        else:
            st = jnp.where(kt >= 0, st, NEG_INF)
        sts.append(st)
    st = jnp.concatenate(sts, axis=0)
    m = jnp.max(st, axis=0, keepdims=True)
    p = jnp.exp2(st - m).astype(BF16)
    vt = jnp.concatenate([r[...] for r in vt_refs], axis=1)
    va = jnp.concatenate([vt, jnp.ones((2 * SUBLANES, vt.shape[1]), BF16)], axis=0)
    acc = jnp.dot(va, p, preferred_element_type=F32)
    ot = acc[:HEAD_DIM] / acc[HEAD_DIM:HEAD_DIM + 1]
    for h in range(GQA_GROUP):
        sl = slice(h * HEAD_DIM, (h + 1) * HEAD_DIM)
        o_h = ot[:, h * tq:(h + 1) * tq].T
        o_ref[:, sl] = (prev_ref[:, sl] + o_h * gate_ref[:, h:h + 1]).astype(out_dtype)


def window_attention(qt, k, vt, branch, gates_t, prev, *, n_seq, seq, tq, window, out_dtype):
    nq = seq // tq
    n_tiles = window // tq + 1
    gw = GQA_GROUP * HEAD_DIM

    def kt_of(i, j):
        return jnp.maximum(i - (n_tiles - 1) + j, 0)

    o_spec = pl.BlockSpec((tq, gw), lambda b, g, i: (b * nq + i, g))
    in_specs = [pl.BlockSpec((gw, tq), lambda b, g, i: (g, b * nq + i))]
    in_specs += [pl.BlockSpec((None, None, tq, HEAD_DIM), lambda b, g, i, j=j: (branch, g, b * nq + kt_of(i, j), 0))
                 for j in range(n_tiles)]
    in_specs += [pl.BlockSpec((None, None, HEAD_DIM, tq), lambda b, g, i, j=j: (branch, g, 0, b * nq + kt_of(i, j)))
                 for j in range(n_tiles)]
    in_specs += [pl.BlockSpec((None, None, tq, GQA_GROUP), lambda b, g, i: (branch, g, b * nq + i, 0)), o_spec]
    kern = functools.partial(_win_kernel, tq=tq, n_tiles=n_tiles, window=window, out_dtype=out_dtype)
    return pl.pallas_call(
        kern,
        grid=(n_seq, N_KV_HEADS, nq),
        in_specs=in_specs,
        out_specs=o_spec,
        out_shape=jax.ShapeDtypeStruct((n_seq * seq, N_HEADS * HEAD_DIM), out_dtype),
        compiler_params=_params("parallel", "parallel", "parallel"),
        name="nsa_win_attention",
    )(qt, *([k] * n_tiles), *([vt] * n_tiles), gates_t, prev)


def _block_onehot(n_keys, nbp):
    e = (np.arange(n_keys)[:, None] // SEL_BLOCK) == np.arange(nbp)[None, :]
    return jnp.asarray(e.astype(np.float32), dtype=BF16)


def _sel_sample_kernel(*refs, n_pages, nbp, past_len, nsteps, tq):
    logical_ref, count_ref = refs[1:3]
    qt_ref, bias_ref, new_ref = refs[3:6]
    page_refs = refs[6:6 + n_pages]
    gate_ref, prev_ref, o_ref, m_ref, acc_ref = refs[6 + n_pages:]
    b = pl.program_id(0)
    step = pl.program_id(1)
    page_rows = page_refs[0].shape[0]
    rows_per_tok = 2 * N_KV_HEADS
    page_tokens = page_rows // rows_per_tok
    blocks_per_page = page_tokens // SEL_BLOCK
    n_needed = count_ref[b]
    cols = qt_ref.shape[2]

    def scores(g, k, key_bias):
        return jnp.dot(k, qt_ref[g], preferred_element_type=F32) + key_bias

    def block_bias(g, first_block, n_blocks, valid):
        rows = bias_ref[g, pl.ds(first_block, n_blocks), :]
        rows = jnp.where(valid, rows, NEG_INF)
        return jnp.concatenate(
            [jnp.broadcast_to(rows[j:j + 1], (SEL_BLOCK, cols)) for j in range(n_blocks)], axis=0)

    def weighted_values(p, v):
        va = jnp.concatenate([v, jnp.ones(v.shape, BF16)], axis=1)
        return lax.dot_general(p, va, _TN, preferred_element_type=F32)

    @pl.when(step == 0)
    def _():
        for g in range(N_KV_HEADS):
            kn = new_ref[pl.ds(g, tq, stride=2 * N_KV_HEADS), :].astype(BF16)
            vn = new_ref[pl.ds(N_KV_HEADS + g, tq, stride=2 * N_KV_HEADS), :].astype(BF16)
            new_bias = block_bias(g, past_len // SEL_BLOCK, 1, True)[:tq]
            st = scores(g, kn, new_bias)
            key = lax.broadcasted_iota(jnp.int32, st.shape, 0)
            tok = jnp.bitwise_and(lax.broadcasted_iota(jnp.int32, st.shape, 1), tq - 1)
            st = jnp.where(key <= tok, st, NEG_INF)
            m = jnp.max(st, axis=0, keepdims=True)
            p = jnp.exp2(st - m).astype(BF16)
            m_ref[g] = m
            acc_ref[g] = weighted_values(p, vn)

    @pl.when(step * n_pages < n_needed)
    def _():
        slots = [step * n_pages + p for p in range(n_pages)]
        first_blocks = [logical_ref[b, slot] * blocks_per_page for slot in slots]
        for g in range(N_KV_HEADS):
            k = jnp.concatenate(
                [page_refs[p][pl.ds(g, page_tokens, stride=rows_per_tok), :] for p in range(n_pages)],
                axis=0).astype(BF16)
            v = jnp.concatenate(
                [page_refs[p][pl.ds(N_KV_HEADS + g, page_tokens, stride=rows_per_tok), :] for p in range(n_pages)],
                axis=0).astype(BF16)
            key_bias = jnp.concatenate(
                [block_bias(g, first_blocks[p], blocks_per_page, slots[p] < n_needed) for p in range(n_pages)],
                axis=0)
            st = scores(g, k, key_bias)
            m_old = m_ref[g]
            m_new = jnp.maximum(m_old, jnp.max(st, axis=0, keepdims=True))
            alpha = jnp.exp2(m_old - m_new)
            p = jnp.exp2(st - m_new).astype(BF16)
            n_cols = alpha.shape[1]
            eye = (lax.broadcasted_iota(jnp.int32, (n_cols, n_cols), 0)
                   == lax.broadcasted_iota(jnp.int32, (n_cols, n_cols), 1))
            alpha_col = jnp.sum(jnp.where(eye, alpha, 0.0), axis=1, keepdims=True)
            acc_ref[g] = alpha_col * acc_ref[g] + weighted_values(p, v)
            m_ref[g] = m_new

    @pl.when(step == nsteps - 1)
    def _():
        for g in range(N_KV_HEADS):
            acc = acc_ref[g]
            o4 = acc[:, :HEAD_DIM] / acc[:, HEAD_DIM:HEAD_DIM + 1]
            for h in range(GQA_GROUP):
                hh = g * GQA_GROUP + h
                sl = slice(hh * HEAD_DIM, (hh + 1) * HEAD_DIM)
                o_ref[:, sl] = prev_ref[:, sl] + o4[h * tq:(h + 1) * tq] * gate_ref[g, :, h:h + 1]


def sel_attention_sample(q, bias, needed, kv_new, cache, layer, page_table, gates_t, prev, *, n_pages_step,
                         past_len):
    n_seq, pages = page_table.shape
    tq = SAMPLE_PAD
    nbp = bias.shape[-1]
    page_rows = cache.shape[2]
    nsteps = pages // n_pages_step
    dm = N_HEADS * HEAD_DIM
    order = jnp.argsort(jnp.logical_not(needed), axis=1, stable=True).astype(jnp.int32)
    count = jnp.sum(needed, axis=1).astype(jnp.int32)
    slot = jnp.minimum(jnp.arange(pages, dtype=jnp.int32)[None, :], count[:, None] - 1)
    logical = jnp.take_along_axis(order, slot, axis=1)
    physical = jnp.take_along_axis(page_table, logical, axis=1)
    row_spec = pl.BlockSpec((tq, dm), lambda b, s, *_: (b, 0))
    assert past_len % SEL_BLOCK + tq <= SEL_BLOCK
    cols = GQA_GROUP * tq
    q_t = q.reshape(n_seq, tq, N_KV_HEADS, GQA_GROUP, HEAD_DIM).transpose(0, 2, 4, 3, 1)
    q_t = q_t.reshape(n_seq, N_KV_HEADS, HEAD_DIM, cols)
    bias_t = jnp.tile(bias.transpose(0, 1, 3, 2), (1, 1, 1, GQA_GROUP)).astype(F32)

    def page_map(p):
        return lambda b, s, phys, *_: (layer, phys[b, s * n_pages_step + p], 0, 0)

    in_specs = [
        pl.BlockSpec((None, N_KV_HEADS, HEAD_DIM, cols), lambda b, s, *_: (b, 0, 0, 0)),
        pl.BlockSpec((None, N_KV_HEADS, nbp, cols), lambda b, s, *_: (b, 0, 0, 0)),
        pl.BlockSpec((tq * 2 * N_KV_HEADS, HEAD_DIM), lambda b, s, *_: (b, 0)),
    ]
    in_specs += [pl.BlockSpec((None, None, page_rows, HEAD_DIM), page_map(p)) for p in range(n_pages_step)]
    in_specs += [
        pl.BlockSpec((None, N_KV_HEADS, tq, GQA_GROUP), lambda b, s, *_: (1, 0, b, 0)),
        row_spec,
    ]
    kern = functools.partial(_sel_sample_kernel, n_pages=n_pages_step, nbp=nbp, past_len=past_len, nsteps=nsteps,
                             tq=tq)
    grid_spec = pltpu.PrefetchScalarGridSpec(
        num_scalar_prefetch=3,
        grid=(n_seq, nsteps),
        in_specs=in_specs,
        out_specs=row_spec,
        scratch_shapes=[
            pltpu.VMEM((N_KV_HEADS, 1, cols), F32),
            pltpu.VMEM((N_KV_HEADS, cols, 2 * HEAD_DIM), F32),
        ],
    )
    return pl.pallas_call(
        kern,
        grid_spec=grid_spec,
        out_shape=jax.ShapeDtypeStruct((n_seq * tq, dm), F32),
        compiler_params=_params("parallel", "arbitrary"),
        name="nsa_sel_attention_sample",
    )(physical, logical, count, q_t, bias_t, kv_new, *([cache] * n_pages_step), gates_t, prev)


def _win_sample_kernel(q_ref, buf_ref, new_ref, gate_ref, prev_ref, o_ref, *, tq, out_dtype):
    rows_per_tok = 2 * N_KV_HEADS
    w_buf = buf_ref.shape[0] // rows_per_tok
    gw = GQA_GROUP * HEAD_DIM
    for g in range(N_KV_HEADS):
        q4 = _group_rows(q_ref[:, g * gw:(g + 1) * gw])
        kb = buf_ref[pl.ds(g, w_buf, stride=rows_per_tok), :].astype(BF16)
        vb = buf_ref[pl.ds(N_KV_HEADS + g, w_buf, stride=rows_per_tok), :].astype(BF16)
        kn = new_ref[pl.ds(g, tq, stride=rows_per_tok), :].astype(BF16)
        vn = new_ref[pl.ds(N_KV_HEADS + g, tq, stride=rows_per_tok), :].astype(BF16)
        sb = lax.dot_general(q4, kb, _NT, preferred_element_type=F32)
        sn = lax.dot_general(q4, kn, _NT, preferred_element_type=F32)
        tb = jnp.bitwise_and(lax.broadcasted_iota(jnp.int32, sb.shape, 0), tq - 1)
        rb = lax.broadcasted_iota(jnp.int32, sb.shape, 1)
        sb = jnp.where(tb + w_buf - rb < WINDOW, sb, NEG_INF)
        tn = jnp.bitwise_and(lax.broadcasted_iota(jnp.int32, sn.shape, 0), tq - 1)
        rn = lax.broadcasted_iota(jnp.int32, sn.shape, 1)
        sn = jnp.where(rn <= tn, sn, NEG_INF)
        m = jnp.maximum(jnp.max(sb, axis=1, keepdims=True), jnp.max(sn, axis=1, keepdims=True))
        pb = jnp.exp2(sb - m)
        pn = jnp.exp2(sn - m)
        l = jnp.sum(pb, axis=1, keepdims=True) + jnp.sum(pn, axis=1, keepdims=True)
        o4 = (jnp.dot(pb.astype(BF16), vb, preferred_element_type=F32)
              + jnp.dot(pn.astype(BF16), vn, preferred_element_type=F32)) / l
        for h in range(GQA_GROUP):
            hh = g * GQA_GROUP + h
            sl = slice(hh * HEAD_DIM, (hh + 1) * HEAD_DIM)
            o_ref[:, sl] = (prev_ref[:, sl] + o4[h * tq:(h + 1) * tq] * gate_ref[g, :, h:h + 1]).astype(out_dtype)


def win_attention_sample(q, win_buf, layer, kv_new, gates_t, prev, *, out_dtype):
    n_seq, buf_rows = win_buf.shape[1], win_buf.shape[2]
    tq = SAMPLE_PAD
    q_spec = pl.BlockSpec((tq, N_HEADS * HEAD_DIM), lambda b: (b, 0))
    kern = functools.partial(_win_sample_kernel, tq=tq, out_dtype=out_dtype)
    return pl.pallas_call(
        kern,
        grid=(n_seq,),
        in_specs=[
            q_spec,
            pl.BlockSpec((None, None, buf_rows, HEAD_DIM), lambda b: (layer, b, 0, 0)),
            pl.BlockSpec((tq * 2 * N_KV_HEADS, HEAD_DIM), lambda b: (b, 0)),
            pl.BlockSpec((None, N_KV_HEADS, tq, GQA_GROUP), lambda b: (2, 0, b, 0)),
            q_spec,
        ],
        out_specs=q_spec,
        out_shape=jax.ShapeDtypeStruct((n_seq * tq, N_HEADS * HEAD_DIM), out_dtype),
        compiler_params=_params("parallel"),
        name="nsa_win_attention_sample",
    )(q, win_buf, kv_new, gates_t, prev)


def _ret_kernel(q_ref, k_ref, v_ref, g_ref, gn_ref, dec_ref, qd_ref, kd_ref, gc_ref, s0_ref, o_ref, sout_ref,
                state_ref, *, n_chunks):
    c = pl.program_id(2)

    @pl.when(c == 0)
    def _():
        state_ref[...] = s0_ref[...]

    for h in range(state_ref.shape[0]):
        ksl = slice(h * RET_KDIM, (h + 1) * RET_KDIM)
        vsl = slice(h * RET_VDIM, (h + 1) * RET_VDIM)
        q = q_ref[:, ksl]
        kf = k_ref[:, ksl]
        v = v_ref[:, vsl]
        state = state_ref[h]
        s = lax.dot_general(q, kf.astype(BF16), _NT, preferred_element_type=F32) * dec_ref[h]
        inner = jnp.dot(s.astype(BF16), v, preferred_element_type=F32)
        cross = jnp.dot(q, state.astype(BF16), preferred_element_type=F32) * qd_ref[h]
        o = inner + cross
        kd = (kf * kd_ref[h]).astype(BF16)
        state_ref[h] = gc_ref[h] * state + lax.dot_general(kd, v, _TN, preferred_element_type=F32)
        mu = jnp.mean(o, axis=-1, keepdims=True)
        d = o - mu
        var = jnp.mean(d * d, axis=-1, keepdims=True)
        on = d * lax.rsqrt(var + GN_EPS) * gn_ref[:, vsl]
        g = g_ref[:, vsl]
        o_ref[:, vsl] = (g * jax.nn.sigmoid(g) * on).astype(o_ref.dtype)

    @pl.when(c == n_chunks - 1)
    def _():
        sout_ref[...] = state_ref[...]


def _ret_tables(chunk, n_tok):
    log_g = jnp.log(1.0 - 2.0 ** (-5.0 - jnp.arange(RET_HEADS, dtype=F32)))
    i = jnp.arange(chunk, dtype=F32)
    rel = i[:, None] - i[None, :]
    decay = jnp.where(rel >= 0, jnp.exp(jnp.maximum(rel, 0.0)[None] * log_g[:, None, None]), 0.0)
    q_dec = jnp.exp((i[None, :] + 1.0) * log_g[:, None])[..., None]
    k_dec = jnp.where(i[None, :] < n_tok, jnp.exp((n_tok - 1.0 - i)[None, :] * log_g[:, None]), 0.0)[..., None]
    g_c = jnp.exp(n_tok * log_g)[:, None, None]
    return decay, q_dec, k_dec, g_c


def retention(q, k, v, g, gn, layer, state0, *, n_seq, seq, chunk, n_tok):
    nc = seq // chunk
    hps = RET_HEADS_PER_STEP
    decay, q_dec, k_dec, g_c = _ret_tables(chunk, n_tok)
    kern = functools.partial(_ret_kernel, n_chunks=nc)
    tok_k = pl.BlockSpec((chunk, hps * RET_KDIM), lambda b, h, c: (b * nc + c, h))
    tok_v = pl.BlockSpec((chunk, hps * RET_VDIM), lambda b, h, c: (b * nc + c, h))
    st_spec = pl.BlockSpec((None, hps, RET_KDIM, RET_VDIM), lambda b, h, c: (b, h, 0, 0))
    return pl.pallas_call(
        kern,
        grid=(n_seq, RET_HEADS // hps, nc),
        in_specs=[
            tok_k, tok_k, tok_v, tok_v,
            pl.BlockSpec((None, 1, hps * RET_VDIM), lambda b, h, c: (layer, 0, h)),
            pl.BlockSpec((hps, chunk, chunk), lambda b, h, c: (h, 0, 0)),
            pl.BlockSpec((hps, chunk, 1), lambda b, h, c: (h, 0, 0)),
            pl.BlockSpec((hps, chunk, 1), lambda b, h, c: (h, 0, 0)),
            pl.BlockSpec((hps, 1, 1), lambda b, h, c: (h, 0, 0)),
            st_spec,
        ],
        out_specs=[tok_v, st_spec],
        out_shape=[
            jax.ShapeDtypeStruct((n_seq * seq, RET_HEADS * RET_VDIM), BF16),
            jax.ShapeDtypeStruct((n_seq, RET_HEADS, RET_KDIM, RET_VDIM), F32),
        ],
        scratch_shapes=[pltpu.VMEM((hps, RET_KDIM, RET_VDIM), F32)],
        compiler_params=_params("parallel", "parallel", "arbitrary"),
        name="retention_chunk",
    )(q, k, v, g, gn, decay, q_dec, k_dec, g_c, state0)


def _project_nsa(xn, w_in, layer, cos, sin_signed, *, tm):
    m, d = xn.shape
    nq = N_HEADS * HEAD_DIM
    half = N_KV_HEADS * HEAD_DIM
    rows_per_tok = 2 * N_KV_HEADS
    tabs_per_seq = cos.shape[0] // tm
    tab_spec = pl.BlockSpec((tm, HEAD_DIM), lambda i, j, k: (i % tabs_per_seq, 0))
    tn_q = 512
    q = matmul(
        [xn], w_in, layer, col_block0=0, n_cols=nq, tm=tm, tn=tn_q, tk=d,
        extras=[cos, sin_signed], extra_specs=[tab_spec, tab_spec],
        out_shapes=[jax.ShapeDtypeStruct((m, nq), BF16), jax.ShapeDtypeStruct((nq, m), BF16)],
        out_specs=[pl.BlockSpec((tm, tn_q), lambda i, j, k: (i, j)), pl.BlockSpec((tn_q, tm), lambda i, j, k: (j, i))],
        epilogue=functools.partial(_ep_rope_q, scale=HEAD_DIM ** -0.5 * LOG2_E), name="nsa_proj_q")
    kv = matmul(
        [xn], w_in, layer, col_block0=nq // half, n_cols=3 * KV_ROW, tm=tm, tn=half, tk=d,
        extras=[cos, sin_signed], extra_specs=[tab_spec, tab_spec],
        out_shapes=[jax.ShapeDtypeStruct((m * rows_per_tok, HEAD_DIM), F32)] * 3
        + [jax.ShapeDtypeStruct((3, N_KV_HEADS, m, HEAD_DIM), BF16),
           jax.ShapeDtypeStruct((3, N_KV_HEADS, HEAD_DIM, m), BF16)],
        out_specs=[pl.BlockSpec((tm * rows_per_tok, HEAD_DIM), lambda i, j, k: (i, 0))] * 3
        + [pl.BlockSpec((None, N_KV_HEADS, tm, HEAD_DIM), lambda i, j, k: (j // 2, 0, i, 0)),
           pl.BlockSpec((None, N_KV_HEADS, HEAD_DIM, tm), lambda i, j, k: (j // 2, 0, 0, i))],
        epilogue=_ep_nsa_kv, name="nsa_proj_kv", column_steps_ordered=True)
    kv = (kv[:3], kv[3], kv[4])
    n_gate = 3 * N_HEADS
    w_gate = jnp.pad(w_in[layer, :, nq + 3 * KV_ROW:], ((0, 0), (0, LANES - n_gate)))[None]
    gates_t = matmul(
        [xn], w_gate, 0, col_block0=0, n_cols=LANES, tm=tm, tn=LANES, tk=d, extras=[], extra_specs=[],
        out_shapes=[jax.ShapeDtypeStruct((3, N_KV_HEADS, m, GQA_GROUP), F32)],
        out_specs=[pl.BlockSpec((3, N_KV_HEADS, tm, GQA_GROUP), lambda i, j, k: (0, 0, i, 0))],
        epilogue=_ep_gates, name="nsa_proj_gate")[0]
    return q, kv, gates_t


def _cmp_bias(w1, pos, b1, layer):
    k_full = CMP_BLOCK * HEAD_DIM
    pos_rows = jnp.zeros((2, SUBLANES, k_full), F32).at[:, 0].set(pos[layer].reshape(2, k_full))
    terms = []
    for e in range(2):
        t = matmul(
            [pos_rows[e]], w1.reshape(-1, k_full, HEAD_DIM), layer * 2 + e, col_block0=0, n_cols=HEAD_DIM,
            tm=SUBLANES, tn=HEAD_DIM, tk=k_full, extras=[], extra_specs=[],
            out_shapes=[jax.ShapeDtypeStruct((SUBLANES, HEAD_DIM), F32)],
            out_specs=[pl.BlockSpec((SUBLANES, HEAD_DIM), lambda i, j, k: (0, 0))],
            epilogue=_ep_cast, name="nsa_cmp_pos_term")[0]
        terms.append(t[0])
    return jnp.stack(terms) + b1[layer]


def _mlp_up(xn, w1, layer, *, tm):
    m, d = xn.shape
    d_ff = w1.shape[-1]
    tn = 1024
    return matmul(
        [xn], w1, layer, col_block0=0, n_cols=d_ff, tm=tm, tn=tn, tk=d, extras=[], extra_specs=[],
        out_shapes=[jax.ShapeDtypeStruct((m, d_ff), BF16)],
        out_specs=[pl.BlockSpec((tm, tn), lambda i, j, k: (i, j))],
        epilogue=_ep_relu2, name="mlp_up")[0]


def _mixer_out_proj(x, w, layer, h, norm_g, *, name):
    m, d = h.shape
    kdim = x.shape[1]
    w_bytes = 2 * kdim * d * w.dtype.itemsize
    per_row = 2 * kdim * x.dtype.itemsize + d * (4 * 4 + 2 * 2 + 4)
    tm = min(m, (VMEM_LIMIT_BYTES * 7 // 8 - w_bytes) // per_row // LANES * LANES)
    while m % tm:
        tm -= LANES
    row = pl.BlockSpec((tm, d), lambda i, j, k: (i, 0))
    return matmul(
        [x], w, layer, col_block0=0, n_cols=d, tm=tm, tn=d, tk=kdim,
        extras=[h, norm_g.reshape(1, d)], extra_specs=[row, pl.BlockSpec((1, d), lambda i, j, k: (0, 0))],
        out_shapes=[jax.ShapeDtypeStruct((m, d), F32), jax.ShapeDtypeStruct((m, d), BF16)], out_specs=[row, row],
        epilogue=_ep_residual_norm, name=name)


def _mlp_down(a, w2, layer, h, *, tm):
    m, d = h.shape
    tn = 1024
    tm = min(tm, 1024)
    tk = 2048 if m >= tm * 2 else a.shape[1]
    spec = pl.BlockSpec((tm, tn), lambda i, j, k: (i, j))
    return matmul(
        [a], w2, layer, col_block0=0, n_cols=d, tm=tm, tn=tn, tk=tk, extras=[h], extra_specs=[spec],
        out_shapes=[jax.ShapeDtypeStruct((m, d), F32)], out_specs=[spec], epilogue=_ep_residual_norm,
        name="mlp_down")[0]


def _nsa_tables(pos):
    cos, sin = _rope_tables(pos, HEAD_DIM // 2)
    return jnp.concatenate([cos, cos], axis=1), jnp.concatenate([-sin, sin], axis=1)


def _nsa_prompt(xn, layer, w_in, cmp_w1, cmp_bias, cmp_w2, *, n_seq, seq):
    tm = PROMPT_ROW_TILE
    cos, sin_signed = _nsa_tables(jnp.arange(seq))
    (q, qt), (kv_rows, k_bf, vt_bf), gates_t = _project_nsa(xn, w_in, layer, cos, sin_signed, tm=tm // 2)
    n_chunks = seq // CMP_STRIDE
    chunks_per_step = 128
    kv_chunks = kv_rows[0].reshape(1, (n_seq * n_chunks) // chunks_per_step, chunks_per_step, CMP_STRIDE,
                                   2 * N_KV_HEADS, HEAD_DIM)
    kc = compress(kv_chunks, 0, None, cmp_w1, cmp_bias, cmp_w2, n_seq=n_seq, n_chunks=n_chunks, n_pages=1)
    n_blk = seq // SEL_BLOCK
    cover_t = _cover_t(n_chunks, n_blk, n_blk)
    o1, val = cmp_attention(q, kc, cover_t, gates_t, n_seq=n_seq, seq=seq, tq=1024, pos0=0)
    bias_t = select_bias(val, n_blk=n_blk, tc=512, causal_seq=seq)
    nbp = pl.cdiv(n_blk, LANES) * LANES
    o2 = selected_attention(qt, k_bf, vt_bf, 1, gates_t, o1, bias_t, _block_onehot(seq, nbp), n_seq=n_seq, seq=seq,
                            tq=512, tk=512)
    o3 = window_attention(qt, k_bf, vt_bf, 2, gates_t, o2, n_seq=n_seq, seq=seq, tq=256, window=WINDOW,
                          out_dtype=BF16)
    return o3, kv_rows


def _nsa_sample(xn, layer, w_in, cmp_w1, cmp_bias, cmp_w2, cache_cmp, cache_sel, win_buf, page_table,
                *, n_seq, n_tok, past_len):
    tq = SAMPLE_PAD
    m = n_seq * tq
    cos, sin_signed = _nsa_tables(jnp.tile(past_len + jnp.arange(tq), n_seq))
    (q, _), (kv_rows, _, _), gates_t = _project_nsa(xn, w_in, layer, cos, sin_signed, tm=m)
    n_chunks = past_len // CMP_STRIDE
    kc = compress(cache_cmp, layer, page_table, cmp_w1, cmp_bias, cmp_w2, n_seq=n_seq, n_chunks=n_chunks,
                  n_pages=32)
    n_blk = pl.cdiv(past_len + n_tok, SEL_BLOCK)
    nbv = pl.cdiv(n_blk, SUBLANES) * SUBLANES
    cover_t = _cover_t(n_chunks, n_blk, nbv)
    o1, val = cmp_attention(q, kc, cover_t, gates_t, n_seq=n_seq, seq=tq, tq=tq, pos0=past_len)
    ncols = n_seq * N_KV_HEADS * n_tok
    bias_t = select_bias(val[..., :n_tok].transpose(2, 0, 1, 3).reshape(nbv, ncols), n_blk=n_blk, tc=ncols)
    blocks_per_page = cache_sel.shape[2] // (2 * N_KV_HEADS * SEL_BLOCK)
    n_pages = page_table.shape[1]
    chosen = (bias_t == 0).reshape(nbv, n_seq, N_KV_HEADS * n_tok).any(axis=2)
    needed = chosen[:n_pages * blocks_per_page].reshape(n_pages, blocks_per_page, n_seq).any(axis=1).T
    nbp = pl.cdiv(n_blk + 1, LANES) * LANES
    bias = bias_t.reshape(nbv, n_seq, N_KV_HEADS, n_tok).transpose(1, 2, 3, 0)
    bias = jnp.pad(bias, ((0, 0), (0, 0), (0, 0), (0, nbp - nbv)), constant_values=NEG_INF)
    bias = jnp.pad(bias, ((0, 0), (0, 0), (0, tq - n_tok), (0, 0)))
    o2 = sel_attention_sample(q, bias, needed, kv_rows[1], cache_sel, layer, page_table, gates_t, o1,
                              n_pages_step=16, past_len=past_len)
    o3 = win_attention_sample(q, win_buf, layer, kv_rows[2], gates_t, o2, out_dtype=BF16)
    return o3, kv_rows


def _ret_layer(xn, layer, w_in, gn, state0, pos, *, n_seq, seq, chunk, n_tok, tm):
    m, d = xn.shape
    nk = RET_HEADS * RET_KDIM
    nv = RET_HEADS * RET_VDIM
    cos, sin = _rope_tables(pos, RET_KDIM // 2)
    tabs_per_seq = cos.shape[0] // tm
    tab_spec = pl.BlockSpec((tm, LANES), lambda i, j, k: (i % tabs_per_seq, 0))
    tn = 512

    def proj(col0, n_cols, dtype, epilogue, extras, extra_specs, name):
        return matmul(
            [xn], w_in, layer, col_block0=col0 // tn, n_cols=n_cols, tm=tm, tn=tn, tk=d,
            extras=extras, extra_specs=extra_specs,
            out_shapes=[jax.ShapeDtypeStruct((m, n_cols), dtype)],
            out_specs=[pl.BlockSpec((tm, tn), lambda i, j, k: (i, j))],
            epilogue=epilogue, name=name)[0]

    q = proj(0, nk, BF16, functools.partial(_ep_rope_ret, scale=1.0), [cos, sin], [tab_spec, tab_spec], "ret_proj_q")
    k = proj(nk, nk, F32, functools.partial(_ep_rope_ret, scale=RET_KDIM ** -0.5), [cos, sin],
             [tab_spec, tab_spec], "ret_proj_k")
    v = proj(2 * nk, nv, BF16, _ep_cast, [], [], "ret_proj_v")
    g = proj(2 * nk + nv, nv, F32, _ep_cast, [], [], "ret_proj_g")
    return retention(q, k, v, g, gn.reshape(gn.shape[0], 1, nv), layer, state0, n_seq=n_seq, seq=seq, chunk=chunk,
                     n_tok=n_tok)


def kernel(x_prompt, x_sample, cache_cmp_kv, cache_sel_kv, state_win_kv, state_ret, page_table, norm_mix, norm_ffn,
           norm_final, nsa_w_in, nsa_w_out, nsa_cmp_pos, nsa_cmp_w1, nsa_cmp_b1, nsa_cmp_w2, ret_w_in, ret_gn,
           ret_w_out, ffn_w1, ffn_w2):
    n_seq_p, seq_p, d = x_prompt.shape
    n_seq_s, n_tok_s, _ = x_sample.shape
    depth = norm_mix.shape[0]
    n_nsa = cache_sel_kv.shape[0]
    pool, page_rows = cache_sel_kv.shape[1], cache_sel_kv.shape[2]
    past_len = page_table.shape[1] * page_rows
    assert past_len % CMP_STRIDE == 0 and n_tok_s < CMP_STRIDE and n_tok_s <= SAMPLE_PAD
    assert state_win_kv.shape[2] == WINDOW and seq_p >= WINDOW

    hp = x_prompt.reshape(n_seq_p * seq_p, d)
    hs = jnp.pad(x_sample, ((0, 0), (0, SAMPLE_PAD - n_tok_s), (0, 0))).reshape(n_seq_s * SAMPLE_PAD, d)
    m_s = n_seq_s * SAMPLE_PAD
    rows_per_tok = 2 * N_KV_HEADS
    cache_cmp = cache_cmp_kv.reshape(n_nsa, pool, page_rows // CMP_STRIDE, CMP_STRIDE, rows_per_tok, HEAD_DIM)
    cache_sel = cache_sel_kv.reshape(n_nsa, pool, page_rows * rows_per_tok, HEAD_DIM)
    win_buf = state_win_kv.reshape(n_nsa, n_seq_s, WINDOW * rows_per_tok, HEAD_DIM)
    cmp_w1 = nsa_cmp_w1.reshape(n_nsa, 2, CMP_BLOCK * HEAD_DIM, HEAD_DIM)
    kv_shape = (2, N_KV_HEADS, HEAD_DIM)
    ffn_w2 = ffn_w2.astype(BF16)
    ret_w_out = ret_w_out.astype(BF16)
    nsa_w_out = nsa_w_out.astype(BF16)

    cmp_p, cmp_s, sel_p, sel_s, win_p, win_s, ret_p, ret_s = [], [], [], [], [], [], [], []
    xp = rmsnorm(hp, norm_mix[0], BF16, 256)
    xs = rmsnorm(hs, norm_mix[0], BF16, m_s)
    for layer in range(depth):
        if layer % 2 == 0:
            a = layer // 2
            cmp_bias = _cmp_bias(cmp_w1, nsa_cmp_pos, nsa_cmp_b1, a)
            op, kv_p = _nsa_prompt(xp, a, nsa_w_in, cmp_w1, cmp_bias, nsa_cmp_w2, n_seq=n_seq_p, seq=seq_p)
            os_, kv_s = _nsa_sample(xs, a, nsa_w_in, cmp_w1, cmp_bias, nsa_cmp_w2, cache_cmp, cache_sel, win_buf,
                                    page_table, n_seq=n_seq_s, n_tok=n_tok_s, past_len=past_len)
            w_out, w_idx = nsa_w_out, a
            kv_p = [r.reshape((n_seq_p, seq_p) + kv_shape) for r in kv_p]
            kv_s = [r.reshape((n_seq_s, SAMPLE_PAD) + kv_shape)[:, :n_tok_s] for r in kv_s]
            cmp_p.append(kv_p[0])
            sel_p.append(kv_p[1])
            win_p.append(kv_p[2][:, seq_p - WINDOW:])
            cmp_s.append(kv_s[0])
            sel_s.append(kv_s[1])
            win_s.append(jnp.concatenate([state_win_kv[a][:, n_tok_s:], kv_s[2]], axis=1))
        else:
            r = layer // 2
            zeros = jnp.zeros((n_seq_p, RET_HEADS, RET_KDIM, RET_VDIM), F32)
            op, st_p = _ret_layer(xp, r, ret_w_in, ret_gn, zeros, jnp.arange(seq_p), n_seq=n_seq_p, seq=seq_p,
                                  chunk=RET_CHUNK, n_tok=RET_CHUNK, tm=PROMPT_ROW_TILE)
            os_, st_s = _ret_layer(xs, r, ret_w_in, ret_gn, state_ret[r],
                                   jnp.tile(past_len + jnp.arange(SAMPLE_PAD), n_seq_s), n_seq=n_seq_s,
                                   seq=SAMPLE_PAD, chunk=SAMPLE_PAD, n_tok=n_tok_s, tm=m_s)
            ret_p.append(st_p)
            ret_s.append(st_s)
            w_out, w_idx = ret_w_out, r
        hp, xp = _mixer_out_proj(op, w_out, w_idx, hp, norm_ffn[layer], name="mixer_out_proj")
        hs, xs = _mixer_out_proj(os_, w_out, w_idx, hs, norm_ffn[layer], name="mixer_out_proj_sample")
        hp = _mlp_down(_mlp_up(xp, ffn_w1, layer, tm=PROMPT_ROW_TILE), ffn_w2, layer, hp, tm=PROMPT_ROW_TILE)
        hs = _mlp_down(_mlp_up(xs, ffn_w1, layer, tm=m_s), ffn_w2, layer, hs, tm=m_s)
        if layer + 1 < depth:
            xp = rmsnorm(hp, norm_mix[layer + 1], BF16, 256)
            xs = rmsnorm(hs, norm_mix[layer + 1], BF16, m_s)
    y_prompt = rmsnorm(hp, norm_final, F32, 256).reshape(n_seq_p, seq_p, d)
    y_sample = rmsnorm(hs, norm_final, F32, m_s).reshape(n_seq_s, SAMPLE_PAD, d)[:, :n_tok_s]
    return (y_prompt, y_sample, jnp.stack(cmp_p), jnp.stack(cmp_s), jnp.stack(sel_p), jnp.stack(sel_s),
            jnp.stack(win_p), jnp.stack(win_s), jnp.stack(ret_p), jnp.stack(ret_s))
```
